```python
import math
import jax, jax.numpy as jnp
from jax import lax
import numpy as np

D_MODEL = 1024
BATCH = 1
SEQ = 16384
DEPTH = 1
DEC_BATCH = 128
DEC_SEQ = 4
PAST_LEN = 16384
PAGE_SIZE = 128

GLA_HEADS = 4
GLA_DK = 64
GLA_DV = 128
GLA_RANK = 16
GLA_TAU = 16.0
GLA_CHUNK = 64
SWA_HEADS = 8
SWA_KV_HEADS = 2
SWA_GROUP = SWA_HEADS // SWA_KV_HEADS
SWA_HD = 64
WINDOW = 128
D_FF = 2816
CONV_W = 3
EPS = 1e-6

GLA_K = GLA_HEADS * GLA_DK
GLA_V = GLA_HEADS * GLA_DV
SWA_Q = SWA_HEADS * SWA_HD
SWA_KV = SWA_KV_HEADS * SWA_HD
IN_SPLITS = (GLA_K, GLA_K, GLA_V, GLA_RANK, GLA_V, SWA_Q, SWA_KV, SWA_KV, D_MODEL, D_MODEL)
IN_COLS = sum(IN_SPLITS)

kernel_name = "gla_swa_sink_alibi_convffn_step"


def _split(z, sizes):
    out = []
    o = 0
    for s in sizes:
        out.append(z[..., o:o + s])
        o += s
    return out


def rmsnorm(x, w):
    xf = x.astype(jnp.float32)
    y = xf * lax.rsqrt(jnp.mean(xf * xf, axis=-1, keepdims=True) + EPS) * w.astype(jnp.float32)
    return y.astype(x.dtype)


def alibi_slopes():
    return jnp.exp2(-(8.0 / SWA_HEADS) * jnp.arange(1, SWA_HEADS + 1, dtype=jnp.float32))


def gla_recurrence(q, k, v, log_a, S0):
    B, T, H, _ = q.shape
    C = math.gcd(T, GLA_CHUNK)
    n = T // C

    def to_chunks(a):
        return a.astype(jnp.float32).reshape(B, n, C, H, a.shape[-1]).transpose(1, 0, 3, 2, 4)

    tri = jnp.tril(jnp.ones((C, C), dtype=bool))

    def step(S, inp):
        qc, kc, vc, lc = inp
        b = jnp.cumsum(lc, axis=2)
        inter = jnp.einsum('bhcd,bhdv->bhcv', qc * jnp.exp(b), S)
        diff = b[:, :, :, None, :] - b[:, :, None, :, :]
        decay = jnp.exp(jnp.where(tri[:, :, None], diff, -jnp.inf))
        att = jnp.einsum('bhid,bhjd,bhijd->bhij', qc, kc, decay)
        out = inter + jnp.einsum('bhij,bhjv->bhiv', att, vc)
        bl = b[:, :, -1:, :]
        S_new = jnp.exp(bl[:, :, 0, :])[..., None] * S + jnp.einsum('bhjd,bhjv->bhdv', kc * jnp.exp(bl - b), vc)
        return S_new, out

    S, o = lax.scan(step, S0.astype(jnp.float32), (to_chunks(q), to_chunks(k), to_chunks(v), to_chunks(log_a)))
    o = o.transpose(1, 0, 3, 2, 4).reshape(B, T, H, GLA_DV)
    return o, S


def sink_attention(q, k, v, rel, valid, sinks):
    s = jnp.einsum('...qkgd,...skd->...kgqs', q, k).astype(jnp.float32) * (SWA_HD ** -0.5)
    slopes = alibi_slopes().reshape(SWA_KV_HEADS, SWA_GROUP)[:, :, None, None]
    bias = -slopes * rel.astype(jnp.float32)
    mask = ((rel >= 0) & (rel < WINDOW)) & valid[..., None, None, None, :]
    s = jnp.where(mask, s + bias, -jnp.inf)
    sink = sinks.astype(jnp.float32).reshape(SWA_KV_HEADS, SWA_GROUP)[:, :, None, None]
    m = jnp.maximum(jnp.max(s, axis=-1, keepdims=True), sink)
    p = jnp.exp(s - m)
    denom = jnp.sum(p, axis=-1, keepdims=True) + jnp.exp(sink - m)
    return jnp.einsum('...kgqs,...skd->...qkgd', (p / denom).astype(v.dtype), v)


def token_mixers(h, S0, kbuf, vbuf, w_in, w_gate_up, b_gate, gla_norm, sinks, w_branch_a, w_branch_b, w_out):
    B, L, _ = h.shape
    z = h @ w_in
    q_a, k_a, v_a, r_a, g_a, q_b, k_b, v_b, gate_a, gate_b = _split(z, IN_SPLITS)

    log_a = jax.nn.log_sigmoid((r_a @ w_gate_up + b_gate).astype(jnp.float32)) / GLA_TAU
    q_a = q_a.reshape(B, L, GLA_HEADS, GLA_DK) * (GLA_DK ** -0.5)
    k_a = k_a.reshape(B, L, GLA_HEADS, GLA_DK)
    v_a = v_a.reshape(B, L, GLA_HEADS, GLA_DV)
    log_a = log_a.reshape(B, L, GLA_HEADS, GLA_DK)
    o_a, S_new = gla_recurrence(q_a, k_a, v_a, log_a, S0)
    o_a = rmsnorm(o_a, gla_norm).astype(h.dtype).reshape(B, L, GLA_V) * jax.nn.silu(g_a)

    q_b = q_b.reshape(B, L, SWA_KV_HEADS, SWA_GROUP, SWA_HD)
    k_b = k_b.reshape(B, L, SWA_KV_HEADS, SWA_HD)
    v_b = v_b.reshape(B, L, SWA_KV_HEADS, SWA_HD)
    if kbuf is None:
        nb = L // WINDOW
        qb = q_b.reshape(B, nb, WINDOW, SWA_KV_HEADS, SWA_GROUP, SWA_HD)

        def band(a):
            a = a.reshape(B, nb, WINDOW, SWA_KV_HEADS, SWA_HD)
            prev = jnp.concatenate([jnp.zeros_like(a[:, :1]), a[:, :-1]], axis=1)
            return jnp.concatenate([prev, a], axis=2)

        rel = jnp.arange(WINDOW)[:, None] + WINDOW - jnp.arange(2 * WINDOW)[None, :]
        valid = (jnp.arange(nb)[:, None] * WINDOW - WINDOW + jnp.arange(2 * WINDOW)[None, :]) >= 0
        o_b = sink_attention(qb, band(k_b), band(v_b), rel, valid, sinks).reshape(B, L, SWA_Q)
        keep = min(WINDOW, L)
        k_new, v_new = k_b[:, L - keep:], v_b[:, L - keep:]
    else:
        buf = kbuf.shape[1]
        k_all = jnp.concatenate([kbuf.astype(k_b.dtype), k_b], axis=1)
        v_all = jnp.concatenate([vbuf.astype(v_b.dtype), v_b], axis=1)
        rel = jnp.arange(L)[:, None] + buf - jnp.arange(buf + L)[None, :]
        valid = jnp.ones((buf + L,), dtype=bool)
        o_b = sink_attention(q_b, k_all, v_all, rel, valid, sinks).reshape(B, L, SWA_Q)
        k_new, v_new = k_all[:, L:], v_all[:, L:]

    merged = jax.nn.sigmoid(gate_a) * (o_a @ w_branch_a) + jax.nn.sigmoid(gate_b) * (o_b @ w_branch_b)
    return merged @ w_out, S_new, k_new, v_new


def conv_ffn(h, conv_buf, w_ffn_in, conv_w, conv_b, w_ffn_out):
    L = h.shape[1]
    u, g = _split(h @ w_ffn_in, (D_FF, D_FF))
    up = jnp.concatenate([conv_buf.astype(u.dtype), u], axis=1)
    c = conv_b
    for j in range(CONV_W):
        c = c + conv_w[j] * up[:, j:j + L]
    y = (jax.nn.gelu(c, approximate=True) * g) @ w_ffn_out
    return y, up[:, L:]


def layer(x, S0, kbuf, vbuf, cbuf, norm_mix_pre, norm_mix_post, w_in, w_gate_up, b_gate, gla_norm, sinks,
          w_branch_a, w_branch_b, w_out, norm_ffn_pre, norm_ffn_post, w_ffn_in, conv_w, conv_b, w_ffn_out):
    m, S, k_new, v_new = token_mixers(rmsnorm(x, norm_mix_pre), S0, kbuf, vbuf, w_in, w_gate_up, b_gate,
                                      gla_norm, sinks, w_branch_a, w_branch_b, w_out)
    x = x + rmsnorm(m, norm_mix_post)
    f, c_new = conv_ffn(rmsnorm(x, norm_ffn_pre), cbuf, w_ffn_in, conv_w, conv_b, w_ffn_out)
    x = x + rmsnorm(f, norm_ffn_post)
    return x, S, k_new, v_new, c_new


def setup_inputs(seed: int = 0) -> dict:
    key = jax.random.key(seed)
    ks = jax.random.split(key, 24)
    f32 = jnp.float32
    buf = min(WINDOW, PAST_LEN)

    def nrm(k, shape, scale):
        return jax.random.normal(k, shape, f32) * scale

    return {
        "x_prompt": nrm(ks[0], (BATCH, SEQ, D_MODEL), 1.0),
        "x_sample": nrm(ks[1], (DEC_BATCH, DEC_SEQ, D_MODEL), 1.0),
        "state_gla": nrm(ks[2], (DEPTH, DEC_BATCH, GLA_HEADS, GLA_DK, GLA_DV), 0.5),
        "cache_swa_k": nrm(ks[3], (DEPTH, DEC_BATCH, buf, SWA_KV_HEADS, SWA_HD), 1.0),
        "cache_swa_v": nrm(ks[4], (DEPTH, DEC_BATCH, buf, SWA_KV_HEADS, SWA_HD), 1.0),
        "state_ffn_conv": nrm(ks[5], (DEPTH, DEC_BATCH, CONV_W - 1, D_FF), 1.0),
        "norm_mix_pre": 1.0 + nrm(ks[6], (DEPTH, D_MODEL), 0.02),
        "norm_mix_post": 1.0 + nrm(ks[7], (DEPTH, D_MODEL), 0.02),
        "w_in": nrm(ks[8], (DEPTH, D_MODEL, IN_COLS), D_MODEL ** -0.5),
        "w_gate_up": nrm(ks[9], (DEPTH, GLA_RANK, GLA_K), GLA_RANK ** -0.5),
        "b_gate": nrm(ks[10], (DEPTH, GLA_K), 0.1),
        "gla_norm": 1.0 + nrm(ks[11], (DEPTH, GLA_DV), 0.02),
        "sinks": nrm(ks[12], (DEPTH, SWA_HEADS), 0.5),
        "w_branch_a": nrm(ks[13], (DEPTH, GLA_V, D_MODEL), GLA_V ** -0.5),
        "w_branch_b": nrm(ks[14], (DEPTH, SWA_Q, D_MODEL), SWA_Q ** -0.5),
        "w_out": nrm(ks[15], (DEPTH, D_MODEL, D_MODEL), D_MODEL ** -0.5),
        "norm_ffn_pre": 1.0 + nrm(ks[16], (DEPTH, D_MODEL), 0.02),
        "norm_ffn_post": 1.0 + nrm(ks[17], (DEPTH, D_MODEL), 0.02),
        "w_ffn_in": nrm(ks[18], (DEPTH, D_MODEL, 2 * D_FF), D_MODEL ** -0.5),
        "conv_w": nrm(ks[19], (DEPTH, CONV_W, D_FF), CONV_W ** -0.5),
        "conv_b": nrm(ks[20], (DEPTH, D_FF), 0.02),
        "w_ffn_out": nrm(ks[21], (DEPTH, D_FF, D_MODEL), D_FF ** -0.5),
    }


def reference(x_prompt, x_sample, state_gla, cache_swa_k, cache_swa_v, state_ffn_conv,
              norm_mix_pre, norm_mix_post, w_in, w_gate_up, b_gate, gla_norm, sinks,
              w_branch_a, w_branch_b, w_out, norm_ffn_pre, norm_ffn_post, w_ffn_in, conv_w, conv_b, w_ffn_out):
    yp, ys = x_prompt, x_sample
    gp, gs, kp, vp, ksm, vsm, cp, cs = [], [], [], [], [], [], [], []
    for l in range(DEPTH):
        params = (norm_mix_pre[l], norm_mix_post[l], w_in[l], w_gate_up[l], b_gate[l], gla_norm[l], sinks[l],
                  w_branch_a[l], w_branch_b[l], w_out[l], norm_ffn_pre[l], norm_ffn_post[l],
                  w_ffn_in[l], conv_w[l], conv_b[l], w_ffn_out[l])
        S0p = jnp.zeros((BATCH, GLA_HEADS, GLA_DK, GLA_DV), jnp.float32)
        c0p = jnp.zeros((BATCH, CONV_W - 1, D_FF), x_prompt.dtype)
        yp, S_p, k_p, v_p, c_p = layer(yp, S0p, None, None, c0p, *params)
        ys, S_s, k_s, v_s, c_s = layer(ys, state_gla[l], cache_swa_k[l], cache_swa_v[l], state_ffn_conv[l], *params)
        gp.append(S_p.astype(x_prompt.dtype)); gs.append(S_s.astype(state_gla.dtype))
        kp.append(k_p); vp.append(v_p)
        ksm.append(k_s.astype(cache_swa_k.dtype)); vsm.append(v_s.astype(cache_swa_v.dtype))
        cp.append(c_p); cs.append(c_s.astype(state_ffn_conv.dtype))
    gla_state_prompt = jnp.stack(gp)
    gla_state_sample = jnp.stack(gs)
    swa_k_prompt = jnp.stack(kp)
    swa_v_prompt = jnp.stack(vp)
    swa_k_sample = jnp.stack(ksm)
    swa_v_sample = jnp.stack(vsm)
    conv_prompt = jnp.stack(cp)
    conv_sample = jnp.stack(cs)
    return (yp, ys, gla_state_prompt, gla_state_sample, swa_k_prompt, swa_v_prompt,
            swa_k_sample, swa_v_sample, conv_prompt, conv_sample)
```

```python
import functools

import jax
import jax.numpy as jnp
from jax import lax
from jax.experimental import pallas as pl
from jax.experimental.pallas import tpu as pltpu

f32 = jnp.float32
bf16 = jnp.bfloat16

D_MODEL = 1024
GLA_HEADS = 4
GLA_DK = 64
GLA_DV = 128
GLA_RANK = 16
GLA_TAU = 16.0
GLA_CHUNK = 64
SWA_HEADS = 8
SWA_KV_HEADS = 2
SWA_HD = 64
WINDOW = 128
D_FF = 2816
CONV_W = 3
EPS = 1e-6
GLA_K = GLA_HEADS * GLA_DK
GLA_V = GLA_HEADS * GLA_DV
SWA_Q = SWA_HEADS * SWA_HD
SWA_KV = SWA_KV_HEADS * SWA_HD
LANES = 128

C_QA = 0
C_KA = C_QA + GLA_K
C_VA = C_KA + GLA_K
C_GA = C_VA + GLA_V
C_QB = C_GA + GLA_V
C_KB = C_QB + SWA_Q
C_VB = C_KB + SWA_KV
C_GTA = C_VB + SWA_KV
C_GTB = C_GTA + D_MODEL
C_RA = C_GTB + D_MODEL
RA_PAD = LANES
IN_COLS_PAD = C_RA + RA_PAD

VMEM_LIMIT = 56 * 1024 * 1024


def _dot(a, b):
    return jnp.dot(a, b, preferred_element_type=f32)


def _dot_nt(a, b):
    return lax.dot_general(a, b, (((1,), (1,)), ((), ())), preferred_element_type=f32)


def _dot_tn(a, b):
    return lax.dot_general(a, b, (((0,), (0,)), ((), ())), preferred_element_type=f32)


def _rms(x, w):
    return x * lax.rsqrt(jnp.mean(x * x, axis=-1, keepdims=True) + EPS) * w


def _split_hi_lo(x):
    hi = x.astype(bf16)
    lo = (x - hi.astype(f32)).astype(bf16)
    return hi, lo


def _log_decay(ra, wup_ref, bg_ref):
    xg = _dot(ra.astype(bf16), wup_ref[...]) + bg_ref[...]
    return jax.nn.log_sigmoid(xg) * (1.0 / GLA_TAU)


def _chunk_cumsum(la, chunk):
    n = la.shape[0]
    r = lax.broadcasted_iota(jnp.int32, (n, n), 0)
    c = lax.broadcasted_iota(jnp.int32, (n, n), 1)
    tri = jnp.where((c <= r) & ((r // chunk) == (c // chunk)), 1.0, 0.0).astype(bf16)
    hi, lo = _split_hi_lo(la)
    return _dot(tri, hi) + _dot(tri, lo)


def _even_head_lanes(shape):
    lane = lax.broadcasted_iota(jnp.int32, shape, len(shape) - 1)
    return (lane % LANES) < GLA_DK


def _gla_out_norm(o, gn_ref, ga):
    outs = []
    for h in range(GLA_HEADS):
        oh = o[:, h * GLA_DV:(h + 1) * GLA_DV]
        outs.append(_rms(oh, gn_ref[...]))
    on = jnp.concatenate(outs, axis=1)
    return on * (ga * jax.nn.sigmoid(ga))


def _mix_tail(x, oa, ob, gate_a, gate_b, wba_ref, wbb_ref, wout_ref, npost_ref):
    merged = (jax.nn.sigmoid(gate_a) * _dot(oa.astype(bf16), wba_ref[...])
              + jax.nn.sigmoid(gate_b) * _dot(ob.astype(bf16), wbb_ref[...]))
    m = _dot(merged.astype(bf16), wout_ref[...])
    return x + _rms(m, npost_ref[...])


def _alibi_slope(head):
    return 2.0 ** (-(8.0 / SWA_HEADS) * (head + 1))


def _kv_variants(x):
    lo = _even_head_lanes(x.shape)
    xr = pltpu.roll(x, SWA_HD, 1)
    zero = jnp.zeros_like(x)
    h0_lo = jnp.where(lo, x, zero).astype(bf16)
    h1_hi = jnp.where(lo, zero, x).astype(bf16)
    h1_lo = jnp.where(lo, xr, zero).astype(bf16)
    h0_hi = jnp.where(lo, zero, xr).astype(bf16)
    return (h0_lo, h0_hi), (h1_lo, h1_hi)


def _softmax_sink(s, sink):
    m = jnp.maximum(jnp.max(s, axis=-1, keepdims=True), sink)
    p = jnp.exp(s - m)
    denom = jnp.sum(p, axis=-1, keepdims=True) + jnp.exp(sink - m)
    return p, 1.0 / denom


def _mix_prompt_body(sink_ref, x_ref, npre_ref, win_ref, wup_ref, bg_ref, gn_ref,
                     wba_ref, wbb_ref, wout_ref, npost_ref,
                     y_ref, st_out_ref, k_out_ref, v_out_ref,
                     st_scr, kcat_scr, vcat_scr, oa_scr, ob_scr, *, T):
    i = pl.program_id(0)
    W = WINDOW
    C = GLA_CHUNK

    @pl.when(i == 0)
    def _():
        st_scr[...] = jnp.zeros_like(st_scr)
        kcat_scr[0:W, :] = jnp.zeros((W, SWA_KV), f32)
        vcat_scr[0:W, :] = jnp.zeros((W, SWA_KV), f32)

    @pl.when(i > 0)
    def _():
        kcat_scr[0:W, :] = kcat_scr[T:T + W, :]
        vcat_scr[0:W, :] = vcat_scr[T:T + W, :]

    x = x_ref[...]
    h = _rms(x, npre_ref[...]).astype(bf16)

    def proj(c0, n):
        return _dot(h, win_ref[:, c0:c0 + n])

    qa = proj(C_QA, GLA_K)
    ka = proj(C_KA, GLA_K)
    va = proj(C_VA, GLA_V)
    la = _log_decay(proj(C_RA, RA_PAD), wup_ref, bg_ref)
    b = _chunk_cumsum(la, C)
    qe = qa * jnp.exp(b) * (GLA_DK ** -0.5)
    ke = (ka * jnp.exp(-b)).astype(bf16)
    even = _even_head_lanes((T, GLA_K))
    qe_even = jnp.where(even, qe, 0.0).astype(bf16)
    qe_odd = jnp.where(even, 0.0, qe).astype(bf16)
    va_b = va.astype(bf16)
    ci = lax.broadcasted_iota(jnp.int32, (C, C), 0)
    cj = lax.broadcasted_iota(jnp.int32, (C, C), 1)
    causal = cj <= ci
    even_c = _even_head_lanes((C, LANES))
    for c in range(T // C):
        rows = slice(c * C, (c + 1) * C)
        b_c = b[rows]
        bl = b_c[C - 1:C, :]
        kl = ka[rows] * jnp.exp(bl - b_c)
        ebl = jnp.exp(bl)
        for p in range(GLA_HEADS // 2):
            lanes = slice(p * LANES, (p + 1) * LANES)
            st_pair = st_scr[:, lanes]
            st_b = st_pair.astype(bf16)
            ke_p = ke[rows, lanes]
            for e, qsel in enumerate((qe_even, qe_odd)):
                hd = 2 * p + e
                q_h = qsel[rows, lanes]
                att = _dot_nt(q_h, ke_p)
                att = jnp.where(causal, att, 0.0).astype(bf16)
                v_h = va_b[rows, hd * GLA_DV:(hd + 1) * GLA_DV]
                o_h = _dot_nt(q_h, st_b) + _dot(att, v_h)
                oa_scr[rows, hd * GLA_DV:(hd + 1) * GLA_DV] = o_h
            kl_p = kl[:, lanes]
            kl_stack = jnp.concatenate(
                [jnp.where(even_c, kl_p, 0.0), jnp.where(even_c, 0.0, kl_p)], axis=0).astype(bf16)
            v_stack = jnp.concatenate(
                [va_b[rows, (2 * p) * GLA_DV:(2 * p + 1) * GLA_DV],
                 va_b[rows, (2 * p + 1) * GLA_DV:(2 * p + 2) * GLA_DV]], axis=0)
            st_scr[:, lanes] = st_pair * ebl[:, lanes] + _dot_tn(v_stack, kl_stack)
    oa = _gla_out_norm(oa_scr[...], gn_ref, proj(C_GA, GLA_V))

    kb = proj(C_KB, SWA_KV)
    vb = proj(C_VB, SWA_KV)
    kcat_scr[W:W + T, :] = kb
    vcat_scr[W:W + T, :] = vb
    k_var = _kv_variants(kcat_scr[...])
    v_var = _kv_variants(vcat_scr[...])
    qi = lax.broadcasted_iota(jnp.int32, (W, 2 * W), 0)
    kc = lax.broadcasted_iota(jnp.int32, (W, 2 * W), 1)
    rel = qi + W - kc
    relf = rel.astype(f32)
    in_window = (rel >= 0) & (rel < W)
    qb = (proj(C_QB, SWA_Q) * (SWA_HD ** -0.5)).astype(bf16)
    for j in range(T // W):
        qrows = slice(j * W, (j + 1) * W)
        band = slice(j * W, j * W + 2 * W)
        if j == 0:
            mask = in_window & ((kc >= W) | (i > 0))
        else:
            mask = in_window
        for p in range(SWA_HEADS // 2):
            kv = (2 * p) // (SWA_HEADS // SWA_KV_HEADS)
            q_p = qb[qrows, p * LANES:(p + 1) * LANES]
            o_pair = None
            for e in range(2):
                hd = 2 * p + e
                s = _dot_nt(q_p, k_var[kv][e][band])
                s = jnp.where(mask, s - _alibi_slope(hd) * relf, -jnp.inf)
                pr, inv = _softmax_sink(s, sink_ref[hd])
                o_e = _dot(pr.astype(bf16), v_var[kv][e][band]) * inv
                o_pair = o_e if o_pair is None else o_pair + o_e
            ob_scr[qrows, p * LANES:(p + 1) * LANES] = o_pair

    y_ref[...] = _mix_tail(x, oa, ob_scr[...], proj(C_GTA, D_MODEL), proj(C_GTB, D_MODEL),
                           wba_ref, wbb_ref, wout_ref, npost_ref)

    @pl.when(i == pl.num_programs(0) - 1)
    def _():
        st_out_ref[...] = st_scr[...].T
        k_out_ref[...] = kcat_scr[T:T + W, :]
        v_out_ref[...] = vcat_scr[T:T + W, :]


def _const_spec(shape):
    nd = len(shape)
    return pl.BlockSpec(shape, lambda i: (0,) * nd, pipeline_mode=pl.Buffered(1))


def _mix_prompt(x, sinks, npre, win, wup, bg, gn, wba, wbb, wout, npost, *, T):
    L = x.shape[0]
    nb = L // T
    body = functools.partial(_mix_prompt_body, T=T)
    return pl.pallas_call(
        body,
        grid=(nb,),
        in_specs=[
            pl.BlockSpec(memory_space=pltpu.SMEM),
            pl.BlockSpec((T, D_MODEL), lambda i: (i, 0)),
            _const_spec(npre.shape), _const_spec(win.shape), _const_spec(wup.shape),
            _const_spec(bg.shape), _const_spec(gn.shape), _const_spec(wba.shape),
            _const_spec(wbb.shape), _const_spec(wout.shape), _const_spec(npost.shape),
        ],
        out_specs=[
            pl.BlockSpec((T, D_MODEL), lambda i: (i, 0)),
            pl.BlockSpec((GLA_K, GLA_DV), lambda i: (0, 0)),
            pl.BlockSpec((WINDOW, SWA_KV), lambda i: (0, 0)),
            pl.BlockSpec((WINDOW, SWA_KV), lambda i: (0, 0)),
        ],
        out_shape=[
            jax.ShapeDtypeStruct((L, D_MODEL), f32),
            jax.ShapeDtypeStruct((GLA_K, GLA_DV), f32),
            jax.ShapeDtypeStruct((WINDOW, SWA_KV), f32),
            jax.ShapeDtypeStruct((WINDOW, SWA_KV), f32),
        ],
        scratch_shapes=[
            pltpu.VMEM((GLA_DV, GLA_K), f32),
            pltpu.VMEM((T + WINDOW, SWA_KV), f32),
            pltpu.VMEM((T + WINDOW, SWA_KV), f32),
            pltpu.VMEM((T, GLA_V), f32),
            pltpu.VMEM((T, SWA_Q), f32),
        ],
        compiler_params=pltpu.CompilerParams(
            dimension_semantics=("arbitrary",), vmem_limit_bytes=VMEM_LIMIT),
        name="mix_prompt",
    )(sinks, x, npre, win, wup, bg, gn, wba, wbb, wout, npost)


def _pre_sample_body(x_ref, npre_ref, win_ref, wup_ref, bg_ref,
                     qe_ref, kl_ref, e3_ref, oin_ref, va_ref, ga_ref, qb_ref, kb_ref, vb_ref,
                     gta_ref, gtb_ref, x_scr, *, NB, NT):
    for t in range(NT):
        x_scr[t * NB:(t + 1) * NB, :] = x_ref[:, t * D_MODEL:(t + 1) * D_MODEL]
    h = _rms(x_scr[...], npre_ref[...]).astype(bf16)

    def proj(c0, n):
        return _dot(h, win_ref[:, c0:c0 + n])

    def put(ref, val, n):
        for t in range(NT):
            ref[:, t * n:(t + 1) * n] = val[t * NB:(t + 1) * NB, :]

    def blk(val, t):
        return val[t * NB:(t + 1) * NB, :]

    put(ga_ref, proj(C_GA, GLA_V), GLA_V)
    put(qb_ref, proj(C_QB, SWA_Q) * (SWA_HD ** -0.5), SWA_Q)
    put(kb_ref, proj(C_KB, SWA_KV), SWA_KV)
    put(vb_ref, proj(C_VB, SWA_KV), SWA_KV)
    put(gta_ref, proj(C_GTA, D_MODEL), D_MODEL)
    put(gtb_ref, proj(C_GTB, D_MODEL), D_MODEL)
    va = proj(C_VA, GLA_V)
    put(va_ref, va, GLA_V)

    qa = proj(C_QA, GLA_K) * (GLA_DK ** -0.5)
    ka = proj(C_KA, GLA_K)
    la = _log_decay(proj(C_RA, RA_PAD), wup_ref, bg_ref)
    b = [blk(la, 0)]
    for t in range(1, NT):
        b.append(b[-1] + blk(la, t))
    e3_ref[...] = jnp.exp(b[NT - 1])
    for t in range(NT):
        qe_ref[:, t * GLA_K:(t + 1) * GLA_K] = blk(qa, t) * jnp.exp(b[t])
        kl_ref[:, t * GLA_K:(t + 1) * GLA_K] = blk(ka, t) * jnp.exp(b[NT - 1] - b[t])
    pairs = [(t, j) for t in range(NT) for j in range(t + 1)]
    prods = [(blk(qa, t) * blk(ka, j) * jnp.exp(b[t] - b[j])).astype(bf16) for t, j in pairs]
    r = lax.broadcasted_iota(jnp.int32, (GLA_K, GLA_V), 0)
    c = lax.broadcasted_iota(jnp.int32, (GLA_K, GLA_V), 1)
    expand = jnp.where((r // GLA_DK) == (c // GLA_DV), 1.0, 0.0).astype(bf16)
    att = _dot(jnp.concatenate(prods, axis=0), expand)
    for t in range(NT):
        acc = None
        for idx, (tt, j) in enumerate(pairs):
            if tt != t:
                continue
            term = att[idx * NB:(idx + 1) * NB, :] * blk(va, j)
            acc = term if acc is None else acc + term
        oin_ref[:, t * GLA_V:(t + 1) * GLA_V] = acc


def _pre_sample(x2, npre, win, wup, bg, *, NT):
    NB = x2.shape[0]
    body = functools.partial(_pre_sample_body, NB=NB, NT=NT)
    widths = (GLA_K, GLA_K, None, GLA_V, GLA_V, GLA_V, SWA_Q, SWA_KV, SWA_KV, D_MODEL, D_MODEL)
    out_shape = [jax.ShapeDtypeStruct((NB, GLA_K if w is None else NT * w), f32) for w in widths]
    return pl.pallas_call(
        body,
        out_shape=out_shape,
        scratch_shapes=[pltpu.VMEM((NB * NT, D_MODEL), f32)],
        compiler_params=pltpu.CompilerParams(vmem_limit_bytes=VMEM_LIMIT),
        name="pre_sample",
    )(x2, npre, win, wup, bg)


def _state_sample_body(sink_ref, qe_ref, kl_ref, e3_ref, oin_ref, va_ref, qb_ref, kb_ref, vb_ref,
                       s0_ref, kc_ref, vc_ref,
                       oa_ref, ob_ref, s1_ref, k1_ref, v1_ref, kcat_scr, vcat_scr, *, BB, NT):
    W = WINDOW
    SK = W + 8
    HT = GLA_HEADS * NT
    hr = lax.broadcasted_iota(jnp.int32, (HT, GLA_K), 0) // NT
    hc = lax.broadcasted_iota(jnp.int32, (HT, GLA_K), 1) // GLA_DK
    own_head = hr == hc
    ones_rows = jnp.ones((16, GLA_DV), bf16)
    zero_rows = jnp.zeros((16, GLA_DV), bf16)
    zero_ht = jnp.zeros((HT, GLA_DV), bf16)
    G2 = 2 * NT
    row = lax.broadcasted_iota(jnp.int32, (G2, SK), 0)
    col = lax.broadcasted_iota(jnp.int32, (G2, SK), 1)
    rel = (row % NT) + W - col
    relf = rel.astype(f32)
    smask = (rel >= 0) & (rel < W)
    first_pair = lax.broadcasted_iota(jnp.int32, (G2, 1), 0) < NT

    for bi in range(BB):
        rows = slice(bi * NT, (bi + 1) * NT)
        s0 = s0_ref[bi]
        q4 = qe_ref[rows, :]
        qm = jnp.where(own_head, jnp.concatenate([q4] * GLA_HEADS, axis=0), 0.0).astype(bf16)
        o_inter = _dot(qm, s0.astype(bf16))
        for hd in range(GLA_HEADS):
            lanes = slice(hd * GLA_DV, (hd + 1) * GLA_DV)
            oa_ref[rows, lanes] = o_inter[hd * NT:(hd + 1) * NT, :] + oin_ref[rows, lanes]
        k4 = kl_ref[rows, :]
        km = jnp.where(own_head, jnp.concatenate([k4] * GLA_HEADS, axis=0), 0.0).astype(bf16)
        e = e3_ref[bi:bi + 1, :]
        e_hi = e.astype(bf16)
        r1 = e - e_hi.astype(f32)
        e_mid = r1.astype(bf16)
        e_lo = (r1 - e_mid.astype(f32)).astype(bf16)
        e_rows = jnp.concatenate([e_hi, e_mid, e_lo, jnp.zeros((13, GLA_K), bf16)], axis=0)
        lhs = jnp.concatenate([km, e_rows], axis=0)
        v4 = va_ref[rows, :].astype(bf16)
        vrep = jnp.concatenate([v4[:, hd * GLA_DV:(hd + 1) * GLA_DV] for hd in range(GLA_HEADS)], axis=0)
        rhs = jnp.concatenate([jnp.concatenate([vrep, zero_ht], axis=1),
                               jnp.concatenate([zero_rows, ones_rows], axis=1)], axis=0)
        res = _dot_tn(lhs, rhs)
        s1_ref[bi] = res[:, GLA_DV:] * s0 + res[:, :GLA_DV]

        kcat_scr[0:W, :] = kc_ref[bi]
        vcat_scr[0:W, :] = vc_ref[bi]
        kcat_scr[W:W + NT, :] = kb_ref[rows, :]
        vcat_scr[W:W + NT, :] = vb_ref[rows, :]
        kcat_scr[W + NT:SK, :] = jnp.zeros((SK - W - NT, SWA_KV), f32)
        vcat_scr[W + NT:SK, :] = jnp.zeros((SK - W - NT, SWA_KV), f32)
        k1_ref[bi] = kcat_scr[NT:NT + W, :]
        v1_ref[bi] = vcat_scr[NT:NT + W, :]
        k_var = _kv_variants(kcat_scr[...])
        v_var = _kv_variants(vcat_scr[...])
        q4b = qb_ref[rows, :].astype(bf16)
        for kv in range(SWA_KV_HEADS):
            p0 = 2 * kv
            q8 = jnp.concatenate([q4b[:, p0 * LANES:(p0 + 1) * LANES],
                                  q4b[:, (p0 + 1) * LANES:(p0 + 2) * LANES]], axis=0)
            o8 = None
            for e_ in range(2):
                h_first = 2 * p0 + e_
                h_second = 2 * (p0 + 1) + e_
                slope = jnp.where(first_pair, _alibi_slope(h_first), _alibi_slope(h_second))
                sink = jnp.where(first_pair, sink_ref[h_first], sink_ref[h_second])
                s = _dot_nt(q8, k_var[kv][e_])
                s = jnp.where(smask, s - slope * relf, -jnp.inf)
                pr, inv = _softmax_sink(s, sink)
                o_e = _dot(pr.astype(bf16), v_var[kv][e_]) * inv
                o8 = o_e if o8 is None else o8 + o_e
            ob_ref[rows, p0 * LANES:(p0 + 1) * LANES] = o8[0:NT, :]
            ob_ref[rows, (p0 + 1) * LANES:(p0 + 2) * LANES] = o8[NT:2 * NT, :]


def _state_sample(sinks, qe, kl, e3, oin, va, qb, kb, vb, s0, kc, vc, *, NT, BB):
    NBS = s0.shape[0]
    body = functools.partial(_state_sample_body, BB=BB, NT=NT)
    R = BB * NT

    def rows(n):
        return pl.BlockSpec((R, n), lambda i: (i, 0))

    def per_seq(shape):
        return pl.BlockSpec((BB,) + shape, lambda i: (i, 0, 0))

    return pl.pallas_call(
        body,
        grid=(NBS // BB,),
        in_specs=[
            pl.BlockSpec(memory_space=pltpu.SMEM),
            rows(GLA_K), rows(GLA_K), pl.BlockSpec((BB, GLA_K), lambda i: (i, 0)),
            rows(GLA_V), rows(GLA_V), rows(SWA_Q), rows(SWA_KV), rows(SWA_KV),
            per_seq((GLA_K, GLA_DV)), per_seq((WINDOW, SWA_KV)), per_seq((WINDOW, SWA_KV)),
        ],
        out_specs=[
            rows(GLA_V), rows(SWA_Q),
            per_seq((GLA_K, GLA_DV)), per_seq((WINDOW, SWA_KV)), per_seq((WINDOW, SWA_KV)),
        ],
        out_shape=[
            jax.ShapeDtypeStruct((NBS * NT, GLA_V), f32),
            jax.ShapeDtypeStruct((NBS * NT, SWA_Q), f32),
            jax.ShapeDtypeStruct((NBS, GLA_K, GLA_DV), f32),
            jax.ShapeDtypeStruct((NBS, WINDOW, SWA_KV), f32),
            jax.ShapeDtypeStruct((NBS, WINDOW, SWA_KV), f32),
        ],
        scratch_shapes=[
            pltpu.VMEM((WINDOW + 8, SWA_KV), f32),
            pltpu.VMEM((WINDOW + 8, SWA_KV), f32),
        ],
        compiler_params=pltpu.CompilerParams(
            dimension_semantics=("arbitrary",), vmem_limit_bytes=VMEM_LIMIT),
        name="state_sample",
    )(sinks, qe, kl, e3, oin, va, qb, kb, vb, s0, kc, vc)


def _post_sample_body(x_ref, oa_ref, ga_ref, ob_ref, gta_ref, gtb_ref, gn_ref,
                      wba_ref, wbb_ref, wout_ref, npost_ref, y_ref):
    oa = _gla_out_norm(oa_ref[...], gn_ref, ga_ref[...])
    y_ref[...] = _mix_tail(x_ref[...], oa, ob_ref[...], gta_ref[...], gtb_ref[...],
                           wba_ref, wbb_ref, wout_ref, npost_ref)


def _post_sample(x, oa, ga, ob, gta, gtb, gn, wba, wbb, wout, npost):
    return pl.pallas_call(
        _post_sample_body,
        out_shape=jax.ShapeDtypeStruct(x.shape, f32),
        compiler_params=pltpu.CompilerParams(vmem_limit_bytes=VMEM_LIMIT),
        name="post_sample",
    )(x, oa, ga, ob, gta, gtb, gn, wba, wbb, wout, npost)


FFN_COLS = 256


def _ffn_columns(h, wffn_ref, cw_ref, cb_ref, up_scr, y_scr, *, T, base, shift):
    for c0 in range(0, D_FF, FFN_COLS):
        cols = slice(c0, c0 + FFN_COLS)
        u = _dot(h, wffn_ref[:, c0:c0 + FFN_COLS])
        g = _dot(h, wffn_ref[:, D_FF + c0:D_FF + c0 + FFN_COLS])
        up_scr[base:base + T, cols] = u
        u1 = up_scr[base - shift:base - shift + T, cols]
        u2 = up_scr[base - 2 * shift:base - 2 * shift + T, cols]
        cv = (cb_ref[:, cols] + cw_ref[2:3, cols] * u + cw_ref[1:2, cols] * u1 + cw_ref[0:1, cols] * u2)
        y_scr[:, cols] = (jax.nn.gelu(cv, approximate=True) * g).astype(bf16)


def _ffn_prompt_body(x_ref, npre_ref, wffn_ref, cw_ref, cb_ref, wo_ref, npost_ref,
                     y_ref, conv_out_ref, up_scr, y_scr, *, T):
    i = pl.program_id(0)
    base = 8

    @pl.when(i == 0)
    def _():
        up_scr[0:base, :] = jnp.zeros((base, D_FF), f32)

    @pl.when(i > 0)
    def _():
        up_scr[0:base, :] = up_scr[T:T + base, :]

    x = x_ref[...]
    h = _rms(x, npre_ref[...]).astype(bf16)
    _ffn_columns(h, wffn_ref, cw_ref, cb_ref, up_scr, y_scr, T=T, base=base, shift=1)
    f = _dot(y_scr[...], wo_ref[...])
    y_ref[...] = x + _rms(f, npost_ref[...])

    @pl.when(i == pl.num_programs(0) - 1)
    def _():
        conv_out_ref[...] = up_scr[T:T + base, :]


def _ffn_prompt(x, npre, wffn, cw, cb, wo, npost, *, T):
    L = x.shape[0]
    body = functools.partial(_ffn_prompt_body, T=T)
    return pl.pallas_call(
        body,
        grid=(L // T,),
        in_specs=[
            pl.BlockSpec((T, D_MODEL), lambda i: (i, 0)),
            _const_spec(npre.shape), _const_spec(wffn.shape), _const_spec(cw.shape),
            _const_spec(cb.shape), _const_spec(wo.shape), _const_spec(npost.shape),
        ],
        out_specs=[
            pl.BlockSpec((T, D_MODEL), lambda i: (i, 0)),
            pl.BlockSpec((8, D_FF), lambda i: (0, 0)),
        ],
        out_shape=[
            jax.ShapeDtypeStruct((L, D_MODEL), f32),
            jax.ShapeDtypeStruct((8, D_FF), f32),
        ],
        scratch_shapes=[
            pltpu.VMEM((T + 8, D_FF), f32),
            pltpu.VMEM((T, D_FF), bf16),
        ],
        compiler_params=pltpu.CompilerParams(
            dimension_semantics=("arbitrary",), vmem_limit_bytes=VMEM_LIMIT),
        name="ffn_prompt",
    )(x, npre, wffn, cw, cb, wo, npost)


def _ffn_sample_body(x_ref, cst_ref, npre_ref, wffn_ref, cw_ref, cb_ref, wo_ref, npost_ref,
                     y_ref, conv_out_ref, x_scr, up_scr, y_scr, *, NB, NT):
    T = NB * NT
    for t in range(NT):
        x_scr[t * NB:(t + 1) * NB, :] = x_ref[:, t * D_MODEL:(t + 1) * D_MODEL]
    for t in range(CONV_W - 1):
        up_scr[t * NB:(t + 1) * NB, :] = cst_ref[:, t * D_FF:(t + 1) * D_FF]
    base = (CONV_W - 1) * NB
    x = x_scr[...]
    h = _rms(x, npre_ref[...]).astype(bf16)
    _ffn_columns(h, wffn_ref, cw_ref, cb_ref, up_scr, y_scr, T=T, base=base, shift=NB)
    f = _dot(y_scr[...], wo_ref[...])
    y = x + _rms(f, npost_ref[...])
    for t in range(NT):
        y_ref[:, t * D_MODEL:(t + 1) * D_MODEL] = y[t * NB:(t + 1) * NB, :]
    for t in range(CONV_W - 1):
        conv_out_ref[:, t * D_FF:(t + 1) * D_FF] = up_scr[T + t * NB:T + (t + 1) * NB, :]


def _ffn_sample(x2, cst2, npre, wffn, cw, cb, wo, npost, *, NT):
    NB = x2.shape[0]
    T = NB * NT
    body = functools.partial(_ffn_sample_body, NB=NB, NT=NT)
    return pl.pallas_call(
        body,
        out_shape=[
            jax.ShapeDtypeStruct((NB, NT * D_MODEL), f32),
            jax.ShapeDtypeStruct((NB, (CONV_W - 1) * D_FF), f32),
        ],
        scratch_shapes=[
            pltpu.VMEM((T, D_MODEL), f32),
            pltpu.VMEM((T + (CONV_W - 1) * NB, D_FF), f32),
            pltpu.VMEM((T, D_FF), bf16),
        ],
        compiler_params=pltpu.CompilerParams(vmem_limit_bytes=VMEM_LIMIT),
        name="ffn_sample",
    )(x2, cst2, npre, wffn, cw, cb, wo, npost)


def _prep_w_in(w_in):
    offs = {}
    o = 0
    for name, n in (("qa", GLA_K), ("ka", GLA_K), ("va", GLA_V), ("ra", GLA_RANK), ("ga", GLA_V),
                    ("qb", SWA_Q), ("kb", SWA_KV), ("vb", SWA_KV), ("gta", D_MODEL), ("gtb", D_MODEL)):
        offs[name] = (o, n)
        o += n
    order = ("qa", "ka", "va", "ga", "qb", "kb", "vb", "gta", "gtb", "ra")
    parts = [w_in[:, offs[k][0]:offs[k][0] + offs[k][1]] for k in order]
    parts.append(jnp.zeros((w_in.shape[0], RA_PAD - GLA_RANK), w_in.dtype))
    return jnp.concatenate(parts, axis=1).astype(bf16)


def kernel(x_prompt, x_sample, state_gla, cache_swa_k, cache_swa_v, state_ffn_conv, norm_mix_pre, norm_mix_post, w_in, w_gate_up, b_gate, gla_norm, sinks, w_branch_a, w_branch_b, w_out, norm_ffn_pre, norm_ffn_post, w_ffn_in, conv_w, conv_b, w_ffn_out):
    depth = w_in.shape[0]
    assert depth == 1
    l = 0
    B, L, _ = x_prompt.shape
    assert B == 1
    NBS, NT, _ = x_sample.shape

    win = _prep_w_in(w_in[l])
    wup = jnp.zeros((RA_PAD, GLA_K), f32).at[:GLA_RANK].set(w_gate_up[l]).astype(bf16)
    bg = b_gate[l].reshape(1, GLA_K)
    gn = gla_norm[l].reshape(1, GLA_DV)
    npre = norm_mix_pre[l].reshape(1, D_MODEL)
    npost = norm_mix_post[l].reshape(1, D_MODEL)
    wba = w_branch_a[l].astype(bf16)
    wbb = w_branch_b[l].astype(bf16)
    wout = w_out[l].astype(bf16)
    fpre = norm_ffn_pre[l].reshape(1, D_MODEL)
    fpost = norm_ffn_post[l].reshape(1, D_MODEL)
    wffn = w_ffn_in[l].astype(bf16)
    cw = conv_w[l]
    cb = conv_b[l].reshape(1, D_FF)
    wo = w_ffn_out[l].astype(bf16)
    sk = sinks[l]

    x1, st_p, k_p, v_p = _mix_prompt(x_prompt[0], sk, npre, win, wup, bg, gn, wba, wbb, wout, npost, T=256)
    y_p, conv_p = _ffn_prompt(x1, fpre, wffn, cw, cb, wo, fpost, T=256)

    y_prompt = y_p[None]
    gla_state_prompt = st_p.reshape(1, 1, GLA_HEADS, GLA_DK, GLA_DV)
    swa_k_prompt = k_p.reshape(1, 1, WINDOW, SWA_KV_HEADS, SWA_HD)
    swa_v_prompt = v_p.reshape(1, 1, WINDOW, SWA_KV_HEADS, SWA_HD)
    conv_prompt = conv_p[8 - (CONV_W - 1):].reshape(1, 1, CONV_W - 1, D_FF)

    R = NBS * NT
    qe, kl, e3, oin, va, ga, qb, kb, vb, gta, gtb = _pre_sample(
        x_sample.reshape(NBS, NT * D_MODEL), npre, win, wup, bg, NT=NT)
    oa_raw, ob, s1, k1, v1 = _state_sample(
        sk, qe.reshape(R, GLA_K), kl.reshape(R, GLA_K), e3, oin.reshape(R, GLA_V), va.reshape(R, GLA_V),
        qb.reshape(R, SWA_Q), kb.reshape(R, SWA_KV), vb.reshape(R, SWA_KV),
        state_gla[l].reshape(NBS, GLA_K, GLA_DV),
        cache_swa_k[l].reshape(NBS, WINDOW, SWA_KV), cache_swa_v[l].reshape(NBS, WINDOW, SWA_KV),
        NT=NT, BB=8)
    x1s = _post_sample(x_sample.reshape(R, D_MODEL), oa_raw, ga.reshape(R, GLA_V), ob,
                       gta.reshape(R, D_MODEL), gtb.reshape(R, D_MODEL), gn, wba, wbb, wout, npost)
    y_s, conv_s = _ffn_sample(x1s.reshape(NBS, NT * D_MODEL),
                              state_ffn_conv[l].reshape(NBS, (CONV_W - 1) * D_FF),
                              fpre, wffn, cw, cb, wo, fpost, NT=NT)

    y_sample = y_s.reshape(NBS, NT, D_MODEL)
    gla_state_sample = s1.reshape(1, NBS, GLA_HEADS, GLA_DK, GLA_DV)
    swa_k_sample = k1.reshape(1, NBS, WINDOW, SWA_KV_HEADS, SWA_HD)
    swa_v_sample = v1.reshape(1, NBS, WINDOW, SWA_KV_HEADS, SWA_HD)
    conv_sample = conv_s.reshape(1, NBS, CONV_W - 1, D_FF)
    return (y_prompt, y_sample, gla_state_prompt, gla_state_sample, swa_k_prompt, swa_v_prompt,
            swa_k_sample, swa_v_sample, conv_prompt, conv_sample)
```

```python
import functools

import jax
import jax.numpy as jnp
from jax import lax
from jax.experimental import pallas as pl
from jax.experimental.pallas import tpu as pltpu

f32 = jnp.float32
bf16 = jnp.bfloat16

D_MODEL = 1024
GLA_HEADS = 4
GLA_DK = 64
GLA_DV = 128
GLA_RANK = 16
GLA_TAU = 16.0
GLA_CHUNK = 64
SWA_HEADS = 8
SWA_KV_HEADS = 2
SWA_HD = 64
WINDOW = 128
D_FF = 2816
CONV_W = 3
EPS = 1e-6
GLA_K = GLA_HEADS * GLA_DK
GLA_V = GLA_HEADS * GLA_DV
SWA_Q = SWA_HEADS * SWA_HD
SWA_KV = SWA_KV_HEADS * SWA_HD
LANES = 128

C_QA = 0
C_KA = C_QA + GLA_K
C_VA = C_KA + GLA_K
C_GA = C_VA + GLA_V
C_QB = C_GA + GLA_V
C_KB = C_QB + SWA_Q
C_VB = C_KB + SWA_KV
C_GTA = C_VB + SWA_KV
C_GTB = C_GTA + D_MODEL
C_RA = C_GTB + D_MODEL
RA_PAD = LANES
IN_COLS_PAD = C_RA + RA_PAD

VMEM_LIMIT = 56 * 1024 * 1024


def _dot(a, b):
    return jnp.dot(a, b, preferred_element_type=f32)


def _dot_nt(a, b):
    return lax.dot_general(a, b, (((1,), (1,)), ((), ())), preferred_element_type=f32)


def _dot_tn(a, b):
    return lax.dot_general(a, b, (((0,), (0,)), ((), ())), preferred_element_type=f32)


def _rms(x, w):
    return x * lax.rsqrt(jnp.mean(x * x, axis=-1, keepdims=True) + EPS) * w


def _split_hi_lo(x):
    hi = x.astype(bf16)
    lo = (x - hi.astype(f32)).astype(bf16)
    return hi, lo


def _log_decay(ra, wup_ref, bg_ref):
    xg = _dot(ra.astype(bf16), wup_ref[...]) + bg_ref[...]
    return jax.nn.log_sigmoid(xg) * (1.0 / GLA_TAU)


def _chunk_cumsum(la, chunk):
    n = la.shape[0]
    r = lax.broadcasted_iota(jnp.int32, (n, n), 0)
    c = lax.broadcasted_iota(jnp.int32, (n, n), 1)
    tri = jnp.where((c <= r) & ((r // chunk) == (c // chunk)), 1.0, 0.0).astype(bf16)
    hi, lo = _split_hi_lo(la)
    return _dot(tri, hi) + _dot(tri, lo)


def _even_head_lanes(shape):
    lane = lax.broadcasted_iota(jnp.int32, shape, len(shape) - 1)
    return (lane % LANES) < GLA_DK


def _gla_out_norm(o, gn_ref, ga):
    outs = []
    for h in range(GLA_HEADS):
        oh = o[:, h * GLA_DV:(h + 1) * GLA_DV]
        outs.append(_rms(oh, gn_ref[...]))
    on = jnp.concatenate(outs, axis=1)
    return on * (ga * jax.nn.sigmoid(ga))


def _mix_tail(x, oa, ob, gate_a, gate_b, wba_ref, wbb_ref, wout_ref, npost_ref):
    merged = (jax.nn.sigmoid(gate_a) * _dot(oa.astype(bf16), wba_ref[...])
              + jax.nn.sigmoid(gate_b) * _dot(ob.astype(bf16), wbb_ref[...]))
    m = _dot(merged.astype(bf16), wout_ref[...])
    return x + _rms(m, npost_ref[...])


def _alibi_slope(head):
    return 2.0 ** (-(8.0 / SWA_HEADS) * (head + 1))


def _kv_variants(x):
    lo = _even_head_lanes(x.shape)
    xr = pltpu.roll(x, SWA_HD, 1)
    zero = jnp.zeros_like(x)
    h0_lo = jnp.where(lo, x, zero).astype(bf16)
    h1_hi = jnp.where(lo, zero, x).astype(bf16)
    h1_lo = jnp.where(lo, xr, zero).astype(bf16)
    h0_hi = jnp.where(lo, zero, xr).astype(bf16)
    return (h0_lo, h0_hi), (h1_lo, h1_hi)


def _softmax_sink(s, sink):
    m = jnp.maximum(jnp.max(s, axis=-1, keepdims=True), sink)
    p = jnp.exp(s - m)
    denom = jnp.sum(p, axis=-1, keepdims=True) + jnp.exp(sink - m)
    return p, 1.0 / denom


def _mix_prompt_body(sink_ref, x_ref, npre_ref, win_ref, wup_ref, bg_ref, gn_ref,
                     wba_ref, wbb_ref, wout_ref, npost_ref,
                     y_ref, st_out_ref, k_out_ref, v_out_ref,
                     st_scr, kcat_scr, vcat_scr, oa_scr, ob_scr, gate_scr, *, T):
    i = pl.program_id(0)
    W = WINDOW
    C = GLA_CHUNK

    @pl.when(i == 0)
    def _():
        st_scr[...] = jnp.zeros_like(st_scr)
        kcat_scr[0:W, :] = jnp.zeros((W, SWA_KV), f32)
        vcat_scr[0:W, :] = jnp.zeros((W, SWA_KV), f32)

    @pl.when(i > 0)
    def _():
        kcat_scr[0:W, :] = kcat_scr[T:T + W, :]
        vcat_scr[0:W, :] = vcat_scr[T:T + W, :]

    x = x_ref[...]
    h = _rms(x, npre_ref[...]).astype(bf16)

    def proj(c0, n):
        return _dot(h, win_ref[:, c0:c0 + n])


    qa = proj(C_QA, GLA_K)
    ka = proj(C_KA, GLA_K)
    va_b = proj(C_VA, GLA_V).astype(bf16)
    la = _log_decay(proj(C_RA, RA_PAD), wup_ref, bg_ref)
    b = _chunk_cumsum(la, C)
    kcat_scr[W:W + T, :] = proj(C_KB, SWA_KV)
    vcat_scr[W:W + T, :] = proj(C_VB, SWA_KV)
    qb = (proj(C_QB, SWA_Q) * (SWA_HD ** -0.5)).astype(bf16)

    qe = qa * jnp.exp(b) * (GLA_DK ** -0.5)
    ke = (ka * jnp.exp(-b)).astype(bf16)
    even = _even_head_lanes((T, GLA_K))
    qe_even = jnp.where(even, qe, 0.0).astype(bf16)
    qe_odd = jnp.where(even, 0.0, qe).astype(bf16)
    k_var = _kv_variants(kcat_scr[...])
    v_var = _kv_variants(vcat_scr[...])

    r2 = lax.broadcasted_iota(jnp.int32, (2 * C, 2 * C), 0)
    c2 = lax.broadcasted_iota(jnp.int32, (2 * C, 2 * C), 1)
    pair_causal = ((r2 // C) == (c2 // C)) & ((c2 % C) <= (r2 % C))
    even_c = _even_head_lanes((C, LANES))
    st = [st_scr[:, p * LANES:(p + 1) * LANES] for p in range(GLA_HEADS // 2)]

    def gla_scores(c):
        rows = slice(c * C, (c + 1) * C)
        out = []
        for p in range(GLA_HEADS // 2):
            lanes = slice(p * LANES, (p + 1) * LANES)
            q2 = jnp.concatenate([qe_even[rows, lanes], qe_odd[rows, lanes]], axis=0)
            ke_p = ke[rows, lanes]
            rhs = jnp.concatenate([ke_p, ke_p, st[p].astype(bf16)], axis=0)
            r = _dot_nt(q2, rhs)
            att = jnp.where(pair_causal, r[:, 0:2 * C], 0.0).astype(bf16)
            out.append((att, r[:, 2 * C:]))
        return out

    def gla_update(c, sc):
        rows = slice(c * C, (c + 1) * C)
        b_c = b[rows]
        bl = b_c[C - 1:C, :]
        kl = ka[rows] * jnp.exp(bl - b_c)
        ebl = jnp.exp(bl)
        for p in range(GLA_HEADS // 2):
            lanes = slice(p * LANES, (p + 1) * LANES)
            att, inter = sc[p]
            v2 = jnp.concatenate(
                [va_b[rows, (2 * p) * GLA_DV:(2 * p + 1) * GLA_DV],
                 va_b[rows, (2 * p + 1) * GLA_DV:(2 * p + 2) * GLA_DV]], axis=0)
            o2 = inter + _dot(att, v2)
            oa_scr[rows, (2 * p) * GLA_DV:(2 * p + 1) * GLA_DV] = o2[0:C]
            oa_scr[rows, (2 * p + 1) * GLA_DV:(2 * p + 2) * GLA_DV] = o2[C:2 * C]
            kl_p = kl[:, lanes]
            kl_stack = jnp.concatenate(
                [jnp.where(even_c, kl_p, 0.0), jnp.where(even_c, 0.0, kl_p)], axis=0).astype(bf16)
            st[p] = st[p] * ebl[:, lanes] + _dot_tn(v2, kl_stack)

    qi = lax.broadcasted_iota(jnp.int32, (W, 2 * W), 0)
    kc = lax.broadcasted_iota(jnp.int32, (W, 2 * W), 1)
    rel = qi + W - kc
    relf = rel.astype(f32)
    in_window = (rel >= 0) & (rel < W)

    def swa_probs(j, kv):
        qrows = slice(j * W, (j + 1) * W)
        band = slice(j * W, j * W + 2 * W)
        if j == 0:
            mask = in_window & ((kc >= W) | (i > 0))
        else:
            mask = in_window
        pairs = (2 * kv, 2 * kv + 1)
        q2 = jnp.concatenate([qb[qrows, p * LANES:(p + 1) * LANES] for p in pairs], axis=0)
        out = []
        for e in range(2):
            s2 = _dot_nt(q2, k_var[kv][e][band])
            probs = []
            for half, p in enumerate(pairs):
                hd = 2 * p + e
                s = s2[half * W:(half + 1) * W]
                s = jnp.where(mask, s - _alibi_slope(hd) * relf, -jnp.inf)
                pr, inv = _softmax_sink(s, sink_ref[hd])
                probs.append((pr * inv).astype(bf16))
            out.append(jnp.concatenate(probs, axis=0))
        return out

    def swa_out(j, kv, probs):
        qrows = slice(j * W, (j + 1) * W)
        band = slice(j * W, j * W + 2 * W)
        o2 = _dot(probs[0], v_var[kv][0][band]) + _dot(probs[1], v_var[kv][1][band])
        for half, p in enumerate((2 * kv, 2 * kv + 1)):
            ob_scr[qrows, p * LANES:(p + 1) * LANES] = o2[half * W:(half + 1) * W]

    n_chunks = T // C
    assert n_chunks == (T // W) * SWA_KV_HEADS
    gw = 2 * D_MODEL // n_chunks
    for idx in range(n_chunks):
        j, kv = idx // SWA_KV_HEADS, idx % SWA_KV_HEADS
        probs = swa_probs(j, kv)
        sc = gla_scores(idx)
        gate_scr[:, idx * gw:(idx + 1) * gw] = proj(C_GTA + idx * gw, gw)
        gla_update(idx, sc)
        swa_out(j, kv, probs)
    for p in range(GLA_HEADS // 2):
        st_scr[:, p * LANES:(p + 1) * LANES] = st[p]

    oa = _gla_out_norm(oa_scr[...], gn_ref, proj(C_GA, GLA_V))
    y_ref[...] = _mix_tail(x, oa, ob_scr[...], gate_scr[:, 0:D_MODEL], gate_scr[:, D_MODEL:2 * D_MODEL],
                           wba_ref, wbb_ref, wout_ref, npost_ref)

    @pl.when(i == pl.num_programs(0) - 1)
    def _():
        st_out_ref[...] = st_scr[...].T
        k_out_ref[...] = kcat_scr[T:T + W, :]
        v_out_ref[...] = vcat_scr[T:T + W, :]


def _const_spec(shape):
    nd = len(shape)
    return pl.BlockSpec(shape, lambda i: (0,) * nd, pipeline_mode=pl.Buffered(1))


def _mix_prompt(x, sinks, npre, win, wup, bg, gn, wba, wbb, wout, npost, *, T):
    L = x.shape[0]
    nb = L // T
    body = functools.partial(_mix_prompt_body, T=T)
    return pl.pallas_call(
        body,
        grid=(nb,),
        in_specs=[
            pl.BlockSpec(memory_space=pltpu.SMEM),
            pl.BlockSpec((T, D_MODEL), lambda i: (i, 0)),
            _const_spec(npre.shape), _const_spec(win.shape), _const_spec(wup.shape),
            _const_spec(bg.shape), _const_spec(gn.shape), _const_spec(wba.shape),
            _const_spec(wbb.shape), _const_spec(wout.shape), _const_spec(npost.shape),
        ],
        out_specs=[
            pl.BlockSpec((T, D_MODEL), lambda i: (i, 0)),
            pl.BlockSpec((GLA_K, GLA_DV), lambda i: (0, 0)),
            pl.BlockSpec((WINDOW, SWA_KV), lambda i: (0, 0)),
            pl.BlockSpec((WINDOW, SWA_KV), lambda i: (0, 0)),
        ],
        out_shape=[
            jax.ShapeDtypeStruct((L, D_MODEL), f32),
            jax.ShapeDtypeStruct((GLA_K, GLA_DV), f32),
            jax.ShapeDtypeStruct((WINDOW, SWA_KV), f32),
            jax.ShapeDtypeStruct((WINDOW, SWA_KV), f32),
        ],
        scratch_shapes=[
            pltpu.VMEM((GLA_DV, GLA_K), f32),
            pltpu.VMEM((T + WINDOW, SWA_KV), f32),
            pltpu.VMEM((T + WINDOW, SWA_KV), f32),
            pltpu.VMEM((T, GLA_V), f32),
            pltpu.VMEM((T, SWA_Q), f32),
            pltpu.VMEM((T, 2 * D_MODEL), f32),
        ],
        compiler_params=pltpu.CompilerParams(
            dimension_semantics=("arbitrary",), vmem_limit_bytes=VMEM_LIMIT),
        name="mix_prompt",
    )(sinks, x, npre, win, wup, bg, gn, wba, wbb, wout, npost)


def _pre_sample_body(x_ref, npre_ref, win_ref, wup_ref, bg_ref,
                     qe_ref, kl_ref, e3_ref, oin_ref, va_ref, ga_ref, qb_ref, kb_ref, vb_ref,
                     gta_ref, gtb_ref, x_scr, *, NB, NT):
    for t in range(NT):
        x_scr[t * NB:(t + 1) * NB, :] = x_ref[:, t * D_MODEL:(t + 1) * D_MODEL]
    h = _rms(x_scr[...], npre_ref[...]).astype(bf16)

    def proj(c0, n):
        return _dot(h, win_ref[:, c0:c0 + n])

    def put(ref, val, n):
        for t in range(NT):
            ref[:, t * n:(t + 1) * n] = val[t * NB:(t + 1) * NB, :]

    def blk(val, t):
        return val[t * NB:(t + 1) * NB, :]

    put(ga_ref, proj(C_GA, GLA_V), GLA_V)
    put(qb_ref, proj(C_QB, SWA_Q) * (SWA_HD ** -0.5), SWA_Q)
    put(kb_ref, proj(C_KB, SWA_KV), SWA_KV)
    put(vb_ref, proj(C_VB, SWA_KV), SWA_KV)
    put(gta_ref, proj(C_GTA, D_MODEL), D_MODEL)
    put(gtb_ref, proj(C_GTB, D_MODEL), D_MODEL)
    va = proj(C_VA, GLA_V)
    put(va_ref, va, GLA_V)

    qa = proj(C_QA, GLA_K) * (GLA_DK ** -0.5)
    ka = proj(C_KA, GLA_K)
    la = _log_decay(proj(C_RA, RA_PAD), wup_ref, bg_ref)
    b = [blk(la, 0)]
    for t in range(1, NT):
        b.append(b[-1] + blk(la, t))
    e3_ref[...] = jnp.exp(b[NT - 1])
    for t in range(NT):
        qe_ref[:, t * GLA_K:(t + 1) * GLA_K] = blk(qa, t) * jnp.exp(b[t])
        kl_ref[:, t * GLA_K:(t + 1) * GLA_K] = blk(ka, t) * jnp.exp(b[NT - 1] - b[t])
    pairs = [(t, j) for t in range(NT) for j in range(t + 1)]
    prods = [(blk(qa, t) * blk(ka, j) * jnp.exp(b[t] - b[j])).astype(bf16) for t, j in pairs]
    r = lax.broadcasted_iota(jnp.int32, (GLA_K, GLA_V), 0)
    c = lax.broadcasted_iota(jnp.int32, (GLA_K, GLA_V), 1)
    expand = jnp.where((r // GLA_DK) == (c // GLA_DV), 1.0, 0.0).astype(bf16)
    att = _dot(jnp.concatenate(prods, axis=0), expand)
    for t in range(NT):
        acc = None
        for idx, (tt, j) in enumerate(pairs):
            if tt != t:
                continue
            term = att[idx * NB:(idx + 1) * NB, :] * blk(va, j)
            acc = term if acc is None else acc + term
        oin_ref[:, t * GLA_V:(t + 1) * GLA_V] = acc


def _pre_sample(x2, npre, win, wup, bg, *, NT):
    NB = x2.shape[0]
    body = functools.partial(_pre_sample_body, NB=NB, NT=NT)
    widths = (GLA_K, GLA_K, None, GLA_V, GLA_V, GLA_V, SWA_Q, SWA_KV, SWA_KV, D_MODEL, D_MODEL)
    out_shape = [jax.ShapeDtypeStruct((NB, GLA_K if w is None else NT * w), f32) for w in widths]
    return pl.pallas_call(
        body,
        out_shape=out_shape,
        scratch_shapes=[pltpu.VMEM((NB * NT, D_MODEL), f32)],
        compiler_params=pltpu.CompilerParams(vmem_limit_bytes=VMEM_LIMIT),
        name="pre_sample",
    )(x2, npre, win, wup, bg)


def _state_sample_body(sink_ref, qe_ref, kl_ref, e3_ref, oin_ref, va_ref, qb_ref, kb_ref, vb_ref,
                       s0_ref, kc_ref, vc_ref,
                       oa_ref, ob_ref, s1_ref, k1_ref, v1_ref, kcat_scr, vcat_scr, *, BB, NT):
    W = WINDOW
    SK = W + 8
    HT = GLA_HEADS * NT
    hr = lax.broadcasted_iota(jnp.int32, (HT, GLA_K), 0) // NT
    hc = lax.broadcasted_iota(jnp.int32, (HT, GLA_K), 1) // GLA_DK
    own_head = hr == hc
    ones_rows = jnp.ones((16, GLA_DV), bf16)
    zero_rows = jnp.zeros((16, GLA_DV), bf16)
    zero_ht = jnp.zeros((HT, GLA_DV), bf16)
    G2 = 2 * NT
    row = lax.broadcasted_iota(jnp.int32, (G2, SK), 0)
    col = lax.broadcasted_iota(jnp.int32, (G2, SK), 1)
    rel = (row % NT) + W - col
    relf = rel.astype(f32)
    smask = (rel >= 0) & (rel < W)
    first_pair = lax.broadcasted_iota(jnp.int32, (G2, 1), 0) < NT

    for bi in range(BB):
        rows = slice(bi * NT, (bi + 1) * NT)
        s0 = s0_ref[bi]
        q4 = qe_ref[rows, :]
        qm = jnp.where(own_head, jnp.concatenate([q4] * GLA_HEADS, axis=0), 0.0).astype(bf16)
        o_inter = _dot(qm, s0.astype(bf16))
        for hd in range(GLA_HEADS):
            lanes = slice(hd * GLA_DV, (hd + 1) * GLA_DV)
            oa_ref[rows, lanes] = o_inter[hd * NT:(hd + 1) * NT, :] + oin_ref[rows, lanes]
        k4 = kl_ref[rows, :]
        km = jnp.where(own_head, jnp.concatenate([k4] * GLA_HEADS, axis=0), 0.0).astype(bf16)
        e = e3_ref[bi:bi + 1, :]
        e_hi = e.astype(bf16)
        r1 = e - e_hi.astype(f32)
        e_mid = r1.astype(bf16)
        e_lo = (r1 - e_mid.astype(f32)).astype(bf16)
        e_rows = jnp.concatenate([e_hi, e_mid, e_lo, jnp.zeros((13, GLA_K), bf16)], axis=0)
        lhs = jnp.concatenate([km, e_rows], axis=0)
        v4 = va_ref[rows, :].astype(bf16)
        vrep = jnp.concatenate([v4[:, hd * GLA_DV:(hd + 1) * GLA_DV] for hd in range(GLA_HEADS)], axis=0)
        rhs = jnp.concatenate([jnp.concatenate([vrep, zero_ht], axis=1),
                               jnp.concatenate([zero_rows, ones_rows], axis=1)], axis=0)
        res = _dot_tn(lhs, rhs)
        s1_ref[bi] = res[:, GLA_DV:] * s0 + res[:, :GLA_DV]

        kcat_scr[0:W, :] = kc_ref[bi]
        vcat_scr[0:W, :] = vc_ref[bi]
        kcat_scr[W:W + NT, :] = kb_ref[rows, :]
        vcat_scr[W:W + NT, :] = vb_ref[rows, :]
        kcat_scr[W + NT:SK, :] = jnp.zeros((SK - W - NT, SWA_KV), f32)
        vcat_scr[W + NT:SK, :] = jnp.zeros((SK - W - NT, SWA_KV), f32)
        k1_ref[bi] = kcat_scr[NT:NT + W, :]
        v1_ref[bi] = vcat_scr[NT:NT + W, :]
        k_var = _kv_variants(kcat_scr[...])
        v_var = _kv_variants(vcat_scr[...])
        q4b = qb_ref[rows, :].astype(bf16)
        for kv in range(SWA_KV_HEADS):
            p0 = 2 * kv
            q8 = jnp.concatenate([q4b[:, p0 * LANES:(p0 + 1) * LANES],
                                  q4b[:, (p0 + 1) * LANES:(p0 + 2) * LANES]], axis=0)
            o8 = None
            for e_ in range(2):
                h_first = 2 * p0 + e_
                h_second = 2 * (p0 + 1) + e_
                slope = jnp.where(first_pair, _alibi_slope(h_first), _alibi_slope(h_second))
                sink = jnp.where(first_pair, sink_ref[h_first], sink_ref[h_second])
                s = _dot_nt(q8, k_var[kv][e_])
                s = jnp.where(smask, s - slope * relf, -jnp.inf)
                pr, inv = _softmax_sink(s, sink)
                o_e = _dot(pr.astype(bf16), v_var[kv][e_]) * inv
                o8 = o_e if o8 is None else o8 + o_e
            ob_ref[rows, p0 * LANES:(p0 + 1) * LANES] = o8[0:NT, :]
            ob_ref[rows, (p0 + 1) * LANES:(p0 + 2) * LANES] = o8[NT:2 * NT, :]


def _state_sample(sinks, qe, kl, e3, oin, va, qb, kb, vb, s0, kc, vc, *, NT, BB):
    NBS = s0.shape[0]
    body = functools.partial(_state_sample_body, BB=BB, NT=NT)
    R = BB * NT

    def rows(n):
        return pl.BlockSpec((R, n), lambda i: (i, 0))

    def per_seq(shape):
        return pl.BlockSpec((BB,) + shape, lambda i: (i, 0, 0))

    return pl.pallas_call(
        body,
        grid=(NBS // BB,),
        in_specs=[
            pl.BlockSpec(memory_space=pltpu.SMEM),
            rows(GLA_K), rows(GLA_K), pl.BlockSpec((BB, GLA_K), lambda i: (i, 0)),
            rows(GLA_V), rows(GLA_V), rows(SWA_Q), rows(SWA_KV), rows(SWA_KV),
            per_seq((GLA_K, GLA_DV)), per_seq((WINDOW, SWA_KV)), per_seq((WINDOW, SWA_KV)),
        ],
        out_specs=[
            rows(GLA_V), rows(SWA_Q),
            per_seq((GLA_K, GLA_DV)), per_seq((WINDOW, SWA_KV)), per_seq((WINDOW, SWA_KV)),
        ],
        out_shape=[
            jax.ShapeDtypeStruct((NBS * NT, GLA_V), f32),
            jax.ShapeDtypeStruct((NBS * NT, SWA_Q), f32),
            jax.ShapeDtypeStruct((NBS, GLA_K, GLA_DV), f32),
            jax.ShapeDtypeStruct((NBS, WINDOW, SWA_KV), f32),
            jax.ShapeDtypeStruct((NBS, WINDOW, SWA_KV), f32),
        ],
        scratch_shapes=[
            pltpu.VMEM((WINDOW + 8, SWA_KV), f32),
            pltpu.VMEM((WINDOW + 8, SWA_KV), f32),
        ],
        compiler_params=pltpu.CompilerParams(
            dimension_semantics=("arbitrary",), vmem_limit_bytes=VMEM_LIMIT),
        name="state_sample",
    )(sinks, qe, kl, e3, oin, va, qb, kb, vb, s0, kc, vc)


def _post_sample_body(x_ref, oa_ref, ga_ref, ob_ref, gta_ref, gtb_ref, gn_ref,
                      wba_ref, wbb_ref, wout_ref, npost_ref, y_ref):
    oa = _gla_out_norm(oa_ref[...], gn_ref, ga_ref[...])
    y_ref[...] = _mix_tail(x_ref[...], oa, ob_ref[...], gta_ref[...], gtb_ref[...],
                           wba_ref, wbb_ref, wout_ref, npost_ref)


def _post_sample(x, oa, ga, ob, gta, gtb, gn, wba, wbb, wout, npost):
    return pl.pallas_call(
        _post_sample_body,
        out_shape=jax.ShapeDtypeStruct(x.shape, f32),
        compiler_params=pltpu.CompilerParams(vmem_limit_bytes=VMEM_LIMIT),
        name="post_sample",
    )(x, oa, ga, ob, gta, gtb, gn, wba, wbb, wout, npost)


FFN_COLS = 256


def _ffn_columns(h, wffn_ref, cw_ref, cb_ref, up_scr, y_scr, *, T, base, shift):
    for c0 in range(0, D_FF, FFN_COLS):
        cols = slice(c0, c0 + FFN_COLS)
        u = _dot(h, wffn_ref[:, c0:c0 + FFN_COLS])
        g = _dot(h, wffn_ref[:, D_FF + c0:D_FF + c0 + FFN_COLS])
        up_scr[base:base + T, cols] = u
        u1 = up_scr[base - shift:base - shift + T, cols]
        u2 = up_scr[base - 2 * shift:base - 2 * shift + T, cols]
        cv = (cb_ref[:, cols] + cw_ref[2:3, cols] * u + cw_ref[1:2, cols] * u1 + cw_ref[0:1, cols] * u2)
        y_scr[:, cols] = (jax.nn.gelu(cv, approximate=True) * g).astype(bf16)


def _ffn_prompt_body(x_ref, npre_ref, wffn_ref, cw_ref, cb_ref, wo_ref, npost_ref,
                     y_ref, conv_out_ref, up_scr, y_scr, *, T):
    i = pl.program_id(0)
    base = 8

    @pl.when(i == 0)
    def _():
        up_scr[0:base, :] = jnp.zeros((base, D_FF), f32)

    @pl.when(i > 0)
    def _():
        up_scr[0:base, :] = up_scr[T:T + base, :]

    x = x_ref[...]
    h = _rms(x, npre_ref[...]).astype(bf16)
    _ffn_columns(h, wffn_ref, cw_ref, cb_ref, up_scr, y_scr, T=T, base=base, shift=1)
    f = _dot(y_scr[...], wo_ref[...])
    y_ref[...] = x + _rms(f, npost_ref[...])

    @pl.when(i == pl.num_programs(0) - 1)
    def _():
        conv_out_ref[...] = up_scr[T:T + base, :]


def _ffn_prompt(x, npre, wffn, cw, cb, wo, npost, *, T):
    L = x.shape[0]
    body = functools.partial(_ffn_prompt_body, T=T)
    return pl.pallas_call(
        body,
        grid=(L // T,),
        in_specs=[
            pl.BlockSpec((T, D_MODEL), lambda i: (i, 0)),
            _const_spec(npre.shape), _const_spec(wffn.shape), _const_spec(cw.shape),
            _const_spec(cb.shape), _const_spec(wo.shape), _const_spec(npost.shape),
        ],
        out_specs=[
            pl.BlockSpec((T, D_MODEL), lambda i: (i, 0)),
            pl.BlockSpec((8, D_FF), lambda i: (0, 0)),
        ],
        out_shape=[
            jax.ShapeDtypeStruct((L, D_MODEL), f32),
            jax.ShapeDtypeStruct((8, D_FF), f32),
        ],
        scratch_shapes=[
            pltpu.VMEM((T + 8, D_FF), f32),
            pltpu.VMEM((T, D_FF), bf16),
        ],
        compiler_params=pltpu.CompilerParams(
            dimension_semantics=("arbitrary",), vmem_limit_bytes=VMEM_LIMIT),
        name="ffn_prompt",
    )(x, npre, wffn, cw, cb, wo, npost)


def _ffn_sample_body(x_ref, cst_ref, npre_ref, wffn_ref, cw_ref, cb_ref, wo_ref, npost_ref,
                     y_ref, conv_out_ref, x_scr, up_scr, y_scr, *, NB, NT):
    T = NB * NT
    for t in range(NT):
        x_scr[t * NB:(t + 1) * NB, :] = x_ref[:, t * D_MODEL:(t + 1) * D_MODEL]
    for t in range(CONV_W - 1):
        up_scr[t * NB:(t + 1) * NB, :] = cst_ref[:, t * D_FF:(t + 1) * D_FF]
    base = (CONV_W - 1) * NB
    x = x_scr[...]
    h = _rms(x, npre_ref[...]).astype(bf16)
    _ffn_columns(h, wffn_ref, cw_ref, cb_ref, up_scr, y_scr, T=T, base=base, shift=NB)
    f = _dot(y_scr[...], wo_ref[...])
    y = x + _rms(f, npost_ref[...])
    for t in range(NT):
        y_ref[:, t * D_MODEL:(t + 1) * D_MODEL] = y[t * NB:(t + 1) * NB, :]
    for t in range(CONV_W - 1):
        conv_out_ref[:, t * D_FF:(t + 1) * D_FF] = up_scr[T + t * NB:T + (t + 1) * NB, :]


def _ffn_sample(x2, cst2, npre, wffn, cw, cb, wo, npost, *, NT):
    NB = x2.shape[0]
    T = NB * NT
    body = functools.partial(_ffn_sample_body, NB=NB, NT=NT)
    return pl.pallas_call(
        body,
        out_shape=[
            jax.ShapeDtypeStruct((NB, NT * D_MODEL), f32),
            jax.ShapeDtypeStruct((NB, (CONV_W - 1) * D_FF), f32),
        ],
        scratch_shapes=[
            pltpu.VMEM((T, D_MODEL), f32),
            pltpu.VMEM((T + (CONV_W - 1) * NB, D_FF), f32),
            pltpu.VMEM((T, D_FF), bf16),
        ],
        compiler_params=pltpu.CompilerParams(vmem_limit_bytes=VMEM_LIMIT),
        name="ffn_sample",
    )(x2, cst2, npre, wffn, cw, cb, wo, npost)


def _prep_w_in(w_in):
    offs = {}
    o = 0
    for name, n in (("qa", GLA_K), ("ka", GLA_K), ("va", GLA_V), ("ra", GLA_RANK), ("ga", GLA_V),
                    ("qb", SWA_Q), ("kb", SWA_KV), ("vb", SWA_KV), ("gta", D_MODEL), ("gtb", D_MODEL)):
        offs[name] = (o, n)
        o += n
    order = ("qa", "ka", "va", "ga", "qb", "kb", "vb", "gta", "gtb", "ra")
    parts = [w_in[:, offs[k][0]:offs[k][0] + offs[k][1]] for k in order]
    parts.append(jnp.zeros((w_in.shape[0], RA_PAD - GLA_RANK), w_in.dtype))
    return jnp.concatenate(parts, axis=1).astype(bf16)


def kernel(x_prompt, x_sample, state_gla, cache_swa_k, cache_swa_v, state_ffn_conv, norm_mix_pre, norm_mix_post, w_in, w_gate_up, b_gate, gla_norm, sinks, w_branch_a, w_branch_b, w_out, norm_ffn_pre, norm_ffn_post, w_ffn_in, conv_w, conv_b, w_ffn_out):
    depth = w_in.shape[0]
    assert depth == 1
    l = 0
    B, L, _ = x_prompt.shape
    assert B == 1
    NBS, NT, _ = x_sample.shape

    win = _prep_w_in(w_in[l])
    wup = jnp.zeros((RA_PAD, GLA_K), f32).at[:GLA_RANK].set(w_gate_up[l]).astype(bf16)
    bg = b_gate[l].reshape(1, GLA_K)
    gn = gla_norm[l].reshape(1, GLA_DV)
    npre = norm_mix_pre[l].reshape(1, D_MODEL)
    npost = norm_mix_post[l].reshape(1, D_MODEL)
    wba = w_branch_a[l].astype(bf16)
    wbb = w_branch_b[l].astype(bf16)
    wout = w_out[l].astype(bf16)
    fpre = norm_ffn_pre[l].reshape(1, D_MODEL)
    fpost = norm_ffn_post[l].reshape(1, D_MODEL)
    wffn = w_ffn_in[l].astype(bf16)
    cw = conv_w[l]
    cb = conv_b[l].reshape(1, D_FF)
    wo = w_ffn_out[l].astype(bf16)
    sk = sinks[l]

    x1, st_p, k_p, v_p = _mix_prompt(x_prompt[0], sk, npre, win, wup, bg, gn, wba, wbb, wout, npost, T=256)
    y_p, conv_p = _ffn_prompt(x1, fpre, wffn, cw, cb, wo, fpost, T=256)

    y_prompt = y_p[None]
    gla_state_prompt = st_p.reshape(1, 1, GLA_HEADS, GLA_DK, GLA_DV)
    swa_k_prompt = k_p.reshape(1, 1, WINDOW, SWA_KV_HEADS, SWA_HD)
    swa_v_prompt = v_p.reshape(1, 1, WINDOW, SWA_KV_HEADS, SWA_HD)
    conv_prompt = conv_p[8 - (CONV_W - 1):].reshape(1, 1, CONV_W - 1, D_FF)

    R = NBS * NT
    qe, kl, e3, oin, va, ga, qb, kb, vb, gta, gtb = _pre_sample(
        x_sample.reshape(NBS, NT * D_MODEL), npre, win, wup, bg, NT=NT)
    oa_raw, ob, s1, k1, v1 = _state_sample(
        sk, qe.reshape(R, GLA_K), kl.reshape(R, GLA_K), e3, oin.reshape(R, GLA_V), va.reshape(R, GLA_V),
        qb.reshape(R, SWA_Q), kb.reshape(R, SWA_KV), vb.reshape(R, SWA_KV),
        state_gla[l].reshape(NBS, GLA_K, GLA_DV),
        cache_swa_k[l].reshape(NBS, WINDOW, SWA_KV), cache_swa_v[l].reshape(NBS, WINDOW, SWA_KV),
        NT=NT, BB=8)
    x1s = _post_sample(x_sample.reshape(R, D_MODEL), oa_raw, ga.reshape(R, GLA_V), ob,
                       gta.reshape(R, D_MODEL), gtb.reshape(R, D_MODEL), gn, wba, wbb, wout, npost)
    y_s, conv_s = _ffn_sample(x1s.reshape(NBS, NT * D_MODEL),
                              state_ffn_conv[l].reshape(NBS, (CONV_W - 1) * D_FF),
                              fpre, wffn, cw, cb, wo, fpost, NT=NT)

    y_sample = y_s.reshape(NBS, NT, D_MODEL)
    gla_state_sample = s1.reshape(1, NBS, GLA_HEADS, GLA_DK, GLA_DV)
    swa_k_sample = k1.reshape(1, NBS, WINDOW, SWA_KV_HEADS, SWA_HD)
    swa_v_sample = v1.reshape(1, NBS, WINDOW, SWA_KV_HEADS, SWA_HD)
    conv_sample = conv_s.reshape(1, NBS, CONV_W - 1, D_FF)
    return (y_prompt, y_sample, gla_state_prompt, gla_state_sample, swa_k_prompt, swa_v_prompt,
            swa_k_sample, swa_v_sample, conv_prompt, conv_sample)
```

```python
import functools

import jax
import jax.numpy as jnp
from jax import lax
from jax.experimental import pallas as pl
from jax.experimental.pallas import tpu as pltpu

f32 = jnp.float32
bf16 = jnp.bfloat16

D_MODEL = 1024
GLA_HEADS = 4
GLA_DK = 64
GLA_DV = 128
GLA_RANK = 16
GLA_TAU = 16.0
GLA_CHUNK = 64
SWA_HEADS = 8
SWA_KV_HEADS = 2
SWA_HD = 64
WINDOW = 128
D_FF = 2816
CONV_W = 3
EPS = 1e-6
GLA_K = GLA_HEADS * GLA_DK
GLA_V = GLA_HEADS * GLA_DV
SWA_Q = SWA_HEADS * SWA_HD
SWA_KV = SWA_KV_HEADS * SWA_HD
LANES = 128

C_QA = 0
C_KA = C_QA + GLA_K
C_VA = C_KA + GLA_K
C_GA = C_VA + GLA_V
C_QB = C_GA + GLA_V
C_KB = C_QB + SWA_Q
C_VB = C_KB + SWA_KV
C_GTA = C_VB + SWA_KV
C_GTB = C_GTA + D_MODEL
C_RA = C_GTB + D_MODEL
RA_PAD = LANES
IN_COLS_PAD = C_RA + RA_PAD

VMEM_LIMIT = 56 * 1024 * 1024


def _dot(a, b):
    return jnp.dot(a, b, preferred_element_type=f32)


def _dot_nt(a, b):
    return lax.dot_general(a, b, (((1,), (1,)), ((), ())), preferred_element_type=f32)


def _dot_tn(a, b):
    return lax.dot_general(a, b, (((0,), (0,)), ((), ())), preferred_element_type=f32)


def _rms(x, w):
    return x * lax.rsqrt(jnp.mean(x * x, axis=-1, keepdims=True) + EPS) * w


def _split_hi_lo(x):
    hi = x.astype(bf16)
    lo = (x - hi.astype(f32)).astype(bf16)
    return hi, lo


def _log_decay(ra, wup_ref, bg_ref):
    xg = _dot(ra.astype(bf16), wup_ref[...]) + bg_ref[...]
    return jax.nn.log_sigmoid(xg) * (1.0 / GLA_TAU)


def _chunk_cumsum(la, chunk):
    n = la.shape[0]
    r = lax.broadcasted_iota(jnp.int32, (n, n), 0)
    c = lax.broadcasted_iota(jnp.int32, (n, n), 1)
    tri = jnp.where((c <= r) & ((r // chunk) == (c // chunk)), 1.0, 0.0).astype(bf16)
    hi, lo = _split_hi_lo(la)
    return _dot(tri, hi) + _dot(tri, lo)


def _even_head_lanes(shape):
    lane = lax.broadcasted_iota(jnp.int32, shape, len(shape) - 1)
    return (lane % LANES) < GLA_DK


def _gla_out_norm(o, gn_ref, ga):
    outs = []
    for h in range(GLA_HEADS):
        oh = o[:, h * GLA_DV:(h + 1) * GLA_DV]
        outs.append(_rms(oh, gn_ref[...]))
    on = jnp.concatenate(outs, axis=1)
    return on * (ga * jax.nn.sigmoid(ga))


def _mix_tail(x, oa, ob, gate_a, gate_b, wba_ref, wbb_ref, wout_ref, npost_ref):
    merged = (jax.nn.sigmoid(gate_a) * _dot(oa.astype(bf16), wba_ref[...])
              + jax.nn.sigmoid(gate_b) * _dot(ob.astype(bf16), wbb_ref[...]))
    m = _dot(merged.astype(bf16), wout_ref[...])
    return x + _rms(m, npost_ref[...])


def _alibi_slope(head):
    return 2.0 ** (-(8.0 / SWA_HEADS) * (head + 1))


def _kv_variants(x):
    lo = _even_head_lanes(x.shape)
    xr = pltpu.roll(x, SWA_HD, 1)
    zero = jnp.zeros_like(x)
    h0_lo = jnp.where(lo, x, zero).astype(bf16)
    h1_hi = jnp.where(lo, zero, x).astype(bf16)
    h1_lo = jnp.where(lo, xr, zero).astype(bf16)
    h0_hi = jnp.where(lo, zero, xr).astype(bf16)
    return (h0_lo, h0_hi), (h1_lo, h1_hi)


def _softmax_sink(s, sink):
    m = jnp.maximum(jnp.max(s, axis=-1, keepdims=True), sink)
    p = jnp.exp(s - m)
    denom = jnp.sum(p, axis=-1, keepdims=True) + jnp.exp(sink - m)
    return p, 1.0 / denom


def _mix_prompt_body(sink_ref, x_ref, npre_ref, win_ref, wup_ref, bg_ref, gn_ref,
                     wba_ref, wbb_ref, wout_ref, npost_ref,
                     y_ref, st_out_ref, k_out_ref, v_out_ref,
                     st_scr, kcat_scr, vcat_scr, oa_scr, ob_scr, gate_scr, *, T):
    i = pl.program_id(0)
    W = WINDOW
    C = GLA_CHUNK

    @pl.when(i == 0)
    def _():
        st_scr[...] = jnp.zeros_like(st_scr)
        kcat_scr[0:W, :] = jnp.zeros((W, SWA_KV), f32)
        vcat_scr[0:W, :] = jnp.zeros((W, SWA_KV), f32)

    @pl.when(i > 0)
    def _():
        kcat_scr[0:W, :] = kcat_scr[T:T + W, :]
        vcat_scr[0:W, :] = vcat_scr[T:T + W, :]

    x = x_ref[...]
    h = _rms(x, npre_ref[...]).astype(bf16)

    def proj(c0, n):
        return _dot(h, win_ref[:, c0:c0 + n])


    qa = proj(C_QA, GLA_K)
    ka = proj(C_KA, GLA_K)
    va_b = proj(C_VA, GLA_V).astype(bf16)
    la = _log_decay(proj(C_RA, RA_PAD), wup_ref, bg_ref)
    b = _chunk_cumsum(la, C)
    kcat_scr[W:W + T, :] = proj(C_KB, SWA_KV)
    vcat_scr[W:W + T, :] = proj(C_VB, SWA_KV)
    qb = (proj(C_QB, SWA_Q) * (SWA_HD ** -0.5)).astype(bf16)

    qe = qa * jnp.exp(b) * (GLA_DK ** -0.5)
    ke = (ka * jnp.exp(-b)).astype(bf16)
    even = _even_head_lanes((T, GLA_K))
    qe_even = jnp.where(even, qe, 0.0).astype(bf16)
    qe_odd = jnp.where(even, 0.0, qe).astype(bf16)
    k_var = _kv_variants(kcat_scr[...])
    v_var = _kv_variants(vcat_scr[...])

    r2 = lax.broadcasted_iota(jnp.int32, (2 * C, 2 * C), 0)
    c2 = lax.broadcasted_iota(jnp.int32, (2 * C, 2 * C), 1)
    pair_causal = ((r2 // C) == (c2 // C)) & ((c2 % C) <= (r2 % C))
    even_c = _even_head_lanes((C, LANES))
    st = [st_scr[:, p * LANES:(p + 1) * LANES] for p in range(GLA_HEADS // 2)]

    def gla_scores(c):
        rows = slice(c * C, (c + 1) * C)
        out = []
        for p in range(GLA_HEADS // 2):
            lanes = slice(p * LANES, (p + 1) * LANES)
            q2 = jnp.concatenate([qe_even[rows, lanes], qe_odd[rows, lanes]], axis=0)
            ke_p = ke[rows, lanes]
            rhs = jnp.concatenate([ke_p, ke_p, st[p].astype(bf16)], axis=0)
            r = _dot_nt(q2, rhs)
            att = jnp.where(pair_causal, r[:, 0:2 * C], 0.0).astype(bf16)
            out.append((att, r[:, 2 * C:]))
        return out

    def gla_update(c, sc):
        rows = slice(c * C, (c + 1) * C)
        b_c = b[rows]
        bl = b_c[C - 1:C, :]
        kl = ka[rows] * jnp.exp(bl - b_c)
        ebl = jnp.exp(bl)
        for p in range(GLA_HEADS // 2):
            lanes = slice(p * LANES, (p + 1) * LANES)
            att, inter = sc[p]
            v2 = jnp.concatenate(
                [va_b[rows, (2 * p) * GLA_DV:(2 * p + 1) * GLA_DV],
                 va_b[rows, (2 * p + 1) * GLA_DV:(2 * p + 2) * GLA_DV]], axis=0)
            o2 = inter + _dot(att, v2)
            oa_scr[rows, (2 * p) * GLA_DV:(2 * p + 1) * GLA_DV] = o2[0:C]
            oa_scr[rows, (2 * p + 1) * GLA_DV:(2 * p + 2) * GLA_DV] = o2[C:2 * C]
            kl_p = kl[:, lanes]
            kl_stack = jnp.concatenate(
                [jnp.where(even_c, kl_p, 0.0), jnp.where(even_c, 0.0, kl_p)], axis=0).astype(bf16)
            st[p] = st[p] * ebl[:, lanes] + _dot_tn(v2, kl_stack)

    qi = lax.broadcasted_iota(jnp.int32, (W, 2 * W), 0)
    kc = lax.broadcasted_iota(jnp.int32, (W, 2 * W), 1)
    rel = qi + W - kc
    relf = rel.astype(f32)
    in_window = (rel >= 0) & (rel < W)

    def swa_probs(j, kv):
        qrows = slice(j * W, (j + 1) * W)
        band = slice(j * W, j * W + 2 * W)
        if j == 0:
            mask = in_window & ((kc >= W) | (i > 0))
        else:
            mask = in_window
        pairs = (2 * kv, 2 * kv + 1)
        q2 = jnp.concatenate([qb[qrows, p * LANES:(p + 1) * LANES] for p in pairs], axis=0)
        out = []
        for e in range(2):
            s2 = _dot_nt(q2, k_var[kv][e][band])
            probs = []
            for half, p in enumerate(pairs):
                hd = 2 * p + e
                s = s2[half * W:(half + 1) * W]
                s = jnp.where(mask, s - _alibi_slope(hd) * relf, -jnp.inf)
                pr, inv = _softmax_sink(s, sink_ref[hd])
                probs.append((pr * inv).astype(bf16))
            out.append(jnp.concatenate(probs, axis=0))
        return out

    def swa_out(j, kv, probs):
        qrows = slice(j * W, (j + 1) * W)
        band = slice(j * W, j * W + 2 * W)
        o2 = _dot(probs[0], v_var[kv][0][band]) + _dot(probs[1], v_var[kv][1][band])
        for half, p in enumerate((2 * kv, 2 * kv + 1)):
            ob_scr[qrows, p * LANES:(p + 1) * LANES] = o2[half * W:(half + 1) * W]

    n_chunks = T // C
    assert n_chunks == (T // W) * SWA_KV_HEADS
    gw = 2 * D_MODEL // n_chunks
    for idx in range(n_chunks):
        j, kv = idx // SWA_KV_HEADS, idx % SWA_KV_HEADS
        probs = swa_probs(j, kv)
        sc = gla_scores(idx)
        gate_scr[:, idx * gw:(idx + 1) * gw] = proj(C_GTA + idx * gw, gw)
        gla_update(idx, sc)
        swa_out(j, kv, probs)
    for p in range(GLA_HEADS // 2):
        st_scr[:, p * LANES:(p + 1) * LANES] = st[p]

    oa = _gla_out_norm(oa_scr[...], gn_ref, proj(C_GA, GLA_V))
    y_ref[...] = _mix_tail(x, oa, ob_scr[...], gate_scr[:, 0:D_MODEL], gate_scr[:, D_MODEL:2 * D_MODEL],
                           wba_ref, wbb_ref, wout_ref, npost_ref)

    @pl.when(i == pl.num_programs(0) - 1)
    def _():
        st_out_ref[...] = st_scr[...].T
        k_out_ref[...] = kcat_scr[T:T + W, :].T
        v_out_ref[...] = vcat_scr[T:T + W, :].T


def _const_spec(shape):
    nd = len(shape)
    return pl.BlockSpec(shape, lambda i: (0,) * nd, pipeline_mode=pl.Buffered(1))


def _mix_prompt(x, sinks, npre, win, wup, bg, gn, wba, wbb, wout, npost, *, T):
    L = x.shape[0]
    nb = L // T
    body = functools.partial(_mix_prompt_body, T=T)
    return pl.pallas_call(
        body,
        grid=(nb,),
        in_specs=[
            pl.BlockSpec(memory_space=pltpu.SMEM),
            pl.BlockSpec((T, D_MODEL), lambda i: (i, 0)),
            _const_spec(npre.shape), _const_spec(win.shape), _const_spec(wup.shape),
            _const_spec(bg.shape), _const_spec(gn.shape), _const_spec(wba.shape),
            _const_spec(wbb.shape), _const_spec(wout.shape), _const_spec(npost.shape),
        ],
        out_specs=[
            pl.BlockSpec((T, D_MODEL), lambda i: (i, 0)),
            pl.BlockSpec((GLA_K, GLA_DV), lambda i: (0, 0)),
            pl.BlockSpec((WINDOW, SWA_KV), lambda i: (0, 0)),
            pl.BlockSpec((WINDOW, SWA_KV), lambda i: (0, 0)),
        ],
        out_shape=[
            jax.ShapeDtypeStruct((L, D_MODEL), f32),
            jax.ShapeDtypeStruct((GLA_K, GLA_DV), f32),
            jax.ShapeDtypeStruct((WINDOW, SWA_KV), f32),
            jax.ShapeDtypeStruct((WINDOW, SWA_KV), f32),
        ],
        scratch_shapes=[
            pltpu.VMEM((GLA_DV, GLA_K), f32),
            pltpu.VMEM((T + WINDOW, SWA_KV), f32),
            pltpu.VMEM((T + WINDOW, SWA_KV), f32),
            pltpu.VMEM((T, GLA_V), f32),
            pltpu.VMEM((T, SWA_Q), f32),
            pltpu.VMEM((T, 2 * D_MODEL), f32),
        ],
        compiler_params=pltpu.CompilerParams(
            dimension_semantics=("arbitrary",), vmem_limit_bytes=VMEM_LIMIT),
        name="mix_prompt",
    )(sinks, x, npre, win, wup, bg, gn, wba, wbb, wout, npost)


def _pre_sample_body(x_ref, npre_ref, win_ref, wup_ref, bg_ref,
                     qe_ref, kl_ref, e3_ref, oin_ref, va_ref, ga_ref, qb_ref, kb_ref, vb_ref,
                     gta_ref, gtb_ref, x_scr, *, NB, NT):
    for t in range(NT):
        x_scr[t * NB:(t + 1) * NB, :] = x_ref[:, t, :]
    h = _rms(x_scr[...], npre_ref[...]).astype(bf16)

    def proj(c0, n):
        return _dot(h, win_ref[:, c0:c0 + n])

    def blk(val, t):
        return val[t * NB:(t + 1) * NB, :]

    ga_ref[...] = proj(C_GA, GLA_V)
    qb_ref[...] = proj(C_QB, SWA_Q) * (SWA_HD ** -0.5)
    kb_ref[...] = proj(C_KB, SWA_KV)
    vb_ref[...] = proj(C_VB, SWA_KV)
    gta_ref[...] = proj(C_GTA, D_MODEL)
    gtb_ref[...] = proj(C_GTB, D_MODEL)
    va = proj(C_VA, GLA_V)
    va_ref[...] = va

    qa = proj(C_QA, GLA_K) * (GLA_DK ** -0.5)
    ka = proj(C_KA, GLA_K)
    la = _log_decay(proj(C_RA, RA_PAD), wup_ref, bg_ref)
    b = [blk(la, 0)]
    for t in range(1, NT):
        b.append(b[-1] + blk(la, t))
    e3_ref[...] = jnp.exp(b[NT - 1])
    for t in range(NT):
        qe_ref[t * NB:(t + 1) * NB, :] = blk(qa, t) * jnp.exp(b[t])
        kl_ref[t * NB:(t + 1) * NB, :] = blk(ka, t) * jnp.exp(b[NT - 1] - b[t])
    pairs = [(t, j) for t in range(NT) for j in range(t + 1)]
    prods = [(blk(qa, t) * blk(ka, j) * jnp.exp(b[t] - b[j])).astype(bf16) for t, j in pairs]
    r = lax.broadcasted_iota(jnp.int32, (GLA_K, GLA_V), 0)
    c = lax.broadcasted_iota(jnp.int32, (GLA_K, GLA_V), 1)
    expand = jnp.where((r // GLA_DK) == (c // GLA_DV), 1.0, 0.0).astype(bf16)
    att = _dot(jnp.concatenate(prods, axis=0), expand)
    for t in range(NT):
        acc = None
        for idx, (tt, j) in enumerate(pairs):
            if tt != t:
                continue
            term = att[idx * NB:(idx + 1) * NB, :] * blk(va, j)
            acc = term if acc is None else acc + term
        oin_ref[t * NB:(t + 1) * NB, :] = acc


def _pre_sample(xs, npre, win, wup, bg):
    NB, NT, _ = xs.shape
    body = functools.partial(_pre_sample_body, NB=NB, NT=NT)
    widths = (GLA_K, GLA_K, None, GLA_V, GLA_V, GLA_V, SWA_Q, SWA_KV, SWA_KV, D_MODEL, D_MODEL)
    out_shape = [jax.ShapeDtypeStruct((NB, GLA_K) if w is None else (NT * NB, w), f32) for w in widths]
    return pl.pallas_call(
        body,
        out_shape=out_shape,
        scratch_shapes=[pltpu.VMEM((NB * NT, D_MODEL), f32)],
        compiler_params=pltpu.CompilerParams(vmem_limit_bytes=VMEM_LIMIT),
        name="pre_sample",
    )(xs, npre, win, wup, bg)


def _state_sample_body(sink_ref, qe_ref, kl_ref, e3_ref, oin_ref, va_ref, qb_ref, kb_ref, vb_ref,
                       s0_ref, kt_ref, vt_ref,
                       oa_ref, ob_ref, s1_ref, kt1_ref, vt1_ref, *, BB, NT):
    W = WINDOW
    SK = 2 * W
    HT = GLA_HEADS * NT
    HALF = SWA_HD
    hr = lax.broadcasted_iota(jnp.int32, (HT, GLA_K), 0) // NT
    hc = lax.broadcasted_iota(jnp.int32, (HT, GLA_K), 1) // GLA_DK
    own_head = hr == hc
    ones_rows = jnp.ones((16, GLA_DV), bf16)
    zero_rows = jnp.zeros((16, GLA_DV), bf16)
    zero_ht = jnp.zeros((HT, GLA_DV), bf16)
    G2 = 2 * NT
    row = lax.broadcasted_iota(jnp.int32, (G2, SK), 0)
    col = lax.broadcasted_iota(jnp.int32, (G2, SK), 1)
    rel = (row % NT) + W - col
    relf = rel.astype(f32)
    smask = (rel >= 0) & (rel < W)
    first_pair = lax.broadcasted_iota(jnp.int32, (G2, 1), 0) < NT
    pad_rows = jnp.zeros((8 - NT, SWA_KV), f32)
    pad_lanes = jnp.zeros((SWA_KV, SK - W - 8), f32)
    zero_half = jnp.zeros((HALF, SK), bf16)

    def head_variants(cat_t, kv):
        blk = cat_t[kv * HALF:(kv + 1) * HALF]
        return (jnp.concatenate([blk, zero_half], axis=0), jnp.concatenate([zero_half, blk], axis=0))

    pending = []
    for bi in range(BB):
        s0 = s0_ref[bi]
        q4 = qe_ref[:, bi, :]
        qm = jnp.where(own_head, jnp.concatenate([q4] * GLA_HEADS, axis=0), 0.0).astype(bf16)
        o_inter = _dot(qm, s0.astype(bf16))
        for hd in range(GLA_HEADS):
            lanes = slice(hd * GLA_DV, (hd + 1) * GLA_DV)
            oa_ref[:, bi, lanes] = o_inter[hd * NT:(hd + 1) * NT, :] + oin_ref[:, bi, lanes]
        k4 = kl_ref[:, bi, :]
        km = jnp.where(own_head, jnp.concatenate([k4] * GLA_HEADS, axis=0), 0.0).astype(bf16)
        e = e3_ref[bi:bi + 1, :]
        e_hi = e.astype(bf16)
        r1 = e - e_hi.astype(f32)
        e_mid = r1.astype(bf16)
        e_lo = (r1 - e_mid.astype(f32)).astype(bf16)
        e_rows = jnp.concatenate([e_hi, e_mid, e_lo, jnp.zeros((13, GLA_K), bf16)], axis=0)
        lhs = jnp.concatenate([km, e_rows], axis=0)
        v4 = va_ref[:, bi, :].astype(bf16)
        vrep = jnp.concatenate([v4[:, hd * GLA_DV:(hd + 1) * GLA_DV] for hd in range(GLA_HEADS)], axis=0)
        rhs = jnp.concatenate([jnp.concatenate([vrep, zero_ht], axis=1),
                               jnp.concatenate([zero_rows, ones_rows], axis=1)], axis=0)
        res = _dot_tn(lhs, rhs)
        s1_ref[bi] = res[:, GLA_DV:] * s0 + res[:, :GLA_DV]

        kt = kt_ref[bi]
        vt = vt_ref[bi]
        knew_t = jnp.concatenate([kb_ref[:, bi, :], pad_rows], axis=0).T
        vnew_t = jnp.concatenate([vb_ref[:, bi, :], pad_rows], axis=0).T
        kt1_ref[bi] = jnp.concatenate([kt[:, NT:], knew_t[:, 0:NT]], axis=1)
        vt1_ref[bi] = jnp.concatenate([vt[:, NT:], vnew_t[:, 0:NT]], axis=1)
        kcat = jnp.concatenate([kt, knew_t, pad_lanes], axis=1).astype(bf16)
        vcat = jnp.concatenate([vt, vnew_t, pad_lanes], axis=1).astype(bf16)
        q4b = qb_ref[:, bi, :].astype(bf16)
        for kv in range(SWA_KV_HEADS):
            p0 = 2 * kv
            q8 = jnp.concatenate([q4b[:, p0 * LANES:(p0 + 1) * LANES],
                                  q4b[:, (p0 + 1) * LANES:(p0 + 2) * LANES]], axis=0)
            scores = [_dot(q8, kvar) for kvar in head_variants(kcat, kv)]
            pending.append((bi, kv, scores, head_variants(vcat, kv)))

    for bi, kv, scores, v_vars in pending:
        p0 = 2 * kv
        o8_t = None
        for e_ in range(2):
            h_first = 2 * p0 + e_
            h_second = 2 * (p0 + 1) + e_
            slope = jnp.where(first_pair, _alibi_slope(h_first), _alibi_slope(h_second))
            sink = jnp.where(first_pair, sink_ref[h_first], sink_ref[h_second])
            s = jnp.where(smask, scores[e_] - slope * relf, -jnp.inf)
            pr, inv = _softmax_sink(s, sink)
            o_t = _dot_nt(v_vars[e_], (pr * inv).astype(bf16))
            o8_t = o_t if o8_t is None else o8_t + o_t
        o8 = o8_t.T
        ob_ref[:, bi, p0 * LANES:(p0 + 1) * LANES] = o8[0:NT, :]
        ob_ref[:, bi, (p0 + 1) * LANES:(p0 + 2) * LANES] = o8[NT:2 * NT, :]


def _state_sample(sinks, qe, kl, e3, oin, va, qb, kb, vb, s0, kt, vt, *, NT, BB):
    NBS = s0.shape[0]
    body = functools.partial(_state_sample_body, BB=BB, NT=NT)

    def tm(a):
        return a.reshape(NT, NBS, a.shape[-1])

    def rows(n):
        return pl.BlockSpec((NT, BB, n), lambda i: (0, i, 0))

    def per_seq(shape):
        return pl.BlockSpec((BB,) + shape, lambda i: (i, 0, 0))

    oa, ob, s1, kt1, vt1 = pl.pallas_call(
        body,
        grid=(NBS // BB,),
        in_specs=[
            pl.BlockSpec(memory_space=pltpu.SMEM),
            rows(GLA_K), rows(GLA_K), pl.BlockSpec((BB, GLA_K), lambda i: (i, 0)),
            rows(GLA_V), rows(GLA_V), rows(SWA_Q), rows(SWA_KV), rows(SWA_KV),
            per_seq((GLA_K, GLA_DV)), per_seq((SWA_KV, WINDOW)), per_seq((SWA_KV, WINDOW)),
        ],
        out_specs=[
            rows(GLA_V), rows(SWA_Q),
            per_seq((GLA_K, GLA_DV)), per_seq((SWA_KV, WINDOW)), per_seq((SWA_KV, WINDOW)),
        ],
        out_shape=[
            jax.ShapeDtypeStruct((NT, NBS, GLA_V), f32),
            jax.ShapeDtypeStruct((NT, NBS, SWA_Q), f32),
            jax.ShapeDtypeStruct((NBS, GLA_K, GLA_DV), f32),
            jax.ShapeDtypeStruct((NBS, SWA_KV, WINDOW), f32),
            jax.ShapeDtypeStruct((NBS, SWA_KV, WINDOW), f32),
        ],
        compiler_params=pltpu.CompilerParams(
            dimension_semantics=("arbitrary",), vmem_limit_bytes=VMEM_LIMIT),
        name="state_sample",
    )(sinks, tm(qe), tm(kl), e3, tm(oin), tm(va), tm(qb), tm(kb), tm(vb), s0, kt, vt)
    return oa.reshape(NT * NBS, GLA_V), ob.reshape(NT * NBS, SWA_Q), s1, kt1, vt1


def _post_sample_body(x_ref, oa_ref, ga_ref, ob_ref, gta_ref, gtb_ref, gn_ref,
                      wba_ref, wbb_ref, wout_ref, npost_ref, y_ref, x_scr, *, NB, NT):
    for t in range(NT):
        x_scr[t * NB:(t + 1) * NB, :] = x_ref[:, t, :]
    oa = _gla_out_norm(oa_ref[...], gn_ref, ga_ref[...])
    y_ref[...] = _mix_tail(x_scr[...], oa, ob_ref[...], gta_ref[...], gtb_ref[...],
                           wba_ref, wbb_ref, wout_ref, npost_ref)


def _post_sample(xs, oa, ga, ob, gta, gtb, gn, wba, wbb, wout, npost):
    NB, NT, _ = xs.shape
    return pl.pallas_call(
        functools.partial(_post_sample_body, NB=NB, NT=NT),
        out_shape=jax.ShapeDtypeStruct((NT * NB, D_MODEL), f32),
        scratch_shapes=[pltpu.VMEM((NT * NB, D_MODEL), f32)],
        compiler_params=pltpu.CompilerParams(vmem_limit_bytes=VMEM_LIMIT),
        name="post_sample",
    )(xs, oa, ga, ob, gta, gtb, gn, wba, wbb, wout, npost)


FFN_COLS = 256


def _ffn_columns(h, wffn_ref, cw_ref, cb_ref, up_scr, y_scr, *, T, base, shift):
    for c0 in range(0, D_FF, FFN_COLS):
        cols = slice(c0, c0 + FFN_COLS)
        u = _dot(h, wffn_ref[:, c0:c0 + FFN_COLS])
        g = _dot(h, wffn_ref[:, D_FF + c0:D_FF + c0 + FFN_COLS])
        up_scr[base:base + T, cols] = u
        u1 = up_scr[base - shift:base - shift + T, cols]
        u2 = up_scr[base - 2 * shift:base - 2 * shift + T, cols]
        cv = (cb_ref[:, cols] + cw_ref[2:3, cols] * u + cw_ref[1:2, cols] * u1 + cw_ref[0:1, cols] * u2)
        y_scr[:, cols] = (jax.nn.gelu(cv, approximate=True) * g).astype(bf16)


def _ffn_prompt_body(x_ref, npre_ref, wffn_ref, cw_ref, cb_ref, wo_ref, npost_ref,
                     y_ref, conv_out_ref, up_scr, y_scr, *, T):
    i = pl.program_id(0)
    base = 8

    @pl.when(i == 0)
    def _():
        up_scr[0:base, :] = jnp.zeros((base, D_FF), f32)

    @pl.when(i > 0)
    def _():
        up_scr[0:base, :] = up_scr[T:T + base, :]

    x = x_ref[...]
    h = _rms(x, npre_ref[...]).astype(bf16)
    _ffn_columns(h, wffn_ref, cw_ref, cb_ref, up_scr, y_scr, T=T, base=base, shift=1)
    f = _dot(y_scr[...], wo_ref[...])
    y_ref[...] = x + _rms(f, npost_ref[...])

    @pl.when(i == pl.num_programs(0) - 1)
    def _():
        conv_out_ref[...] = up_scr[T:T + base, :]


def _ffn_prompt(x, npre, wffn, cw, cb, wo, npost, *, T):
    L = x.shape[0]
    body = functools.partial(_ffn_prompt_body, T=T)
    return pl.pallas_call(
        body,
        grid=(L // T,),
        in_specs=[
            pl.BlockSpec((T, D_MODEL), lambda i: (i, 0)),
            _const_spec(npre.shape), _const_spec(wffn.shape), _const_spec(cw.shape),
            _const_spec(cb.shape), _const_spec(wo.shape), _const_spec(npost.shape),
        ],
        out_specs=[
            pl.BlockSpec((T, D_MODEL), lambda i: (i, 0)),
            pl.BlockSpec((8, D_FF), lambda i: (0, 0)),
        ],
        out_shape=[
            jax.ShapeDtypeStruct((L, D_MODEL), f32),
            jax.ShapeDtypeStruct((8, D_FF), f32),
        ],
        scratch_shapes=[
            pltpu.VMEM((T + 8, D_FF), f32),
            pltpu.VMEM((T, D_FF), bf16),
        ],
        compiler_params=pltpu.CompilerParams(
            dimension_semantics=("arbitrary",), vmem_limit_bytes=VMEM_LIMIT),
        name="ffn_prompt",
    )(x, npre, wffn, cw, cb, wo, npost)


def _ffn_sample_body(x_ref, cst_ref, npre_ref, wffn_ref, cw_ref, cb_ref, wo_ref, npost_ref,
                     y_ref, conv_out_ref, up_scr, y_scr, *, NB, NT):
    T = NB * NT
    for t in range(CONV_W - 1):
        up_scr[t * NB:(t + 1) * NB, :] = cst_ref[:, t, :]
    base = (CONV_W - 1) * NB
    x = x_ref[...]
    h = _rms(x, npre_ref[...]).astype(bf16)
    _ffn_columns(h, wffn_ref, cw_ref, cb_ref, up_scr, y_scr, T=T, base=base, shift=NB)
    f = _dot(y_scr[...], wo_ref[...])
    y = x + _rms(f, npost_ref[...])
    for t in range(NT):
        y_ref[:, t, :] = y[t * NB:(t + 1) * NB, :]
    for t in range(CONV_W - 1):
        conv_out_ref[:, t, :] = up_scr[T + t * NB:T + (t + 1) * NB, :]


def _ffn_sample(x, cst, npre, wffn, cw, cb, wo, npost):
    NB = cst.shape[0]
    T = x.shape[0]
    NT = T // NB
    body = functools.partial(_ffn_sample_body, NB=NB, NT=NT)
    return pl.pallas_call(
        body,
        out_shape=[
            jax.ShapeDtypeStruct((NB, NT, D_MODEL), f32),
            jax.ShapeDtypeStruct((NB, CONV_W - 1, D_FF), f32),
        ],
        scratch_shapes=[
            pltpu.VMEM((T + (CONV_W - 1) * NB, D_FF), f32),
            pltpu.VMEM((T, D_FF), bf16),
        ],
        compiler_params=pltpu.CompilerParams(vmem_limit_bytes=VMEM_LIMIT),
        name="ffn_sample",
    )(x, cst, npre, wffn, cw, cb, wo, npost)


def _prep_w_in(w_in):
    offs = {}
    o = 0
    for name, n in (("qa", GLA_K), ("ka", GLA_K), ("va", GLA_V), ("ra", GLA_RANK), ("ga", GLA_V),
                    ("qb", SWA_Q), ("kb", SWA_KV), ("vb", SWA_KV), ("gta", D_MODEL), ("gtb", D_MODEL)):
        offs[name] = (o, n)
        o += n
    order = ("qa", "ka", "va", "ga", "qb", "kb", "vb", "gta", "gtb", "ra")
    parts = [w_in[:, offs[k][0]:offs[k][0] + offs[k][1]] for k in order]
    parts.append(jnp.zeros((w_in.shape[0], RA_PAD - GLA_RANK), w_in.dtype))
    return jnp.concatenate(parts, axis=1).astype(bf16)


def kernel(x_prompt, x_sample, state_gla, cache_swa_k, cache_swa_v, state_ffn_conv, norm_mix_pre, norm_mix_post, w_in, w_gate_up, b_gate, gla_norm, sinks, w_branch_a, w_branch_b, w_out, norm_ffn_pre, norm_ffn_post, w_ffn_in, conv_w, conv_b, w_ffn_out):
    depth = w_in.shape[0]
    assert depth == 1
    l = 0
    B, L, _ = x_prompt.shape
    assert B == 1
    NBS, NT, _ = x_sample.shape

    win = _prep_w_in(w_in[l])
    wup = jnp.zeros((RA_PAD, GLA_K), f32).at[:GLA_RANK].set(w_gate_up[l]).astype(bf16)
    bg = b_gate[l].reshape(1, GLA_K)
    gn = gla_norm[l].reshape(1, GLA_DV)
    npre = norm_mix_pre[l].reshape(1, D_MODEL)
    npost = norm_mix_post[l].reshape(1, D_MODEL)
    wba = w_branch_a[l].astype(bf16)
    wbb = w_branch_b[l].astype(bf16)
    wout = w_out[l].astype(bf16)
    fpre = norm_ffn_pre[l].reshape(1, D_MODEL)
    fpost = norm_ffn_post[l].reshape(1, D_MODEL)
    wffn = w_ffn_in[l].astype(bf16)
    cw = conv_w[l]
    cb = conv_b[l].reshape(1, D_FF)
    wo = w_ffn_out[l].astype(bf16)
    sk = sinks[l]

    x1, st_p, k_p, v_p = _mix_prompt(x_prompt[0], sk, npre, win, wup, bg, gn, wba, wbb, wout, npost, T=256)
    y_p, conv_p = _ffn_prompt(x1, fpre, wffn, cw, cb, wo, fpost, T=256)

    y_prompt = y_p[None]
    gla_state_prompt = st_p.reshape(1, 1, GLA_HEADS, GLA_DK, GLA_DV)
    swa_k_prompt = jnp.transpose(k_p.reshape(SWA_KV_HEADS, SWA_HD, WINDOW), (2, 0, 1))[None, None]
    swa_v_prompt = jnp.transpose(v_p.reshape(SWA_KV_HEADS, SWA_HD, WINDOW), (2, 0, 1))[None, None]
    conv_prompt = conv_p[8 - (CONV_W - 1):].reshape(1, 1, CONV_W - 1, D_FF)

    qe, kl, e3, oin, va, ga, qb, kb, vb, gta, gtb = _pre_sample(x_sample, npre, win, wup, bg)
    kt = jnp.transpose(cache_swa_k[l], (0, 2, 3, 1)).reshape(NBS, SWA_KV, WINDOW)
    vt = jnp.transpose(cache_swa_v[l], (0, 2, 3, 1)).reshape(NBS, SWA_KV, WINDOW)
    oa_raw, ob, s1, kt1, vt1 = _state_sample(
        sk, qe, kl, e3, oin, va, qb, kb, vb, state_gla[l].reshape(NBS, GLA_K, GLA_DV), kt, vt, NT=NT, BB=8)
    x1s = _post_sample(x_sample, oa_raw, ga, ob, gta, gtb, gn, wba, wbb, wout, npost)
    y_sample, conv_s = _ffn_sample(x1s, state_ffn_conv[l], fpre, wffn, cw, cb, wo, fpost)

    def cache_out(t):
        return jnp.transpose(t.reshape(NBS, SWA_KV_HEADS, SWA_HD, WINDOW), (0, 3, 1, 2))[None]

    gla_state_sample = s1.reshape(1, NBS, GLA_HEADS, GLA_DK, GLA_DV)
    swa_k_sample = cache_out(kt1)
    swa_v_sample = cache_out(vt1)
    conv_sample = conv_s[None]
    return (y_prompt, y_sample, gla_state_prompt, gla_state_sample, swa_k_prompt, swa_v_prompt,
            swa_k_sample, swa_v_sample, conv_prompt, conv_sample)
```

```python
import functools

import jax
import jax.numpy as jnp
from jax import lax
from jax.experimental import pallas as pl
from jax.experimental.pallas import tpu as pltpu

f32 = jnp.float32
bf16 = jnp.bfloat16

D_MODEL = 1024
GLA_HEADS = 4
GLA_DK = 64
GLA_DV = 128
GLA_RANK = 16
GLA_TAU = 16.0
GLA_CHUNK = 64
SWA_HEADS = 8
SWA_KV_HEADS = 2
SWA_HD = 64
WINDOW = 128
D_FF = 2816
CONV_W = 3
EPS = 1e-6
GLA_K = GLA_HEADS * GLA_DK
GLA_V = GLA_HEADS * GLA_DV
SWA_Q = SWA_HEADS * SWA_HD
SWA_KV = SWA_KV_HEADS * SWA_HD
LANES = 128

C_QA = 0
C_KA = C_QA + GLA_K
C_VA = C_KA + GLA_K
C_GA = C_VA + GLA_V
C_QB = C_GA + GLA_V
C_KB = C_QB + SWA_Q
C_VB = C_KB + SWA_KV
C_GTA = C_VB + SWA_KV
C_GTB = C_GTA + D_MODEL
C_RA = C_GTB + D_MODEL
RA_PAD = LANES
PREP_ROWS = 256
IN_COLS_PAD = C_RA + PREP_ROWS

VMEM_LIMIT = 56 * 1024 * 1024


def _dot(a, b):
    return jnp.dot(a, b, preferred_element_type=f32)


def _dot_nt(a, b):
    return lax.dot_general(a, b, (((1,), (1,)), ((), ())), preferred_element_type=f32)


def _dot_tn(a, b):
    return lax.dot_general(a, b, (((0,), (0,)), ((), ())), preferred_element_type=f32)


def _rms(x, w):
    return x * lax.rsqrt(jnp.mean(x * x, axis=-1, keepdims=True) + EPS) * w


def _split_hi_lo(x):
    hi = x.astype(bf16)
    lo = (x - hi.astype(f32)).astype(bf16)
    return hi, lo


def _log_decay(ra, wup_ref, bg_ref):
    xg = _dot(ra.astype(bf16), wup_ref[...]) + bg_ref[...]
    return jax.nn.log_sigmoid(xg) * (1.0 / GLA_TAU)


def _chunk_cumsum(la, chunk):
    n = la.shape[0]
    r = lax.broadcasted_iota(jnp.int32, (n, n), 0)
    c = lax.broadcasted_iota(jnp.int32, (n, n), 1)
    tri = jnp.where((c <= r) & ((r // chunk) == (c // chunk)), 1.0, 0.0).astype(bf16)
    hi, lo = _split_hi_lo(la)
    return _dot(tri, hi) + _dot(tri, lo)


def _even_head_lanes(shape):
    lane = lax.broadcasted_iota(jnp.int32, shape, len(shape) - 1)
    return (lane % LANES) < GLA_DK


def _gla_out_norm(o, gn_ref, ga):
    outs = []
    for h in range(GLA_HEADS):
        oh = o[:, h * GLA_DV:(h + 1) * GLA_DV]
        outs.append(_rms(oh, gn_ref[...]))
    on = jnp.concatenate(outs, axis=1)
    return on * (ga * jax.nn.sigmoid(ga))


def _mix_tail(x, oa, ob, gate_a, gate_b, wba_ref, wbb_ref, wout_ref, npost_ref):
    merged = (jax.nn.sigmoid(gate_a) * _dot(oa.astype(bf16), wba_ref[...])
              + jax.nn.sigmoid(gate_b) * _dot(ob.astype(bf16), wbb_ref[...]))
    m = _dot(merged.astype(bf16), wout_ref[...])
    return x + _rms(m, npost_ref[...])


def _alibi_slope(head):
    return 2.0 ** (-(8.0 / SWA_HEADS) * (head + 1))


def _kv_variants(x):
    lo = _even_head_lanes(x.shape)
    xr = pltpu.roll(x, SWA_HD, 1)
    zero = jnp.zeros_like(x)
    h0_lo = jnp.where(lo, x, zero).astype(bf16)
    h1_hi = jnp.where(lo, zero, x).astype(bf16)
    h1_lo = jnp.where(lo, xr, zero).astype(bf16)
    h0_hi = jnp.where(lo, zero, xr).astype(bf16)
    return (h0_lo, h0_hi), (h1_lo, h1_hi)


def _softmax_sink(s, sink):
    m = jnp.maximum(jnp.max(s, axis=-1, keepdims=True), sink)
    p = jnp.exp(s - m)
    denom = jnp.sum(p, axis=-1, keepdims=True) + jnp.exp(sink - m)
    return p, 1.0 / denom


def _mix_prompt_body(sink_ref, x_ref, npre_ref, win_ref, wup_ref, bg_ref, gn_ref,
                     wba_ref, wbb_ref, wout_ref, npost_ref,
                     y_ref, st_out_ref, k_out_ref, v_out_ref,
                     st_scr, kcat_scr, vcat_scr, oa_scr, ob_scr, gate_scr, *, T):
    i = pl.program_id(0)
    W = WINDOW
    C = GLA_CHUNK

    @pl.when(i == 0)
    def _():
        st_scr[...] = jnp.zeros_like(st_scr)
        kcat_scr[0:W, :] = jnp.zeros((W, SWA_KV), f32)
        vcat_scr[0:W, :] = jnp.zeros((W, SWA_KV), f32)

    @pl.when(i > 0)
    def _():
        kcat_scr[0:W, :] = kcat_scr[T:T + W, :]
        vcat_scr[0:W, :] = vcat_scr[T:T + W, :]

    x = x_ref[...]
    h = _rms(x, npre_ref[...]).astype(bf16)

    def proj(c0, n):
        return _dot(h, win_ref[:, c0:c0 + n])


    qa = proj(C_QA, GLA_K)
    ka = proj(C_KA, GLA_K)
    va_b = proj(C_VA, GLA_V).astype(bf16)
    la = _log_decay(proj(C_RA, RA_PAD), wup_ref, bg_ref)
    b = _chunk_cumsum(la, C)
    kcat_scr[W:W + T, :] = proj(C_KB, SWA_KV)
    vcat_scr[W:W + T, :] = proj(C_VB, SWA_KV)
    qb = (proj(C_QB, SWA_Q) * (SWA_HD ** -0.5)).astype(bf16)

    qe = qa * jnp.exp(b) * (GLA_DK ** -0.5)
    ke = (ka * jnp.exp(-b)).astype(bf16)
    even = _even_head_lanes((T, GLA_K))
    qe_even = jnp.where(even, qe, 0.0).astype(bf16)
    qe_odd = jnp.where(even, 0.0, qe).astype(bf16)
    k_var = _kv_variants(kcat_scr[...])
    v_var = _kv_variants(vcat_scr[...])

    r2 = lax.broadcasted_iota(jnp.int32, (2 * C, 2 * C), 0)
    c2 = lax.broadcasted_iota(jnp.int32, (2 * C, 2 * C), 1)
    pair_causal = ((r2 // C) == (c2 // C)) & ((c2 % C) <= (r2 % C))
    even_c = _even_head_lanes((C, LANES))
    st = [st_scr[:, p * LANES:(p + 1) * LANES] for p in range(GLA_HEADS // 2)]

    def gla_scores(c):
        rows = slice(c * C, (c + 1) * C)
        out = []
        for p in range(GLA_HEADS // 2):
            lanes = slice(p * LANES, (p + 1) * LANES)
            q2 = jnp.concatenate([qe_even[rows, lanes], qe_odd[rows, lanes]], axis=0)
            ke_p = ke[rows, lanes]
            rhs = jnp.concatenate([ke_p, ke_p, st[p].astype(bf16)], axis=0)
            r = _dot_nt(q2, rhs)
            att = jnp.where(pair_causal, r[:, 0:2 * C], 0.0).astype(bf16)
            out.append((att, r[:, 2 * C:]))
        return out

    def gla_update(c, sc):
        rows = slice(c * C, (c + 1) * C)
        b_c = b[rows]
        bl = b_c[C - 1:C, :]
        kl = ka[rows] * jnp.exp(bl - b_c)
        ebl = jnp.exp(bl)
        for p in range(GLA_HEADS // 2):
            lanes = slice(p * LANES, (p + 1) * LANES)
            att, inter = sc[p]
            v2 = jnp.concatenate(
                [va_b[rows, (2 * p) * GLA_DV:(2 * p + 1) * GLA_DV],
                 va_b[rows, (2 * p + 1) * GLA_DV:(2 * p + 2) * GLA_DV]], axis=0)
            o2 = inter + _dot(att, v2)
            oa_scr[rows, (2 * p) * GLA_DV:(2 * p + 1) * GLA_DV] = o2[0:C]
            oa_scr[rows, (2 * p + 1) * GLA_DV:(2 * p + 2) * GLA_DV] = o2[C:2 * C]
            kl_p = kl[:, lanes]
            kl_stack = jnp.concatenate(
                [jnp.where(even_c, kl_p, 0.0), jnp.where(even_c, 0.0, kl_p)], axis=0).astype(bf16)
            st[p] = st[p] * ebl[:, lanes] + _dot_tn(v2, kl_stack)

    qi = lax.broadcasted_iota(jnp.int32, (W, 2 * W), 0)
    kc = lax.broadcasted_iota(jnp.int32, (W, 2 * W), 1)
    rel = qi + W - kc
    relf = rel.astype(f32)
    in_window = (rel >= 0) & (rel < W)

    def swa_probs(j, kv):
        qrows = slice(j * W, (j + 1) * W)
        band = slice(j * W, j * W + 2 * W)
        if j == 0:
            mask = in_window & ((kc >= W) | (i > 0))
        else:
            mask = in_window
        pairs = (2 * kv, 2 * kv + 1)
        q2 = jnp.concatenate([qb[qrows, p * LANES:(p + 1) * LANES] for p in pairs], axis=0)
        out = []
        for e in range(2):
            s2 = _dot_nt(q2, k_var[kv][e][band])
            probs = []
            for half, p in enumerate(pairs):
                hd = 2 * p + e
                s = s2[half * W:(half + 1) * W]
                s = jnp.where(mask, s - _alibi_slope(hd) * relf, -jnp.inf)
                pr, inv = _softmax_sink(s, sink_ref[hd])
                probs.append((pr * inv).astype(bf16))
            out.append(jnp.concatenate(probs, axis=0))
        return out

    def swa_out(j, kv, probs):
        qrows = slice(j * W, (j + 1) * W)
        band = slice(j * W, j * W + 2 * W)
        o2 = _dot(probs[0], v_var[kv][0][band]) + _dot(probs[1], v_var[kv][1][band])
        for half, p in enumerate((2 * kv, 2 * kv + 1)):
            ob_scr[qrows, p * LANES:(p + 1) * LANES] = o2[half * W:(half + 1) * W]

    n_chunks = T // C
    assert n_chunks == (T // W) * SWA_KV_HEADS
    gw = 2 * D_MODEL // n_chunks
    for idx in range(n_chunks):
        j, kv = idx // SWA_KV_HEADS, idx % SWA_KV_HEADS
        probs = swa_probs(j, kv)
        sc = gla_scores(idx)
        gate_scr[:, idx * gw:(idx + 1) * gw] = proj(C_GTA + idx * gw, gw)
        gla_update(idx, sc)
        swa_out(j, kv, probs)
    for p in range(GLA_HEADS // 2):
        st_scr[:, p * LANES:(p + 1) * LANES] = st[p]

    oa = _gla_out_norm(oa_scr[...], gn_ref, proj(C_GA, GLA_V))
    y_ref[...] = _mix_tail(x, oa, ob_scr[...], gate_scr[:, 0:D_MODEL], gate_scr[:, D_MODEL:2 * D_MODEL],
                           wba_ref, wbb_ref, wout_ref, npost_ref)

    @pl.when(i == pl.num_programs(0) - 1)
    def _():
        st_out_ref[...] = st_scr[...].T
        k_out_ref[...] = kcat_scr[T:T + W, :].T
        v_out_ref[...] = vcat_scr[T:T + W, :].T


def _const_spec(shape):
    nd = len(shape)
    return pl.BlockSpec(shape, lambda i: (0,) * nd, pipeline_mode=pl.Buffered(1))


def _mix_prompt(x, sinks, npre, win, wup, bg, gn, wba, wbb, wout, npost, *, T):
    L = x.shape[0]
    nb = L // T
    body = functools.partial(_mix_prompt_body, T=T)
    return pl.pallas_call(
        body,
        grid=(nb,),
        in_specs=[
            pl.BlockSpec(memory_space=pltpu.SMEM),
            pl.BlockSpec((T, D_MODEL), lambda i: (i, 0)),
            _const_spec(npre.shape), _const_spec(win.shape), _const_spec(wup.shape),
            _const_spec(bg.shape), _const_spec(gn.shape), _const_spec(wba.shape),
            _const_spec(wbb.shape), _const_spec(wout.shape), _const_spec(npost.shape),
        ],
        out_specs=[
            pl.BlockSpec((T, D_MODEL), lambda i: (i, 0)),
            pl.BlockSpec((GLA_K, GLA_DV), lambda i: (0, 0)),
            pl.BlockSpec((WINDOW, SWA_KV), lambda i: (0, 0)),
            pl.BlockSpec((WINDOW, SWA_KV), lambda i: (0, 0)),
        ],
        out_shape=[
            jax.ShapeDtypeStruct((L, D_MODEL), f32),
            jax.ShapeDtypeStruct((GLA_K, GLA_DV), f32),
            jax.ShapeDtypeStruct((WINDOW, SWA_KV), f32),
            jax.ShapeDtypeStruct((WINDOW, SWA_KV), f32),
        ],
        scratch_shapes=[
            pltpu.VMEM((GLA_DV, GLA_K), f32),
            pltpu.VMEM((T + WINDOW, SWA_KV), f32),
            pltpu.VMEM((T + WINDOW, SWA_KV), f32),
            pltpu.VMEM((T, GLA_V), f32),
            pltpu.VMEM((T, SWA_Q), f32),
            pltpu.VMEM((T, 2 * D_MODEL), f32),
        ],
        compiler_params=pltpu.CompilerParams(
            dimension_semantics=("arbitrary",), vmem_limit_bytes=VMEM_LIMIT),
        name="mix_prompt",
    )(sinks, x, npre, win, wup, bg, gn, wba, wbb, wout, npost)


def _pre_sample_body(x_ref, npre_ref, win_ref, wup_ref, bg_ref,
                     qe_ref, kl_ref, e3_ref, oin_ref, va_ref, ga_ref, qb_ref, kb_ref, vb_ref,
                     gta_ref, gtb_ref, x_scr, *, NB, NT):
    for t in range(NT):
        x_scr[t * NB:(t + 1) * NB, :] = x_ref[:, t, :]
    h = _rms(x_scr[...], npre_ref[...]).astype(bf16)

    def proj(c0, n):
        return _dot(h, win_ref[:, c0:c0 + n])

    def blk(val, t):
        return val[t * NB:(t + 1) * NB, :]

    ga_ref[...] = proj(C_GA, GLA_V)
    qb_ref[...] = proj(C_QB, SWA_Q) * (SWA_HD ** -0.5)
    kb_ref[...] = proj(C_KB, SWA_KV)
    vb_ref[...] = proj(C_VB, SWA_KV)
    gta_ref[...] = proj(C_GTA, D_MODEL)
    gtb_ref[...] = proj(C_GTB, D_MODEL)
    va = proj(C_VA, GLA_V)
    va_ref[...] = va

    qa = proj(C_QA, GLA_K) * (GLA_DK ** -0.5)
    ka = proj(C_KA, GLA_K)
    la = _log_decay(proj(C_RA, RA_PAD), wup_ref, bg_ref)
    b = [blk(la, 0)]
    for t in range(1, NT):
        b.append(b[-1] + blk(la, t))
    e3_ref[...] = jnp.exp(b[NT - 1])
    for t in range(NT):
        qe_ref[t * NB:(t + 1) * NB, :] = blk(qa, t) * jnp.exp(b[t])
        kl_ref[t * NB:(t + 1) * NB, :] = blk(ka, t) * jnp.exp(b[NT - 1] - b[t])
    pairs = [(t, j) for t in range(NT) for j in range(t + 1)]
    prods = [(blk(qa, t) * blk(ka, j) * jnp.exp(b[t] - b[j])).astype(bf16) for t, j in pairs]
    r = lax.broadcasted_iota(jnp.int32, (GLA_K, GLA_V), 0)
    c = lax.broadcasted_iota(jnp.int32, (GLA_K, GLA_V), 1)
    expand = jnp.where((r // GLA_DK) == (c // GLA_DV), 1.0, 0.0).astype(bf16)
    att = _dot(jnp.concatenate(prods, axis=0), expand)
    for t in range(NT):
        acc = None
        for idx, (tt, j) in enumerate(pairs):
            if tt != t:
                continue
            term = att[idx * NB:(idx + 1) * NB, :] * blk(va, j)
            acc = term if acc is None else acc + term
        oin_ref[t * NB:(t + 1) * NB, :] = acc


def _pre_sample(xs, npre, win, wup, bg):
    NB, NT, _ = xs.shape
    body = functools.partial(_pre_sample_body, NB=NB, NT=NT)
    widths = (GLA_K, GLA_K, None, GLA_V, GLA_V, GLA_V, SWA_Q, SWA_KV, SWA_KV, D_MODEL, D_MODEL)
    out_shape = [jax.ShapeDtypeStruct((NB, GLA_K) if w is None else (NT * NB, w), f32) for w in widths]
    return pl.pallas_call(
        body,
        out_shape=out_shape,
        scratch_shapes=[pltpu.VMEM((NB * NT, D_MODEL), f32)],
        compiler_params=pltpu.CompilerParams(vmem_limit_bytes=VMEM_LIMIT),
        name="pre_sample",
    )(xs, npre, win, wup, bg)


def _state_sample_body(sink_ref, qe_ref, kl_ref, e3_ref, oin_ref, va_ref, qb_ref, kb_ref, vb_ref,
                       s0_ref, kt_ref, vt_ref,
                       oa_ref, ob_ref, s1_ref, kt1_ref, vt1_ref, *, BB, NT):
    W = WINDOW
    SK = 2 * W
    HT = GLA_HEADS * NT
    HALF = SWA_HD
    hr = lax.broadcasted_iota(jnp.int32, (HT, GLA_K), 0) // NT
    hc = lax.broadcasted_iota(jnp.int32, (HT, GLA_K), 1) // GLA_DK
    own_head = hr == hc
    ones_rows = jnp.ones((16, GLA_DV), bf16)
    zero_rows = jnp.zeros((16, GLA_DV), bf16)
    zero_ht = jnp.zeros((HT, GLA_DV), bf16)
    G2 = 2 * NT
    row = lax.broadcasted_iota(jnp.int32, (G2, SK), 0)
    col = lax.broadcasted_iota(jnp.int32, (G2, SK), 1)
    rel = (row % NT) + W - col
    relf = rel.astype(f32)
    smask = (rel >= 0) & (rel < W)
    first_pair = lax.broadcasted_iota(jnp.int32, (G2, 1), 0) < NT
    pad_rows = jnp.zeros((8 - NT, SWA_KV), f32)
    pad_lanes = jnp.zeros((SWA_KV, SK - W - 8), f32)
    zero_half = jnp.zeros((HALF, SK), bf16)

    def head_variants(cat_t, kv):
        blk = cat_t[kv * HALF:(kv + 1) * HALF]
        return (jnp.concatenate([blk, zero_half], axis=0), jnp.concatenate([zero_half, blk], axis=0))

    pending = []
    for bi in range(BB):
        s0 = s0_ref[bi]
        q4 = qe_ref[:, bi, :]
        qm = jnp.where(own_head, jnp.concatenate([q4] * GLA_HEADS, axis=0), 0.0).astype(bf16)
        o_inter = _dot(qm, s0.astype(bf16))
        for hd in range(GLA_HEADS):
            lanes = slice(hd * GLA_DV, (hd + 1) * GLA_DV)
            oa_ref[:, bi, lanes] = o_inter[hd * NT:(hd + 1) * NT, :] + oin_ref[:, bi, lanes]
        k4 = kl_ref[:, bi, :]
        km = jnp.where(own_head, jnp.concatenate([k4] * GLA_HEADS, axis=0), 0.0).astype(bf16)
        e = e3_ref[bi:bi + 1, :]
        e_hi = e.astype(bf16)
        r1 = e - e_hi.astype(f32)
        e_mid = r1.astype(bf16)
        e_lo = (r1 - e_mid.astype(f32)).astype(bf16)
        e_rows = jnp.concatenate([e_hi, e_mid, e_lo, jnp.zeros((13, GLA_K), bf16)], axis=0)
        lhs = jnp.concatenate([km, e_rows], axis=0)
        v4 = va_ref[:, bi, :].astype(bf16)
        vrep = jnp.concatenate([v4[:, hd * GLA_DV:(hd + 1) * GLA_DV] for hd in range(GLA_HEADS)], axis=0)
        rhs = jnp.concatenate([jnp.concatenate([vrep, zero_ht], axis=1),
                               jnp.concatenate([zero_rows, ones_rows], axis=1)], axis=0)
        res = _dot_tn(lhs, rhs)
        s1_ref[bi] = res[:, GLA_DV:] * s0 + res[:, :GLA_DV]

        kt = kt_ref[bi]
        vt = vt_ref[bi]
        knew_t = jnp.concatenate([kb_ref[:, bi, :], pad_rows], axis=0).T
        vnew_t = jnp.concatenate([vb_ref[:, bi, :], pad_rows], axis=0).T
        kt1_ref[bi] = jnp.concatenate([kt[:, NT:], knew_t[:, 0:NT]], axis=1)
        vt1_ref[bi] = jnp.concatenate([vt[:, NT:], vnew_t[:, 0:NT]], axis=1)
        kcat = jnp.concatenate([kt, knew_t, pad_lanes], axis=1).astype(bf16)
        vcat = jnp.concatenate([vt, vnew_t, pad_lanes], axis=1).astype(bf16)
        q4b = qb_ref[:, bi, :].astype(bf16)
        for kv in range(SWA_KV_HEADS):
            p0 = 2 * kv
            q8 = jnp.concatenate([q4b[:, p0 * LANES:(p0 + 1) * LANES],
                                  q4b[:, (p0 + 1) * LANES:(p0 + 2) * LANES]], axis=0)
            scores = [_dot(q8, kvar) for kvar in head_variants(kcat, kv)]
            pending.append((bi, kv, scores, head_variants(vcat, kv)))

    for bi, kv, scores, v_vars in pending:
        p0 = 2 * kv
        o8_t = None
        for e_ in range(2):
            h_first = 2 * p0 + e_
            h_second = 2 * (p0 + 1) + e_
            slope = jnp.where(first_pair, _alibi_slope(h_first), _alibi_slope(h_second))
            sink = jnp.where(first_pair, sink_ref[h_first], sink_ref[h_second])
            s = jnp.where(smask, scores[e_] - slope * relf, -jnp.inf)
            pr, inv = _softmax_sink(s, sink)
            o_t = _dot_nt(v_vars[e_], (pr * inv).astype(bf16))
            o8_t = o_t if o8_t is None else o8_t + o_t
        o8 = o8_t.T
        ob_ref[:, bi, p0 * LANES:(p0 + 1) * LANES] = o8[0:NT, :]
        ob_ref[:, bi, (p0 + 1) * LANES:(p0 + 2) * LANES] = o8[NT:2 * NT, :]


def _state_sample(sinks, qe, kl, e3, oin, va, qb, kb, vb, s0, kt, vt, *, NT, BB):
    NBS = s0.shape[0]
    body = functools.partial(_state_sample_body, BB=BB, NT=NT)

    def tm(a):
        return a.reshape(NT, NBS, a.shape[-1])

    def rows(n):
        return pl.BlockSpec((NT, BB, n), lambda i: (0, i, 0))

    def per_seq(shape):
        return pl.BlockSpec((BB,) + shape, lambda i: (i, 0, 0))

    oa, ob, s1, kt1, vt1 = pl.pallas_call(
        body,
        grid=(NBS // BB,),
        in_specs=[
            pl.BlockSpec(memory_space=pltpu.SMEM),
            rows(GLA_K), rows(GLA_K), pl.BlockSpec((BB, GLA_K), lambda i: (i, 0)),
            rows(GLA_V), rows(GLA_V), rows(SWA_Q), rows(SWA_KV), rows(SWA_KV),
            per_seq((GLA_K, GLA_DV)), per_seq((SWA_KV, WINDOW)), per_seq((SWA_KV, WINDOW)),
        ],
        out_specs=[
            rows(GLA_V), rows(SWA_Q),
            per_seq((GLA_K, GLA_DV)), per_seq((SWA_KV, WINDOW)), per_seq((SWA_KV, WINDOW)),
        ],
        out_shape=[
            jax.ShapeDtypeStruct((NT, NBS, GLA_V), f32),
            jax.ShapeDtypeStruct((NT, NBS, SWA_Q), f32),
            jax.ShapeDtypeStruct((NBS, GLA_K, GLA_DV), f32),
            jax.ShapeDtypeStruct((NBS, SWA_KV, WINDOW), f32),
            jax.ShapeDtypeStruct((NBS, SWA_KV, WINDOW), f32),
        ],
        compiler_params=pltpu.CompilerParams(
            dimension_semantics=("arbitrary",), vmem_limit_bytes=VMEM_LIMIT),
        name="state_sample",
    )(sinks, tm(qe), tm(kl), e3, tm(oin), tm(va), tm(qb), tm(kb), tm(vb), s0, kt, vt)
    return oa.reshape(NT * NBS, GLA_V), ob.reshape(NT * NBS, SWA_Q), s1, kt1, vt1


def _post_sample_body(x_ref, oa_ref, ga_ref, ob_ref, gta_ref, gtb_ref, gn_ref,
                      wba_ref, wbb_ref, wout_ref, npost_ref, y_ref, x_scr, *, NB, NT):
    for t in range(NT):
        x_scr[t * NB:(t + 1) * NB, :] = x_ref[:, t, :]
    oa = _gla_out_norm(oa_ref[...], gn_ref, ga_ref[...])
    y_ref[...] = _mix_tail(x_scr[...], oa, ob_ref[...], gta_ref[...], gtb_ref[...],
                           wba_ref, wbb_ref, wout_ref, npost_ref)


def _post_sample(xs, oa, ga, ob, gta, gtb, gn, wba, wbb, wout, npost):
    NB, NT, _ = xs.shape
    return pl.pallas_call(
        functools.partial(_post_sample_body, NB=NB, NT=NT),
        out_shape=jax.ShapeDtypeStruct((NT * NB, D_MODEL), f32),
        scratch_shapes=[pltpu.VMEM((NT * NB, D_MODEL), f32)],
        compiler_params=pltpu.CompilerParams(vmem_limit_bytes=VMEM_LIMIT),
        name="post_sample",
    )(xs, oa, ga, ob, gta, gtb, gn, wba, wbb, wout, npost)


FFN_COLS = 256


def _ffn_columns(h, wffn_ref, cw_ref, cb_ref, up_scr, y_scr, *, T, base, shift):
    for c0 in range(0, D_FF, FFN_COLS):
        cols = slice(c0, c0 + FFN_COLS)
        u = _dot(h, wffn_ref[:, c0:c0 + FFN_COLS])
        g = _dot(h, wffn_ref[:, D_FF + c0:D_FF + c0 + FFN_COLS])
        up_scr[base:base + T, cols] = u
        u1 = up_scr[base - shift:base - shift + T, cols]
        u2 = up_scr[base - 2 * shift:base - 2 * shift + T, cols]
        cv = (cb_ref[:, cols] + cw_ref[2:3, cols] * u + cw_ref[1:2, cols] * u1 + cw_ref[0:1, cols] * u2)
        y_scr[:, cols] = (jax.nn.gelu(cv, approximate=True) * g).astype(bf16)


def _ffn_prompt_body(x_ref, npre_ref, wffn_ref, cw_ref, cb_ref, wo_ref, npost_ref,
                     y_ref, conv_out_ref, up_scr, y_scr, *, T):
    i = pl.program_id(0)
    base = 8

    @pl.when(i == 0)
    def _():
        up_scr[0:base, :] = jnp.zeros((base, D_FF), f32)

    @pl.when(i > 0)
    def _():
        up_scr[0:base, :] = up_scr[T:T + base, :]

    x = x_ref[...]
    h = _rms(x, npre_ref[...]).astype(bf16)
    _ffn_columns(h, wffn_ref, cw_ref, cb_ref, up_scr, y_scr, T=T, base=base, shift=1)
    f = _dot(y_scr[...], wo_ref[...])
    y_ref[...] = x + _rms(f, npost_ref[...])

    @pl.when(i == pl.num_programs(0) - 1)
    def _():
        conv_out_ref[...] = up_scr[T:T + base, :]


def _ffn_prompt(x, npre, wffn, cw, cb, wo, npost, *, T):
    L = x.shape[0]
    body = functools.partial(_ffn_prompt_body, T=T)
    return pl.pallas_call(
        body,
        grid=(L // T,),
        in_specs=[
            pl.BlockSpec((T, D_MODEL), lambda i: (i, 0)),
            _const_spec(npre.shape), _const_spec(wffn.shape), _const_spec(cw.shape),
            _const_spec(cb.shape), _const_spec(wo.shape), _const_spec(npost.shape),
        ],
        out_specs=[
            pl.BlockSpec((T, D_MODEL), lambda i: (i, 0)),
            pl.BlockSpec((8, D_FF), lambda i: (0, 0)),
        ],
        out_shape=[
            jax.ShapeDtypeStruct((L, D_MODEL), f32),
            jax.ShapeDtypeStruct((8, D_FF), f32),
        ],
        scratch_shapes=[
            pltpu.VMEM((T + 8, D_FF), f32),
            pltpu.VMEM((T, D_FF), bf16),
        ],
        compiler_params=pltpu.CompilerParams(
            dimension_semantics=("arbitrary",), vmem_limit_bytes=VMEM_LIMIT),
        name="ffn_prompt",
    )(x, npre, wffn, cw, cb, wo, npost)


def _ffn_sample_body(x_ref, cst_ref, npre_ref, wffn_ref, cw_ref, cb_ref, wo_ref, npost_ref,
                     y_ref, conv_out_ref, up_scr, y_scr, *, NB, NT):
    T = NB * NT
    for t in range(CONV_W - 1):
        up_scr[t * NB:(t + 1) * NB, :] = cst_ref[:, t, :]
    base = (CONV_W - 1) * NB
    x = x_ref[...]
    h = _rms(x, npre_ref[...]).astype(bf16)
    _ffn_columns(h, wffn_ref, cw_ref, cb_ref, up_scr, y_scr, T=T, base=base, shift=NB)
    f = _dot(y_scr[...], wo_ref[...])
    y = x + _rms(f, npost_ref[...])
    for t in range(NT):
        y_ref[:, t, :] = y[t * NB:(t + 1) * NB, :]
    for t in range(CONV_W - 1):
        conv_out_ref[:, t, :] = up_scr[T + t * NB:T + (t + 1) * NB, :]


def _ffn_sample(x, cst, npre, wffn, cw, cb, wo, npost):
    NB = cst.shape[0]
    T = x.shape[0]
    NT = T // NB
    body = functools.partial(_ffn_sample_body, NB=NB, NT=NT)
    return pl.pallas_call(
        body,
        out_shape=[
            jax.ShapeDtypeStruct((NB, NT, D_MODEL), f32),
            jax.ShapeDtypeStruct((NB, CONV_W - 1, D_FF), f32),
        ],
        scratch_shapes=[
            pltpu.VMEM((T + (CONV_W - 1) * NB, D_FF), f32),
            pltpu.VMEM((T, D_FF), bf16),
        ],
        compiler_params=pltpu.CompilerParams(vmem_limit_bytes=VMEM_LIMIT),
        name="ffn_sample",
    )(x, cst, npre, wffn, cw, cb, wo, npost)


def _prep_w_in_body(wt_ref, o_ref, *, n_plain):
    j = pl.program_id(0)
    x = wt_ref[...]
    r = lax.broadcasted_iota(jnp.int32, x.shape, 0)
    x = jnp.where((j < n_plain) | (r < GLA_RANK), x, 0.0)
    o_ref[...] = x.T.astype(bf16)


def _prep_w_in(w_in):
    d_in, n_cols = w_in.shape
    head = C_GA
    tail_src = head + GLA_RANK
    n_head = head // PREP_ROWS
    n_tail = (n_cols - tail_src) // PREP_ROWS
    assert head % PREP_ROWS == 0 and (n_cols - tail_src) % PREP_ROWS == 0
    n_plain = n_head + n_tail
    assert C_RA == n_plain * PREP_ROWS

    def row_off(j):
        off = jnp.where(j < n_head, j * PREP_ROWS,
                        jnp.where(j < n_plain, tail_src + (j - n_head) * PREP_ROWS, head))
        return pl.multiple_of(off, 8)

    return pl.pallas_call(
        functools.partial(_prep_w_in_body, n_plain=n_plain),
        grid=(n_plain + 1,),
        in_specs=[pl.BlockSpec((pl.Element(PREP_ROWS), pl.Element(d_in)), lambda j: (row_off(j), 0))],
        out_specs=pl.BlockSpec((d_in, PREP_ROWS), lambda j: (0, j)),
        out_shape=jax.ShapeDtypeStruct((d_in, (n_plain + 1) * PREP_ROWS), bf16),
        compiler_params=pltpu.CompilerParams(dimension_semantics=("arbitrary",)),
        name="prep_w_in",
    )(jnp.swapaxes(w_in, 0, 1))


def kernel(x_prompt, x_sample, state_gla, cache_swa_k, cache_swa_v, state_ffn_conv, norm_mix_pre, norm_mix_post, w_in, w_gate_up, b_gate, gla_norm, sinks, w_branch_a, w_branch_b, w_out, norm_ffn_pre, norm_ffn_post, w_ffn_in, conv_w, conv_b, w_ffn_out):
    depth = w_in.shape[0]
    assert depth == 1
    l = 0
    B, L, _ = x_prompt.shape
    assert B == 1
    NBS, NT, _ = x_sample.shape

    win = _prep_w_in(w_in[l])
    wup = jnp.zeros((RA_PAD, GLA_K), f32).at[:GLA_RANK].set(w_gate_up[l]).astype(bf16)
    bg = b_gate[l].reshape(1, GLA_K)
    gn = gla_norm[l].reshape(1, GLA_DV)
    npre = norm_mix_pre[l].reshape(1, D_MODEL)
    npost = norm_mix_post[l].reshape(1, D_MODEL)
    wba = w_branch_a[l].astype(bf16)
    wbb = w_branch_b[l].astype(bf16)
    wout = w_out[l].astype(bf16)
    fpre = norm_ffn_pre[l].reshape(1, D_MODEL)
    fpost = norm_ffn_post[l].reshape(1, D_MODEL)
    wffn = w_ffn_in[l].astype(bf16)
    cw = conv_w[l]
    cb = conv_b[l].reshape(1, D_FF)
    wo = w_ffn_out[l].astype(bf16)
    sk = sinks[l]

    x1, st_p, k_p, v_p = _mix_prompt(x_prompt[0], sk, npre, win, wup, bg, gn, wba, wbb, wout, npost, T=256)
    y_p, conv_p = _ffn_prompt(x1, fpre, wffn, cw, cb, wo, fpost, T=512)

    y_prompt = y_p[None]
    gla_state_prompt = st_p.reshape(1, 1, GLA_HEADS, GLA_DK, GLA_DV)
    swa_k_prompt = jnp.transpose(k_p.reshape(SWA_KV_HEADS, SWA_HD, WINDOW), (2, 0, 1))[None, None]
    swa_v_prompt = jnp.transpose(v_p.reshape(SWA_KV_HEADS, SWA_HD, WINDOW), (2, 0, 1))[None, None]
    conv_prompt = conv_p[8 - (CONV_W - 1):].reshape(1, 1, CONV_W - 1, D_FF)

    qe, kl, e3, oin, va, ga, qb, kb, vb, gta, gtb = _pre_sample(x_sample, npre, win, wup, bg)
    kt = jnp.transpose(cache_swa_k[l], (0, 2, 3, 1)).reshape(NBS, SWA_KV, WINDOW)
    vt = jnp.transpose(cache_swa_v[l], (0, 2, 3, 1)).reshape(NBS, SWA_KV, WINDOW)
    oa_raw, ob, s1, kt1, vt1 = _state_sample(
        sk, qe, kl, e3, oin, va, qb, kb, vb, state_gla[l].reshape(NBS, GLA_K, GLA_DV), kt, vt, NT=NT, BB=8)
    x1s = _post_sample(x_sample, oa_raw, ga, ob, gta, gtb, gn, wba, wbb, wout, npost)
    y_sample, conv_s = _ffn_sample(x1s, state_ffn_conv[l], fpre, wffn, cw, cb, wo, fpost)

    def cache_out(t):
        return jnp.transpose(t.reshape(NBS, SWA_KV_HEADS, SWA_HD, WINDOW), (0, 3, 1, 2))[None]

    gla_state_sample = s1.reshape(1, NBS, GLA_HEADS, GLA_DK, GLA_DV)
    swa_k_sample = cache_out(kt1)
    swa_v_sample = cache_out(vt1)
    conv_sample = conv_s[None]
    return (y_prompt, y_sample, gla_state_prompt, gla_state_sample, swa_k_prompt, swa_v_prompt,
            swa_k_sample, swa_v_sample, conv_prompt, conv_sample)
```

```python
import functools

import jax
import jax.numpy as jnp
from jax import lax
from jax.experimental import pallas as pl
from jax.experimental.pallas import tpu as pltpu

f32 = jnp.float32
bf16 = jnp.bfloat16

D_MODEL = 1024
GLA_HEADS = 4
GLA_DK = 64
GLA_DV = 128
GLA_RANK = 16
GLA_TAU = 16.0
GLA_CHUNK = 64
GLA_SAFE_DECAY = 60.0
SWA_HEADS = 8
SWA_KV_HEADS = 2
SWA_HD = 64
WINDOW = 128
D_FF = 2816
CONV_W = 3
EPS = 1e-6
GLA_K = GLA_HEADS * GLA_DK
GLA_V = GLA_HEADS * GLA_DV
SWA_Q = SWA_HEADS * SWA_HD
SWA_KV = SWA_KV_HEADS * SWA_HD
LANES = 128

C_QA = 0
C_KA = C_QA + GLA_K
C_VA = C_KA + GLA_K
C_GA = C_VA + GLA_V
C_QB = C_GA + GLA_V
C_KB = C_QB + SWA_Q
C_VB = C_KB + SWA_KV
C_GTA = C_VB + SWA_KV
C_GTB = C_GTA + D_MODEL
C_RA = C_GTB + D_MODEL
RA_PAD = LANES
PREP_ROWS = 256
IN_COLS_PAD = C_RA + PREP_ROWS

VMEM_LIMIT = 56 * 1024 * 1024


def _dot(a, b):
    return jnp.dot(a, b, preferred_element_type=f32)


def _dot_nt(a, b):
    return lax.dot_general(a, b, (((1,), (1,)), ((), ())), preferred_element_type=f32)


def _dot_tn(a, b):
    return lax.dot_general(a, b, (((0,), (0,)), ((), ())), preferred_element_type=f32)


def _rms(x, w):
    return x * lax.rsqrt(jnp.mean(x * x, axis=-1, keepdims=True) + EPS) * w


def _split_hi_lo(x):
    hi = x.astype(bf16)
    lo = (x - hi.astype(f32)).astype(bf16)
    return hi, lo


def _log_decay(ra, wup_ref, bg_ref):
    xg = _dot(ra.astype(bf16), wup_ref[...]) + bg_ref[...]
    return jax.nn.log_sigmoid(xg) * (1.0 / GLA_TAU)


def _chunk_cumsum(la, chunk):
    n = la.shape[0]
    r = lax.broadcasted_iota(jnp.int32, (n, n), 0)
    c = lax.broadcasted_iota(jnp.int32, (n, n), 1)
    tri = jnp.where((c <= r) & ((r // chunk) == (c // chunk)), 1.0, 0.0).astype(bf16)
    hi, lo = _split_hi_lo(la)
    return _dot(tri, hi) + _dot(tri, lo)


def _even_head_lanes(shape):
    lane = lax.broadcasted_iota(jnp.int32, shape, len(shape) - 1)
    return (lane % LANES) < GLA_DK


def _gla_out_norm(o, gn_ref, ga):
    outs = []
    for h in range(GLA_HEADS):
        oh = o[:, h * GLA_DV:(h + 1) * GLA_DV]
        outs.append(_rms(oh, gn_ref[...]))
    on = jnp.concatenate(outs, axis=1)
    return on * (ga * jax.nn.sigmoid(ga))


def _mix_tail(x, oa, ob, gate_a, gate_b, wba_ref, wbb_ref, wout_ref, npost_ref):
    merged = (jax.nn.sigmoid(gate_a) * _dot(oa.astype(bf16), wba_ref[...])
              + jax.nn.sigmoid(gate_b) * _dot(ob.astype(bf16), wbb_ref[...]))
    m = _dot(merged.astype(bf16), wout_ref[...])
    return x + _rms(m, npost_ref[...])


def _alibi_slope(head):
    return 2.0 ** (-(8.0 / SWA_HEADS) * (head + 1))


def _kv_variants(x):
    lo = _even_head_lanes(x.shape)
    xr = pltpu.roll(x, SWA_HD, 1)
    zero = jnp.zeros_like(x)
    h0_lo = jnp.where(lo, x, zero).astype(bf16)
    h1_hi = jnp.where(lo, zero, x).astype(bf16)
    h1_lo = jnp.where(lo, xr, zero).astype(bf16)
    h0_hi = jnp.where(lo, zero, xr).astype(bf16)
    return (h0_lo, h0_hi), (h1_lo, h1_hi)


def _softmax_sink(s, sink):
    m = jnp.maximum(jnp.max(s, axis=-1, keepdims=True), sink)
    p = jnp.exp(s - m)
    denom = jnp.sum(p, axis=-1, keepdims=True) + jnp.exp(sink - m)
    return p, 1.0 / denom


def _mix_prompt_body(sink_ref, x_ref, npre_ref, win_ref, wup_ref, bg_ref, gn_ref,
                     wba_ref, wbb_ref, wout_ref, npost_ref,
                     y_ref, st_out_ref, k_out_ref, v_out_ref,
                     st_scr, kcat_scr, vcat_scr, oa_scr, ob_scr, gate_scr, inter_scr, *, T):
    i = pl.program_id(0)
    W = WINDOW
    C = GLA_CHUNK

    @pl.when(i == 0)
    def _():
        st_scr[...] = jnp.zeros_like(st_scr)
        kcat_scr[0:W, :] = jnp.zeros((W, SWA_KV), f32)
        vcat_scr[0:W, :] = jnp.zeros((W, SWA_KV), f32)

    @pl.when(i > 0)
    def _():
        kcat_scr[0:W, :] = kcat_scr[T:T + W, :]
        vcat_scr[0:W, :] = vcat_scr[T:T + W, :]

    x = x_ref[...]
    h = _rms(x, npre_ref[...]).astype(bf16)

    def proj(c0, n):
        return _dot(h, win_ref[:, c0:c0 + n])


    qa = proj(C_QA, GLA_K)
    ka = proj(C_KA, GLA_K)
    va_b = proj(C_VA, GLA_V).astype(bf16)
    la = _log_decay(proj(C_RA, RA_PAD), wup_ref, bg_ref)
    b = _chunk_cumsum(la, C)
    decay_floor = jnp.min(b)
    kcat_scr[W:W + T, :] = proj(C_KB, SWA_KV)
    vcat_scr[W:W + T, :] = proj(C_VB, SWA_KV)
    qb = (proj(C_QB, SWA_Q) * (SWA_HD ** -0.5)).astype(bf16)

    qe = qa * jnp.exp(b) * (GLA_DK ** -0.5)
    ke = (ka * jnp.exp(-b)).astype(bf16)
    even = _even_head_lanes((T, GLA_K))
    qe_even = jnp.where(even, qe, 0.0).astype(bf16)
    qe_odd = jnp.where(even, 0.0, qe).astype(bf16)
    k_var = _kv_variants(kcat_scr[...])
    v_var = _kv_variants(vcat_scr[...])

    r2 = lax.broadcasted_iota(jnp.int32, (2 * C, 2 * C), 0)
    c2 = lax.broadcasted_iota(jnp.int32, (2 * C, 2 * C), 1)
    pair_causal = ((r2 // C) == (c2 // C)) & ((c2 % C) <= (r2 % C))
    even_c = _even_head_lanes((C, LANES))
    st = [st_scr[:, p * LANES:(p + 1) * LANES] for p in range(GLA_HEADS // 2)]

    def gla_scores(c):
        rows = slice(c * C, (c + 1) * C)
        out = []
        for p in range(GLA_HEADS // 2):
            lanes = slice(p * LANES, (p + 1) * LANES)
            q2 = jnp.concatenate([qe_even[rows, lanes], qe_odd[rows, lanes]], axis=0)
            ke_p = ke[rows, lanes]
            rhs = jnp.concatenate([ke_p, ke_p, st[p].astype(bf16)], axis=0)
            r = _dot_nt(q2, rhs)
            att = jnp.where(pair_causal, r[:, 0:2 * C], 0.0).astype(bf16)
            out.append((att, r[:, 2 * C:]))
        return out

    def gla_update(c, sc):
        rows = slice(c * C, (c + 1) * C)
        b_c = b[rows]
        bl = b_c[C - 1:C, :]
        kl = ka[rows] * jnp.exp(bl - b_c)
        ebl = jnp.exp(bl)
        for p in range(GLA_HEADS // 2):
            lanes = slice(p * LANES, (p + 1) * LANES)
            att, inter = sc[p]
            v2 = jnp.concatenate(
                [va_b[rows, (2 * p) * GLA_DV:(2 * p + 1) * GLA_DV],
                 va_b[rows, (2 * p + 1) * GLA_DV:(2 * p + 2) * GLA_DV]], axis=0)
            o2 = inter + _dot(att, v2)
            for e in range(2):
                hl = slice((2 * p + e) * GLA_DV, (2 * p + e + 1) * GLA_DV)
                oa_scr[rows, hl] = o2[e * C:(e + 1) * C]
                inter_scr[rows, hl] = inter[e * C:(e + 1) * C]
            kl_p = kl[:, lanes]
            kl_stack = jnp.concatenate(
                [jnp.where(even_c, kl_p, 0.0), jnp.where(even_c, 0.0, kl_p)], axis=0).astype(bf16)
            st[p] = st[p] * ebl[:, lanes] + _dot_tn(v2, kl_stack)

    qi = lax.broadcasted_iota(jnp.int32, (W, 2 * W), 0)
    kc = lax.broadcasted_iota(jnp.int32, (W, 2 * W), 1)
    rel = qi + W - kc
    relf = rel.astype(f32)
    in_window = (rel >= 0) & (rel < W)

    def swa_probs(j, kv):
        qrows = slice(j * W, (j + 1) * W)
        band = slice(j * W, j * W + 2 * W)
        if j == 0:
            mask = in_window & ((kc >= W) | (i > 0))
        else:
            mask = in_window
        pairs = (2 * kv, 2 * kv + 1)
        q2 = jnp.concatenate([qb[qrows, p * LANES:(p + 1) * LANES] for p in pairs], axis=0)
        out = []
        for e in range(2):
            s2 = _dot_nt(q2, k_var[kv][e][band])
            probs = []
            for half, p in enumerate(pairs):
                hd = 2 * p + e
                s = s2[half * W:(half + 1) * W]
                s = jnp.where(mask, s - _alibi_slope(hd) * relf, -jnp.inf)
                pr, inv = _softmax_sink(s, sink_ref[hd])
                probs.append((pr * inv).astype(bf16))
            out.append(jnp.concatenate(probs, axis=0))
        return out

    def swa_out(j, kv, probs):
        qrows = slice(j * W, (j + 1) * W)
        band = slice(j * W, j * W + 2 * W)
        o2 = _dot(probs[0], v_var[kv][0][band]) + _dot(probs[1], v_var[kv][1][band])
        for half, p in enumerate((2 * kv, 2 * kv + 1)):
            ob_scr[qrows, p * LANES:(p + 1) * LANES] = o2[half * W:(half + 1) * W]

    n_chunks = T // C
    assert n_chunks == (T // W) * SWA_KV_HEADS
    gw = 2 * D_MODEL // n_chunks
    for idx in range(n_chunks):
        j, kv = idx // SWA_KV_HEADS, idx % SWA_KV_HEADS
        probs = swa_probs(j, kv)
        sc = gla_scores(idx)
        gate_scr[:, idx * gw:(idx + 1) * gw] = proj(C_GTA + idx * gw, gw)
        gla_update(idx, sc)
        swa_out(j, kv, probs)
    for p in range(GLA_HEADS // 2):
        st_scr[:, p * LANES:(p + 1) * LANES] = st[p]
    ga = proj(C_GA, GLA_V)
    gated_b = jax.nn.sigmoid(gate_scr[:, D_MODEL:2 * D_MODEL]) * _dot(ob_scr[...].astype(bf16), wbb_ref[...])
    sig_a = jax.nn.sigmoid(gate_scr[:, 0:D_MODEL])

    def finish(oa_raw):
        oa = _gla_out_norm(oa_raw, gn_ref, ga)
        merged = sig_a * _dot(oa.astype(bf16), wba_ref[...]) + gated_b
        m = _dot(merged.astype(bf16), wout_ref[...])
        y_ref[...] = x + _rms(m, npost_ref[...])

    finish(oa_scr[...])

    @pl.when(decay_floor < -GLA_SAFE_DECAY)
    def _():
        qs = qa * (GLA_DK ** -0.5)
        va_f = va_b.astype(f32)
        pos = lax.broadcasted_iota(jnp.int32, (T, 1), 0) % C
        er = lax.broadcasted_iota(jnp.int32, (GLA_K, GLA_V), 0)
        ec = lax.broadcasted_iota(jnp.int32, (GLA_K, GLA_V), 1)
        expand = jnp.where((er // GLA_DK) == (ec // GLA_DV), 1.0, 0.0).astype(bf16)

        def offset_term(d, acc):
            valid = pos >= d
            expo = jnp.where(valid, b - pltpu.roll(b, d, 0), 0.0)
            prod = jnp.where(valid, qs * pltpu.roll(ka, d, 0) * jnp.exp(expo), 0.0)
            return acc + _dot(prod.astype(bf16), expand) * pltpu.roll(va_f, d, 0)

        intra = lax.fori_loop(0, C, offset_term, jnp.zeros((T, GLA_V), f32))
        finish(inter_scr[...] + intra)

    @pl.when(i == pl.num_programs(0) - 1)
    def _():
        st_out_ref[...] = st_scr[...].T
        k_out_ref[...] = kcat_scr[T:T + W, :].T
        v_out_ref[...] = vcat_scr[T:T + W, :].T


def _const_spec(shape):
    nd = len(shape)
    return pl.BlockSpec(shape, lambda i: (0,) * nd, pipeline_mode=pl.Buffered(1))


def _mix_prompt(x, sinks, npre, win, wup, bg, gn, wba, wbb, wout, npost, *, T):
    L = x.shape[0]
    nb = L // T
    body = functools.partial(_mix_prompt_body, T=T)
    return pl.pallas_call(
        body,
        grid=(nb,),
        in_specs=[
            pl.BlockSpec(memory_space=pltpu.SMEM),
            pl.BlockSpec((T, D_MODEL), lambda i: (i, 0)),
            _const_spec(npre.shape), _const_spec(win.shape), _const_spec(wup.shape),
            _const_spec(bg.shape), _const_spec(gn.shape), _const_spec(wba.shape),
            _const_spec(wbb.shape), _const_spec(wout.shape), _const_spec(npost.shape),
        ],
        out_specs=[
            pl.BlockSpec((T, D_MODEL), lambda i: (i, 0)),
            pl.BlockSpec((GLA_K, GLA_DV), lambda i: (0, 0)),
            pl.BlockSpec((WINDOW, SWA_KV), lambda i: (0, 0)),
            pl.BlockSpec((WINDOW, SWA_KV), lambda i: (0, 0)),
        ],
        out_shape=[
            jax.ShapeDtypeStruct((L, D_MODEL), f32),
            jax.ShapeDtypeStruct((GLA_K, GLA_DV), f32),
            jax.ShapeDtypeStruct((WINDOW, SWA_KV), f32),
            jax.ShapeDtypeStruct((WINDOW, SWA_KV), f32),
        ],
        scratch_shapes=[
            pltpu.VMEM((GLA_DV, GLA_K), f32),
            pltpu.VMEM((T + WINDOW, SWA_KV), f32),
            pltpu.VMEM((T + WINDOW, SWA_KV), f32),
            pltpu.VMEM((T, GLA_V), f32),
            pltpu.VMEM((T, SWA_Q), f32),
            pltpu.VMEM((T, 2 * D_MODEL), f32),
            pltpu.VMEM((T, GLA_V), f32),
        ],
        compiler_params=pltpu.CompilerParams(
            dimension_semantics=("arbitrary",), vmem_limit_bytes=VMEM_LIMIT),
        name="mix_prompt",
    )(sinks, x, npre, win, wup, bg, gn, wba, wbb, wout, npost)


def _pre_sample_body(x_ref, npre_ref, win_ref, wup_ref, bg_ref,
                     qe_ref, kl_ref, e3_ref, oin_ref, va_ref, ga_ref, qb_ref, kb_ref, vb_ref,
                     gta_ref, gtb_ref, x_scr, *, NB, NT):
    for t in range(NT):
        x_scr[t * NB:(t + 1) * NB, :] = x_ref[:, t, :]
    h = _rms(x_scr[...], npre_ref[...]).astype(bf16)

    def proj(c0, n):
        return _dot(h, win_ref[:, c0:c0 + n])

    def blk(val, t):
        return val[t * NB:(t + 1) * NB, :]

    ga_ref[...] = proj(C_GA, GLA_V)
    qb_ref[...] = proj(C_QB, SWA_Q) * (SWA_HD ** -0.5)
    kb_ref[...] = proj(C_KB, SWA_KV)
    vb_ref[...] = proj(C_VB, SWA_KV)
    gta_ref[...] = proj(C_GTA, D_MODEL)
    gtb_ref[...] = proj(C_GTB, D_MODEL)
    va = proj(C_VA, GLA_V)
    va_ref[...] = va

    qa = proj(C_QA, GLA_K) * (GLA_DK ** -0.5)
    ka = proj(C_KA, GLA_K)
    la = _log_decay(proj(C_RA, RA_PAD), wup_ref, bg_ref)
    b = [blk(la, 0)]
    for t in range(1, NT):
        b.append(b[-1] + blk(la, t))
    e3_ref[...] = jnp.exp(b[NT - 1])
    for t in range(NT):
        qe_ref[t * NB:(t + 1) * NB, :] = blk(qa, t) * jnp.exp(b[t])
        kl_ref[t * NB:(t + 1) * NB, :] = blk(ka, t) * jnp.exp(b[NT - 1] - b[t])
    pairs = [(t, j) for t in range(NT) for j in range(t + 1)]
    prods = [(blk(qa, t) * blk(ka, j) * jnp.exp(b[t] - b[j])).astype(bf16) for t, j in pairs]
    r = lax.broadcasted_iota(jnp.int32, (GLA_K, GLA_V), 0)
    c = lax.broadcasted_iota(jnp.int32, (GLA_K, GLA_V), 1)
    expand = jnp.where((r // GLA_DK) == (c // GLA_DV), 1.0, 0.0).astype(bf16)
    att = _dot(jnp.concatenate(prods, axis=0), expand)
    for t in range(NT):
        acc = None
        for idx, (tt, j) in enumerate(pairs):
            if tt != t:
                continue
            term = att[idx * NB:(idx + 1) * NB, :] * blk(va, j)
            acc = term if acc is None else acc + term
        oin_ref[t * NB:(t + 1) * NB, :] = acc


def _pre_sample(xs, npre, win, wup, bg):
    NB, NT, _ = xs.shape
    body = functools.partial(_pre_sample_body, NB=NB, NT=NT)
    widths = (GLA_K, GLA_K, None, GLA_V, GLA_V, GLA_V, SWA_Q, SWA_KV, SWA_KV, D_MODEL, D_MODEL)
    out_shape = [jax.ShapeDtypeStruct((NB, GLA_K) if w is None else (NT * NB, w), f32) for w in widths]
    return pl.pallas_call(
        body,
        out_shape=out_shape,
        scratch_shapes=[pltpu.VMEM((NB * NT, D_MODEL), f32)],
        compiler_params=pltpu.CompilerParams(vmem_limit_bytes=VMEM_LIMIT),
        name="pre_sample",
    )(xs, npre, win, wup, bg)


def _state_sample_body(sink_ref, qe_ref, kl_ref, e3_ref, oin_ref, va_ref, qb_ref, kb_ref, vb_ref,
                       s0_ref, kt_ref, vt_ref,
                       oa_ref, ob_ref, s1_ref, kt1_ref, vt1_ref, *, BB, NT):
    W = WINDOW
    SK = 2 * W
    HT = GLA_HEADS * NT
    HALF = SWA_HD
    hr = lax.broadcasted_iota(jnp.int32, (HT, GLA_K), 0) // NT
    hc = lax.broadcasted_iota(jnp.int32, (HT, GLA_K), 1) // GLA_DK
    own_head = hr == hc
    ones_rows = jnp.ones((16, GLA_DV), bf16)
    zero_rows = jnp.zeros((16, GLA_DV), bf16)
    zero_ht = jnp.zeros((HT, GLA_DV), bf16)
    G2 = 2 * NT
    row = lax.broadcasted_iota(jnp.int32, (G2, SK), 0)
    col = lax.broadcasted_iota(jnp.int32, (G2, SK), 1)
    rel = (row % NT) + W - col
    relf = rel.astype(f32)
    smask = (rel >= 0) & (rel < W)
    first_pair = lax.broadcasted_iota(jnp.int32, (G2, 1), 0) < NT
    pad_rows = jnp.zeros((8 - NT, SWA_KV), f32)
    pad_lanes = jnp.zeros((SWA_KV, SK - W - 8), f32)
    zero_half = jnp.zeros((HALF, SK), bf16)

    def head_variants(cat_t, kv):
        blk = cat_t[kv * HALF:(kv + 1) * HALF]
        return (jnp.concatenate([blk, zero_half], axis=0), jnp.concatenate([zero_half, blk], axis=0))

    pending = []
    for bi in range(BB):
        s0 = s0_ref[bi]
        q4 = qe_ref[:, bi, :]
        qm = jnp.where(own_head, jnp.concatenate([q4] * GLA_HEADS, axis=0), 0.0).astype(bf16)
        o_inter = _dot(qm, s0.astype(bf16))
        for hd in range(GLA_HEADS):
            lanes = slice(hd * GLA_DV, (hd + 1) * GLA_DV)
            oa_ref[:, bi, lanes] = o_inter[hd * NT:(hd + 1) * NT, :] + oin_ref[:, bi, lanes]
        k4 = kl_ref[:, bi, :]
        km = jnp.where(own_head, jnp.concatenate([k4] * GLA_HEADS, axis=0), 0.0).astype(bf16)
        e = e3_ref[bi:bi + 1, :]
        e_hi = e.astype(bf16)
        r1 = e - e_hi.astype(f32)
        e_mid = r1.astype(bf16)
        e_lo = (r1 - e_mid.astype(f32)).astype(bf16)
        e_rows = jnp.concatenate([e_hi, e_mid, e_lo, jnp.zeros((13, GLA_K), bf16)], axis=0)
        lhs = jnp.concatenate([km, e_rows], axis=0)
        v4 = va_ref[:, bi, :].astype(bf16)
        vrep = jnp.concatenate([v4[:, hd * GLA_DV:(hd + 1) * GLA_DV] for hd in range(GLA_HEADS)], axis=0)
        rhs = jnp.concatenate([jnp.concatenate([vrep, zero_ht], axis=1),
                               jnp.concatenate([zero_rows, ones_rows], axis=1)], axis=0)
        res = _dot_tn(lhs, rhs)
        s1_ref[bi] = res[:, GLA_DV:] * s0 + res[:, :GLA_DV]

        kt = kt_ref[bi]
        vt = vt_ref[bi]
        knew_t = jnp.concatenate([kb_ref[:, bi, :], pad_rows], axis=0).T
        vnew_t = jnp.concatenate([vb_ref[:, bi, :], pad_rows], axis=0).T
        kt1_ref[bi] = jnp.concatenate([kt[:, NT:], knew_t[:, 0:NT]], axis=1)
        vt1_ref[bi] = jnp.concatenate([vt[:, NT:], vnew_t[:, 0:NT]], axis=1)
        kcat = jnp.concatenate([kt, knew_t, pad_lanes], axis=1).astype(bf16)
        vcat = jnp.concatenate([vt, vnew_t, pad_lanes], axis=1).astype(bf16)
        q4b = qb_ref[:, bi, :].astype(bf16)
        for kv in range(SWA_KV_HEADS):
            p0 = 2 * kv
            q8 = jnp.concatenate([q4b[:, p0 * LANES:(p0 + 1) * LANES],
                                  q4b[:, (p0 + 1) * LANES:(p0 + 2) * LANES]], axis=0)
            scores = [_dot(q8, kvar) for kvar in head_variants(kcat, kv)]
            pending.append((bi, kv, scores, head_variants(vcat, kv)))

    for bi, kv, scores, v_vars in pending:
        p0 = 2 * kv
        o8_t = None
        for e_ in range(2):
            h_first = 2 * p0 + e_
            h_second = 2 * (p0 + 1) + e_
            slope = jnp.where(first_pair, _alibi_slope(h_first), _alibi_slope(h_second))
            sink = jnp.where(first_pair, sink_ref[h_first], sink_ref[h_second])
            s = jnp.where(smask, scores[e_] - slope * relf, -jnp.inf)
            pr, inv = _softmax_sink(s, sink)
            o_t = _dot_nt(v_vars[e_], (pr * inv).astype(bf16))
            o8_t = o_t if o8_t is None else o8_t + o_t
        o8 = o8_t.T
        ob_ref[:, bi, p0 * LANES:(p0 + 1) * LANES] = o8[0:NT, :]
        ob_ref[:, bi, (p0 + 1) * LANES:(p0 + 2) * LANES] = o8[NT:2 * NT, :]


def _state_sample(sinks, qe, kl, e3, oin, va, qb, kb, vb, s0, kt, vt, *, NT, BB):
    NBS = s0.shape[0]
    body = functools.partial(_state_sample_body, BB=BB, NT=NT)

    def tm(a):
        return a.reshape(NT, NBS, a.shape[-1])

    def rows(n):
        return pl.BlockSpec((NT, BB, n), lambda i: (0, i, 0))

    def per_seq(shape):
        return pl.BlockSpec((BB,) + shape, lambda i: (i, 0, 0))

    oa, ob, s1, kt1, vt1 = pl.pallas_call(
        body,
        grid=(NBS // BB,),
        in_specs=[
            pl.BlockSpec(memory_space=pltpu.SMEM),
            rows(GLA_K), rows(GLA_K), pl.BlockSpec((BB, GLA_K), lambda i: (i, 0)),
            rows(GLA_V), rows(GLA_V), rows(SWA_Q), rows(SWA_KV), rows(SWA_KV),
            per_seq((GLA_K, GLA_DV)), per_seq((SWA_KV, WINDOW)), per_seq((SWA_KV, WINDOW)),
        ],
        out_specs=[
            rows(GLA_V), rows(SWA_Q),
            per_seq((GLA_K, GLA_DV)), per_seq((SWA_KV, WINDOW)), per_seq((SWA_KV, WINDOW)),
        ],
        out_shape=[
            jax.ShapeDtypeStruct((NT, NBS, GLA_V), f32),
            jax.ShapeDtypeStruct((NT, NBS, SWA_Q), f32),
            jax.ShapeDtypeStruct((NBS, GLA_K, GLA_DV), f32),
            jax.ShapeDtypeStruct((NBS, SWA_KV, WINDOW), f32),
            jax.ShapeDtypeStruct((NBS, SWA_KV, WINDOW), f32),
        ],
        compiler_params=pltpu.CompilerParams(
            dimension_semantics=("arbitrary",), vmem_limit_bytes=VMEM_LIMIT),
        name="state_sample",
    )(sinks, tm(qe), tm(kl), e3, tm(oin), tm(va), tm(qb), tm(kb), tm(vb), s0, kt, vt)
    return oa.reshape(NT * NBS, GLA_V), ob.reshape(NT * NBS, SWA_Q), s1, kt1, vt1


def _post_sample_body(x_ref, oa_ref, ga_ref, ob_ref, gta_ref, gtb_ref, gn_ref,
                      wba_ref, wbb_ref, wout_ref, npost_ref, y_ref, x_scr, *, NB, NT):
    for t in range(NT):
        x_scr[t * NB:(t + 1) * NB, :] = x_ref[:, t, :]
    oa = _gla_out_norm(oa_ref[...], gn_ref, ga_ref[...])
    y_ref[...] = _mix_tail(x_scr[...], oa, ob_ref[...], gta_ref[...], gtb_ref[...],
                           wba_ref, wbb_ref, wout_ref, npost_ref)


def _post_sample(xs, oa, ga, ob, gta, gtb, gn, wba, wbb, wout, npost):
    NB, NT, _ = xs.shape
    return pl.pallas_call(
        functools.partial(_post_sample_body, NB=NB, NT=NT),
        out_shape=jax.ShapeDtypeStruct((NT * NB, D_MODEL), f32),
        scratch_shapes=[pltpu.VMEM((NT * NB, D_MODEL), f32)],
        compiler_params=pltpu.CompilerParams(vmem_limit_bytes=VMEM_LIMIT),
        name="post_sample",
    )(xs, oa, ga, ob, gta, gtb, gn, wba, wbb, wout, npost)


FFN_COLS = 256


def _ffn_columns(h, wffn_ref, cw_ref, cb_ref, up_scr, y_scr, *, T, base, shift):
    for c0 in range(0, D_FF, FFN_COLS):
        cols = slice(c0, c0 + FFN_COLS)
        u = _dot(h, wffn_ref[:, c0:c0 + FFN_COLS])
        g = _dot(h, wffn_ref[:, D_FF + c0:D_FF + c0 + FFN_COLS])
        up_scr[base:base + T, cols] = u
        u1 = up_scr[base - shift:base - shift + T, cols]
        u2 = up_scr[base - 2 * shift:base - 2 * shift + T, cols]
        cv = (cb_ref[:, cols] + cw_ref[2:3, cols] * u + cw_ref[1:2, cols] * u1 + cw_ref[0:1, cols] * u2)
        y_scr[:, cols] = (jax.nn.gelu(cv, approximate=True) * g).astype(bf16)


def _ffn_prompt_body(x_ref, npre_ref, wffn_ref, cw_ref, cb_ref, wo_ref, npost_ref,
                     y_ref, conv_out_ref, up_scr, y_scr, *, T):
    i = pl.program_id(0)
    base = 8

    @pl.when(i == 0)
    def _():
        up_scr[0:base, :] = jnp.zeros((base, D_FF), f32)

    @pl.when(i > 0)
    def _():
        up_scr[0:base, :] = up_scr[T:T + base, :]

    x = x_ref[...]
    h = _rms(x, npre_ref[...]).astype(bf16)
    _ffn_columns(h, wffn_ref, cw_ref, cb_ref, up_scr, y_scr, T=T, base=base, shift=1)
    f = _dot(y_scr[...], wo_ref[...])
    y_ref[...] = x + _rms(f, npost_ref[...])

    @pl.when(i == pl.num_programs(0) - 1)
    def _():
        conv_out_ref[...] = up_scr[T:T + base, :]


def _ffn_prompt(x, npre, wffn, cw, cb, wo, npost, *, T):
    L = x.shape[0]
    body = functools.partial(_ffn_prompt_body, T=T)
    return pl.pallas_call(
        body,
        grid=(L // T,),
        in_specs=[
            pl.BlockSpec((T, D_MODEL), lambda i: (i, 0)),
            _const_spec(npre.shape), _const_spec(wffn.shape), _const_spec(cw.shape),
            _const_spec(cb.shape), _const_spec(wo.shape), _const_spec(npost.shape),
        ],
        out_specs=[
            pl.BlockSpec((T, D_MODEL), lambda i: (i, 0)),
            pl.BlockSpec((8, D_FF), lambda i: (0, 0)),
        ],
        out_shape=[
            jax.ShapeDtypeStruct((L, D_MODEL), f32),
            jax.ShapeDtypeStruct((8, D_FF), f32),
        ],
        scratch_shapes=[
            pltpu.VMEM((T + 8, D_FF), f32),
            pltpu.VMEM((T, D_FF), bf16),
        ],
        compiler_params=pltpu.CompilerParams(
            dimension_semantics=("arbitrary",), vmem_limit_bytes=VMEM_LIMIT),
        name="ffn_prompt",
    )(x, npre, wffn, cw, cb, wo, npost)


def _ffn_sample_body(x_ref, cst_ref, npre_ref, wffn_ref, cw_ref, cb_ref, wo_ref, npost_ref,
                     y_ref, conv_out_ref, up_scr, y_scr, *, NB, NT):
    T = NB * NT
    for t in range(CONV_W - 1):
        up_scr[t * NB:(t + 1) * NB, :] = cst_ref[:, t, :]
    base = (CONV_W - 1) * NB
    x = x_ref[...]
    h = _rms(x, npre_ref[...]).astype(bf16)
    _ffn_columns(h, wffn_ref, cw_ref, cb_ref, up_scr, y_scr, T=T, base=base, shift=NB)
    f = _dot(y_scr[...], wo_ref[...])
    y = x + _rms(f, npost_ref[...])
    for t in range(NT):
        y_ref[:, t, :] = y[t * NB:(t + 1) * NB, :]
    for t in range(CONV_W - 1):
        conv_out_ref[:, t, :] = up_scr[T + t * NB:T + (t + 1) * NB, :]


def _ffn_sample(x, cst, npre, wffn, cw, cb, wo, npost):
    NB = cst.shape[0]
    T = x.shape[0]
    NT = T // NB
    body = functools.partial(_ffn_sample_body, NB=NB, NT=NT)
    return pl.pallas_call(
        body,
        out_shape=[
            jax.ShapeDtypeStruct((NB, NT, D_MODEL), f32),
            jax.ShapeDtypeStruct((NB, CONV_W - 1, D_FF), f32),
        ],
        scratch_shapes=[
            pltpu.VMEM((T + (CONV_W - 1) * NB, D_FF), f32),
            pltpu.VMEM((T, D_FF), bf16),
        ],
        compiler_params=pltpu.CompilerParams(vmem_limit_bytes=VMEM_LIMIT),
        name="ffn_sample",
    )(x, cst, npre, wffn, cw, cb, wo, npost)


def _prep_w_in_body(wt_ref, o_ref, *, n_plain):
    j = pl.program_id(0)
    x = wt_ref[...]
    r = lax.broadcasted_iota(jnp.int32, x.shape, 0)
    x = jnp.where((j < n_plain) | (r < GLA_RANK), x, 0.0)
    o_ref[...] = x.T.astype(bf16)


def _prep_w_in(w_in):
    d_in, n_cols = w_in.shape
    head = C_GA
    tail_src = head + GLA_RANK
    n_head = head // PREP_ROWS
    n_tail = (n_cols - tail_src) // PREP_ROWS
    assert head % PREP_ROWS == 0 and (n_cols - tail_src) % PREP_ROWS == 0
    n_plain = n_head + n_tail
    assert C_RA == n_plain * PREP_ROWS

    def row_off(j):
        off = jnp.where(j < n_head, j * PREP_ROWS,
                        jnp.where(j < n_plain, tail_src + (j - n_head) * PREP_ROWS, head))
        return pl.multiple_of(off, 8)

    return pl.pallas_call(
        functools.partial(_prep_w_in_body, n_plain=n_plain),
        grid=(n_plain + 1,),
        in_specs=[pl.BlockSpec((pl.Element(PREP_ROWS), pl.Element(d_in)), lambda j: (row_off(j), 0))],
        out_specs=pl.BlockSpec((d_in, PREP_ROWS), lambda j: (0, j)),
        out_shape=jax.ShapeDtypeStruct((d_in, (n_plain + 1) * PREP_ROWS), bf16),
        compiler_params=pltpu.CompilerParams(dimension_semantics=("arbitrary",)),
        name="prep_w_in",
    )(jnp.swapaxes(w_in, 0, 1))


def kernel(x_prompt, x_sample, state_gla, cache_swa_k, cache_swa_v, state_ffn_conv, norm_mix_pre, norm_mix_post, w_in, w_gate_up, b_gate, gla_norm, sinks, w_branch_a, w_branch_b, w_out, norm_ffn_pre, norm_ffn_post, w_ffn_in, conv_w, conv_b, w_ffn_out):
    depth = w_in.shape[0]
    assert depth == 1
    l = 0
    B, L, _ = x_prompt.shape
    assert B == 1
    NBS, NT, _ = x_sample.shape

    win = _prep_w_in(w_in[l])
    wup = jnp.zeros((RA_PAD, GLA_K), f32).at[:GLA_RANK].set(w_gate_up[l]).astype(bf16)
    bg = b_gate[l].reshape(1, GLA_K)
    gn = gla_norm[l].reshape(1, GLA_DV)
    npre = norm_mix_pre[l].reshape(1, D_MODEL)
    npost = norm_mix_post[l].reshape(1, D_MODEL)
    wba = w_branch_a[l].astype(bf16)
    wbb = w_branch_b[l].astype(bf16)
    wout = w_out[l].astype(bf16)
    fpre = norm_ffn_pre[l].reshape(1, D_MODEL)
    fpost = norm_ffn_post[l].reshape(1, D_MODEL)
    wffn = w_ffn_in[l].astype(bf16)
    cw = conv_w[l]
    cb = conv_b[l].reshape(1, D_FF)
    wo = w_ffn_out[l].astype(bf16)
    sk = sinks[l]

    x1, st_p, k_p, v_p = _mix_prompt(x_prompt[0], sk, npre, win, wup, bg, gn, wba, wbb, wout, npost, T=256)
    y_p, conv_p = _ffn_prompt(x1, fpre, wffn, cw, cb, wo, fpost, T=512)

    y_prompt = y_p[None]
    gla_state_prompt = st_p.reshape(1, 1, GLA_HEADS, GLA_DK, GLA_DV)
    swa_k_prompt = jnp.transpose(k_p.reshape(SWA_KV_HEADS, SWA_HD, WINDOW), (2, 0, 1))[None, None]
    swa_v_prompt = jnp.transpose(v_p.reshape(SWA_KV_HEADS, SWA_HD, WINDOW), (2, 0, 1))[None, None]
    conv_prompt = conv_p[8 - (CONV_W - 1):].reshape(1, 1, CONV_W - 1, D_FF)

    qe, kl, e3, oin, va, ga, qb, kb, vb, gta, gtb = _pre_sample(x_sample, npre, win, wup, bg)
    kt = jnp.transpose(cache_swa_k[l], (0, 2, 3, 1)).reshape(NBS, SWA_KV, WINDOW)
    vt = jnp.transpose(cache_swa_v[l], (0, 2, 3, 1)).reshape(NBS, SWA_KV, WINDOW)
    oa_raw, ob, s1, kt1, vt1 = _state_sample(
        sk, qe, kl, e3, oin, va, qb, kb, vb, state_gla[l].reshape(NBS, GLA_K, GLA_DV), kt, vt, NT=NT, BB=8)
    x1s = _post_sample(x_sample, oa_raw, ga, ob, gta, gtb, gn, wba, wbb, wout, npost)
    y_sample, conv_s = _ffn_sample(x1s, state_ffn_conv[l], fpre, wffn, cw, cb, wo, fpost)

    def cache_out(t):
        return jnp.transpose(t.reshape(NBS, SWA_KV_HEADS, SWA_HD, WINDOW), (0, 3, 1, 2))[None]

    gla_state_sample = s1.reshape(1, NBS, GLA_HEADS, GLA_DK, GLA_DV)
    swa_k_sample = cache_out(kt1)
    swa_v_sample = cache_out(vt1)
    conv_sample = conv_s[None]
    return (y_prompt, y_sample, gla_state_prompt, gla_state_sample, swa_k_prompt, swa_v_prompt,
            swa_k_sample, swa_v_sample, conv_prompt, conv_sample)
```

```python
import functools

import jax
import jax.numpy as jnp
from jax import lax
from jax.experimental import pallas as pl
from jax.experimental.pallas import tpu as pltpu

f32 = jnp.float32
bf16 = jnp.bfloat16

D_MODEL = 1024
GLA_HEADS = 4
GLA_DK = 64
GLA_DV = 128
GLA_RANK = 16
GLA_TAU = 16.0
GLA_CHUNK = 64
GLA_SAFE_DECAY = 60.0
SWA_HEADS = 8
SWA_KV_HEADS = 2
SWA_HD = 64
WINDOW = 128
D_FF = 2816
CONV_W = 3
EPS = 1e-6
GLA_K = GLA_HEADS * GLA_DK
GLA_V = GLA_HEADS * GLA_DV
SWA_Q = SWA_HEADS * SWA_HD
SWA_KV = SWA_KV_HEADS * SWA_HD
LANES = 128

C_QA = 0
C_KA = C_QA + GLA_K
C_VA = C_KA + GLA_K
C_GA = C_VA + GLA_V
C_QB = C_GA + GLA_V
C_KB = C_QB + SWA_Q
C_VB = C_KB + SWA_KV
C_GTA = C_VB + SWA_KV
C_GTB = C_GTA + D_MODEL
C_RA = C_GTB + D_MODEL
RA_PAD = LANES
PREP_ROWS = 256
IN_COLS_PAD = C_RA + PREP_ROWS

VMEM_LIMIT = 56 * 1024 * 1024


def _dot(a, b):
    return jnp.dot(a, b, preferred_element_type=f32)


def _dot_nt(a, b):
    return lax.dot_general(a, b, (((1,), (1,)), ((), ())), preferred_element_type=f32)


def _dot_tn(a, b):
    return lax.dot_general(a, b, (((0,), (0,)), ((), ())), preferred_element_type=f32)


def _rms(x, w):
    return x * lax.rsqrt(jnp.mean(x * x, axis=-1, keepdims=True) + EPS) * w


def _gelu_tanh(x):
    k = 2.0 * 0.7978845608028654
    z2 = x * (k + (k * 0.044715) * (x * x))
    return x / (1.0 + jnp.exp(-z2))


def _split_hi_lo(x):
    hi = x.astype(bf16)
    lo = (x - hi.astype(f32)).astype(bf16)
    return hi, lo


def _log_decay(ra, wup_ref, bg_ref):
    xg = _dot(ra.astype(bf16), wup_ref[...]) + bg_ref[...]
    return jax.nn.log_sigmoid(xg) * (1.0 / GLA_TAU)


def _chunk_cumsum(la, chunk):
    n = la.shape[0]
    r = lax.broadcasted_iota(jnp.int32, (n, n), 0)
    c = lax.broadcasted_iota(jnp.int32, (n, n), 1)
    tri = jnp.where((c <= r) & ((r // chunk) == (c // chunk)), 1.0, 0.0).astype(bf16)
    hi, lo = _split_hi_lo(la)
    return _dot(tri, hi) + _dot(tri, lo)


def _even_head_lanes(shape):
    lane = lax.broadcasted_iota(jnp.int32, shape, len(shape) - 1)
    return (lane % LANES) < GLA_DK


def _gla_out_norm(o, gn_ref, ga):
    outs = []
    for h in range(GLA_HEADS):
        oh = o[:, h * GLA_DV:(h + 1) * GLA_DV]
        outs.append(_rms(oh, gn_ref[...]))
    on = jnp.concatenate(outs, axis=1)
    return on * (ga * jax.nn.sigmoid(ga))


def _mix_tail(x, oa, ob, gate_a, gate_b, wba_ref, wbb_ref, wout_ref, npost_ref):
    merged = (jax.nn.sigmoid(gate_a) * _dot(oa.astype(bf16), wba_ref[...])
              + jax.nn.sigmoid(gate_b) * _dot(ob.astype(bf16), wbb_ref[...]))
    m = _dot(merged.astype(bf16), wout_ref[...])
    return x + _rms(m, npost_ref[...])


def _alibi_slope(head):
    return 2.0 ** (-(8.0 / SWA_HEADS) * (head + 1))


def _kv_variants(x):
    lo = _even_head_lanes(x.shape)
    xr = pltpu.roll(x, SWA_HD, 1)
    zero = jnp.zeros_like(x)
    h0_lo = jnp.where(lo, x, zero).astype(bf16)
    h1_hi = jnp.where(lo, zero, x).astype(bf16)
    h1_lo = jnp.where(lo, xr, zero).astype(bf16)
    h0_hi = jnp.where(lo, zero, xr).astype(bf16)
    return (h0_lo, h0_hi), (h1_lo, h1_hi)


def _softmax_sink(s, sink):
    m = jnp.maximum(jnp.max(s, axis=-1, keepdims=True), sink)
    p = jnp.exp(s - m)
    denom = jnp.sum(p, axis=-1, keepdims=True) + jnp.exp(sink - m)
    return p, 1.0 / denom


def _mix_prompt_body(sink_ref, x_ref, npre_ref, win_ref, wup_ref, bg_ref, gn_ref,
                     wba_ref, wbb_ref, wout_ref, npost_ref,
                     y_ref, st_out_ref, k_out_ref, v_out_ref,
                     st_scr, kcat_scr, vcat_scr, oa_scr, ob_scr, gate_scr, inter_scr, *, T):
    i = pl.program_id(0)
    W = WINDOW
    C = GLA_CHUNK

    @pl.when(i == 0)
    def _():
        st_scr[...] = jnp.zeros_like(st_scr)
        kcat_scr[0:W, :] = jnp.zeros((W, SWA_KV), f32)
        vcat_scr[0:W, :] = jnp.zeros((W, SWA_KV), f32)

    @pl.when(i > 0)
    def _():
        kcat_scr[0:W, :] = kcat_scr[T:T + W, :]
        vcat_scr[0:W, :] = vcat_scr[T:T + W, :]

    x = x_ref[...]
    h = _rms(x, npre_ref[...]).astype(bf16)

    def proj(c0, n):
        return _dot(h, win_ref[:, c0:c0 + n])


    qa = proj(C_QA, GLA_K)
    ka = proj(C_KA, GLA_K)
    va_b = proj(C_VA, GLA_V).astype(bf16)
    la = _log_decay(proj(C_RA, RA_PAD), wup_ref, bg_ref)
    b = _chunk_cumsum(la, C)
    decay_floor = jnp.min(b)
    kcat_scr[W:W + T, :] = proj(C_KB, SWA_KV)
    vcat_scr[W:W + T, :] = proj(C_VB, SWA_KV)
    qb = (proj(C_QB, SWA_Q) * (SWA_HD ** -0.5)).astype(bf16)

    qe = qa * jnp.exp(b) * (GLA_DK ** -0.5)
    ke = (ka * jnp.exp(-b)).astype(bf16)
    even = _even_head_lanes((T, GLA_K))
    qe_even = jnp.where(even, qe, 0.0).astype(bf16)
    qe_odd = jnp.where(even, 0.0, qe).astype(bf16)
    k_var = _kv_variants(kcat_scr[...])
    v_var = _kv_variants(vcat_scr[...])

    r2 = lax.broadcasted_iota(jnp.int32, (2 * C, 2 * C), 0)
    c2 = lax.broadcasted_iota(jnp.int32, (2 * C, 2 * C), 1)
    pair_causal = ((r2 // C) == (c2 // C)) & ((c2 % C) <= (r2 % C))
    even_c = _even_head_lanes((C, LANES))
    st = [st_scr[:, p * LANES:(p + 1) * LANES] for p in range(GLA_HEADS // 2)]

    def gla_scores(c):
        rows = slice(c * C, (c + 1) * C)
        out = []
        for p in range(GLA_HEADS // 2):
            lanes = slice(p * LANES, (p + 1) * LANES)
            q2 = jnp.concatenate([qe_even[rows, lanes], qe_odd[rows, lanes]], axis=0)
            ke_p = ke[rows, lanes]
            rhs = jnp.concatenate([ke_p, ke_p, st[p].astype(bf16)], axis=0)
            r = _dot_nt(q2, rhs)
            att = jnp.where(pair_causal, r[:, 0:2 * C], 0.0).astype(bf16)
            out.append((att, r[:, 2 * C:]))
        return out

    def gla_update(c, sc):
        rows = slice(c * C, (c + 1) * C)
        b_c = b[rows]
        bl = b_c[C - 1:C, :]
        kl = ka[rows] * jnp.exp(bl - b_c)
        ebl = jnp.exp(bl)
        for p in range(GLA_HEADS // 2):
            lanes = slice(p * LANES, (p + 1) * LANES)
            att, inter = sc[p]
            v2 = jnp.concatenate(
                [va_b[rows, (2 * p) * GLA_DV:(2 * p + 1) * GLA_DV],
                 va_b[rows, (2 * p + 1) * GLA_DV:(2 * p + 2) * GLA_DV]], axis=0)
            o2 = inter + _dot(att, v2)
            for e in range(2):
                hl = slice((2 * p + e) * GLA_DV, (2 * p + e + 1) * GLA_DV)
                oa_scr[rows, hl] = o2[e * C:(e + 1) * C]
                inter_scr[rows, hl] = inter[e * C:(e + 1) * C]
            kl_p = kl[:, lanes]
            kl_stack = jnp.concatenate(
                [jnp.where(even_c, kl_p, 0.0), jnp.where(even_c, 0.0, kl_p)], axis=0).astype(bf16)
            st[p] = st[p] * ebl[:, lanes] + _dot_tn(v2, kl_stack)

    qi = lax.broadcasted_iota(jnp.int32, (W, 2 * W), 0)
    kc = lax.broadcasted_iota(jnp.int32, (W, 2 * W), 1)
    rel = qi + W - kc
    relf = rel.astype(f32)
    in_window = (rel >= 0) & (rel < W)

    def swa_probs(j, kv):
        qrows = slice(j * W, (j + 1) * W)
        band = slice(j * W, j * W + 2 * W)
        if j == 0:
            mask = in_window & ((kc >= W) | (i > 0))
        else:
            mask = in_window
        pairs = (2 * kv, 2 * kv + 1)
        q2 = jnp.concatenate([qb[qrows, p * LANES:(p + 1) * LANES] for p in pairs], axis=0)
        out = []
        for e in range(2):
            s2 = _dot_nt(q2, k_var[kv][e][band])
            probs = []
            for half, p in enumerate(pairs):
                hd = 2 * p + e
                s = s2[half * W:(half + 1) * W]
                s = jnp.where(mask, s - _alibi_slope(hd) * relf, -jnp.inf)
                pr, inv = _softmax_sink(s, sink_ref[hd])
                probs.append((pr * inv).astype(bf16))
            out.append(jnp.concatenate(probs, axis=0))
        return out

    def swa_out(j, kv, probs):
        qrows = slice(j * W, (j + 1) * W)
        band = slice(j * W, j * W + 2 * W)
        o2 = _dot(probs[0], v_var[kv][0][band]) + _dot(probs[1], v_var[kv][1][band])
        for half, p in enumerate((2 * kv, 2 * kv + 1)):
            ob_scr[qrows, p * LANES:(p + 1) * LANES] = o2[half * W:(half + 1) * W]

    n_chunks = T // C
    assert n_chunks == (T // W) * SWA_KV_HEADS
    gw = 2 * D_MODEL // n_chunks
    for idx in range(n_chunks):
        j, kv = idx // SWA_KV_HEADS, idx % SWA_KV_HEADS
        probs = swa_probs(j, kv)
        sc = gla_scores(idx)
        gate_scr[:, idx * gw:(idx + 1) * gw] = proj(C_GTA + idx * gw, gw)
        gla_update(idx, sc)
        swa_out(j, kv, probs)
    for p in range(GLA_HEADS // 2):
        st_scr[:, p * LANES:(p + 1) * LANES] = st[p]
    ga = proj(C_GA, GLA_V)
    gated_b = jax.nn.sigmoid(gate_scr[:, D_MODEL:2 * D_MODEL]) * _dot(ob_scr[...].astype(bf16), wbb_ref[...])
    sig_a = jax.nn.sigmoid(gate_scr[:, 0:D_MODEL])

    def finish(oa_raw):
        oa = _gla_out_norm(oa_raw, gn_ref, ga)
        merged = sig_a * _dot(oa.astype(bf16), wba_ref[...]) + gated_b
        m = _dot(merged.astype(bf16), wout_ref[...])
        y_ref[...] = x + _rms(m, npost_ref[...])

    finish(oa_scr[...])

    @pl.when(decay_floor < -GLA_SAFE_DECAY)
    def _():
        qs = qa * (GLA_DK ** -0.5)
        va_f = va_b.astype(f32)
        pos = lax.broadcasted_iota(jnp.int32, (T, 1), 0) % C
        er = lax.broadcasted_iota(jnp.int32, (GLA_K, GLA_V), 0)
        ec = lax.broadcasted_iota(jnp.int32, (GLA_K, GLA_V), 1)
        expand = jnp.where((er // GLA_DK) == (ec // GLA_DV), 1.0, 0.0).astype(bf16)

        def offset_term(d, acc):
            valid = pos >= d
            expo = jnp.where(valid, b - pltpu.roll(b, d, 0), 0.0)
            prod = jnp.where(valid, qs * pltpu.roll(ka, d, 0) * jnp.exp(expo), 0.0)
            return acc + _dot(prod.astype(bf16), expand) * pltpu.roll(va_f, d, 0)

        intra = lax.fori_loop(0, C, offset_term, jnp.zeros((T, GLA_V), f32))
        finish(inter_scr[...] + intra)

    @pl.when(i == pl.num_programs(0) - 1)
    def _():
        st_out_ref[...] = st_scr[...].T
        k_out_ref[...] = kcat_scr[T:T + W, :].T
        v_out_ref[...] = vcat_scr[T:T + W, :].T


def _const_spec(shape):
    nd = len(shape)
    return pl.BlockSpec(shape, lambda i: (0,) * nd, pipeline_mode=pl.Buffered(1))


def _mix_prompt(x, sinks, npre, win, wup, bg, gn, wba, wbb, wout, npost, *, T):
    L = x.shape[0]
    nb = L // T
    body = functools.partial(_mix_prompt_body, T=T)
    return pl.pallas_call(
        body,
        grid=(nb,),
        in_specs=[
            pl.BlockSpec(memory_space=pltpu.SMEM),
            pl.BlockSpec((T, D_MODEL), lambda i: (i, 0)),
            _const_spec(npre.shape), _const_spec(win.shape), _const_spec(wup.shape),
            _const_spec(bg.shape), _const_spec(gn.shape), _const_spec(wba.shape),
            _const_spec(wbb.shape), _const_spec(wout.shape), _const_spec(npost.shape),
        ],
        out_specs=[
            pl.BlockSpec((T, D_MODEL), lambda i: (i, 0)),
            pl.BlockSpec((GLA_K, GLA_DV), lambda i: (0, 0)),
            pl.BlockSpec((WINDOW, SWA_KV), lambda i: (0, 0)),
            pl.BlockSpec((WINDOW, SWA_KV), lambda i: (0, 0)),
        ],
        out_shape=[
            jax.ShapeDtypeStruct((L, D_MODEL), f32),
            jax.ShapeDtypeStruct((GLA_K, GLA_DV), f32),
            jax.ShapeDtypeStruct((WINDOW, SWA_KV), f32),
            jax.ShapeDtypeStruct((WINDOW, SWA_KV), f32),
        ],
        scratch_shapes=[
            pltpu.VMEM((GLA_DV, GLA_K), f32),
            pltpu.VMEM((T + WINDOW, SWA_KV), f32),
            pltpu.VMEM((T + WINDOW, SWA_KV), f32),
            pltpu.VMEM((T, GLA_V), f32),
            pltpu.VMEM((T, SWA_Q), f32),
            pltpu.VMEM((T, 2 * D_MODEL), f32),
            pltpu.VMEM((T, GLA_V), f32),
        ],
        compiler_params=pltpu.CompilerParams(
            dimension_semantics=("arbitrary",), vmem_limit_bytes=VMEM_LIMIT),
        name="mix_prompt",
    )(sinks, x, npre, win, wup, bg, gn, wba, wbb, wout, npost)


def _pre_sample_body(x_ref, npre_ref, win_ref, wup_ref, bg_ref,
                     qe_ref, kl_ref, e3_ref, oin_ref, va_ref, ga_ref, qb_ref, kb_ref, vb_ref,
                     gta_ref, gtb_ref, x_scr, *, NB, NT):
    for t in range(NT):
        x_scr[t * NB:(t + 1) * NB, :] = x_ref[:, t, :]
    h = _rms(x_scr[...], npre_ref[...]).astype(bf16)

    def proj(c0, n):
        return _dot(h, win_ref[:, c0:c0 + n])

    def blk(val, t):
        return val[t * NB:(t + 1) * NB, :]

    ga_ref[...] = proj(C_GA, GLA_V)
    qb_ref[...] = proj(C_QB, SWA_Q) * (SWA_HD ** -0.5)
    kb_ref[...] = proj(C_KB, SWA_KV)
    vb_ref[...] = proj(C_VB, SWA_KV)
    gta_ref[...] = proj(C_GTA, D_MODEL)
    gtb_ref[...] = proj(C_GTB, D_MODEL)
    va = proj(C_VA, GLA_V)
    va_ref[...] = va

    qa = proj(C_QA, GLA_K) * (GLA_DK ** -0.5)
    ka = proj(C_KA, GLA_K)
    la = _log_decay(proj(C_RA, RA_PAD), wup_ref, bg_ref)
    b = [blk(la, 0)]
    for t in range(1, NT):
        b.append(b[-1] + blk(la, t))
    e3_ref[...] = jnp.exp(b[NT - 1])
    for t in range(NT):
        qe_ref[t * NB:(t + 1) * NB, :] = blk(qa, t) * jnp.exp(b[t])
        kl_ref[t * NB:(t + 1) * NB, :] = blk(ka, t) * jnp.exp(b[NT - 1] - b[t])
    pairs = [(t, j) for t in range(NT) for j in range(t + 1)]
    prods = [(blk(qa, t) * blk(ka, j) * jnp.exp(b[t] - b[j])).astype(bf16) for t, j in pairs]
    r = lax.broadcasted_iota(jnp.int32, (GLA_K, GLA_V), 0)
    c = lax.broadcasted_iota(jnp.int32, (GLA_K, GLA_V), 1)
    expand = jnp.where((r // GLA_DK) == (c // GLA_DV), 1.0, 0.0).astype(bf16)
    att = _dot(jnp.concatenate(prods, axis=0), expand)
    for t in range(NT):
        acc = None
        for idx, (tt, j) in enumerate(pairs):
            if tt != t:
                continue
            term = att[idx * NB:(idx + 1) * NB, :] * blk(va, j)
            acc = term if acc is None else acc + term
        oin_ref[t * NB:(t + 1) * NB, :] = acc


def _pre_sample(xs, npre, win, wup, bg):
    NB, NT, _ = xs.shape
    body = functools.partial(_pre_sample_body, NB=NB, NT=NT)
    widths = (GLA_K, GLA_K, None, GLA_V, GLA_V, GLA_V, SWA_Q, SWA_KV, SWA_KV, D_MODEL, D_MODEL)
    out_shape = [jax.ShapeDtypeStruct((NB, GLA_K) if w is None else (NT * NB, w), f32) for w in widths]
    return pl.pallas_call(
        body,
        out_shape=out_shape,
        scratch_shapes=[pltpu.VMEM((NB * NT, D_MODEL), f32)],
        compiler_params=pltpu.CompilerParams(vmem_limit_bytes=VMEM_LIMIT),
        name="pre_sample",
    )(xs, npre, win, wup, bg)


def _state_sample_body(sink_ref, qe_ref, kl_ref, e3_ref, oin_ref, va_ref, qb_ref, kb_ref, vb_ref,
                       s0_ref, kt_ref, vt_ref,
                       oa_ref, ob_ref, s1_ref, kt1_ref, vt1_ref, *, BB, NT):
    W = WINDOW
    SK = 2 * W
    HT = GLA_HEADS * NT
    HALF = SWA_HD
    hr = lax.broadcasted_iota(jnp.int32, (HT, GLA_K), 0) // NT
    hc = lax.broadcasted_iota(jnp.int32, (HT, GLA_K), 1) // GLA_DK
    own_head = hr == hc
    ones_rows = jnp.ones((16, GLA_DV), bf16)
    zero_rows = jnp.zeros((16, GLA_DV), bf16)
    zero_ht = jnp.zeros((HT, GLA_DV), bf16)
    G2 = 2 * NT
    row = lax.broadcasted_iota(jnp.int32, (G2, SK), 0)
    col = lax.broadcasted_iota(jnp.int32, (G2, SK), 1)
    rel = (row % NT) + W - col
    relf = rel.astype(f32)
    smask = (rel >= 0) & (rel < W)
    first_pair = lax.broadcasted_iota(jnp.int32, (G2, 1), 0) < NT
    pad_rows = jnp.zeros((8 - NT, SWA_KV), f32)
    pad_lanes = jnp.zeros((SWA_KV, SK - W - 8), f32)
    zero_half = jnp.zeros((HALF, SK), bf16)

    def head_variants(cat_t, kv):
        blk = cat_t[kv * HALF:(kv + 1) * HALF]
        return (jnp.concatenate([blk, zero_half], axis=0), jnp.concatenate([zero_half, blk], axis=0))

    pending = []
    for bi in range(BB):
        s0 = s0_ref[bi]
        q4 = qe_ref[:, bi, :]
        qm = jnp.where(own_head, jnp.concatenate([q4] * GLA_HEADS, axis=0), 0.0).astype(bf16)
        o_inter = _dot(qm, s0.astype(bf16))
        for hd in range(GLA_HEADS):
            lanes = slice(hd * GLA_DV, (hd + 1) * GLA_DV)
            oa_ref[:, bi, lanes] = o_inter[hd * NT:(hd + 1) * NT, :] + oin_ref[:, bi, lanes]
        k4 = kl_ref[:, bi, :]
        km = jnp.where(own_head, jnp.concatenate([k4] * GLA_HEADS, axis=0), 0.0).astype(bf16)
        e = e3_ref[bi:bi + 1, :]
        e_hi = e.astype(bf16)
        r1 = e - e_hi.astype(f32)
        e_mid = r1.astype(bf16)
        e_lo = (r1 - e_mid.astype(f32)).astype(bf16)
        e_rows = jnp.concatenate([e_hi, e_mid, e_lo, jnp.zeros((13, GLA_K), bf16)], axis=0)
        lhs = jnp.concatenate([km, e_rows], axis=0)
        v4 = va_ref[:, bi, :].astype(bf16)
        vrep = jnp.concatenate([v4[:, hd * GLA_DV:(hd + 1) * GLA_DV] for hd in range(GLA_HEADS)], axis=0)
        rhs = jnp.concatenate([jnp.concatenate([vrep, zero_ht], axis=1),
                               jnp.concatenate([zero_rows, ones_rows], axis=1)], axis=0)
        res = _dot_tn(lhs, rhs)
        s1_ref[bi] = res[:, GLA_DV:] * s0 + res[:, :GLA_DV]

        kt = kt_ref[bi]
        vt = vt_ref[bi]
        knew_t = jnp.concatenate([kb_ref[:, bi, :], pad_rows], axis=0).T
        vnew_t = jnp.concatenate([vb_ref[:, bi, :], pad_rows], axis=0).T
        kt1_ref[bi] = jnp.concatenate([kt[:, NT:], knew_t[:, 0:NT]], axis=1)
        vt1_ref[bi] = jnp.concatenate([vt[:, NT:], vnew_t[:, 0:NT]], axis=1)
        kcat = jnp.concatenate([kt, knew_t, pad_lanes], axis=1).astype(bf16)
        vcat = jnp.concatenate([vt, vnew_t, pad_lanes], axis=1).astype(bf16)
        q4b = qb_ref[:, bi, :].astype(bf16)
        for kv in range(SWA_KV_HEADS):
            p0 = 2 * kv
            q8 = jnp.concatenate([q4b[:, p0 * LANES:(p0 + 1) * LANES],
                                  q4b[:, (p0 + 1) * LANES:(p0 + 2) * LANES]], axis=0)
            scores = [_dot(q8, kvar) for kvar in head_variants(kcat, kv)]
            pending.append((bi, kv, scores, head_variants(vcat, kv)))

    for bi, kv, scores, v_vars in pending:
        p0 = 2 * kv
        o8_t = None
        for e_ in range(2):
            h_first = 2 * p0 + e_
            h_second = 2 * (p0 + 1) + e_
            slope = jnp.where(first_pair, _alibi_slope(h_first), _alibi_slope(h_second))
            sink = jnp.where(first_pair, sink_ref[h_first], sink_ref[h_second])
            s = jnp.where(smask, scores[e_] - slope * relf, -jnp.inf)
            pr, inv = _softmax_sink(s, sink)
            o_t = _dot_nt(v_vars[e_], (pr * inv).astype(bf16))
            o8_t = o_t if o8_t is None else o8_t + o_t
        o8 = o8_t.T
        ob_ref[:, bi, p0 * LANES:(p0 + 1) * LANES] = o8[0:NT, :]
        ob_ref[:, bi, (p0 + 1) * LANES:(p0 + 2) * LANES] = o8[NT:2 * NT, :]


def _state_sample(sinks, qe, kl, e3, oin, va, qb, kb, vb, s0, kt, vt, *, NT, BB):
    NBS = s0.shape[0]
    body = functools.partial(_state_sample_body, BB=BB, NT=NT)

    def tm(a):
        return a.reshape(NT, NBS, a.shape[-1])

    def rows(n):
        return pl.BlockSpec((NT, BB, n), lambda i: (0, i, 0))

    def per_seq(shape):
        return pl.BlockSpec((BB,) + shape, lambda i: (i, 0, 0))

    oa, ob, s1, kt1, vt1 = pl.pallas_call(
        body,
        grid=(NBS // BB,),
        in_specs=[
            pl.BlockSpec(memory_space=pltpu.SMEM),
            rows(GLA_K), rows(GLA_K), pl.BlockSpec((BB, GLA_K), lambda i: (i, 0)),
            rows(GLA_V), rows(GLA_V), rows(SWA_Q), rows(SWA_KV), rows(SWA_KV),
            per_seq((GLA_K, GLA_DV)), per_seq((SWA_KV, WINDOW)), per_seq((SWA_KV, WINDOW)),
        ],
        out_specs=[
            rows(GLA_V), rows(SWA_Q),
            per_seq((GLA_K, GLA_DV)), per_seq((SWA_KV, WINDOW)), per_seq((SWA_KV, WINDOW)),
        ],
        out_shape=[
            jax.ShapeDtypeStruct((NT, NBS, GLA_V), f32),
            jax.ShapeDtypeStruct((NT, NBS, SWA_Q), f32),
            jax.ShapeDtypeStruct((NBS, GLA_K, GLA_DV), f32),
            jax.ShapeDtypeStruct((NBS, SWA_KV, WINDOW), f32),
            jax.ShapeDtypeStruct((NBS, SWA_KV, WINDOW), f32),
        ],
        compiler_params=pltpu.CompilerParams(
            dimension_semantics=("arbitrary",), vmem_limit_bytes=VMEM_LIMIT),
        name="state_sample",
    )(sinks, tm(qe), tm(kl), e3, tm(oin), tm(va), tm(qb), tm(kb), tm(vb), s0, kt, vt)
    return oa.reshape(NT * NBS, GLA_V), ob.reshape(NT * NBS, SWA_Q), s1, kt1, vt1


def _post_sample_body(x_ref, oa_ref, ga_ref, ob_ref, gta_ref, gtb_ref, gn_ref,
                      wba_ref, wbb_ref, wout_ref, npost_ref, y_ref, x_scr, *, NB, NT):
    for t in range(NT):
        x_scr[t * NB:(t + 1) * NB, :] = x_ref[:, t, :]
    oa = _gla_out_norm(oa_ref[...], gn_ref, ga_ref[...])
    y_ref[...] = _mix_tail(x_scr[...], oa, ob_ref[...], gta_ref[...], gtb_ref[...],
                           wba_ref, wbb_ref, wout_ref, npost_ref)


def _post_sample(xs, oa, ga, ob, gta, gtb, gn, wba, wbb, wout, npost):
    NB, NT, _ = xs.shape
    return pl.pallas_call(
        functools.partial(_post_sample_body, NB=NB, NT=NT),
        out_shape=jax.ShapeDtypeStruct((NT * NB, D_MODEL), f32),
        scratch_shapes=[pltpu.VMEM((NT * NB, D_MODEL), f32)],
        compiler_params=pltpu.CompilerParams(vmem_limit_bytes=VMEM_LIMIT),
        name="post_sample",
    )(xs, oa, ga, ob, gta, gtb, gn, wba, wbb, wout, npost)


FFN_COLS = 256


def _ffn_columns(h, wffn_ref, cw_ref, cb_ref, up_scr, y_scr, *, T, base, shift):
    for c0 in range(0, D_FF, FFN_COLS):
        cols = slice(c0, c0 + FFN_COLS)
        u = _dot(h, wffn_ref[:, c0:c0 + FFN_COLS])
        g = _dot(h, wffn_ref[:, D_FF + c0:D_FF + c0 + FFN_COLS])
        up_scr[base:base + T, cols] = u
        u1 = up_scr[base - shift:base - shift + T, cols]
        u2 = up_scr[base - 2 * shift:base - 2 * shift + T, cols]
        cv = (cb_ref[:, cols] + cw_ref[2:3, cols] * u + cw_ref[1:2, cols] * u1 + cw_ref[0:1, cols] * u2)
        y_scr[:, cols] = (_gelu_tanh(cv) * g).astype(bf16)


def _ffn_prompt_body(x_ref, npre_ref, wffn_ref, cw_ref, cb_ref, wo_ref, npost_ref,
                     y_ref, conv_out_ref, up_scr, y_scr, *, T):
    i = pl.program_id(0)
    base = 8

    @pl.when(i == 0)
    def _():
        up_scr[0:base, :] = jnp.zeros((base, D_FF), f32)

    @pl.when(i > 0)
    def _():
        up_scr[0:base, :] = up_scr[T:T + base, :]

    x = x_ref[...]
    h = _rms(x, npre_ref[...]).astype(bf16)
    _ffn_columns(h, wffn_ref, cw_ref, cb_ref, up_scr, y_scr, T=T, base=base, shift=1)
    f = _dot(y_scr[...], wo_ref[...])
    y_ref[...] = x + _rms(f, npost_ref[...])

    @pl.when(i == pl.num_programs(0) - 1)
    def _():
        conv_out_ref[...] = up_scr[T:T + base, :]


def _ffn_prompt(x, npre, wffn, cw, cb, wo, npost, *, T):
    L = x.shape[0]
    body = functools.partial(_ffn_prompt_body, T=T)
    return pl.pallas_call(
        body,
        grid=(L // T,),
        in_specs=[
            pl.BlockSpec((T, D_MODEL), lambda i: (i, 0)),
            _const_spec(npre.shape), _const_spec(wffn.shape), _const_spec(cw.shape),
            _const_spec(cb.shape), _const_spec(wo.shape), _const_spec(npost.shape),
        ],
        out_specs=[
            pl.BlockSpec((T, D_MODEL), lambda i: (i, 0)),
            pl.BlockSpec((8, D_FF), lambda i: (0, 0)),
        ],
        out_shape=[
            jax.ShapeDtypeStruct((L, D_MODEL), f32),
            jax.ShapeDtypeStruct((8, D_FF), f32),
        ],
        scratch_shapes=[
            pltpu.VMEM((T + 8, D_FF), f32),
            pltpu.VMEM((T, D_FF), bf16),
        ],
        compiler_params=pltpu.CompilerParams(
            dimension_semantics=("arbitrary",), vmem_limit_bytes=VMEM_LIMIT),
        name="ffn_prompt",
    )(x, npre, wffn, cw, cb, wo, npost)


def _ffn_sample_body(x_ref, cst_ref, npre_ref, wffn_ref, cw_ref, cb_ref, wo_ref, npost_ref,
                     y_ref, conv_out_ref, up_scr, y_scr, *, NB, NT):
    T = NB * NT
    for t in range(CONV_W - 1):
        up_scr[t * NB:(t + 1) * NB, :] = cst_ref[:, t, :]
    base = (CONV_W - 1) * NB
    x = x_ref[...]
    h = _rms(x, npre_ref[...]).astype(bf16)
    _ffn_columns(h, wffn_ref, cw_ref, cb_ref, up_scr, y_scr, T=T, base=base, shift=NB)
    f = _dot(y_scr[...], wo_ref[...])
    y = x + _rms(f, npost_ref[...])
    for t in range(NT):
        y_ref[:, t, :] = y[t * NB:(t + 1) * NB, :]
    for t in range(CONV_W - 1):
        conv_out_ref[:, t, :] = up_scr[T + t * NB:T + (t + 1) * NB, :]


def _ffn_sample(x, cst, npre, wffn, cw, cb, wo, npost):
    NB = cst.shape[0]
    T = x.shape[0]
    NT = T // NB
    body = functools.partial(_ffn_sample_body, NB=NB, NT=NT)
    return pl.pallas_call(
        body,
        out_shape=[
            jax.ShapeDtypeStruct((NB, NT, D_MODEL), f32),
            jax.ShapeDtypeStruct((NB, CONV_W - 1, D_FF), f32),
        ],
        scratch_shapes=[
            pltpu.VMEM((T + (CONV_W - 1) * NB, D_FF), f32),
            pltpu.VMEM((T, D_FF), bf16),
        ],
        compiler_params=pltpu.CompilerParams(vmem_limit_bytes=VMEM_LIMIT),
        name="ffn_sample",
    )(x, cst, npre, wffn, cw, cb, wo, npost)


def _prep_w_in_body(*refs, n_plain, per_step):
    o_ref = refs[-1]
    j = pl.program_id(0)
    for k, wt_ref in enumerate(refs[:-1]):
        x = wt_ref[...]
        r = lax.broadcasted_iota(jnp.int32, x.shape, 0)
        x = jnp.where((j * per_step + k < n_plain) | (r < GLA_RANK), x, 0.0)
        o_ref[:, k * PREP_ROWS:(k + 1) * PREP_ROWS] = x.T.astype(bf16)


def _prep_w_in(w_in):
    d_in, n_cols = w_in.shape
    head = C_GA
    tail_src = head + GLA_RANK
    n_head = head // PREP_ROWS
    n_tail = (n_cols - tail_src) // PREP_ROWS
    assert head % PREP_ROWS == 0 and (n_cols - tail_src) % PREP_ROWS == 0
    n_plain = n_head + n_tail
    assert C_RA == n_plain * PREP_ROWS
    per_step = 2
    assert (n_plain + 1) % per_step == 0

    def row_off(blk):
        off = jnp.where(blk < n_head, blk * PREP_ROWS,
                        jnp.where(blk < n_plain, tail_src + (blk - n_head) * PREP_ROWS, head))
        return pl.multiple_of(off, 8)

    def in_spec(k):
        return pl.BlockSpec((pl.Element(PREP_ROWS), pl.Element(d_in)), lambda j: (row_off(j * per_step + k), 0))

    wt = jnp.swapaxes(w_in, 0, 1)
    return pl.pallas_call(
        functools.partial(_prep_w_in_body, n_plain=n_plain, per_step=per_step),
        grid=((n_plain + 1) // per_step,),
        in_specs=[in_spec(k) for k in range(per_step)],
        out_specs=pl.BlockSpec((d_in, per_step * PREP_ROWS), lambda j: (0, j)),
        out_shape=jax.ShapeDtypeStruct((d_in, (n_plain + 1) * PREP_ROWS), bf16),
        compiler_params=pltpu.CompilerParams(dimension_semantics=("arbitrary",)),
        name="prep_w_in",
    )(*([wt] * per_step))


def kernel(x_prompt, x_sample, state_gla, cache_swa_k, cache_swa_v, state_ffn_conv, norm_mix_pre, norm_mix_post, w_in, w_gate_up, b_gate, gla_norm, sinks, w_branch_a, w_branch_b, w_out, norm_ffn_pre, norm_ffn_post, w_ffn_in, conv_w, conv_b, w_ffn_out):
    depth = w_in.shape[0]
    assert depth == 1
    l = 0
    B, L, _ = x_prompt.shape
    assert B == 1
    NBS, NT, _ = x_sample.shape

    win = _prep_w_in(w_in[l])
    wup = jnp.zeros((RA_PAD, GLA_K), f32).at[:GLA_RANK].set(w_gate_up[l]).astype(bf16)
    bg = b_gate[l].reshape(1, GLA_K)
    gn = gla_norm[l].reshape(1, GLA_DV)
    npre = norm_mix_pre[l].reshape(1, D_MODEL)
    npost = norm_mix_post[l].reshape(1, D_MODEL)
    wba = w_branch_a[l].astype(bf16)
    wbb = w_branch_b[l].astype(bf16)
    wout = w_out[l].astype(bf16)
    fpre = norm_ffn_pre[l].reshape(1, D_MODEL)
    fpost = norm_ffn_post[l].reshape(1, D_MODEL)
    wffn = w_ffn_in[l].astype(bf16)
    cw = conv_w[l]
    cb = conv_b[l].reshape(1, D_FF)
    wo = w_ffn_out[l].astype(bf16)
    sk = sinks[l]

    x1, st_p, k_p, v_p = _mix_prompt(x_prompt[0], sk, npre, win, wup, bg, gn, wba, wbb, wout, npost, T=256)
    y_p, conv_p = _ffn_prompt(x1, fpre, wffn, cw, cb, wo, fpost, T=512)

    y_prompt = y_p[None]
    gla_state_prompt = st_p.reshape(1, 1, GLA_HEADS, GLA_DK, GLA_DV)
    swa_k_prompt = jnp.transpose(k_p.reshape(SWA_KV_HEADS, SWA_HD, WINDOW), (2, 0, 1))[None, None]
    swa_v_prompt = jnp.transpose(v_p.reshape(SWA_KV_HEADS, SWA_HD, WINDOW), (2, 0, 1))[None, None]
    conv_prompt = conv_p[8 - (CONV_W - 1):].reshape(1, 1, CONV_W - 1, D_FF)

    qe, kl, e3, oin, va, ga, qb, kb, vb, gta, gtb = _pre_sample(x_sample, npre, win, wup, bg)
    kt = jnp.transpose(cache_swa_k[l], (0, 2, 3, 1)).reshape(NBS, SWA_KV, WINDOW)
    vt = jnp.transpose(cache_swa_v[l], (0, 2, 3, 1)).reshape(NBS, SWA_KV, WINDOW)
    oa_raw, ob, s1, kt1, vt1 = _state_sample(
        sk, qe, kl, e3, oin, va, qb, kb, vb, state_gla[l].reshape(NBS, GLA_K, GLA_DV), kt, vt, NT=NT, BB=8)
    x1s = _post_sample(x_sample, oa_raw, ga, ob, gta, gtb, gn, wba, wbb, wout, npost)
    y_sample, conv_s = _ffn_sample(x1s, state_ffn_conv[l], fpre, wffn, cw, cb, wo, fpost)

    def cache_out(t):
        return jnp.transpose(t.reshape(NBS, SWA_KV_HEADS, SWA_HD, WINDOW), (0, 3, 1, 2))[None]

    gla_state_sample = s1.reshape(1, NBS, GLA_HEADS, GLA_DK, GLA_DV)
    swa_k_sample = cache_out(kt1)
    swa_v_sample = cache_out(vt1)
    conv_sample = conv_s[None]
    return (y_prompt, y_sample, gla_state_prompt, gla_state_sample, swa_k_prompt, swa_v_prompt,
            swa_k_sample, swa_v_sample, conv_prompt, conv_sample)
```

```python
import functools

import jax
import jax.numpy as jnp
from jax import lax
from jax.experimental import pallas as pl
from jax.experimental.pallas import tpu as pltpu

f32 = jnp.float32
bf16 = jnp.bfloat16

D_MODEL = 1024
GLA_HEADS = 4
GLA_DK = 64
GLA_DV = 128
GLA_RANK = 16
GLA_TAU = 16.0
GLA_CHUNK = 64
GLA_SAFE_DECAY = 60.0
SWA_HEADS = 8
SWA_KV_HEADS = 2
SWA_HD = 64
WINDOW = 128
D_FF = 2816
CONV_W = 3
EPS = 1e-6
GLA_K = GLA_HEADS * GLA_DK
GLA_V = GLA_HEADS * GLA_DV
SWA_Q = SWA_HEADS * SWA_HD
SWA_KV = SWA_KV_HEADS * SWA_HD
LANES = 128
LOG2E = 1.4426950408889634

C_QA = 0
C_KA = C_QA + GLA_K
C_VA = C_KA + GLA_K
C_GA = C_VA + GLA_V
C_QB = C_GA + GLA_V
C_KB = C_QB + SWA_Q
C_VB = C_KB + SWA_KV
C_GTA = C_VB + SWA_KV
C_GTB = C_GTA + D_MODEL
C_RA = C_GTB + D_MODEL
RA_PAD = LANES
PREP_ROWS = 256
IN_COLS_PAD = C_RA + PREP_ROWS

VMEM_LIMIT = 56 * 1024 * 1024


def _dot(a, b):
    return jnp.dot(a, b, preferred_element_type=f32)


def _dot_nt(a, b):
    return lax.dot_general(a, b, (((1,), (1,)), ((), ())), preferred_element_type=f32)


def _dot_tn(a, b):
    return lax.dot_general(a, b, (((0,), (0,)), ((), ())), preferred_element_type=f32)


def _rms(x, w):
    return x * lax.rsqrt(jnp.mean(x * x, axis=-1, keepdims=True) + EPS) * w


def _gelu_tanh(x):
    k = -2.0 * 0.7978845608028654 * LOG2E
    return x / (1.0 + jnp.exp2(x * (k + (k * 0.044715) * (x * x))))


def _split_hi_lo(x):
    hi = x.astype(bf16)
    lo = (x - hi.astype(f32)).astype(bf16)
    return hi, lo


def _log_decay(ra, wup_ref, bg_ref):
    xg = _dot(ra.astype(bf16), wup_ref[...]) + bg_ref[...]
    return jax.nn.log_sigmoid(xg) * (1.0 / GLA_TAU)


def _chunk_cumsum(la, chunk):
    n = la.shape[0]
    r = lax.broadcasted_iota(jnp.int32, (n, n), 0)
    c = lax.broadcasted_iota(jnp.int32, (n, n), 1)
    tri = jnp.where((c <= r) & ((r // chunk) == (c // chunk)), 1.0, 0.0).astype(bf16)
    hi, lo = _split_hi_lo(la)
    return _dot(tri, hi) + _dot(tri, lo)


def _even_head_lanes(shape):
    lane = lax.broadcasted_iota(jnp.int32, shape, len(shape) - 1)
    return (lane % LANES) < GLA_DK


def _gla_out_norm(o, gn_ref, ga):
    outs = []
    for h in range(GLA_HEADS):
        oh = o[:, h * GLA_DV:(h + 1) * GLA_DV]
        outs.append(_rms(oh, gn_ref[...]))
    on = jnp.concatenate(outs, axis=1)
    return on * (ga * jax.nn.sigmoid(ga))


def _mix_tail(x, oa, ob, gate_a, gate_b, wba_ref, wbb_ref, wout_ref, npost_ref):
    merged = (jax.nn.sigmoid(gate_a) * _dot(oa.astype(bf16), wba_ref[...])
              + jax.nn.sigmoid(gate_b) * _dot(ob.astype(bf16), wbb_ref[...]))
    m = _dot(merged.astype(bf16), wout_ref[...])
    return x + _rms(m, npost_ref[...])


def _alibi_slope(head):
    return LOG2E * 2.0 ** (-(8.0 / SWA_HEADS) * (head + 1))


SWA_Q_SCALE = LOG2E * SWA_HD ** -0.5


def _kv_variants(x):
    lo = _even_head_lanes(x.shape)
    xr = pltpu.roll(x, SWA_HD, 1)
    zero = jnp.zeros_like(x)
    h0_lo = jnp.where(lo, x, zero).astype(bf16)
    h1_hi = jnp.where(lo, zero, x).astype(bf16)
    h1_lo = jnp.where(lo, xr, zero).astype(bf16)
    h0_hi = jnp.where(lo, zero, xr).astype(bf16)
    return (h0_lo, h0_hi), (h1_lo, h1_hi)


def _softmax_sink(s, sink):
    m = jnp.maximum(jnp.max(s, axis=-1, keepdims=True), sink)
    p = jnp.exp2(s - m)
    denom = jnp.sum(p, axis=-1, keepdims=True) + jnp.exp2(sink - m)
    return p, 1.0 / denom


def _mix_prompt_body(sink_ref, x_ref, npre_ref, win_ref, wup_ref, bg_ref, gn_ref,
                     wba_ref, wbb_ref, wout_ref, npost_ref,
                     y_ref, st_out_ref, k_out_ref, v_out_ref,
                     st_scr, kcat_scr, vcat_scr, oa_scr, ob_scr, gate_scr, inter_scr, *, T):
    i = pl.program_id(0)
    W = WINDOW
    C = GLA_CHUNK

    @pl.when(i == 0)
    def _():
        st_scr[...] = jnp.zeros_like(st_scr)
        kcat_scr[0:W, :] = jnp.zeros((W, SWA_KV), f32)
        vcat_scr[0:W, :] = jnp.zeros((W, SWA_KV), f32)

    @pl.when(i > 0)
    def _():
        kcat_scr[0:W, :] = kcat_scr[T:T + W, :]
        vcat_scr[0:W, :] = vcat_scr[T:T + W, :]

    x = x_ref[...]
    rms_f = lax.rsqrt(jnp.mean(x * x, axis=-1, keepdims=True) + EPS)
    h = (x * npre_ref[...]).astype(bf16)
    rms_b = {n: jnp.broadcast_to(rms_f, (T, n)) for n in (LANES, 2 * LANES)}

    def proj(c0, n):
        w = 2 * LANES if n % (2 * LANES) == 0 else LANES
        return jnp.concatenate([_dot(h, win_ref[:, c:c + w]) * rms_b[w] for c in range(c0, c0 + n, w)], axis=1)


    qa = proj(C_QA, GLA_K)
    ka = proj(C_KA, GLA_K)
    va_b = proj(C_VA, GLA_V).astype(bf16)
    la = _log_decay(proj(C_RA, RA_PAD), wup_ref, bg_ref)
    b = _chunk_cumsum(la, C)
    decay_floor = jnp.min(b)
    kcat_scr[W:W + T, :] = proj(C_KB, SWA_KV)
    vcat_scr[W:W + T, :] = proj(C_VB, SWA_KV)
    qb = (proj(C_QB, SWA_Q) * SWA_Q_SCALE).astype(bf16)

    qe = qa * jnp.exp(b) * (GLA_DK ** -0.5)
    ke = (ka * jnp.exp(-b)).astype(bf16)
    even = _even_head_lanes((T, GLA_K))
    qe_even = jnp.where(even, qe, 0.0).astype(bf16)
    qe_odd = jnp.where(even, 0.0, qe).astype(bf16)
    k_var = _kv_variants(kcat_scr[...])
    v_var = _kv_variants(vcat_scr[...])

    r2 = lax.broadcasted_iota(jnp.int32, (2 * C, 2 * C), 0)
    c2 = lax.broadcasted_iota(jnp.int32, (2 * C, 2 * C), 1)
    pair_causal = ((r2 // C) == (c2 // C)) & ((c2 % C) <= (r2 % C))
    even_c = _even_head_lanes((C, LANES))
    st = [st_scr[:, p * LANES:(p + 1) * LANES] for p in range(GLA_HEADS // 2)]

    def gla_scores(c):
        rows = slice(c * C, (c + 1) * C)
        out = []
        for p in range(GLA_HEADS // 2):
            lanes = slice(p * LANES, (p + 1) * LANES)
            q2 = jnp.concatenate([qe_even[rows, lanes], qe_odd[rows, lanes]], axis=0)
            ke_p = ke[rows, lanes]
            rhs = jnp.concatenate([ke_p, ke_p, st[p].astype(bf16)], axis=0)
            r = _dot_nt(q2, rhs)
            att = jnp.where(pair_causal, r[:, 0:2 * C], 0.0).astype(bf16)
            out.append((att, r[:, 2 * C:]))
        return out

    def gla_update(c, sc):
        rows = slice(c * C, (c + 1) * C)
        b_c = b[rows]
        bl = b_c[C - 1:C, :]
        kl = ka[rows] * jnp.exp(bl - b_c)
        ebl = jnp.exp(bl)
        for p in range(GLA_HEADS // 2):
            lanes = slice(p * LANES, (p + 1) * LANES)
            att, inter = sc[p]
            v2 = jnp.concatenate(
                [va_b[rows, (2 * p) * GLA_DV:(2 * p + 1) * GLA_DV],
                 va_b[rows, (2 * p + 1) * GLA_DV:(2 * p + 2) * GLA_DV]], axis=0)
            o2 = inter + _dot(att, v2)
            for e in range(2):
                hl = slice((2 * p + e) * GLA_DV, (2 * p + e + 1) * GLA_DV)
                oa_scr[rows, hl] = o2[e * C:(e + 1) * C]
                inter_scr[rows, hl] = inter[e * C:(e + 1) * C]
            kl_p = kl[:, lanes]
            kl_stack = jnp.concatenate(
                [jnp.where(even_c, kl_p, 0.0), jnp.where(even_c, 0.0, kl_p)], axis=0).astype(bf16)
            st[p] = st[p] * ebl[:, lanes] + _dot_tn(v2, kl_stack)

    qi = lax.broadcasted_iota(jnp.int32, (W, 2 * W), 0)
    kc = lax.broadcasted_iota(jnp.int32, (W, 2 * W), 1)
    rel = qi + W - kc
    relf = rel.astype(f32)
    in_window = (rel >= 0) & (rel < W)

    def swa_probs(j, kv):
        qrows = slice(j * W, (j + 1) * W)
        band = slice(j * W, j * W + 2 * W)
        if j == 0:
            mask = in_window & ((kc >= W) | (i > 0))
        else:
            mask = in_window
        pairs = (2 * kv, 2 * kv + 1)
        q2 = jnp.concatenate([qb[qrows, p * LANES:(p + 1) * LANES] for p in pairs], axis=0)
        out = []
        for e in range(2):
            s2 = _dot_nt(q2, k_var[kv][e][band])
            probs = []
            for half, p in enumerate(pairs):
                hd = 2 * p + e
                s = s2[half * W:(half + 1) * W]
                s = jnp.where(mask, s - _alibi_slope(hd) * relf, -jnp.inf)
                pr, inv = _softmax_sink(s, sink_ref[hd] * LOG2E)
                probs.append((pr * inv).astype(bf16))
            out.append(jnp.concatenate(probs, axis=0))
        return out

    def swa_out(j, kv, probs):
        qrows = slice(j * W, (j + 1) * W)
        band = slice(j * W, j * W + 2 * W)
        o2 = _dot(probs[0], v_var[kv][0][band]) + _dot(probs[1], v_var[kv][1][band])
        for half, p in enumerate((2 * kv, 2 * kv + 1)):
            ob_scr[qrows, p * LANES:(p + 1) * LANES] = o2[half * W:(half + 1) * W]

    n_chunks = T // C
    assert n_chunks == (T // W) * SWA_KV_HEADS
    gw = 2 * D_MODEL // n_chunks
    for idx in range(n_chunks):
        j, kv = idx // SWA_KV_HEADS, idx % SWA_KV_HEADS
        probs = swa_probs(j, kv)
        sc = gla_scores(idx)
        gate_scr[:, idx * gw:(idx + 1) * gw] = proj(C_GTA + idx * gw, gw)
        gla_update(idx, sc)
        swa_out(j, kv, probs)
    for p in range(GLA_HEADS // 2):
        st_scr[:, p * LANES:(p + 1) * LANES] = st[p]
    ga = proj(C_GA, GLA_V)
    gated_b = jax.nn.sigmoid(gate_scr[:, D_MODEL:2 * D_MODEL]) * _dot(ob_scr[...].astype(bf16), wbb_ref[...])
    sig_a = jax.nn.sigmoid(gate_scr[:, 0:D_MODEL])

    def finish(oa_raw):
        oa = _gla_out_norm(oa_raw, gn_ref, ga)
        merged = sig_a * _dot(oa.astype(bf16), wba_ref[...]) + gated_b
        m = _dot(merged.astype(bf16), wout_ref[...])
        y_ref[...] = x + _rms(m, npost_ref[...])

    finish(oa_scr[...])

    @pl.when(decay_floor < -GLA_SAFE_DECAY)
    def _():
        qs = qa * (GLA_DK ** -0.5)
        va_f = va_b.astype(f32)
        pos = lax.broadcasted_iota(jnp.int32, (T, 1), 0) % C
        er = lax.broadcasted_iota(jnp.int32, (GLA_K, GLA_V), 0)
        ec = lax.broadcasted_iota(jnp.int32, (GLA_K, GLA_V), 1)
        expand = jnp.where((er // GLA_DK) == (ec // GLA_DV), 1.0, 0.0).astype(bf16)

        def offset_term(d, acc):
            valid = pos >= d
            expo = jnp.where(valid, b - pltpu.roll(b, d, 0), 0.0)
            prod = jnp.where(valid, qs * pltpu.roll(ka, d, 0) * jnp.exp(expo), 0.0)
            return acc + _dot(prod.astype(bf16), expand) * pltpu.roll(va_f, d, 0)

        intra = lax.fori_loop(0, C, offset_term, jnp.zeros((T, GLA_V), f32))
        finish(inter_scr[...] + intra)

    @pl.when(i == pl.num_programs(0) - 1)
    def _():
        st_out_ref[...] = st_scr[...].T
        k_out_ref[...] = kcat_scr[T:T + W, :].T
        v_out_ref[...] = vcat_scr[T:T + W, :].T


def _const_spec(shape):
    nd = len(shape)
    return pl.BlockSpec(shape, lambda i: (0,) * nd, pipeline_mode=pl.Buffered(1))


def _mix_prompt(x, sinks, npre, win, wup, bg, gn, wba, wbb, wout, npost, *, T):
    L = x.shape[0]
    nb = L // T
    body = functools.partial(_mix_prompt_body, T=T)
    return pl.pallas_call(
        body,
        grid=(nb,),
        in_specs=[
            pl.BlockSpec(memory_space=pltpu.SMEM),
            pl.BlockSpec((T, D_MODEL), lambda i: (i, 0)),
            _const_spec(npre.shape), _const_spec(win.shape), _const_spec(wup.shape),
            _const_spec(bg.shape), _const_spec(gn.shape), _const_spec(wba.shape),
            _const_spec(wbb.shape), _const_spec(wout.shape), _const_spec(npost.shape),
        ],
        out_specs=[
            pl.BlockSpec((T, D_MODEL), lambda i: (i, 0)),
            pl.BlockSpec((GLA_K, GLA_DV), lambda i: (0, 0)),
            pl.BlockSpec((WINDOW, SWA_KV), lambda i: (0, 0)),
            pl.BlockSpec((WINDOW, SWA_KV), lambda i: (0, 0)),
        ],
        out_shape=[
            jax.ShapeDtypeStruct((L, D_MODEL), f32),
            jax.ShapeDtypeStruct((GLA_K, GLA_DV), f32),
            jax.ShapeDtypeStruct((WINDOW, SWA_KV), f32),
            jax.ShapeDtypeStruct((WINDOW, SWA_KV), f32),
        ],
        scratch_shapes=[
            pltpu.VMEM((GLA_DV, GLA_K), f32),
            pltpu.VMEM((T + WINDOW, SWA_KV), f32),
            pltpu.VMEM((T + WINDOW, SWA_KV), f32),
            pltpu.VMEM((T, GLA_V), f32),
            pltpu.VMEM((T, SWA_Q), f32),
            pltpu.VMEM((T, 2 * D_MODEL), f32),
            pltpu.VMEM((T, GLA_V), f32),
        ],
        compiler_params=pltpu.CompilerParams(
            dimension_semantics=("arbitrary",), vmem_limit_bytes=VMEM_LIMIT),
        name="mix_prompt",
    )(sinks, x, npre, win, wup, bg, gn, wba, wbb, wout, npost)


def _pre_sample_body(x_ref, npre_ref, win_ref, wup_ref, bg_ref,
                     qe_ref, kl_ref, e3_ref, oin_ref, va_ref, ga_ref, qb_ref, kb_ref, vb_ref,
                     gta_ref, gtb_ref, x_scr, *, NB, NT):
    for t in range(NT):
        x_scr[t * NB:(t + 1) * NB, :] = x_ref[:, t, :]
    h = _rms(x_scr[...], npre_ref[...]).astype(bf16)

    def proj(c0, n):
        return _dot(h, win_ref[:, c0:c0 + n])

    def blk(val, t):
        return val[t * NB:(t + 1) * NB, :]

    ga_ref[...] = proj(C_GA, GLA_V)
    qb_ref[...] = proj(C_QB, SWA_Q) * SWA_Q_SCALE
    kb_ref[...] = proj(C_KB, SWA_KV)
    vb_ref[...] = proj(C_VB, SWA_KV)
    gta_ref[...] = proj(C_GTA, D_MODEL)
    gtb_ref[...] = proj(C_GTB, D_MODEL)
    va = proj(C_VA, GLA_V)
    va_ref[...] = va

    qa = proj(C_QA, GLA_K) * (GLA_DK ** -0.5)
    ka = proj(C_KA, GLA_K)
    la = _log_decay(proj(C_RA, RA_PAD), wup_ref, bg_ref)
    b = [blk(la, 0)]
    for t in range(1, NT):
        b.append(b[-1] + blk(la, t))
    e3_ref[...] = jnp.exp(b[NT - 1])
    for t in range(NT):
        qe_ref[t * NB:(t + 1) * NB, :] = blk(qa, t) * jnp.exp(b[t])
        kl_ref[t * NB:(t + 1) * NB, :] = blk(ka, t) * jnp.exp(b[NT - 1] - b[t])
    pairs = [(t, j) for t in range(NT) for j in range(t + 1)]
    prods = [(blk(qa, t) * blk(ka, j) * jnp.exp(b[t] - b[j])).astype(bf16) for t, j in pairs]
    r = lax.broadcasted_iota(jnp.int32, (GLA_K, GLA_V), 0)
    c = lax.broadcasted_iota(jnp.int32, (GLA_K, GLA_V), 1)
    expand = jnp.where((r // GLA_DK) == (c // GLA_DV), 1.0, 0.0).astype(bf16)
    att = _dot(jnp.concatenate(prods, axis=0), expand)
    for t in range(NT):
        acc = None
        for idx, (tt, j) in enumerate(pairs):
            if tt != t:
                continue
            term = att[idx * NB:(idx + 1) * NB, :] * blk(va, j)
            acc = term if acc is None else acc + term
        oin_ref[t * NB:(t + 1) * NB, :] = acc


def _pre_sample(xs, npre, win, wup, bg):
    NB, NT, _ = xs.shape
    body = functools.partial(_pre_sample_body, NB=NB, NT=NT)
    widths = (GLA_K, GLA_K, None, GLA_V, GLA_V, GLA_V, SWA_Q, SWA_KV, SWA_KV, D_MODEL, D_MODEL)
    out_shape = [jax.ShapeDtypeStruct((NB, GLA_K) if w is None else (NT * NB, w), f32) for w in widths]
    return pl.pallas_call(
        body,
        out_shape=out_shape,
        scratch_shapes=[pltpu.VMEM((NB * NT, D_MODEL), f32)],
        compiler_params=pltpu.CompilerParams(vmem_limit_bytes=VMEM_LIMIT),
        name="pre_sample",
    )(xs, npre, win, wup, bg)


def _state_sample_body(sink_ref, qe_ref, kl_ref, e3_ref, oin_ref, va_ref, qb_ref, kb_ref, vb_ref,
                       s0_ref, kt_ref, vt_ref,
                       oa_ref, ob_ref, s1_ref, kt1_ref, vt1_ref, *, BB, NT):
    W = WINDOW
    SK = 2 * W
    HT = GLA_HEADS * NT
    HALF = SWA_HD
    hr = lax.broadcasted_iota(jnp.int32, (HT, GLA_K), 0) // NT
    hc = lax.broadcasted_iota(jnp.int32, (HT, GLA_K), 1) // GLA_DK
    own_head = hr == hc
    ones_rows = jnp.ones((16, GLA_DV), bf16)
    zero_rows = jnp.zeros((16, GLA_DV), bf16)
    zero_ht = jnp.zeros((HT, GLA_DV), bf16)
    G2 = 2 * NT
    row = lax.broadcasted_iota(jnp.int32, (G2, SK), 0)
    col = lax.broadcasted_iota(jnp.int32, (G2, SK), 1)
    rel = (row % NT) + W - col
    relf = rel.astype(f32)
    smask = (rel >= 0) & (rel < W)
    first_pair = lax.broadcasted_iota(jnp.int32, (G2, 1), 0) < NT
    pad_rows = jnp.zeros((8 - NT, SWA_KV), f32)
    pad_lanes = jnp.zeros((SWA_KV, SK - W - 8), f32)
    zero_half = jnp.zeros((HALF, SK), bf16)

    def head_variants(cat_t, kv):
        blk = cat_t[kv * HALF:(kv + 1) * HALF]
        return (jnp.concatenate([blk, zero_half], axis=0), jnp.concatenate([zero_half, blk], axis=0))

    pending = []
    for bi in range(BB):
        s0 = s0_ref[bi]
        q4 = qe_ref[:, bi, :]
        qm = jnp.where(own_head, jnp.concatenate([q4] * GLA_HEADS, axis=0), 0.0).astype(bf16)
        o_inter = _dot(qm, s0.astype(bf16))
        for hd in range(GLA_HEADS):
            lanes = slice(hd * GLA_DV, (hd + 1) * GLA_DV)
            oa_ref[:, bi, lanes] = o_inter[hd * NT:(hd + 1) * NT, :] + oin_ref[:, bi, lanes]
        k4 = kl_ref[:, bi, :]
        km = jnp.where(own_head, jnp.concatenate([k4] * GLA_HEADS, axis=0), 0.0).astype(bf16)
        e = e3_ref[bi:bi + 1, :]
        e_hi = e.astype(bf16)
        r1 = e - e_hi.astype(f32)
        e_mid = r1.astype(bf16)
        e_lo = (r1 - e_mid.astype(f32)).astype(bf16)
        e_rows = jnp.concatenate([e_hi, e_mid, e_lo, jnp.zeros((13, GLA_K), bf16)], axis=0)
        lhs = jnp.concatenate([km, e_rows], axis=0)
        v4 = va_ref[:, bi, :].astype(bf16)
        vrep = jnp.concatenate([v4[:, hd * GLA_DV:(hd + 1) * GLA_DV] for hd in range(GLA_HEADS)], axis=0)
        rhs = jnp.concatenate([jnp.concatenate([vrep, zero_ht], axis=1),
                               jnp.concatenate([zero_rows, ones_rows], axis=1)], axis=0)
        res = _dot_tn(lhs, rhs)
        s1_ref[bi] = res[:, GLA_DV:] * s0 + res[:, :GLA_DV]

        kt = kt_ref[bi]
        vt = vt_ref[bi]
        knew_t = jnp.concatenate([kb_ref[:, bi, :], pad_rows], axis=0).T
        vnew_t = jnp.concatenate([vb_ref[:, bi, :], pad_rows], axis=0).T
        kt1_ref[bi] = jnp.concatenate([kt[:, NT:], knew_t[:, 0:NT]], axis=1)
        vt1_ref[bi] = jnp.concatenate([vt[:, NT:], vnew_t[:, 0:NT]], axis=1)
        kcat = jnp.concatenate([kt, knew_t, pad_lanes], axis=1).astype(bf16)
        vcat = jnp.concatenate([vt, vnew_t, pad_lanes], axis=1).astype(bf16)
        q4b = qb_ref[:, bi, :].astype(bf16)
        for kv in range(SWA_KV_HEADS):
            p0 = 2 * kv
            q8 = jnp.concatenate([q4b[:, p0 * LANES:(p0 + 1) * LANES],
                                  q4b[:, (p0 + 1) * LANES:(p0 + 2) * LANES]], axis=0)
            scores = [_dot(q8, kvar) for kvar in head_variants(kcat, kv)]
            pending.append((bi, kv, scores, head_variants(vcat, kv)))

    for bi, kv, scores, v_vars in pending:
        p0 = 2 * kv
        o8_t = None
        for e_ in range(2):
            h_first = 2 * p0 + e_
            h_second = 2 * (p0 + 1) + e_
            slope = jnp.where(first_pair, _alibi_slope(h_first), _alibi_slope(h_second))
            sink = jnp.where(first_pair, sink_ref[h_first] * LOG2E, sink_ref[h_second] * LOG2E)
            s = jnp.where(smask, scores[e_] - slope * relf, -jnp.inf)
            pr, inv = _softmax_sink(s, sink)
            o_t = _dot_nt(v_vars[e_], (pr * inv).astype(bf16))
            o8_t = o_t if o8_t is None else o8_t + o_t
        o8 = o8_t.T
        ob_ref[:, bi, p0 * LANES:(p0 + 1) * LANES] = o8[0:NT, :]
        ob_ref[:, bi, (p0 + 1) * LANES:(p0 + 2) * LANES] = o8[NT:2 * NT, :]


def _state_sample(sinks, qe, kl, e3, oin, va, qb, kb, vb, s0, kt, vt, *, NT, BB):
    NBS = s0.shape[0]
    assert NBS % BB == 0
    body = functools.partial(_state_sample_body, BB=BB, NT=NT)

    def tm(a):
        return a.reshape(NT, NBS, a.shape[-1])

    def rows(n):
        return pl.BlockSpec((NT, BB, n), lambda i: (0, i, 0))

    def per_seq(shape):
        return pl.BlockSpec((BB,) + shape, lambda i: (i, 0, 0))

    oa, ob, s1, kt1, vt1 = pl.pallas_call(
        body,
        grid=(NBS // BB,),
        in_specs=[
            pl.BlockSpec(memory_space=pltpu.SMEM),
            rows(GLA_K), rows(GLA_K), pl.BlockSpec((BB, GLA_K), lambda i: (i, 0)),
            rows(GLA_V), rows(GLA_V), rows(SWA_Q), rows(SWA_KV), rows(SWA_KV),
            per_seq((GLA_K, GLA_DV)), per_seq((SWA_KV, WINDOW)), per_seq((SWA_KV, WINDOW)),
        ],
        out_specs=[
            rows(GLA_V), rows(SWA_Q),
            per_seq((GLA_K, GLA_DV)), per_seq((SWA_KV, WINDOW)), per_seq((SWA_KV, WINDOW)),
        ],
        out_shape=[
            jax.ShapeDtypeStruct((NT, NBS, GLA_V), f32),
            jax.ShapeDtypeStruct((NT, NBS, SWA_Q), f32),
            jax.ShapeDtypeStruct((NBS, GLA_K, GLA_DV), f32),
            jax.ShapeDtypeStruct((NBS, SWA_KV, WINDOW), f32),
            jax.ShapeDtypeStruct((NBS, SWA_KV, WINDOW), f32),
        ],
        compiler_params=pltpu.CompilerParams(
            dimension_semantics=("arbitrary",), vmem_limit_bytes=VMEM_LIMIT),
        name="state_sample",
    )(sinks, tm(qe), tm(kl), e3, tm(oin), tm(va), tm(qb), tm(kb), tm(vb), s0, kt, vt)
    return oa.reshape(NT * NBS, GLA_V), ob.reshape(NT * NBS, SWA_Q), s1, kt1, vt1


def _post_sample_body(x_ref, oa_ref, ga_ref, ob_ref, gta_ref, gtb_ref, gn_ref,
                      wba_ref, wbb_ref, wout_ref, npost_ref, y_ref, x_scr, *, NB, NT):
    for t in range(NT):
        x_scr[t * NB:(t + 1) * NB, :] = x_ref[:, t, :]
    oa = _gla_out_norm(oa_ref[...], gn_ref, ga_ref[...])
    y_ref[...] = _mix_tail(x_scr[...], oa, ob_ref[...], gta_ref[...], gtb_ref[...],
                           wba_ref, wbb_ref, wout_ref, npost_ref)


def _post_sample(xs, oa, ga, ob, gta, gtb, gn, wba, wbb, wout, npost):
    NB, NT, _ = xs.shape
    return pl.pallas_call(
        functools.partial(_post_sample_body, NB=NB, NT=NT),
        out_shape=jax.ShapeDtypeStruct((NT * NB, D_MODEL), f32),
        scratch_shapes=[pltpu.VMEM((NT * NB, D_MODEL), f32)],
        compiler_params=pltpu.CompilerParams(vmem_limit_bytes=VMEM_LIMIT),
        name="post_sample",
    )(xs, oa, ga, ob, gta, gtb, gn, wba, wbb, wout, npost)


FFN_COLS = 256


def _ffn_columns(h, wffn_ref, cw_ref, cb_ref, up_scr, y_scr, *, T, base, shift):
    for c0 in range(0, D_FF, FFN_COLS):
        cols = slice(c0, c0 + FFN_COLS)
        u = _dot(h, wffn_ref[:, c0:c0 + FFN_COLS])
        g = _dot(h, wffn_ref[:, D_FF + c0:D_FF + c0 + FFN_COLS])
        up_scr[base:base + T, cols] = u
        u1 = up_scr[base - shift:base - shift + T, cols]
        u2 = up_scr[base - 2 * shift:base - 2 * shift + T, cols]
        cv = (cb_ref[:, cols] + cw_ref[2:3, cols] * u + cw_ref[1:2, cols] * u1 + cw_ref[0:1, cols] * u2)
        y_scr[:, cols] = (_gelu_tanh(cv) * g).astype(bf16)


def _ffn_prompt_body(x_ref, npre_ref, wffn_ref, cw_ref, cb_ref, wo_ref, npost_ref,
                     y_ref, conv_out_ref, up_scr, y_scr, *, T):
    i = pl.program_id(0)
    base = 8

    @pl.when(i == 0)
    def _():
        up_scr[0:base, :] = jnp.zeros((base, D_FF), f32)

    @pl.when(i > 0)
    def _():
        up_scr[0:base, :] = up_scr[T:T + base, :]

    x = x_ref[...]
    h = _rms(x, npre_ref[...]).astype(bf16)
    _ffn_columns(h, wffn_ref, cw_ref, cb_ref, up_scr, y_scr, T=T, base=base, shift=1)
    f = _dot(y_scr[...], wo_ref[...])
    y_ref[...] = x + _rms(f, npost_ref[...])

    @pl.when(i == pl.num_programs(0) - 1)
    def _():
        conv_out_ref[...] = up_scr[T:T + base, :]


def _ffn_prompt(x, npre, wffn, cw, cb, wo, npost, *, T):
    L = x.shape[0]
    body = functools.partial(_ffn_prompt_body, T=T)
    return pl.pallas_call(
        body,
        grid=(L // T,),
        in_specs=[
            pl.BlockSpec((T, D_MODEL), lambda i: (i, 0)),
            _const_spec(npre.shape), _const_spec(wffn.shape), _const_spec(cw.shape),
            _const_spec(cb.shape), _const_spec(wo.shape), _const_spec(npost.shape),
        ],
        out_specs=[
            pl.BlockSpec((T, D_MODEL), lambda i: (i, 0)),
            pl.BlockSpec((8, D_FF), lambda i: (0, 0)),
        ],
        out_shape=[
            jax.ShapeDtypeStruct((L, D_MODEL), f32),
            jax.ShapeDtypeStruct((8, D_FF), f32),
        ],
        scratch_shapes=[
            pltpu.VMEM((T + 8, D_FF), f32),
            pltpu.VMEM((T, D_FF), bf16),
        ],
        compiler_params=pltpu.CompilerParams(
            dimension_semantics=("arbitrary",), vmem_limit_bytes=VMEM_LIMIT),
        name="ffn_prompt",
    )(x, npre, wffn, cw, cb, wo, npost)


def _ffn_sample_body(x_ref, cst_ref, npre_ref, wffn_ref, cw_ref, cb_ref, wo_ref, npost_ref,
                     y_ref, conv_out_ref, up_scr, y_scr, *, NB, NT):
    T = NB * NT
    for t in range(CONV_W - 1):
        up_scr[t * NB:(t + 1) * NB, :] = cst_ref[:, t, :]
    base = (CONV_W - 1) * NB
    x = x_ref[...]
    h = _rms(x, npre_ref[...]).astype(bf16)
    _ffn_columns(h, wffn_ref, cw_ref, cb_ref, up_scr, y_scr, T=T, base=base, shift=NB)
    f = _dot(y_scr[...], wo_ref[...])
    y = x + _rms(f, npost_ref[...])
    for t in range(NT):
        y_ref[:, t, :] = y[t * NB:(t + 1) * NB, :]
    for t in range(CONV_W - 1):
        conv_out_ref[:, t, :] = up_scr[T + t * NB:T + (t + 1) * NB, :]


def _ffn_sample(x, cst, npre, wffn, cw, cb, wo, npost):
    NB = cst.shape[0]
    T = x.shape[0]
    NT = T // NB
    body = functools.partial(_ffn_sample_body, NB=NB, NT=NT)
    return pl.pallas_call(
        body,
        out_shape=[
            jax.ShapeDtypeStruct((NB, NT, D_MODEL), f32),
            jax.ShapeDtypeStruct((NB, CONV_W - 1, D_FF), f32),
        ],
        scratch_shapes=[
            pltpu.VMEM((T + (CONV_W - 1) * NB, D_FF), f32),
            pltpu.VMEM((T, D_FF), bf16),
        ],
        compiler_params=pltpu.CompilerParams(vmem_limit_bytes=VMEM_LIMIT),
        name="ffn_sample",
    )(x, cst, npre, wffn, cw, cb, wo, npost)


def _prep_w_in_body(*refs, n_plain, per_step):
    o_ref = refs[-1]
    j = pl.program_id(0)
    for k, wt_ref in enumerate(refs[:-1]):
        x = wt_ref[...]
        r = lax.broadcasted_iota(jnp.int32, x.shape, 0)
        x = jnp.where((j * per_step + k < n_plain) | (r < GLA_RANK), x, 0.0)
        o_ref[:, k * PREP_ROWS:(k + 1) * PREP_ROWS] = x.T.astype(bf16)


def _prep_w_in(w_in):
    d_in, n_cols = w_in.shape
    head = C_GA
    tail_src = head + GLA_RANK
    n_head = head // PREP_ROWS
    n_tail = (n_cols - tail_src) // PREP_ROWS
    assert head % PREP_ROWS == 0 and (n_cols - tail_src) % PREP_ROWS == 0
    n_plain = n_head + n_tail
    assert C_RA == n_plain * PREP_ROWS
    per_step = 2
    assert (n_plain + 1) % per_step == 0

    def row_off(blk):
        off = jnp.where(blk < n_head, blk * PREP_ROWS,
                        jnp.where(blk < n_plain, tail_src + (blk - n_head) * PREP_ROWS, head))
        return pl.multiple_of(off, 8)

    def in_spec(k):
        return pl.BlockSpec((pl.Element(PREP_ROWS), pl.Element(d_in)), lambda j: (row_off(j * per_step + k), 0))

    wt = jnp.swapaxes(w_in, 0, 1)
    return pl.pallas_call(
        functools.partial(_prep_w_in_body, n_plain=n_plain, per_step=per_step),
        grid=((n_plain + 1) // per_step,),
        in_specs=[in_spec(k) for k in range(per_step)],
        out_specs=pl.BlockSpec((d_in, per_step * PREP_ROWS), lambda j: (0, j)),
        out_shape=jax.ShapeDtypeStruct((d_in, (n_plain + 1) * PREP_ROWS), bf16),
        compiler_params=pltpu.CompilerParams(dimension_semantics=("arbitrary",)),
        name="prep_w_in",
    )(*([wt] * per_step))


def kernel(x_prompt, x_sample, state_gla, cache_swa_k, cache_swa_v, state_ffn_conv, norm_mix_pre, norm_mix_post, w_in, w_gate_up, b_gate, gla_norm, sinks, w_branch_a, w_branch_b, w_out, norm_ffn_pre, norm_ffn_post, w_ffn_in, conv_w, conv_b, w_ffn_out):
    depth = w_in.shape[0]
    assert depth == 1
    l = 0
    B, L, _ = x_prompt.shape
    assert B == 1
    NBS, NT, _ = x_sample.shape

    win = _prep_w_in(w_in[l])
    wup = jnp.zeros((RA_PAD, GLA_K), f32).at[:GLA_RANK].set(w_gate_up[l]).astype(bf16)
    bg = b_gate[l].reshape(1, GLA_K)
    gn = gla_norm[l].reshape(1, GLA_DV)
    npre = norm_mix_pre[l].reshape(1, D_MODEL)
    npost = norm_mix_post[l].reshape(1, D_MODEL)
    wba = w_branch_a[l].astype(bf16)
    wbb = w_branch_b[l].astype(bf16)
    wout = w_out[l].astype(bf16)
    fpre = norm_ffn_pre[l].reshape(1, D_MODEL)
    fpost = norm_ffn_post[l].reshape(1, D_MODEL)
    wffn = w_ffn_in[l].astype(bf16)
    cw = conv_w[l]
    cb = conv_b[l].reshape(1, D_FF)
    wo = w_ffn_out[l].astype(bf16)
    sk = sinks[l]

    x1, st_p, k_p, v_p = _mix_prompt(x_prompt[0], sk, npre, win, wup, bg, gn, wba, wbb, wout, npost, T=256)
    y_p, conv_p = _ffn_prompt(x1, fpre, wffn, cw, cb, wo, fpost, T=512)

    y_prompt = y_p[None]
    gla_state_prompt = st_p.reshape(1, 1, GLA_HEADS, GLA_DK, GLA_DV)
    swa_k_prompt = jnp.transpose(k_p.reshape(SWA_KV_HEADS, SWA_HD, WINDOW), (2, 0, 1))[None, None]
    swa_v_prompt = jnp.transpose(v_p.reshape(SWA_KV_HEADS, SWA_HD, WINDOW), (2, 0, 1))[None, None]
    conv_prompt = conv_p[8 - (CONV_W - 1):].reshape(1, 1, CONV_W - 1, D_FF)

    qe, kl, e3, oin, va, ga, qb, kb, vb, gta, gtb = _pre_sample(x_sample, npre, win, wup, bg)
    kt = jnp.transpose(cache_swa_k[l], (0, 2, 3, 1)).reshape(NBS, SWA_KV, WINDOW)
    vt = jnp.transpose(cache_swa_v[l], (0, 2, 3, 1)).reshape(NBS, SWA_KV, WINDOW)
    oa_raw, ob, s1, kt1, vt1 = _state_sample(
        sk, qe, kl, e3, oin, va, qb, kb, vb, state_gla[l].reshape(NBS, GLA_K, GLA_DV), kt, vt, NT=NT, BB=16)
    x1s = _post_sample(x_sample, oa_raw, ga, ob, gta, gtb, gn, wba, wbb, wout, npost)
    y_sample, conv_s = _ffn_sample(x1s, state_ffn_conv[l], fpre, wffn, cw, cb, wo, fpost)

    def cache_out(t):
        return jnp.transpose(t.reshape(NBS, SWA_KV_HEADS, SWA_HD, WINDOW), (0, 3, 1, 2))[None]

    gla_state_sample = s1.reshape(1, NBS, GLA_HEADS, GLA_DK, GLA_DV)
    swa_k_sample = cache_out(kt1)
    swa_v_sample = cache_out(vt1)
    conv_sample = conv_s[None]
    return (y_prompt, y_sample, gla_state_prompt, gla_state_sample, swa_k_prompt, swa_v_prompt,
            swa_k_sample, swa_v_sample, conv_prompt, conv_sample)
```

```python
import functools

import jax
import jax.numpy as jnp
from jax import lax
from jax.experimental import pallas as pl
from jax.experimental.pallas import tpu as pltpu

f32 = jnp.float32
bf16 = jnp.bfloat16

D_MODEL = 1024
GLA_HEADS = 4
GLA_DK = 64
GLA_DV = 128
GLA_RANK = 16
GLA_TAU = 16.0
GLA_CHUNK = 64
GLA_SAFE_DECAY = 60.0
SWA_HEADS = 8
SWA_KV_HEADS = 2
SWA_HD = 64
WINDOW = 128
D_FF = 2816
CONV_W = 3
EPS = 1e-6
GLA_K = GLA_HEADS * GLA_DK
GLA_V = GLA_HEADS * GLA_DV
SWA_Q = SWA_HEADS * SWA_HD
SWA_KV = SWA_KV_HEADS * SWA_HD
LANES = 128
LOG2E = 1.4426950408889634

C_QA = 0
C_KA = C_QA + GLA_K
C_VA = C_KA + GLA_K
C_GA = C_VA + GLA_V
C_QB = C_GA + GLA_V
C_KB = C_QB + SWA_Q
C_VB = C_KB + SWA_KV
C_GTA = C_VB + SWA_KV
C_GTB = C_GTA + D_MODEL
C_RA = C_GTB + D_MODEL
RA_PAD = LANES
PREP_ROWS = 256
IN_COLS_PAD = C_RA + PREP_ROWS

VMEM_LIMIT = 56 * 1024 * 1024


def _dot(a, b):
    return jnp.dot(a, b, preferred_element_type=f32)


def _dot_nt(a, b):
    return lax.dot_general(a, b, (((1,), (1,)), ((), ())), preferred_element_type=f32)


def _dot_tn(a, b):
    return lax.dot_general(a, b, (((0,), (0,)), ((), ())), preferred_element_type=f32)


def _rms(x, w):
    return x * lax.rsqrt(jnp.mean(x * x, axis=-1, keepdims=True) + EPS) * w


def _gelu_tanh(x):
    k = -2.0 * 0.7978845608028654 * LOG2E
    return x / (1.0 + jnp.exp2(x * (k + (k * 0.044715) * (x * x))))


def _split_hi_lo(x):
    hi = x.astype(bf16)
    lo = (x - hi.astype(f32)).astype(bf16)
    return hi, lo


def _log_decay(ra, wup_ref, bg_ref):
    xg = _dot(ra.astype(bf16), wup_ref[...]) + bg_ref[...]
    return jax.nn.log_sigmoid(xg) * (1.0 / GLA_TAU)


def _chunk_cumsum(la, chunk):
    n = la.shape[0]
    r = lax.broadcasted_iota(jnp.int32, (n, n), 0)
    c = lax.broadcasted_iota(jnp.int32, (n, n), 1)
    tri = jnp.where((c <= r) & ((r // chunk) == (c // chunk)), 1.0, 0.0).astype(bf16)
    hi, lo = _split_hi_lo(la)
    return _dot(tri, hi) + _dot(tri, lo)


def _even_head_lanes(shape):
    lane = lax.broadcasted_iota(jnp.int32, shape, len(shape) - 1)
    return (lane % LANES) < GLA_DK


def _gla_out_norm(o, gn_ref, ga):
    outs = []
    for h in range(GLA_HEADS):
        oh = o[:, h * GLA_DV:(h + 1) * GLA_DV]
        outs.append(_rms(oh, gn_ref[...]))
    on = jnp.concatenate(outs, axis=1)
    return on * (ga * jax.nn.sigmoid(ga))


def _mix_tail(x, oa, ob, gate_a, gate_b, wba_ref, wbb_ref, wout_ref, npost_ref):
    merged = (jax.nn.sigmoid(gate_a) * _dot(oa.astype(bf16), wba_ref[...])
              + jax.nn.sigmoid(gate_b) * _dot(ob.astype(bf16), wbb_ref[...]))
    m = _dot(merged.astype(bf16), wout_ref[...])
    return x + _rms(m, npost_ref[...])


def _alibi_slope(head):
    return LOG2E * 2.0 ** (-(8.0 / SWA_HEADS) * (head + 1))


SWA_Q_SCALE = LOG2E * SWA_HD ** -0.5


def _kv_variants(x):
    lo = _even_head_lanes(x.shape)
    xr = pltpu.roll(x, SWA_HD, 1)
    zero = jnp.zeros_like(x)
    h0_lo = jnp.where(lo, x, zero).astype(bf16)
    h1_hi = jnp.where(lo, zero, x).astype(bf16)
    h1_lo = jnp.where(lo, xr, zero).astype(bf16)
    h0_hi = jnp.where(lo, zero, xr).astype(bf16)
    return (h0_lo, h0_hi), (h1_lo, h1_hi)


def _softmax_sink(s, sink):
    m = jnp.maximum(jnp.max(s, axis=-1, keepdims=True), sink)
    p = jnp.exp2(s - m)
    denom = jnp.sum(p, axis=-1, keepdims=True) + jnp.exp2(sink - m)
    return p, 1.0 / denom


def _mix_prompt_body(sink_ref, x_ref, npre_ref, win_ref, wup_ref, bg_ref, gn_ref,
                     wba_ref, wbb_ref, wout_ref, npost_ref,
                     y_ref, st_out_ref, k_out_ref, v_out_ref,
                     st_scr, kcat_scr, vcat_scr, oa_scr, ob_scr, gate_scr, inter_scr, *, T):
    i = pl.program_id(0)
    W = WINDOW
    C = GLA_CHUNK

    @pl.when(i == 0)
    def _():
        st_scr[...] = jnp.zeros_like(st_scr)
        kcat_scr[0:W, :] = jnp.zeros((W, SWA_KV), f32)
        vcat_scr[0:W, :] = jnp.zeros((W, SWA_KV), f32)

    @pl.when(i > 0)
    def _():
        kcat_scr[0:W, :] = kcat_scr[T:T + W, :]
        vcat_scr[0:W, :] = vcat_scr[T:T + W, :]

    x = x_ref[...]
    rms_f = lax.rsqrt(jnp.mean(x * x, axis=-1, keepdims=True) + EPS)
    h = (x * npre_ref[...]).astype(bf16)
    rms_b = {n: jnp.broadcast_to(rms_f, (T, n)) for n in (LANES, 2 * LANES)}

    def proj(c0, n):
        w = 2 * LANES if n % (2 * LANES) == 0 else LANES
        return jnp.concatenate([_dot(h, win_ref[:, c:c + w]) * rms_b[w] for c in range(c0, c0 + n, w)], axis=1)


    xg = _dot(proj(C_RA, RA_PAD).astype(bf16), wup_ref[...]) + bg_ref[...]
    kcat_scr[W:W + T, :] = proj(C_KB, SWA_KV)
    vcat_scr[W:W + T, :] = proj(C_VB, SWA_KV)
    qb = (proj(C_QB, SWA_Q) * SWA_Q_SCALE).astype(bf16)
    la = jax.nn.log_sigmoid(xg) * (1.0 / GLA_TAU)
    b = _chunk_cumsum(la, C)
    decay_floor = jnp.min(b)
    qa = proj(C_QA, GLA_K)
    ka = proj(C_KA, GLA_K)
    va_b = proj(C_VA, GLA_V).astype(bf16)

    qe = qa * jnp.exp(b) * (GLA_DK ** -0.5)
    ke = (ka * jnp.exp(-b)).astype(bf16)
    even = _even_head_lanes((T, GLA_K))
    qe_even = jnp.where(even, qe, 0.0).astype(bf16)
    qe_odd = jnp.where(even, 0.0, qe).astype(bf16)
    k_var = _kv_variants(kcat_scr[...])
    v_var = _kv_variants(vcat_scr[...])

    r2 = lax.broadcasted_iota(jnp.int32, (2 * C, 2 * C), 0)
    c2 = lax.broadcasted_iota(jnp.int32, (2 * C, 2 * C), 1)
    pair_causal = ((r2 // C) == (c2 // C)) & ((c2 % C) <= (r2 % C))
    even_c = _even_head_lanes((C, LANES))
    st = [st_scr[:, p * LANES:(p + 1) * LANES] for p in range(GLA_HEADS // 2)]

    def gla_scores(c):
        rows = slice(c * C, (c + 1) * C)
        out = []
        for p in range(GLA_HEADS // 2):
            lanes = slice(p * LANES, (p + 1) * LANES)
            q2 = jnp.concatenate([qe_even[rows, lanes], qe_odd[rows, lanes]], axis=0)
            ke_p = ke[rows, lanes]
            rhs = jnp.concatenate([ke_p, ke_p, st[p].astype(bf16)], axis=0)
            r = _dot_nt(q2, rhs)
            att = jnp.where(pair_causal, r[:, 0:2 * C], 0.0).astype(bf16)
            out.append((att, r[:, 2 * C:]))
        return out

    def gla_update(c, sc):
        rows = slice(c * C, (c + 1) * C)
        b_c = b[rows]
        bl = b_c[C - 1:C, :]
        kl = ka[rows] * jnp.exp(bl - b_c)
        ebl = jnp.exp(bl)
        for p in range(GLA_HEADS // 2):
            lanes = slice(p * LANES, (p + 1) * LANES)
            att, inter = sc[p]
            v2 = jnp.concatenate(
                [va_b[rows, (2 * p) * GLA_DV:(2 * p + 1) * GLA_DV],
                 va_b[rows, (2 * p + 1) * GLA_DV:(2 * p + 2) * GLA_DV]], axis=0)
            o2 = inter + _dot(att, v2)
            for e in range(2):
                hl = slice((2 * p + e) * GLA_DV, (2 * p + e + 1) * GLA_DV)
                oa_scr[rows, hl] = o2[e * C:(e + 1) * C]
                inter_scr[rows, hl] = inter[e * C:(e + 1) * C]
            kl_p = kl[:, lanes]
            kl_stack = jnp.concatenate(
                [jnp.where(even_c, kl_p, 0.0), jnp.where(even_c, 0.0, kl_p)], axis=0).astype(bf16)
            st[p] = st[p] * ebl[:, lanes] + _dot_tn(v2, kl_stack)

    qi = lax.broadcasted_iota(jnp.int32, (W, 2 * W), 0)
    kc = lax.broadcasted_iota(jnp.int32, (W, 2 * W), 1)
    rel = qi + W - kc
    relf = rel.astype(f32)
    in_window = (rel >= 0) & (rel < W)

    def swa_probs(j, kv):
        qrows = slice(j * W, (j + 1) * W)
        band = slice(j * W, j * W + 2 * W)
        if j == 0:
            mask = in_window & ((kc >= W) | (i > 0))
        else:
            mask = in_window
        pairs = (2 * kv, 2 * kv + 1)
        q2 = jnp.concatenate([qb[qrows, p * LANES:(p + 1) * LANES] for p in pairs], axis=0)
        out = []
        for e in range(2):
            s2 = _dot_nt(q2, k_var[kv][e][band])
            probs = []
            for half, p in enumerate(pairs):
                hd = 2 * p + e
                s = s2[half * W:(half + 1) * W]
                s = jnp.where(mask, s - _alibi_slope(hd) * relf, -jnp.inf)
                pr, inv = _softmax_sink(s, sink_ref[hd] * LOG2E)
                probs.append((pr * inv).astype(bf16))
            out.append(jnp.concatenate(probs, axis=0))
        return out

    def swa_out(j, kv, probs):
        qrows = slice(j * W, (j + 1) * W)
        band = slice(j * W, j * W + 2 * W)
        o2 = _dot(probs[0], v_var[kv][0][band]) + _dot(probs[1], v_var[kv][1][band])
        for half, p in enumerate((2 * kv, 2 * kv + 1)):
            ob_scr[qrows, p * LANES:(p + 1) * LANES] = o2[half * W:(half + 1) * W]

    n_chunks = T // C
    assert n_chunks == (T // W) * SWA_KV_HEADS
    gw = 2 * D_MODEL // n_chunks
    for idx in range(n_chunks):
        j, kv = idx // SWA_KV_HEADS, idx % SWA_KV_HEADS
        probs = swa_probs(j, kv)
        sc = gla_scores(idx)
        gate_scr[:, idx * gw:(idx + 1) * gw] = proj(C_GTA + idx * gw, gw)
        gla_update(idx, sc)
        swa_out(j, kv, probs)
    for p in range(GLA_HEADS // 2):
        st_scr[:, p * LANES:(p + 1) * LANES] = st[p]
    ga = proj(C_GA, GLA_V)
    gated_b = jax.nn.sigmoid(gate_scr[:, D_MODEL:2 * D_MODEL]) * _dot(ob_scr[...].astype(bf16), wbb_ref[...])
    sig_a = jax.nn.sigmoid(gate_scr[:, 0:D_MODEL])

    def finish(oa_raw):
        oa = _gla_out_norm(oa_raw, gn_ref, ga)
        merged = sig_a * _dot(oa.astype(bf16), wba_ref[...]) + gated_b
        m = _dot(merged.astype(bf16), wout_ref[...])
        y_ref[...] = x + _rms(m, npost_ref[...])

    finish(oa_scr[...])

    @pl.when(decay_floor < -GLA_SAFE_DECAY)
    def _():
        qs = qa * (GLA_DK ** -0.5)
        va_f = va_b.astype(f32)
        pos = lax.broadcasted_iota(jnp.int32, (T, 1), 0) % C
        er = lax.broadcasted_iota(jnp.int32, (GLA_K, GLA_V), 0)
        ec = lax.broadcasted_iota(jnp.int32, (GLA_K, GLA_V), 1)
        expand = jnp.where((er // GLA_DK) == (ec // GLA_DV), 1.0, 0.0).astype(bf16)

        def offset_term(d, acc):
            valid = pos >= d
            expo = jnp.where(valid, b - pltpu.roll(b, d, 0), 0.0)
            prod = jnp.where(valid, qs * pltpu.roll(ka, d, 0) * jnp.exp(expo), 0.0)
            return acc + _dot(prod.astype(bf16), expand) * pltpu.roll(va_f, d, 0)

        intra = lax.fori_loop(0, C, offset_term, jnp.zeros((T, GLA_V), f32))
        finish(inter_scr[...] + intra)

    @pl.when(i == pl.num_programs(0) - 1)
    def _():
        st_out_ref[...] = st_scr[...].T
        k_out_ref[...] = kcat_scr[T:T + W, :].T
        v_out_ref[...] = vcat_scr[T:T + W, :].T


def _const_spec(shape):
    nd = len(shape)
    return pl.BlockSpec(shape, lambda i: (0,) * nd, pipeline_mode=pl.Buffered(1))


def _mix_prompt(x, sinks, npre, win, wup, bg, gn, wba, wbb, wout, npost, *, T):
    L = x.shape[0]
    nb = L // T
    body = functools.partial(_mix_prompt_body, T=T)
    return pl.pallas_call(
        body,
        grid=(nb,),
        in_specs=[
            pl.BlockSpec(memory_space=pltpu.SMEM),
            pl.BlockSpec((T, D_MODEL), lambda i: (i, 0)),
            _const_spec(npre.shape), _const_spec(win.shape), _const_spec(wup.shape),
            _const_spec(bg.shape), _const_spec(gn.shape), _const_spec(wba.shape),
            _const_spec(wbb.shape), _const_spec(wout.shape), _const_spec(npost.shape),
        ],
        out_specs=[
            pl.BlockSpec((T, D_MODEL), lambda i: (i, 0)),
            pl.BlockSpec((GLA_K, GLA_DV), lambda i: (0, 0)),
            pl.BlockSpec((WINDOW, SWA_KV), lambda i: (0, 0)),
            pl.BlockSpec((WINDOW, SWA_KV), lambda i: (0, 0)),
        ],
        out_shape=[
            jax.ShapeDtypeStruct((L, D_MODEL), f32),
            jax.ShapeDtypeStruct((GLA_K, GLA_DV), f32),
            jax.ShapeDtypeStruct((WINDOW, SWA_KV), f32),
            jax.ShapeDtypeStruct((WINDOW, SWA_KV), f32),
        ],
        scratch_shapes=[
            pltpu.VMEM((GLA_DV, GLA_K), f32),
            pltpu.VMEM((T + WINDOW, SWA_KV), f32),
            pltpu.VMEM((T + WINDOW, SWA_KV), f32),
            pltpu.VMEM((T, GLA_V), f32),
            pltpu.VMEM((T, SWA_Q), f32),
            pltpu.VMEM((T, 2 * D_MODEL), f32),
            pltpu.VMEM((T, GLA_V), f32),
        ],
        compiler_params=pltpu.CompilerParams(
            dimension_semantics=("arbitrary",), vmem_limit_bytes=VMEM_LIMIT),
        name="mix_prompt",
    )(sinks, x, npre, win, wup, bg, gn, wba, wbb, wout, npost)


def _pre_sample_body(x_ref, npre_ref, win_ref, wup_ref, bg_ref,
                     qe_ref, kl_ref, e3_ref, oin_ref, va_ref, ga_ref, qb_ref, kb_ref, vb_ref,
                     gta_ref, gtb_ref, x_scr, *, NB, NT):
    for t in range(NT):
        x_scr[t * NB:(t + 1) * NB, :] = x_ref[:, t, :]
    h = _rms(x_scr[...], npre_ref[...]).astype(bf16)

    def proj(c0, n):
        return _dot(h, win_ref[:, c0:c0 + n])

    def blk(val, t):
        return val[t * NB:(t + 1) * NB, :]

    ga_ref[...] = proj(C_GA, GLA_V)
    qb_ref[...] = proj(C_QB, SWA_Q) * SWA_Q_SCALE
    kb_ref[...] = proj(C_KB, SWA_KV)
    vb_ref[...] = proj(C_VB, SWA_KV)
    gta_ref[...] = proj(C_GTA, D_MODEL)
    gtb_ref[...] = proj(C_GTB, D_MODEL)
    va = proj(C_VA, GLA_V)
    va_ref[...] = va

    qa = proj(C_QA, GLA_K) * (GLA_DK ** -0.5)
    ka = proj(C_KA, GLA_K)
    la = _log_decay(proj(C_RA, RA_PAD), wup_ref, bg_ref)
    b = [blk(la, 0)]
    for t in range(1, NT):
        b.append(b[-1] + blk(la, t))
    e3_ref[...] = jnp.exp(b[NT - 1])
    for t in range(NT):
        qe_ref[t * NB:(t + 1) * NB, :] = blk(qa, t) * jnp.exp(b[t])
        kl_ref[t * NB:(t + 1) * NB, :] = blk(ka, t) * jnp.exp(b[NT - 1] - b[t])
    pairs = [(t, j) for t in range(NT) for j in range(t + 1)]
    prods = [(blk(qa, t) * blk(ka, j) * jnp.exp(b[t] - b[j])).astype(bf16) for t, j in pairs]
    r = lax.broadcasted_iota(jnp.int32, (GLA_K, GLA_V), 0)
    c = lax.broadcasted_iota(jnp.int32, (GLA_K, GLA_V), 1)
    expand = jnp.where((r // GLA_DK) == (c // GLA_DV), 1.0, 0.0).astype(bf16)
    att = _dot(jnp.concatenate(prods, axis=0), expand)
    for t in range(NT):
        acc = None
        for idx, (tt, j) in enumerate(pairs):
            if tt != t:
                continue
            term = att[idx * NB:(idx + 1) * NB, :] * blk(va, j)
            acc = term if acc is None else acc + term
        oin_ref[t * NB:(t + 1) * NB, :] = acc


def _pre_sample(xs, npre, win, wup, bg):
    NB, NT, _ = xs.shape
    body = functools.partial(_pre_sample_body, NB=NB, NT=NT)
    widths = (GLA_K, GLA_K, None, GLA_V, GLA_V, GLA_V, SWA_Q, SWA_KV, SWA_KV, D_MODEL, D_MODEL)
    out_shape = [jax.ShapeDtypeStruct((NB, GLA_K) if w is None else (NT * NB, w), f32) for w in widths]
    return pl.pallas_call(
        body,
        out_shape=out_shape,
        scratch_shapes=[pltpu.VMEM((NB * NT, D_MODEL), f32)],
        compiler_params=pltpu.CompilerParams(vmem_limit_bytes=VMEM_LIMIT),
        name="pre_sample",
    )(xs, npre, win, wup, bg)


def _state_sample_body(sink_ref, qe_ref, kl_ref, e3_ref, oin_ref, va_ref, qb_ref, kb_ref, vb_ref,
                       s0_ref, kt_ref, vt_ref,
                       oa_ref, ob_ref, s1_ref, kt1_ref, vt1_ref, *, BB, NT):
    W = WINDOW
    SK = 2 * W
    HT = GLA_HEADS * NT
    HALF = SWA_HD
    hr = lax.broadcasted_iota(jnp.int32, (HT, GLA_K), 0) // NT
    hc = lax.broadcasted_iota(jnp.int32, (HT, GLA_K), 1) // GLA_DK
    own_head = hr == hc
    ones_rows = jnp.ones((16, GLA_DV), bf16)
    zero_rows = jnp.zeros((16, GLA_DV), bf16)
    zero_ht = jnp.zeros((HT, GLA_DV), bf16)
    G2 = 2 * NT
    row = lax.broadcasted_iota(jnp.int32, (G2, SK), 0)
    col = lax.broadcasted_iota(jnp.int32, (G2, SK), 1)
    rel = (row % NT) + W - col
    relf = rel.astype(f32)
    smask = (rel >= 0) & (rel < W)
    first_pair = lax.broadcasted_iota(jnp.int32, (G2, 1), 0) < NT
    pad_rows = jnp.zeros((8 - NT, SWA_KV), f32)
    pad_lanes = jnp.zeros((SWA_KV, SK - W - 8), f32)
    zero_half = jnp.zeros((HALF, SK), bf16)

    def head_variants(cat_t, kv):
        blk = cat_t[kv * HALF:(kv + 1) * HALF]
        return (jnp.concatenate([blk, zero_half], axis=0), jnp.concatenate([zero_half, blk], axis=0))

    pending = []
    for bi in range(BB):
        s0 = s0_ref[bi]
        q4 = qe_ref[:, bi, :]
        qm = jnp.where(own_head, jnp.concatenate([q4] * GLA_HEADS, axis=0), 0.0).astype(bf16)
        o_inter = _dot(qm, s0.astype(bf16))
        for hd in range(GLA_HEADS):
            lanes = slice(hd * GLA_DV, (hd + 1) * GLA_DV)
            oa_ref[:, bi, lanes] = o_inter[hd * NT:(hd + 1) * NT, :] + oin_ref[:, bi, lanes]
        k4 = kl_ref[:, bi, :]
        km = jnp.where(own_head, jnp.concatenate([k4] * GLA_HEADS, axis=0), 0.0).astype(bf16)
        e = e3_ref[bi:bi + 1, :]
        e_hi = e.astype(bf16)
        r1 = e - e_hi.astype(f32)
        e_mid = r1.astype(bf16)
        e_lo = (r1 - e_mid.astype(f32)).astype(bf16)
        e_rows = jnp.concatenate([e_hi, e_mid, e_lo, jnp.zeros((13, GLA_K), bf16)], axis=0)
        lhs = jnp.concatenate([km, e_rows], axis=0)
        v4 = va_ref[:, bi, :].astype(bf16)
        vrep = jnp.concatenate([v4[:, hd * GLA_DV:(hd + 1) * GLA_DV] for hd in range(GLA_HEADS)], axis=0)
        rhs = jnp.concatenate([jnp.concatenate([vrep, zero_ht], axis=1),
                               jnp.concatenate([zero_rows, ones_rows], axis=1)], axis=0)
        res = _dot_tn(lhs, rhs)
        s1_ref[bi] = res[:, GLA_DV:] * s0 + res[:, :GLA_DV]

        kt = kt_ref[bi]
        vt = vt_ref[bi]
        knew_t = jnp.concatenate([kb_ref[:, bi, :], pad_rows], axis=0).T
        vnew_t = jnp.concatenate([vb_ref[:, bi, :], pad_rows], axis=0).T
        kt1_ref[bi] = jnp.concatenate([kt[:, NT:], knew_t[:, 0:NT]], axis=1)
        vt1_ref[bi] = jnp.concatenate([vt[:, NT:], vnew_t[:, 0:NT]], axis=1)
        kcat = jnp.concatenate([kt, knew_t, pad_lanes], axis=1).astype(bf16)
        vcat = jnp.concatenate([vt, vnew_t, pad_lanes], axis=1).astype(bf16)
        q4b = qb_ref[:, bi, :].astype(bf16)
        for kv in range(SWA_KV_HEADS):
            p0 = 2 * kv
            q8 = jnp.concatenate([q4b[:, p0 * LANES:(p0 + 1) * LANES],
                                  q4b[:, (p0 + 1) * LANES:(p0 + 2) * LANES]], axis=0)
            scores = [_dot(q8, kvar) for kvar in head_variants(kcat, kv)]
            pending.append((bi, kv, scores, head_variants(vcat, kv)))

    for bi, kv, scores, v_vars in pending:
        p0 = 2 * kv
        o8_t = None
        for e_ in range(2):
            h_first = 2 * p0 + e_
            h_second = 2 * (p0 + 1) + e_
            slope = jnp.where(first_pair, _alibi_slope(h_first), _alibi_slope(h_second))
            sink = jnp.where(first_pair, sink_ref[h_first] * LOG2E, sink_ref[h_second] * LOG2E)
            s = jnp.where(smask, scores[e_] - slope * relf, -jnp.inf)
            pr, inv = _softmax_sink(s, sink)
            o_t = _dot_nt(v_vars[e_], (pr * inv).astype(bf16))
            o8_t = o_t if o8_t is None else o8_t + o_t
        o8 = o8_t.T
        ob_ref[:, bi, p0 * LANES:(p0 + 1) * LANES] = o8[0:NT, :]
        ob_ref[:, bi, (p0 + 1) * LANES:(p0 + 2) * LANES] = o8[NT:2 * NT, :]


def _state_sample(sinks, qe, kl, e3, oin, va, qb, kb, vb, s0, kt, vt, *, NT, BB):
    NBS = s0.shape[0]
    assert NBS % BB == 0
    body = functools.partial(_state_sample_body, BB=BB, NT=NT)

    def tm(a):
        return a.reshape(NT, NBS, a.shape[-1])

    def rows(n):
        return pl.BlockSpec((NT, BB, n), lambda i: (0, i, 0))

    def per_seq(shape):
        return pl.BlockSpec((BB,) + shape, lambda i: (i, 0, 0))

    oa, ob, s1, kt1, vt1 = pl.pallas_call(
        body,
        grid=(NBS // BB,),
        in_specs=[
            pl.BlockSpec(memory_space=pltpu.SMEM),
            rows(GLA_K), rows(GLA_K), pl.BlockSpec((BB, GLA_K), lambda i: (i, 0)),
            rows(GLA_V), rows(GLA_V), rows(SWA_Q), rows(SWA_KV), rows(SWA_KV),
            per_seq((GLA_K, GLA_DV)), per_seq((SWA_KV, WINDOW)), per_seq((SWA_KV, WINDOW)),
        ],
        out_specs=[
            rows(GLA_V), rows(SWA_Q),
            per_seq((GLA_K, GLA_DV)), per_seq((SWA_KV, WINDOW)), per_seq((SWA_KV, WINDOW)),
        ],
        out_shape=[
            jax.ShapeDtypeStruct((NT, NBS, GLA_V), f32),
            jax.ShapeDtypeStruct((NT, NBS, SWA_Q), f32),
            jax.ShapeDtypeStruct((NBS, GLA_K, GLA_DV), f32),
            jax.ShapeDtypeStruct((NBS, SWA_KV, WINDOW), f32),
            jax.ShapeDtypeStruct((NBS, SWA_KV, WINDOW), f32),
        ],
        compiler_params=pltpu.CompilerParams(
            dimension_semantics=("arbitrary",), vmem_limit_bytes=VMEM_LIMIT),
        name="state_sample",
    )(sinks, tm(qe), tm(kl), e3, tm(oin), tm(va), tm(qb), tm(kb), tm(vb), s0, kt, vt)
    return oa.reshape(NT * NBS, GLA_V), ob.reshape(NT * NBS, SWA_Q), s1, kt1, vt1


def _post_sample_body(x_ref, oa_ref, ga_ref, ob_ref, gta_ref, gtb_ref, gn_ref,
                      wba_ref, wbb_ref, wout_ref, npost_ref, y_ref, x_scr, *, NB, NT):
    for t in range(NT):
        x_scr[t * NB:(t + 1) * NB, :] = x_ref[:, t, :]
    oa = _gla_out_norm(oa_ref[...], gn_ref, ga_ref[...])
    y_ref[...] = _mix_tail(x_scr[...], oa, ob_ref[...], gta_ref[...], gtb_ref[...],
                           wba_ref, wbb_ref, wout_ref, npost_ref)


def _post_sample(xs, oa, ga, ob, gta, gtb, gn, wba, wbb, wout, npost):
    NB, NT, _ = xs.shape
    return pl.pallas_call(
        functools.partial(_post_sample_body, NB=NB, NT=NT),
        out_shape=jax.ShapeDtypeStruct((NT * NB, D_MODEL), f32),
        scratch_shapes=[pltpu.VMEM((NT * NB, D_MODEL), f32)],
        compiler_params=pltpu.CompilerParams(vmem_limit_bytes=VMEM_LIMIT),
        name="post_sample",
    )(xs, oa, ga, ob, gta, gtb, gn, wba, wbb, wout, npost)


FFN_COLS = 256


def _ffn_columns(h, wffn_ref, cw_ref, cb_ref, up_scr, y_scr, *, T, base, shift):
    for c0 in range(0, D_FF, FFN_COLS):
        cols = slice(c0, c0 + FFN_COLS)
        u = _dot(h, wffn_ref[:, c0:c0 + FFN_COLS])
        g = _dot(h, wffn_ref[:, D_FF + c0:D_FF + c0 + FFN_COLS])
        up_scr[base:base + T, cols] = u
        u1 = up_scr[base - shift:base - shift + T, cols]
        u2 = up_scr[base - 2 * shift:base - 2 * shift + T, cols]
        cv = (cb_ref[:, cols] + cw_ref[2:3, cols] * u + cw_ref[1:2, cols] * u1 + cw_ref[0:1, cols] * u2)
        y_scr[:, cols] = (_gelu_tanh(cv) * g).astype(bf16)


def _ffn_prompt_body(x_ref, npre_ref, wffn_ref, cw_ref, cb_ref, wo_ref, npost_ref,
                     y_ref, conv_out_ref, up_scr, y_scr, *, T):
    i = pl.program_id(0)
    base = 8

    @pl.when(i == 0)
    def _():
        up_scr[0:base, :] = jnp.zeros((base, D_FF), f32)

    @pl.when(i > 0)
    def _():
        up_scr[0:base, :] = up_scr[T:T + base, :]

    x = x_ref[...]
    h = _rms(x, npre_ref[...]).astype(bf16)
    _ffn_columns(h, wffn_ref, cw_ref, cb_ref, up_scr, y_scr, T=T, base=base, shift=1)
    f = _dot(y_scr[...], wo_ref[...])
    y_ref[...] = x + _rms(f, npost_ref[...])

    @pl.when(i == pl.num_programs(0) - 1)
    def _():
        conv_out_ref[...] = up_scr[T:T + base, :]


def _ffn_prompt(x, npre, wffn, cw, cb, wo, npost, *, T):
    L = x.shape[0]
    body = functools.partial(_ffn_prompt_body, T=T)
    return pl.pallas_call(
        body,
        grid=(L // T,),
        in_specs=[
            pl.BlockSpec((T, D_MODEL), lambda i: (i, 0)),
            _const_spec(npre.shape), _const_spec(wffn.shape), _const_spec(cw.shape),
            _const_spec(cb.shape), _const_spec(wo.shape), _const_spec(npost.shape),
        ],
        out_specs=[
            pl.BlockSpec((T, D_MODEL), lambda i: (i, 0)),
            pl.BlockSpec((8, D_FF), lambda i: (0, 0)),
        ],
        out_shape=[
            jax.ShapeDtypeStruct((L, D_MODEL), f32),
            jax.ShapeDtypeStruct((8, D_FF), f32),
        ],
        scratch_shapes=[
            pltpu.VMEM((T + 8, D_FF), f32),
            pltpu.VMEM((T, D_FF), bf16),
        ],
        compiler_params=pltpu.CompilerParams(
            dimension_semantics=("arbitrary",), vmem_limit_bytes=VMEM_LIMIT),
        name="ffn_prompt",
    )(x, npre, wffn, cw, cb, wo, npost)


def _ffn_sample_body(x_ref, cst_ref, npre_ref, wffn_ref, cw_ref, cb_ref, wo_ref, npost_ref,
                     y_ref, conv_out_ref, up_scr, y_scr, *, NB, NT):
    T = NB * NT
    for t in range(CONV_W - 1):
        up_scr[t * NB:(t + 1) * NB, :] = cst_ref[:, t, :]
    base = (CONV_W - 1) * NB
    x = x_ref[...]
    h = _rms(x, npre_ref[...]).astype(bf16)
    _ffn_columns(h, wffn_ref, cw_ref, cb_ref, up_scr, y_scr, T=T, base=base, shift=NB)
    f = _dot(y_scr[...], wo_ref[...])
    y = x + _rms(f, npost_ref[...])
    for t in range(NT):
        y_ref[:, t, :] = y[t * NB:(t + 1) * NB, :]
    for t in range(CONV_W - 1):
        conv_out_ref[:, t, :] = up_scr[T + t * NB:T + (t + 1) * NB, :]


def _ffn_sample(x, cst, npre, wffn, cw, cb, wo, npost):
    NB = cst.shape[0]
    T = x.shape[0]
    NT = T // NB
    body = functools.partial(_ffn_sample_body, NB=NB, NT=NT)
    return pl.pallas_call(
        body,
        out_shape=[
            jax.ShapeDtypeStruct((NB, NT, D_MODEL), f32),
            jax.ShapeDtypeStruct((NB, CONV_W - 1, D_FF), f32),
        ],
        scratch_shapes=[
            pltpu.VMEM((T + (CONV_W - 1) * NB, D_FF), f32),
            pltpu.VMEM((T, D_FF), bf16),
        ],
        compiler_params=pltpu.CompilerParams(vmem_limit_bytes=VMEM_LIMIT),
        name="ffn_sample",
    )(x, cst, npre, wffn, cw, cb, wo, npost)


def _prep_w_in_body(*refs, n_plain, per_step):
    o_ref = refs[-1]
    j = pl.program_id(0)
    for k, wt_ref in enumerate(refs[:-1]):
        x = wt_ref[...]
        r = lax.broadcasted_iota(jnp.int32, x.shape, 0)
        x = jnp.where((j * per_step + k < n_plain) | (r < GLA_RANK), x, 0.0)
        o_ref[:, k * PREP_ROWS:(k + 1) * PREP_ROWS] = x.T.astype(bf16)


def _prep_w_in(w_in):
    d_in, n_cols = w_in.shape
    head = C_GA
    tail_src = head + GLA_RANK
    n_head = head // PREP_ROWS
    n_tail = (n_cols - tail_src) // PREP_ROWS
    assert head % PREP_ROWS == 0 and (n_cols - tail_src) % PREP_ROWS == 0
    n_plain = n_head + n_tail
    assert C_RA == n_plain * PREP_ROWS
    per_step = 2
    assert (n_plain + 1) % per_step == 0

    def row_off(blk):
        off = jnp.where(blk < n_head, blk * PREP_ROWS,
                        jnp.where(blk < n_plain, tail_src + (blk - n_head) * PREP_ROWS, head))
        return pl.multiple_of(off, 8)

    def in_spec(k):
        return pl.BlockSpec((pl.Element(PREP_ROWS), pl.Element(d_in)), lambda j: (row_off(j * per_step + k), 0))

    wt = jnp.swapaxes(w_in, 0, 1)
    return pl.pallas_call(
        functools.partial(_prep_w_in_body, n_plain=n_plain, per_step=per_step),
        grid=((n_plain + 1) // per_step,),
        in_specs=[in_spec(k) for k in range(per_step)],
        out_specs=pl.BlockSpec((d_in, per_step * PREP_ROWS), lambda j: (0, j)),
        out_shape=jax.ShapeDtypeStruct((d_in, (n_plain + 1) * PREP_ROWS), bf16),
        compiler_params=pltpu.CompilerParams(dimension_semantics=("arbitrary",)),
        name="prep_w_in",
    )(*([wt] * per_step))


def kernel(x_prompt, x_sample, state_gla, cache_swa_k, cache_swa_v, state_ffn_conv, norm_mix_pre, norm_mix_post, w_in, w_gate_up, b_gate, gla_norm, sinks, w_branch_a, w_branch_b, w_out, norm_ffn_pre, norm_ffn_post, w_ffn_in, conv_w, conv_b, w_ffn_out):
    depth = w_in.shape[0]
    assert depth == 1
    l = 0
    B, L, _ = x_prompt.shape
    assert B == 1
    NBS, NT, _ = x_sample.shape

    win = _prep_w_in(w_in[l])
    wup = jnp.zeros((RA_PAD, GLA_K), f32).at[:GLA_RANK].set(w_gate_up[l]).astype(bf16)
    bg = b_gate[l].reshape(1, GLA_K)
    gn = gla_norm[l].reshape(1, GLA_DV)
    npre = norm_mix_pre[l].reshape(1, D_MODEL)
    npost = norm_mix_post[l].reshape(1, D_MODEL)
    wba = w_branch_a[l].astype(bf16)
    wbb = w_branch_b[l].astype(bf16)
    wout = w_out[l].astype(bf16)
    fpre = norm_ffn_pre[l].reshape(1, D_MODEL)
    fpost = norm_ffn_post[l].reshape(1, D_MODEL)
    wffn = w_ffn_in[l].astype(bf16)
    cw = conv_w[l]
    cb = conv_b[l].reshape(1, D_FF)
    wo = w_ffn_out[l].astype(bf16)
    sk = sinks[l]

    x1, st_p, k_p, v_p = _mix_prompt(x_prompt[0], sk, npre, win, wup, bg, gn, wba, wbb, wout, npost, T=256)
    y_p, conv_p = _ffn_prompt(x1, fpre, wffn, cw, cb, wo, fpost, T=512)

    y_prompt = y_p[None]
    gla_state_prompt = st_p.reshape(1, 1, GLA_HEADS, GLA_DK, GLA_DV)
    swa_k_prompt = jnp.transpose(k_p.reshape(SWA_KV_HEADS, SWA_HD, WINDOW), (2, 0, 1))[None, None]
    swa_v_prompt = jnp.transpose(v_p.reshape(SWA_KV_HEADS, SWA_HD, WINDOW), (2, 0, 1))[None, None]
    conv_prompt = conv_p[8 - (CONV_W - 1):].reshape(1, 1, CONV_W - 1, D_FF)

    qe, kl, e3, oin, va, ga, qb, kb, vb, gta, gtb = _pre_sample(x_sample, npre, win, wup, bg)
    kt = jnp.transpose(cache_swa_k[l], (0, 2, 3, 1)).reshape(NBS, SWA_KV, WINDOW)
    vt = jnp.transpose(cache_swa_v[l], (0, 2, 3, 1)).reshape(NBS, SWA_KV, WINDOW)
    oa_raw, ob, s1, kt1, vt1 = _state_sample(
        sk, qe, kl, e3, oin, va, qb, kb, vb, state_gla[l].reshape(NBS, GLA_K, GLA_DV), kt, vt, NT=NT, BB=16)
    x1s = _post_sample(x_sample, oa_raw, ga, ob, gta, gtb, gn, wba, wbb, wout, npost)
    y_sample, conv_s = _ffn_sample(x1s, state_ffn_conv[l], fpre, wffn, cw, cb, wo, fpost)

    def cache_out(t):
        return jnp.transpose(t.reshape(NBS, SWA_KV_HEADS, SWA_HD, WINDOW), (0, 3, 1, 2))[None]

    gla_state_sample = s1.reshape(1, NBS, GLA_HEADS, GLA_DK, GLA_DV)
    swa_k_sample = cache_out(kt1)
    swa_v_sample = cache_out(vt1)
    conv_sample = conv_s[None]
    return (y_prompt, y_sample, gla_state_prompt, gla_state_sample, swa_k_prompt, swa_v_prompt,
            swa_k_sample, swa_v_sample, conv_prompt, conv_sample)
```

```python
import functools

import jax
import jax.numpy as jnp
from jax import lax
from jax.experimental import pallas as pl
from jax.experimental.pallas import tpu as pltpu

f32 = jnp.float32
bf16 = jnp.bfloat16

D_MODEL = 1024
GLA_HEADS = 4
GLA_DK = 64
GLA_DV = 128
GLA_RANK = 16
GLA_TAU = 16.0
GLA_CHUNK = 64
GLA_SAFE_DECAY = 60.0
SWA_HEADS = 8
SWA_KV_HEADS = 2
SWA_HD = 64
WINDOW = 128
D_FF = 2816
CONV_W = 3
EPS = 1e-6
GLA_K = GLA_HEADS * GLA_DK
GLA_V = GLA_HEADS * GLA_DV
SWA_Q = SWA_HEADS * SWA_HD
SWA_KV = SWA_KV_HEADS * SWA_HD
LANES = 128
LOG2E = 1.4426950408889634

C_QA = 0
C_KA = C_QA + GLA_K
C_VA = C_KA + GLA_K
C_GA = C_VA + GLA_V
C_QB = C_GA + GLA_V
C_KB = C_QB + SWA_Q
C_VB = C_KB + SWA_KV
C_GTA = C_VB + SWA_KV
C_GTB = C_GTA + D_MODEL
C_RA = C_GTB + D_MODEL
RA_PAD = LANES
PREP_ROWS = 256
IN_COLS_PAD = C_RA + PREP_ROWS

VMEM_LIMIT = 56 * 1024 * 1024


def _dot(a, b):
    return jnp.dot(a, b, preferred_element_type=f32)


def _dot_nt(a, b):
    return lax.dot_general(a, b, (((1,), (1,)), ((), ())), preferred_element_type=f32)


def _dot_tn(a, b):
    return lax.dot_general(a, b, (((0,), (0,)), ((), ())), preferred_element_type=f32)


def _rms(x, w):
    return x * lax.rsqrt(jnp.mean(x * x, axis=-1, keepdims=True) + EPS) * w


def _gelu_tanh(x):
    k = -2.0 * 0.7978845608028654 * LOG2E
    return x / (1.0 + jnp.exp2(x * (k + (k * 0.044715) * (x * x))))


def _split_hi_lo(x):
    hi = x.astype(bf16)
    lo = (x - hi.astype(f32)).astype(bf16)
    return hi, lo


def _chunk_cumsum(la, chunk):
    n = la.shape[0]
    r = lax.broadcasted_iota(jnp.int32, (n, n), 0)
    c = lax.broadcasted_iota(jnp.int32, (n, n), 1)
    tri = jnp.where((c <= r) & ((r // chunk) == (c // chunk)), 1.0, 0.0).astype(bf16)
    hi, lo = _split_hi_lo(la)
    return _dot(tri, hi) + _dot(tri, lo)


def _even_head_lanes(shape):
    lane = lax.broadcasted_iota(jnp.int32, shape, len(shape) - 1)
    return (lane % LANES) < GLA_DK


def _gla_out_norm(o, gn_ref, ga):
    outs = []
    for h in range(GLA_HEADS):
        oh = o[:, h * GLA_DV:(h + 1) * GLA_DV]
        outs.append(_rms(oh, gn_ref[...]))
    on = jnp.concatenate(outs, axis=1)
    return on * (ga * jax.nn.sigmoid(ga))


def _mix_tail(x, oa, ob, gate_a, gate_b, wba_ref, wbb_ref, wout_ref, npost_ref):
    merged = (jax.nn.sigmoid(gate_a) * _dot(oa.astype(bf16), wba_ref[...])
              + jax.nn.sigmoid(gate_b) * _dot(ob.astype(bf16), wbb_ref[...]))
    m = _dot(merged.astype(bf16), wout_ref[...])
    return x + _rms(m, npost_ref[...])


def _alibi_slope(head):
    return LOG2E * 2.0 ** (-(8.0 / SWA_HEADS) * (head + 1))


SWA_Q_SCALE = LOG2E * SWA_HD ** -0.5


def _kv_variants(x):
    lo = _even_head_lanes(x.shape)
    xr = pltpu.roll(x, SWA_HD, 1)
    zero = jnp.zeros_like(x)
    h0_lo = jnp.where(lo, x, zero).astype(bf16)
    h1_hi = jnp.where(lo, zero, x).astype(bf16)
    h1_lo = jnp.where(lo, xr, zero).astype(bf16)
    h0_hi = jnp.where(lo, zero, xr).astype(bf16)
    return (h0_lo, h0_hi), (h1_lo, h1_hi)


def _softmax_sink(s, sink):
    m = jnp.maximum(jnp.max(s, axis=-1, keepdims=True), sink)
    p = jnp.exp2(s - m)
    denom = jnp.sum(p, axis=-1, keepdims=True) + jnp.exp2(sink - m)
    return p, 1.0 / denom


def _mix_prompt_body(sink_ref, x_ref, npre_ref, win_ref, wup_ref, bg_ref, gn_ref,
                     wba_ref, wbb_ref, wout_ref, npost_ref,
                     y_ref, st_out_ref, k_out_ref, v_out_ref,
                     st_scr, kcat_scr, vcat_scr, oa_scr, ob_scr, gate_scr, inter_scr, *, T):
    i = pl.program_id(0)
    W = WINDOW
    C = GLA_CHUNK

    @pl.when(i == 0)
    def _():
        st_scr[...] = jnp.zeros_like(st_scr)
        kcat_scr[0:W, :] = jnp.zeros((W, SWA_KV), f32)
        vcat_scr[0:W, :] = jnp.zeros((W, SWA_KV), f32)

    @pl.when(i > 0)
    def _():
        kcat_scr[0:W, :] = kcat_scr[T:T + W, :]
        vcat_scr[0:W, :] = vcat_scr[T:T + W, :]

    x = x_ref[...]
    rms_f = lax.rsqrt(jnp.mean(x * x, axis=-1, keepdims=True) + EPS)
    h = (x * npre_ref[...]).astype(bf16)
    rms_b = {n: jnp.broadcast_to(rms_f, (T, n)) for n in (LANES, 2 * LANES)}

    def proj(c0, n):
        w = 2 * LANES if n % (2 * LANES) == 0 else LANES
        return jnp.concatenate([_dot(h, win_ref[:, c:c + w]) * rms_b[w] for c in range(c0, c0 + n, w)], axis=1)


    xg = _dot(proj(C_RA, RA_PAD).astype(bf16), wup_ref[...]) + bg_ref[...]
    kcat_scr[W:W + T, :] = proj(C_KB, SWA_KV)
    vcat_scr[W:W + T, :] = proj(C_VB, SWA_KV)
    qb = (proj(C_QB, SWA_Q) * SWA_Q_SCALE).astype(bf16)
    la = jax.nn.log_sigmoid(xg) * (1.0 / GLA_TAU)
    b = _chunk_cumsum(la, C)
    decay_floor = jnp.min(b)
    qa = proj(C_QA, GLA_K)
    ka = proj(C_KA, GLA_K)
    va_b = proj(C_VA, GLA_V).astype(bf16)

    qe = qa * jnp.exp(b) * (GLA_DK ** -0.5)
    ke = (ka * jnp.exp(-b)).astype(bf16)
    even = _even_head_lanes((T, GLA_K))
    qe_even = jnp.where(even, qe, 0.0).astype(bf16)
    qe_odd = jnp.where(even, 0.0, qe).astype(bf16)
    k_var = _kv_variants(kcat_scr[...])
    v_var = _kv_variants(vcat_scr[...])

    r2 = lax.broadcasted_iota(jnp.int32, (2 * C, 2 * C), 0)
    c2 = lax.broadcasted_iota(jnp.int32, (2 * C, 2 * C), 1)
    pair_causal = ((r2 // C) == (c2 // C)) & ((c2 % C) <= (r2 % C))
    even_c = _even_head_lanes((C, LANES))
    st = [st_scr[:, p * LANES:(p + 1) * LANES] for p in range(GLA_HEADS // 2)]

    def gla_scores(c):
        rows = slice(c * C, (c + 1) * C)
        out = []
        for p in range(GLA_HEADS // 2):
            lanes = slice(p * LANES, (p + 1) * LANES)
            q2 = jnp.concatenate([qe_even[rows, lanes], qe_odd[rows, lanes]], axis=0)
            ke_p = ke[rows, lanes]
            rhs = jnp.concatenate([ke_p, ke_p, st[p].astype(bf16)], axis=0)
            r = _dot_nt(q2, rhs)
            att = jnp.where(pair_causal, r[:, 0:2 * C], 0.0).astype(bf16)
            out.append((att, r[:, 2 * C:]))
        return out

    def gla_update(c, sc):
        rows = slice(c * C, (c + 1) * C)
        b_c = b[rows]
        bl = b_c[C - 1:C, :]
        kl = ka[rows] * jnp.exp(bl - b_c)
        ebl = jnp.exp(bl)
        for p in range(GLA_HEADS // 2):
            lanes = slice(p * LANES, (p + 1) * LANES)
            att, inter = sc[p]
            v2 = jnp.concatenate(
                [va_b[rows, (2 * p) * GLA_DV:(2 * p + 1) * GLA_DV],
                 va_b[rows, (2 * p + 1) * GLA_DV:(2 * p + 2) * GLA_DV]], axis=0)
            o2 = inter + _dot(att, v2)
            for e in range(2):
                hl = slice((2 * p + e) * GLA_DV, (2 * p + e + 1) * GLA_DV)
                oa_scr[rows, hl] = o2[e * C:(e + 1) * C]
                inter_scr[rows, hl] = inter[e * C:(e + 1) * C]
            kl_p = kl[:, lanes]
            kl_stack = jnp.concatenate(
                [jnp.where(even_c, kl_p, 0.0), jnp.where(even_c, 0.0, kl_p)], axis=0).astype(bf16)
            st[p] = st[p] * ebl[:, lanes] + _dot_tn(v2, kl_stack)

    qi = lax.broadcasted_iota(jnp.int32, (W, 2 * W), 0)
    kc = lax.broadcasted_iota(jnp.int32, (W, 2 * W), 1)
    rel = qi + W - kc
    relf = rel.astype(f32)
    in_window = (rel >= 0) & (rel < W)

    def swa_probs(j, kv):
        qrows = slice(j * W, (j + 1) * W)
        band = slice(j * W, j * W + 2 * W)
        if j == 0:
            mask = in_window & ((kc >= W) | (i > 0))
        else:
            mask = in_window
        pairs = (2 * kv, 2 * kv + 1)
        q2 = jnp.concatenate([qb[qrows, p * LANES:(p + 1) * LANES] for p in pairs], axis=0)
        out = []
        for e in range(2):
            s2 = _dot_nt(q2, k_var[kv][e][band])
            probs = []
            for half, p in enumerate(pairs):
                hd = 2 * p + e
                s = s2[half * W:(half + 1) * W]
                s = jnp.where(mask, s - _alibi_slope(hd) * relf, -jnp.inf)
                pr, inv = _softmax_sink(s, sink_ref[hd] * LOG2E)
                probs.append((pr * inv).astype(bf16))
            out.append(jnp.concatenate(probs, axis=0))
        return out

    def swa_out(j, kv, probs):
        qrows = slice(j * W, (j + 1) * W)
        band = slice(j * W, j * W + 2 * W)
        o2 = _dot(probs[0], v_var[kv][0][band]) + _dot(probs[1], v_var[kv][1][band])
        for half, p in enumerate((2 * kv, 2 * kv + 1)):
            ob_scr[qrows, p * LANES:(p + 1) * LANES] = o2[half * W:(half + 1) * W]

    n_chunks = T // C
    assert n_chunks == (T // W) * SWA_KV_HEADS
    gw = 2 * D_MODEL // n_chunks
    for idx in range(n_chunks):
        j, kv = idx // SWA_KV_HEADS, idx % SWA_KV_HEADS
        probs = swa_probs(j, kv)
        sc = gla_scores(idx)
        gate_scr[:, idx * gw:(idx + 1) * gw] = proj(C_GTA + idx * gw, gw)
        gla_update(idx, sc)
        swa_out(j, kv, probs)
    for p in range(GLA_HEADS // 2):
        st_scr[:, p * LANES:(p + 1) * LANES] = st[p]
    ga = proj(C_GA, GLA_V)
    gated_b = jax.nn.sigmoid(gate_scr[:, D_MODEL:2 * D_MODEL]) * _dot(ob_scr[...].astype(bf16), wbb_ref[...])
    sig_a = jax.nn.sigmoid(gate_scr[:, 0:D_MODEL])

    def finish(oa_raw):
        oa = _gla_out_norm(oa_raw, gn_ref, ga)
        merged = sig_a * _dot(oa.astype(bf16), wba_ref[...]) + gated_b
        m = _dot(merged.astype(bf16), wout_ref[...])
        y_ref[...] = x + _rms(m, npost_ref[...])

    finish(oa_scr[...])

    @pl.when(decay_floor < -GLA_SAFE_DECAY)
    def _():
        qs = qa * (GLA_DK ** -0.5)
        va_f = va_b.astype(f32)
        pos = lax.broadcasted_iota(jnp.int32, (T, 1), 0) % C
        er = lax.broadcasted_iota(jnp.int32, (GLA_K, GLA_V), 0)
        ec = lax.broadcasted_iota(jnp.int32, (GLA_K, GLA_V), 1)
        expand = jnp.where((er // GLA_DK) == (ec // GLA_DV), 1.0, 0.0).astype(bf16)

        def offset_term(d, acc):
            valid = pos >= d
            expo = jnp.where(valid, b - pltpu.roll(b, d, 0), 0.0)
            prod = jnp.where(valid, qs * pltpu.roll(ka, d, 0) * jnp.exp(expo), 0.0)
            return acc + _dot(prod.astype(bf16), expand) * pltpu.roll(va_f, d, 0)

        intra = lax.fori_loop(0, C, offset_term, jnp.zeros((T, GLA_V), f32))
        finish(inter_scr[...] + intra)

    @pl.when(i == pl.num_programs(0) - 1)
    def _():
        st_out_ref[...] = st_scr[...].T
        k_out_ref[...] = kcat_scr[T:T + W, :].T
        v_out_ref[...] = vcat_scr[T:T + W, :].T


def _const_spec(shape):
    nd = len(shape)
    return pl.BlockSpec(shape, lambda i: (0,) * nd, pipeline_mode=pl.Buffered(1))


def _mix_prompt(x, sinks, npre, win, wup, bg, gn, wba, wbb, wout, npost, *, T):
    L = x.shape[0]
    nb = L // T
    body = functools.partial(_mix_prompt_body, T=T)
    return pl.pallas_call(
        body,
        grid=(nb,),
        in_specs=[
            pl.BlockSpec(memory_space=pltpu.SMEM),
            pl.BlockSpec((T, D_MODEL), lambda i: (i, 0)),
            _const_spec(npre.shape), _const_spec(win.shape), _const_spec(wup.shape),
            _const_spec(bg.shape), _const_spec(gn.shape), _const_spec(wba.shape),
            _const_spec(wbb.shape), _const_spec(wout.shape), _const_spec(npost.shape),
        ],
        out_specs=[
            pl.BlockSpec((T, D_MODEL), lambda i: (i, 0)),
            pl.BlockSpec((GLA_K, GLA_DV), lambda i: (0, 0)),
            pl.BlockSpec((WINDOW, SWA_KV), lambda i: (0, 0)),
            pl.BlockSpec((WINDOW, SWA_KV), lambda i: (0, 0)),
        ],
        out_shape=[
            jax.ShapeDtypeStruct((L, D_MODEL), f32),
            jax.ShapeDtypeStruct((GLA_K, GLA_DV), f32),
            jax.ShapeDtypeStruct((WINDOW, SWA_KV), f32),
            jax.ShapeDtypeStruct((WINDOW, SWA_KV), f32),
        ],
        scratch_shapes=[
            pltpu.VMEM((GLA_DV, GLA_K), f32),
            pltpu.VMEM((T + WINDOW, SWA_KV), f32),
            pltpu.VMEM((T + WINDOW, SWA_KV), f32),
            pltpu.VMEM((T, GLA_V), f32),
            pltpu.VMEM((T, SWA_Q), f32),
            pltpu.VMEM((T, 2 * D_MODEL), f32),
            pltpu.VMEM((T, GLA_V), f32),
        ],
        compiler_params=pltpu.CompilerParams(
            dimension_semantics=("arbitrary",), vmem_limit_bytes=VMEM_LIMIT),
        name="mix_prompt",
    )(sinks, x, npre, win, wup, bg, gn, wba, wbb, wout, npost)


def _pre_sample_body(x_ref, npre_ref, win_ref, wup_ref, bg_ref,
                     qe_ref, kl_ref, e3_ref, oin_ref, va_ref, ga_ref, qb_ref, kb_ref, vb_ref,
                     gta_ref, gtb_ref, x_scr, *, NB, NT):
    for t in range(NT):
        x_scr[t * NB:(t + 1) * NB, :] = x_ref[:, t, :]
    h = _rms(x_scr[...], npre_ref[...]).astype(bf16)

    def proj(c0, n):
        return _dot(h, win_ref[:, c0:c0 + n])

    def blk(val, t):
        return val[t * NB:(t + 1) * NB, :]

    xg = _dot(proj(C_RA, RA_PAD).astype(bf16), wup_ref[...]) + bg_ref[...]
    qa = proj(C_QA, GLA_K) * (GLA_DK ** -0.5)
    ka = proj(C_KA, GLA_K)
    va = proj(C_VA, GLA_V)
    va_ref[...] = va
    la = jax.nn.log_sigmoid(xg) * (1.0 / GLA_TAU)
    b = [blk(la, 0)]
    for t in range(1, NT):
        b.append(b[-1] + blk(la, t))
    e3_ref[...] = jnp.exp(b[NT - 1])
    for t in range(NT):
        qe_ref[t * NB:(t + 1) * NB, :] = blk(qa, t) * jnp.exp(b[t])
        kl_ref[t * NB:(t + 1) * NB, :] = blk(ka, t) * jnp.exp(b[NT - 1] - b[t])
    pairs = [(t, j) for t in range(NT) for j in range(t + 1)]
    prods = [(blk(qa, t) * blk(ka, j) * jnp.exp(b[t] - b[j])).astype(bf16) for t, j in pairs]
    r = lax.broadcasted_iota(jnp.int32, (GLA_K, GLA_V), 0)
    c = lax.broadcasted_iota(jnp.int32, (GLA_K, GLA_V), 1)
    expand = jnp.where((r // GLA_DK) == (c // GLA_DV), 1.0, 0.0).astype(bf16)
    ga_ref[...] = proj(C_GA, GLA_V)
    qb_ref[...] = proj(C_QB, SWA_Q) * SWA_Q_SCALE
    kb_ref[...] = proj(C_KB, SWA_KV)
    vb_ref[...] = proj(C_VB, SWA_KV)
    att = _dot(jnp.concatenate(prods, axis=0), expand)
    gta_ref[...] = proj(C_GTA, D_MODEL)
    gtb_ref[...] = proj(C_GTB, D_MODEL)
    for t in range(NT):
        acc = None
        for idx, (tt, j) in enumerate(pairs):
            if tt != t:
                continue
            term = att[idx * NB:(idx + 1) * NB, :] * blk(va, j)
            acc = term if acc is None else acc + term
        oin_ref[t * NB:(t + 1) * NB, :] = acc


def _pre_sample(xs, npre, win, wup, bg):
    NB, NT, _ = xs.shape
    body = functools.partial(_pre_sample_body, NB=NB, NT=NT)
    widths = (GLA_K, GLA_K, None, GLA_V, GLA_V, GLA_V, SWA_Q, SWA_KV, SWA_KV, D_MODEL, D_MODEL)
    out_shape = [jax.ShapeDtypeStruct((NB, GLA_K) if w is None else (NT * NB, w), f32) for w in widths]
    return pl.pallas_call(
        body,
        out_shape=out_shape,
        scratch_shapes=[pltpu.VMEM((NB * NT, D_MODEL), f32)],
        compiler_params=pltpu.CompilerParams(vmem_limit_bytes=VMEM_LIMIT),
        name="pre_sample",
    )(xs, npre, win, wup, bg)


def _state_sample_body(sink_ref, qe_ref, kl_ref, e3_ref, oin_ref, va_ref, qb_ref, kb_ref, vb_ref,
                       s0_ref, kt_ref, vt_ref,
                       oa_ref, ob_ref, s1_ref, kt1_ref, vt1_ref, *, BB, NT):
    W = WINDOW
    SK = 2 * W
    HT = GLA_HEADS * NT
    HALF = SWA_HD
    hr = lax.broadcasted_iota(jnp.int32, (HT, GLA_K), 0) // NT
    hc = lax.broadcasted_iota(jnp.int32, (HT, GLA_K), 1) // GLA_DK
    own_head = hr == hc
    ones_rows = jnp.ones((16, GLA_DV), bf16)
    zero_rows = jnp.zeros((16, GLA_DV), bf16)
    zero_ht = jnp.zeros((HT, GLA_DV), bf16)
    G2 = 2 * NT
    row = lax.broadcasted_iota(jnp.int32, (G2, SK), 0)
    col = lax.broadcasted_iota(jnp.int32, (G2, SK), 1)
    rel = (row % NT) + W - col
    relf = rel.astype(f32)
    smask = (rel >= 0) & (rel < W)
    first_pair = lax.broadcasted_iota(jnp.int32, (G2, 1), 0) < NT
    pad_rows = jnp.zeros((8 - NT, SWA_KV), f32)
    pad_lanes = jnp.zeros((SWA_KV, SK - W - 8), f32)
    zero_half = jnp.zeros((HALF, SK), bf16)

    def head_variants(cat_t, kv):
        blk = cat_t[kv * HALF:(kv + 1) * HALF]
        return (jnp.concatenate([blk, zero_half], axis=0), jnp.concatenate([zero_half, blk], axis=0))

    pending = []
    for bi in range(BB):
        s0 = s0_ref[bi]
        q4 = qe_ref[:, bi, :]
        qm = jnp.where(own_head, jnp.concatenate([q4] * GLA_HEADS, axis=0), 0.0).astype(bf16)
        o_inter = _dot(qm, s0.astype(bf16))
        for hd in range(GLA_HEADS):
            lanes = slice(hd * GLA_DV, (hd + 1) * GLA_DV)
            oa_ref[:, bi, lanes] = o_inter[hd * NT:(hd + 1) * NT, :] + oin_ref[:, bi, lanes]
        k4 = kl_ref[:, bi, :]
        km = jnp.where(own_head, jnp.concatenate([k4] * GLA_HEADS, axis=0), 0.0).astype(bf16)
        e = e3_ref[bi:bi + 1, :]
        e_hi = e.astype(bf16)
        r1 = e - e_hi.astype(f32)
        e_mid = r1.astype(bf16)
        e_lo = (r1 - e_mid.astype(f32)).astype(bf16)
        e_rows = jnp.concatenate([e_hi, e_mid, e_lo, jnp.zeros((13, GLA_K), bf16)], axis=0)
        lhs = jnp.concatenate([km, e_rows], axis=0)
        v4 = va_ref[:, bi, :].astype(bf16)
        vrep = jnp.concatenate([v4[:, hd * GLA_DV:(hd + 1) * GLA_DV] for hd in range(GLA_HEADS)], axis=0)
        rhs = jnp.concatenate([jnp.concatenate([vrep, zero_ht], axis=1),
                               jnp.concatenate([zero_rows, ones_rows], axis=1)], axis=0)
        res = _dot_tn(lhs, rhs)
        s1_ref[bi] = res[:, GLA_DV:] * s0 + res[:, :GLA_DV]

        kt = kt_ref[bi]
        vt = vt_ref[bi]
        knew_t = jnp.concatenate([kb_ref[:, bi, :], pad_rows], axis=0).T
        vnew_t = jnp.concatenate([vb_ref[:, bi, :], pad_rows], axis=0).T
        kt1_ref[bi] = jnp.concatenate([kt[:, NT:], knew_t[:, 0:NT]], axis=1)
        vt1_ref[bi] = jnp.concatenate([vt[:, NT:], vnew_t[:, 0:NT]], axis=1)
        kcat = jnp.concatenate([kt, knew_t, pad_lanes], axis=1).astype(bf16)
        vcat = jnp.concatenate([vt, vnew_t, pad_lanes], axis=1).astype(bf16)
        q4b = qb_ref[:, bi, :].astype(bf16)
        for kv in range(SWA_KV_HEADS):
            p0 = 2 * kv
            q8 = jnp.concatenate([q4b[:, p0 * LANES:(p0 + 1) * LANES],
                                  q4b[:, (p0 + 1) * LANES:(p0 + 2) * LANES]], axis=0)
            scores = [_dot(q8, kvar) for kvar in head_variants(kcat, kv)]
            pending.append((bi, kv, scores, head_variants(vcat, kv)))

    for bi, kv, scores, v_vars in pending:
        p0 = 2 * kv
        o8_t = None
        for e_ in range(2):
            h_first = 2 * p0 + e_
            h_second = 2 * (p0 + 1) + e_
            slope = jnp.where(first_pair, _alibi_slope(h_first), _alibi_slope(h_second))
            sink = jnp.where(first_pair, sink_ref[h_first] * LOG2E, sink_ref[h_second] * LOG2E)
            s = jnp.where(smask, scores[e_] - slope * relf, -jnp.inf)
            pr, inv = _softmax_sink(s, sink)
            o_t = _dot_nt(v_vars[e_], (pr * inv).astype(bf16))
            o8_t = o_t if o8_t is None else o8_t + o_t
        o8 = o8_t.T
        ob_ref[:, bi, p0 * LANES:(p0 + 1) * LANES] = o8[0:NT, :]
        ob_ref[:, bi, (p0 + 1) * LANES:(p0 + 2) * LANES] = o8[NT:2 * NT, :]


def _state_sample(sinks, qe, kl, e3, oin, va, qb, kb, vb, s0, kt, vt, *, NT, BB):
    NBS = s0.shape[0]
    assert NBS % BB == 0
    body = functools.partial(_state_sample_body, BB=BB, NT=NT)

    def tm(a):
        return a.reshape(NT, NBS, a.shape[-1])

    def rows(n):
        return pl.BlockSpec((NT, BB, n), lambda i: (0, i, 0))

    def per_seq(shape):
        return pl.BlockSpec((BB,) + shape, lambda i: (i, 0, 0))

    oa, ob, s1, kt1, vt1 = pl.pallas_call(
        body,
        grid=(NBS // BB,),
        in_specs=[
            pl.BlockSpec(memory_space=pltpu.SMEM),
            rows(GLA_K), rows(GLA_K), pl.BlockSpec((BB, GLA_K), lambda i: (i, 0)),
            rows(GLA_V), rows(GLA_V), rows(SWA_Q), rows(SWA_KV), rows(SWA_KV),
            per_seq((GLA_K, GLA_DV)), per_seq((SWA_KV, WINDOW)), per_seq((SWA_KV, WINDOW)),
        ],
        out_specs=[
            rows(GLA_V), rows(SWA_Q),
            per_seq((GLA_K, GLA_DV)), per_seq((SWA_KV, WINDOW)), per_seq((SWA_KV, WINDOW)),
        ],
        out_shape=[
            jax.ShapeDtypeStruct((NT, NBS, GLA_V), f32),
            jax.ShapeDtypeStruct((NT, NBS, SWA_Q), f32),
            jax.ShapeDtypeStruct((NBS, GLA_K, GLA_DV), f32),
            jax.ShapeDtypeStruct((NBS, SWA_KV, WINDOW), f32),
            jax.ShapeDtypeStruct((NBS, SWA_KV, WINDOW), f32),
        ],
        compiler_params=pltpu.CompilerParams(
            dimension_semantics=("arbitrary",), vmem_limit_bytes=VMEM_LIMIT),
        name="state_sample",
    )(sinks, tm(qe), tm(kl), e3, tm(oin), tm(va), tm(qb), tm(kb), tm(vb), s0, kt, vt)
    return oa.reshape(NT * NBS, GLA_V), ob.reshape(NT * NBS, SWA_Q), s1, kt1, vt1


def _post_sample_body(x_ref, oa_ref, ga_ref, ob_ref, gta_ref, gtb_ref, gn_ref,
                      wba_ref, wbb_ref, wout_ref, npost_ref, y_ref, x_scr, *, NB, NT):
    for t in range(NT):
        x_scr[t * NB:(t + 1) * NB, :] = x_ref[:, t, :]
    oa = _gla_out_norm(oa_ref[...], gn_ref, ga_ref[...])
    y_ref[...] = _mix_tail(x_scr[...], oa, ob_ref[...], gta_ref[...], gtb_ref[...],
                           wba_ref, wbb_ref, wout_ref, npost_ref)


def _post_sample(xs, oa, ga, ob, gta, gtb, gn, wba, wbb, wout, npost):
    NB, NT, _ = xs.shape
    return pl.pallas_call(
        functools.partial(_post_sample_body, NB=NB, NT=NT),
        out_shape=jax.ShapeDtypeStruct((NT * NB, D_MODEL), f32),
        scratch_shapes=[pltpu.VMEM((NT * NB, D_MODEL), f32)],
        compiler_params=pltpu.CompilerParams(vmem_limit_bytes=VMEM_LIMIT),
        name="post_sample",
    )(xs, oa, ga, ob, gta, gtb, gn, wba, wbb, wout, npost)


FFN_COLS = 256


def _ffn_columns(h, wffn_ref, cw_ref, cb_ref, up_scr, y_scr, *, T, base, shift):
    for c0 in range(0, D_FF, FFN_COLS):
        cols = slice(c0, c0 + FFN_COLS)
        u = _dot(h, wffn_ref[:, c0:c0 + FFN_COLS])
        g = _dot(h, wffn_ref[:, D_FF + c0:D_FF + c0 + FFN_COLS])
        up_scr[base:base + T, cols] = u
        u1 = up_scr[base - shift:base - shift + T, cols]
        u2 = up_scr[base - 2 * shift:base - 2 * shift + T, cols]
        cv = (cb_ref[:, cols] + cw_ref[2:3, cols] * u + cw_ref[1:2, cols] * u1 + cw_ref[0:1, cols] * u2)
        y_scr[:, cols] = (_gelu_tanh(cv) * g).astype(bf16)


def _ffn_prompt_body(x_ref, npre_ref, wffn_ref, cw_ref, cb_ref, wo_ref, npost_ref,
                     y_ref, conv_out_ref, up_scr, y_scr, *, T):
    i = pl.program_id(0)
    base = 8

    @pl.when(i == 0)
    def _():
        up_scr[0:base, :] = jnp.zeros((base, D_FF), f32)

    @pl.when(i > 0)
    def _():
        up_scr[0:base, :] = up_scr[T:T + base, :]

    x = x_ref[...]
    h = _rms(x, npre_ref[...]).astype(bf16)
    _ffn_columns(h, wffn_ref, cw_ref, cb_ref, up_scr, y_scr, T=T, base=base, shift=1)
    f = _dot(y_scr[...], wo_ref[...])
    y_ref[...] = x + _rms(f, npost_ref[...])

    @pl.when(i == pl.num_programs(0) - 1)
    def _():
        conv_out_ref[...] = up_scr[T:T + base, :]


def _ffn_prompt(x, npre, wffn, cw, cb, wo, npost, *, T):
    L = x.shape[0]
    body = functools.partial(_ffn_prompt_body, T=T)
    return pl.pallas_call(
        body,
        grid=(L // T,),
        in_specs=[
            pl.BlockSpec((T, D_MODEL), lambda i: (i, 0)),
            _const_spec(npre.shape), _const_spec(wffn.shape), _const_spec(cw.shape),
            _const_spec(cb.shape), _const_spec(wo.shape), _const_spec(npost.shape),
        ],
        out_specs=[
            pl.BlockSpec((T, D_MODEL), lambda i: (i, 0)),
            pl.BlockSpec((8, D_FF), lambda i: (0, 0)),
        ],
        out_shape=[
            jax.ShapeDtypeStruct((L, D_MODEL), f32),
            jax.ShapeDtypeStruct((8, D_FF), f32),
        ],
        scratch_shapes=[
            pltpu.VMEM((T + 8, D_FF), f32),
            pltpu.VMEM((T, D_FF), bf16),
        ],
        compiler_params=pltpu.CompilerParams(
            dimension_semantics=("arbitrary",), vmem_limit_bytes=VMEM_LIMIT),
        name="ffn_prompt",
    )(x, npre, wffn, cw, cb, wo, npost)


def _ffn_sample_body(x_ref, cst_ref, npre_ref, wffn_ref, cw_ref, cb_ref, wo_ref, npost_ref,
                     y_ref, conv_out_ref, up_scr, y_scr, *, NB, NT):
    T = NB * NT
    for t in range(CONV_W - 1):
        up_scr[t * NB:(t + 1) * NB, :] = cst_ref[:, t, :]
    base = (CONV_W - 1) * NB
    x = x_ref[...]
    h = _rms(x, npre_ref[...]).astype(bf16)
    _ffn_columns(h, wffn_ref, cw_ref, cb_ref, up_scr, y_scr, T=T, base=base, shift=NB)
    f = _dot(y_scr[...], wo_ref[...])
    y = x + _rms(f, npost_ref[...])
    for t in range(NT):
        y_ref[:, t, :] = y[t * NB:(t + 1) * NB, :]
    for t in range(CONV_W - 1):
        conv_out_ref[:, t, :] = up_scr[T + t * NB:T + (t + 1) * NB, :]


def _ffn_sample(x, cst, npre, wffn, cw, cb, wo, npost):
    NB = cst.shape[0]
    T = x.shape[0]
    NT = T // NB
    body = functools.partial(_ffn_sample_body, NB=NB, NT=NT)
    return pl.pallas_call(
        body,
        out_shape=[
            jax.ShapeDtypeStruct((NB, NT, D_MODEL), f32),
            jax.ShapeDtypeStruct((NB, CONV_W - 1, D_FF), f32),
        ],
        scratch_shapes=[
            pltpu.VMEM((T + (CONV_W - 1) * NB, D_FF), f32),
            pltpu.VMEM((T, D_FF), bf16),
        ],
        compiler_params=pltpu.CompilerParams(vmem_limit_bytes=VMEM_LIMIT),
        name="ffn_sample",
    )(x, cst, npre, wffn, cw, cb, wo, npost)


def _prep_w_in_body(*refs, n_plain, per_step):
    o_ref = refs[-1]
    j = pl.program_id(0)
    for k, wt_ref in enumerate(refs[:-1]):
        x = wt_ref[...]
        r = lax.broadcasted_iota(jnp.int32, x.shape, 0)
        x = jnp.where((j * per_step + k < n_plain) | (r < GLA_RANK), x, 0.0)
        o_ref[:, k * PREP_ROWS:(k + 1) * PREP_ROWS] = x.T.astype(bf16)


def _prep_w_in(w_in):
    d_in, n_cols = w_in.shape
    head = C_GA
    tail_src = head + GLA_RANK
    n_head = head // PREP_ROWS
    n_tail = (n_cols - tail_src) // PREP_ROWS
    assert head % PREP_ROWS == 0 and (n_cols - tail_src) % PREP_ROWS == 0
    n_plain = n_head + n_tail
    assert C_RA == n_plain * PREP_ROWS
    per_step = 2
    assert (n_plain + 1) % per_step == 0

    def row_off(blk):
        off = jnp.where(blk < n_head, blk * PREP_ROWS,
                        jnp.where(blk < n_plain, tail_src + (blk - n_head) * PREP_ROWS, head))
        return pl.multiple_of(off, 8)

    def in_spec(k):
        return pl.BlockSpec((pl.Element(PREP_ROWS), pl.Element(d_in)), lambda j: (row_off(j * per_step + k), 0))

    wt = jnp.swapaxes(w_in, 0, 1)
    return pl.pallas_call(
        functools.partial(_prep_w_in_body, n_plain=n_plain, per_step=per_step),
        grid=((n_plain + 1) // per_step,),
        in_specs=[in_spec(k) for k in range(per_step)],
        out_specs=pl.BlockSpec((d_in, per_step * PREP_ROWS), lambda j: (0, j)),
        out_shape=jax.ShapeDtypeStruct((d_in, (n_plain + 1) * PREP_ROWS), bf16),
        compiler_params=pltpu.CompilerParams(dimension_semantics=("arbitrary",)),
        name="prep_w_in",
    )(*([wt] * per_step))


def kernel(x_prompt, x_sample, state_gla, cache_swa_k, cache_swa_v, state_ffn_conv, norm_mix_pre, norm_mix_post, w_in, w_gate_up, b_gate, gla_norm, sinks, w_branch_a, w_branch_b, w_out, norm_ffn_pre, norm_ffn_post, w_ffn_in, conv_w, conv_b, w_ffn_out):
    depth = w_in.shape[0]
    assert depth == 1
    l = 0
    B, L, _ = x_prompt.shape
    assert B == 1
    NBS, NT, _ = x_sample.shape

    win = _prep_w_in(w_in[l])
    wup = jnp.zeros((RA_PAD, GLA_K), f32).at[:GLA_RANK].set(w_gate_up[l]).astype(bf16)
    bg = b_gate[l].reshape(1, GLA_K)
    gn = gla_norm[l].reshape(1, GLA_DV)
    npre = norm_mix_pre[l].reshape(1, D_MODEL)
    npost = norm_mix_post[l].reshape(1, D_MODEL)
    wba = w_branch_a[l].astype(bf16)
    wbb = w_branch_b[l].astype(bf16)
    wout = w_out[l].astype(bf16)
    fpre = norm_ffn_pre[l].reshape(1, D_MODEL)
    fpost = norm_ffn_post[l].reshape(1, D_MODEL)
    wffn = w_ffn_in[l].astype(bf16)
    cw = conv_w[l]
    cb = conv_b[l].reshape(1, D_FF)
    wo = w_ffn_out[l].astype(bf16)
    sk = sinks[l]

    x1, st_p, k_p, v_p = _mix_prompt(x_prompt[0], sk, npre, win, wup, bg, gn, wba, wbb, wout, npost, T=256)
    y_p, conv_p = _ffn_prompt(x1, fpre, wffn, cw, cb, wo, fpost, T=512)

    y_prompt = y_p[None]
    gla_state_prompt = st_p.reshape(1, 1, GLA_HEADS, GLA_DK, GLA_DV)
    swa_k_prompt = jnp.transpose(k_p.reshape(SWA_KV_HEADS, SWA_HD, WINDOW), (2, 0, 1))[None, None]
    swa_v_prompt = jnp.transpose(v_p.reshape(SWA_KV_HEADS, SWA_HD, WINDOW), (2, 0, 1))[None, None]
    conv_prompt = conv_p[8 - (CONV_W - 1):].reshape(1, 1, CONV_W - 1, D_FF)

    qe, kl, e3, oin, va, ga, qb, kb, vb, gta, gtb = _pre_sample(x_sample, npre, win, wup, bg)
    kt = jnp.transpose(cache_swa_k[l], (0, 2, 3, 1)).reshape(NBS, SWA_KV, WINDOW)
    vt = jnp.transpose(cache_swa_v[l], (0, 2, 3, 1)).reshape(NBS, SWA_KV, WINDOW)
    oa_raw, ob, s1, kt1, vt1 = _state_sample(
        sk, qe, kl, e3, oin, va, qb, kb, vb, state_gla[l].reshape(NBS, GLA_K, GLA_DV), kt, vt, NT=NT, BB=16)
    x1s = _post_sample(x_sample, oa_raw, ga, ob, gta, gtb, gn, wba, wbb, wout, npost)
    y_sample, conv_s = _ffn_sample(x1s, state_ffn_conv[l], fpre, wffn, cw, cb, wo, fpost)

    def cache_out(t):
        return jnp.transpose(t.reshape(NBS, SWA_KV_HEADS, SWA_HD, WINDOW), (0, 3, 1, 2))[None]

    gla_state_sample = s1.reshape(1, NBS, GLA_HEADS, GLA_DK, GLA_DV)
    swa_k_sample = cache_out(kt1)
    swa_v_sample = cache_out(vt1)
    conv_sample = conv_s[None]
    return (y_prompt, y_sample, gla_state_prompt, gla_state_sample, swa_k_prompt, swa_v_prompt,
            swa_k_sample, swa_v_sample, conv_prompt, conv_sample)
```

```python
import functools

import jax
import jax.numpy as jnp
from jax import lax
from jax.experimental import pallas as pl
from jax.experimental.pallas import tpu as pltpu

f32 = jnp.float32
bf16 = jnp.bfloat16

D_MODEL = 1024
GLA_HEADS = 4
GLA_DK = 64
GLA_DV = 128
GLA_RANK = 16
GLA_TAU = 16.0
GLA_CHUNK = 64
GLA_SAFE_DECAY = 60.0
SWA_HEADS = 8
SWA_KV_HEADS = 2
SWA_HD = 64
WINDOW = 128
D_FF = 2816
CONV_W = 3
EPS = 1e-6
GLA_K = GLA_HEADS * GLA_DK
GLA_V = GLA_HEADS * GLA_DV
SWA_Q = SWA_HEADS * SWA_HD
SWA_KV = SWA_KV_HEADS * SWA_HD
LANES = 128
LOG2E = 1.4426950408889634

C_QA = 0
C_KA = C_QA + GLA_K
C_VA = C_KA + GLA_K
C_GA = C_VA + GLA_V
C_QB = C_GA + GLA_V
C_KB = C_QB + SWA_Q
C_VB = C_KB + SWA_KV
C_GTA = C_VB + SWA_KV
C_GTB = C_GTA + D_MODEL
C_RA = C_GTB + D_MODEL
RA_PAD = LANES
PREP_ROWS = 256
IN_COLS_PAD = C_RA + PREP_ROWS

MIX_BLOCK = 256
FFN_BLOCK = 512
FFN_COLS = 256
STATE_SEQS = 16
VMEM_LIMIT = 56 * 1024 * 1024


def _dot(a, b):
    return jnp.dot(a, b, preferred_element_type=f32)


def _dot_nt(a, b):
    return lax.dot_general(a, b, (((1,), (1,)), ((), ())), preferred_element_type=f32)


def _dot_tn(a, b):
    return lax.dot_general(a, b, (((0,), (0,)), ((), ())), preferred_element_type=f32)


def _rms(x, w):
    return x * lax.rsqrt(jnp.mean(x * x, axis=-1, keepdims=True) + EPS) * w


def _gelu_tanh(x):
    k = -2.0 * 0.7978845608028654 * LOG2E
    return x / (1.0 + jnp.exp2(x * (k + (k * 0.044715) * (x * x))))


def _split_hi_lo(x):
    hi = x.astype(bf16)
    lo = (x - hi.astype(f32)).astype(bf16)
    return hi, lo


def _chunk_cumsum(la, chunk):
    n = la.shape[0]
    r = lax.broadcasted_iota(jnp.int32, (n, n), 0)
    c = lax.broadcasted_iota(jnp.int32, (n, n), 1)
    tri = jnp.where((c <= r) & ((r // chunk) == (c // chunk)), 1.0, 0.0).astype(bf16)
    hi, lo = _split_hi_lo(la)
    return _dot(tri, hi) + _dot(tri, lo)


def _even_head_lanes(shape):
    lane = lax.broadcasted_iota(jnp.int32, shape, len(shape) - 1)
    return (lane % LANES) < GLA_DK


def _gla_out_norm(o, gn_ref, ga):
    outs = []
    for h in range(GLA_HEADS):
        oh = o[:, h * GLA_DV:(h + 1) * GLA_DV]
        outs.append(_rms(oh, gn_ref[...]))
    on = jnp.concatenate(outs, axis=1)
    return on * (ga * jax.nn.sigmoid(ga))


def _mix_tail(x, oa, ob, gate_a, gate_b, wba_ref, wbb_ref, wout_ref, npost_ref):
    merged = (jax.nn.sigmoid(gate_a) * _dot(oa.astype(bf16), wba_ref[...])
              + jax.nn.sigmoid(gate_b) * _dot(ob.astype(bf16), wbb_ref[...]))
    m = _dot(merged.astype(bf16), wout_ref[...])
    return x + _rms(m, npost_ref[...])


def _alibi_slope(head):
    return LOG2E * 2.0 ** (-(8.0 / SWA_HEADS) * (head + 1))


SWA_Q_SCALE = LOG2E * SWA_HD ** -0.5


def _kv_variants(x):
    lo = _even_head_lanes(x.shape)
    xr = pltpu.roll(x, SWA_HD, 1)
    zero = jnp.zeros_like(x)
    h0_lo = jnp.where(lo, x, zero).astype(bf16)
    h1_hi = jnp.where(lo, zero, x).astype(bf16)
    h1_lo = jnp.where(lo, xr, zero).astype(bf16)
    h0_hi = jnp.where(lo, zero, xr).astype(bf16)
    return (h0_lo, h0_hi), (h1_lo, h1_hi)


def _softmax_sink(s, sink):
    m = jnp.maximum(jnp.max(s, axis=-1, keepdims=True), sink)
    p = jnp.exp2(s - m)
    denom = jnp.sum(p, axis=-1, keepdims=True) + jnp.exp2(sink - m)
    return p, 1.0 / denom


def _mix_prompt_body(sink_ref, x_ref, npre_ref, win_ref, wup_ref, bg_ref, gn_ref,
                     wba_ref, wbb_ref, wout_ref, npost_ref,
                     y_ref, st_out_ref, k_out_ref, v_out_ref,
                     st_scr, kcat_scr, vcat_scr, oa_scr, ob_scr, gate_scr, inter_scr, *, T):
    i = pl.program_id(0)
    W = WINDOW
    C = GLA_CHUNK

    @pl.when(i == 0)
    def _():
        st_scr[...] = jnp.zeros_like(st_scr)
        kcat_scr[0:W, :] = jnp.zeros((W, SWA_KV), f32)
        vcat_scr[0:W, :] = jnp.zeros((W, SWA_KV), f32)

    @pl.when(i > 0)
    def _():
        kcat_scr[0:W, :] = kcat_scr[T:T + W, :]
        vcat_scr[0:W, :] = vcat_scr[T:T + W, :]

    x = x_ref[...]
    rms_f = lax.rsqrt(jnp.mean(x * x, axis=-1, keepdims=True) + EPS)
    h = (x * npre_ref[...]).astype(bf16)
    rms_b = {n: jnp.broadcast_to(rms_f, (T, n)) for n in (LANES, 2 * LANES)}

    def proj(c0, n):
        w = 2 * LANES if n % (2 * LANES) == 0 else LANES
        return jnp.concatenate([_dot(h, win_ref[:, c:c + w]) * rms_b[w] for c in range(c0, c0 + n, w)], axis=1)


    xg = _dot(proj(C_RA, RA_PAD).astype(bf16), wup_ref[...]) + bg_ref[...]
    kcat_scr[W:W + T, :] = proj(C_KB, SWA_KV)
    vcat_scr[W:W + T, :] = proj(C_VB, SWA_KV)
    qb = (proj(C_QB, SWA_Q) * SWA_Q_SCALE).astype(bf16)
    la = jax.nn.log_sigmoid(xg) * (1.0 / GLA_TAU)
    b = _chunk_cumsum(la, C)
    decay_floor = jnp.min(b)
    qa = proj(C_QA, GLA_K)
    ka = proj(C_KA, GLA_K)
    va_b = proj(C_VA, GLA_V).astype(bf16)

    qe = qa * jnp.exp(b) * (GLA_DK ** -0.5)
    ke = (ka * jnp.exp(-b)).astype(bf16)
    even = _even_head_lanes((T, GLA_K))
    qe_even = jnp.where(even, qe, 0.0).astype(bf16)
    qe_odd = jnp.where(even, 0.0, qe).astype(bf16)
    k_var = _kv_variants(kcat_scr[...])
    v_var = _kv_variants(vcat_scr[...])

    r2 = lax.broadcasted_iota(jnp.int32, (2 * C, 2 * C), 0)
    c2 = lax.broadcasted_iota(jnp.int32, (2 * C, 2 * C), 1)
    pair_causal = ((r2 // C) == (c2 // C)) & ((c2 % C) <= (r2 % C))
    even_c = _even_head_lanes((C, LANES))
    st = [st_scr[:, p * LANES:(p + 1) * LANES] for p in range(GLA_HEADS // 2)]

    def gla_scores(c):
        rows = slice(c * C, (c + 1) * C)
        out = []
        for p in range(GLA_HEADS // 2):
            lanes = slice(p * LANES, (p + 1) * LANES)
            q2 = jnp.concatenate([qe_even[rows, lanes], qe_odd[rows, lanes]], axis=0)
            ke_p = ke[rows, lanes]
            rhs = jnp.concatenate([ke_p, ke_p, st[p].astype(bf16)], axis=0)
            r = _dot_nt(q2, rhs)
            att = jnp.where(pair_causal, r[:, 0:2 * C], 0.0).astype(bf16)
            out.append((att, r[:, 2 * C:]))
        return out

    def gla_update(c, sc):
        rows = slice(c * C, (c + 1) * C)
        b_c = b[rows]
        bl = b_c[C - 1:C, :]
        kl = ka[rows] * jnp.exp(bl - b_c)
        ebl = jnp.exp(bl)
        for p in range(GLA_HEADS // 2):
            lanes = slice(p * LANES, (p + 1) * LANES)
            att, inter = sc[p]
            v2 = jnp.concatenate(
                [va_b[rows, (2 * p) * GLA_DV:(2 * p + 1) * GLA_DV],
                 va_b[rows, (2 * p + 1) * GLA_DV:(2 * p + 2) * GLA_DV]], axis=0)
            o2 = inter + _dot(att, v2)
            for e in range(2):
                hl = slice((2 * p + e) * GLA_DV, (2 * p + e + 1) * GLA_DV)
                oa_scr[rows, hl] = o2[e * C:(e + 1) * C]
                inter_scr[rows, hl] = inter[e * C:(e + 1) * C]
            kl_p = kl[:, lanes]
            kl_stack = jnp.concatenate(
                [jnp.where(even_c, kl_p, 0.0), jnp.where(even_c, 0.0, kl_p)], axis=0).astype(bf16)
            st[p] = st[p] * ebl[:, lanes] + _dot_tn(v2, kl_stack)

    qi = lax.broadcasted_iota(jnp.int32, (W, 2 * W), 0)
    kc = lax.broadcasted_iota(jnp.int32, (W, 2 * W), 1)
    rel = qi + W - kc
    relf = rel.astype(f32)
    in_window = (rel >= 0) & (rel < W)

    def swa_probs(j, kv):
        qrows = slice(j * W, (j + 1) * W)
        band = slice(j * W, j * W + 2 * W)
        if j == 0:
            mask = in_window & ((kc >= W) | (i > 0))
        else:
            mask = in_window
        pairs = (2 * kv, 2 * kv + 1)
        q2 = jnp.concatenate([qb[qrows, p * LANES:(p + 1) * LANES] for p in pairs], axis=0)
        out = []
        for e in range(2):
            s2 = _dot_nt(q2, k_var[kv][e][band])
            probs = []
            for half, p in enumerate(pairs):
                hd = 2 * p + e
                s = s2[half * W:(half + 1) * W]
                s = jnp.where(mask, s - _alibi_slope(hd) * relf, -jnp.inf)
                pr, inv = _softmax_sink(s, sink_ref[hd] * LOG2E)
                probs.append((pr * inv).astype(bf16))
            out.append(jnp.concatenate(probs, axis=0))
        return out

    def swa_out(j, kv, probs):
        qrows = slice(j * W, (j + 1) * W)
        band = slice(j * W, j * W + 2 * W)
        o2 = _dot(probs[0], v_var[kv][0][band]) + _dot(probs[1], v_var[kv][1][band])
        for half, p in enumerate((2 * kv, 2 * kv + 1)):
            ob_scr[qrows, p * LANES:(p + 1) * LANES] = o2[half * W:(half + 1) * W]

    n_chunks = T // C
    assert n_chunks == (T // W) * SWA_KV_HEADS
    gw = 2 * D_MODEL // n_chunks
    for idx in range(n_chunks):
        j, kv = idx // SWA_KV_HEADS, idx % SWA_KV_HEADS
        probs = swa_probs(j, kv)
        sc = gla_scores(idx)
        gate_scr[:, idx * gw:(idx + 1) * gw] = proj(C_GTA + idx * gw, gw)
        gla_update(idx, sc)
        swa_out(j, kv, probs)
    for p in range(GLA_HEADS // 2):
        st_scr[:, p * LANES:(p + 1) * LANES] = st[p]
    ga = proj(C_GA, GLA_V)
    gated_b = jax.nn.sigmoid(gate_scr[:, D_MODEL:2 * D_MODEL]) * _dot(ob_scr[...].astype(bf16), wbb_ref[...])
    sig_a = jax.nn.sigmoid(gate_scr[:, 0:D_MODEL])

    def finish(oa_raw):
        oa = _gla_out_norm(oa_raw, gn_ref, ga)
        merged = sig_a * _dot(oa.astype(bf16), wba_ref[...]) + gated_b
        m = _dot(merged.astype(bf16), wout_ref[...])
        y_ref[...] = x + _rms(m, npost_ref[...])

    finish(oa_scr[...])

    @pl.when(decay_floor < -GLA_SAFE_DECAY)
    def _():
        qs = qa * (GLA_DK ** -0.5)
        va_f = va_b.astype(f32)
        pos = lax.broadcasted_iota(jnp.int32, (T, 1), 0) % C
        er = lax.broadcasted_iota(jnp.int32, (GLA_K, GLA_V), 0)
        ec = lax.broadcasted_iota(jnp.int32, (GLA_K, GLA_V), 1)
        expand = jnp.where((er // GLA_DK) == (ec // GLA_DV), 1.0, 0.0).astype(bf16)

        def offset_term(d, acc):
            valid = pos >= d
            expo = jnp.where(valid, b - pltpu.roll(b, d, 0), 0.0)
            prod = jnp.where(valid, qs * pltpu.roll(ka, d, 0) * jnp.exp(expo), 0.0)
            return acc + _dot(prod.astype(bf16), expand) * pltpu.roll(va_f, d, 0)

        intra = lax.fori_loop(0, C, offset_term, jnp.zeros((T, GLA_V), f32))
        finish(inter_scr[...] + intra)

    @pl.when(i == pl.num_programs(0) - 1)
    def _():
        st_out_ref[...] = st_scr[...].T
        k_out_ref[...] = kcat_scr[T:T + W, :].T
        v_out_ref[...] = vcat_scr[T:T + W, :].T


def _const_spec(shape):
    nd = len(shape)
    return pl.BlockSpec(shape, lambda i: (0,) * nd, pipeline_mode=pl.Buffered(1))


def _mix_prompt(x, sinks, npre, win, wup, bg, gn, wba, wbb, wout, npost, *, T):
    L = x.shape[0]
    nb = L // T
    body = functools.partial(_mix_prompt_body, T=T)
    return pl.pallas_call(
        body,
        grid=(nb,),
        in_specs=[
            pl.BlockSpec(memory_space=pltpu.SMEM),
            pl.BlockSpec((T, D_MODEL), lambda i: (i, 0)),
            _const_spec(npre.shape), _const_spec(win.shape), _const_spec(wup.shape),
            _const_spec(bg.shape), _const_spec(gn.shape), _const_spec(wba.shape),
            _const_spec(wbb.shape), _const_spec(wout.shape), _const_spec(npost.shape),
        ],
        out_specs=[
            pl.BlockSpec((T, D_MODEL), lambda i: (i, 0)),
            pl.BlockSpec((GLA_K, GLA_DV), lambda i: (0, 0)),
            pl.BlockSpec((WINDOW, SWA_KV), lambda i: (0, 0)),
            pl.BlockSpec((WINDOW, SWA_KV), lambda i: (0, 0)),
        ],
        out_shape=[
            jax.ShapeDtypeStruct((L, D_MODEL), f32),
            jax.ShapeDtypeStruct((GLA_K, GLA_DV), f32),
            jax.ShapeDtypeStruct((WINDOW, SWA_KV), f32),
            jax.ShapeDtypeStruct((WINDOW, SWA_KV), f32),
        ],
        scratch_shapes=[
            pltpu.VMEM((GLA_DV, GLA_K), f32),
            pltpu.VMEM((T + WINDOW, SWA_KV), f32),
            pltpu.VMEM((T + WINDOW, SWA_KV), f32),
            pltpu.VMEM((T, GLA_V), f32),
            pltpu.VMEM((T, SWA_Q), f32),
            pltpu.VMEM((T, 2 * D_MODEL), f32),
            pltpu.VMEM((T, GLA_V), f32),
        ],
        compiler_params=pltpu.CompilerParams(
            dimension_semantics=("arbitrary",), vmem_limit_bytes=VMEM_LIMIT),
        name="mix_prompt",
    )(sinks, x, npre, win, wup, bg, gn, wba, wbb, wout, npost)


def _pre_sample_body(x_ref, npre_ref, win_ref, wup_ref, bg_ref,
                     qe_ref, kl_ref, e3_ref, oin_ref, va_ref, ga_ref, qb_ref, kb_ref, vb_ref,
                     gta_ref, gtb_ref, x_scr, *, NB, NT):
    for t in range(NT):
        x_scr[t * NB:(t + 1) * NB, :] = x_ref[:, t, :]
    h = _rms(x_scr[...], npre_ref[...]).astype(bf16)

    def proj(c0, n):
        return _dot(h, win_ref[:, c0:c0 + n])

    def blk(val, t):
        return val[t * NB:(t + 1) * NB, :]

    xg = _dot(proj(C_RA, RA_PAD).astype(bf16), wup_ref[...]) + bg_ref[...]
    qa = proj(C_QA, GLA_K) * (GLA_DK ** -0.5)
    ka = proj(C_KA, GLA_K)
    va = proj(C_VA, GLA_V)
    va_ref[...] = va
    la = jax.nn.log_sigmoid(xg) * (1.0 / GLA_TAU)
    b = [blk(la, 0)]
    for t in range(1, NT):
        b.append(b[-1] + blk(la, t))
    e3_ref[...] = jnp.exp(b[NT - 1])
    for t in range(NT):
        qe_ref[t * NB:(t + 1) * NB, :] = blk(qa, t) * jnp.exp(b[t])
        kl_ref[t * NB:(t + 1) * NB, :] = blk(ka, t) * jnp.exp(b[NT - 1] - b[t])
    pairs = [(t, j) for t in range(NT) for j in range(t + 1)]
    prods = [(blk(qa, t) * blk(ka, j) * jnp.exp(b[t] - b[j])).astype(bf16) for t, j in pairs]
    r = lax.broadcasted_iota(jnp.int32, (GLA_K, GLA_V), 0)
    c = lax.broadcasted_iota(jnp.int32, (GLA_K, GLA_V), 1)
    expand = jnp.where((r // GLA_DK) == (c // GLA_DV), 1.0, 0.0).astype(bf16)
    ga_ref[...] = proj(C_GA, GLA_V)
    qb_ref[...] = proj(C_QB, SWA_Q) * SWA_Q_SCALE
    kb_ref[...] = proj(C_KB, SWA_KV)
    vb_ref[...] = proj(C_VB, SWA_KV)
    att = _dot(jnp.concatenate(prods, axis=0), expand)
    gta_ref[...] = proj(C_GTA, D_MODEL)
    gtb_ref[...] = proj(C_GTB, D_MODEL)
    for t in range(NT):
        acc = None
        for idx, (tt, j) in enumerate(pairs):
            if tt != t:
                continue
            term = att[idx * NB:(idx + 1) * NB, :] * blk(va, j)
            acc = term if acc is None else acc + term
        oin_ref[t * NB:(t + 1) * NB, :] = acc


def _pre_sample(xs, npre, win, wup, bg):
    NB, NT, _ = xs.shape
    body = functools.partial(_pre_sample_body, NB=NB, NT=NT)
    widths = (GLA_K, GLA_K, None, GLA_V, GLA_V, GLA_V, SWA_Q, SWA_KV, SWA_KV, D_MODEL, D_MODEL)
    out_shape = [jax.ShapeDtypeStruct((NB, GLA_K) if w is None else (NT * NB, w), f32) for w in widths]
    return pl.pallas_call(
        body,
        out_shape=out_shape,
        scratch_shapes=[pltpu.VMEM((NB * NT, D_MODEL), f32)],
        compiler_params=pltpu.CompilerParams(vmem_limit_bytes=VMEM_LIMIT),
        name="pre_sample",
    )(xs, npre, win, wup, bg)


def _state_sample_body(sink_ref, qe_ref, kl_ref, e3_ref, oin_ref, va_ref, qb_ref, kb_ref, vb_ref,
                       s0_ref, kt_ref, vt_ref,
                       oa_ref, ob_ref, s1_ref, kt1_ref, vt1_ref, *, BB, NT):
    W = WINDOW
    SK = 2 * W
    HT = GLA_HEADS * NT
    HALF = SWA_HD
    hr = lax.broadcasted_iota(jnp.int32, (HT, GLA_K), 0) // NT
    hc = lax.broadcasted_iota(jnp.int32, (HT, GLA_K), 1) // GLA_DK
    own_head = hr == hc
    ones_rows = jnp.ones((16, GLA_DV), bf16)
    zero_rows = jnp.zeros((16, GLA_DV), bf16)
    zero_ht = jnp.zeros((HT, GLA_DV), bf16)
    G2 = 2 * NT
    row = lax.broadcasted_iota(jnp.int32, (G2, SK), 0)
    col = lax.broadcasted_iota(jnp.int32, (G2, SK), 1)
    rel = (row % NT) + W - col
    relf = rel.astype(f32)
    smask = (rel >= 0) & (rel < W)
    first_pair = lax.broadcasted_iota(jnp.int32, (G2, 1), 0) < NT
    pad_rows = jnp.zeros((8 - NT, SWA_KV), f32)
    pad_lanes = jnp.zeros((SWA_KV, SK - W - 8), f32)
    zero_half = jnp.zeros((HALF, SK), bf16)

    def head_variants(cat_t, kv):
        blk = cat_t[kv * HALF:(kv + 1) * HALF]
        return (jnp.concatenate([blk, zero_half], axis=0), jnp.concatenate([zero_half, blk], axis=0))

    pending = []
    for bi in range(BB):
        s0 = s0_ref[bi]
        q4 = qe_ref[:, bi, :]
        qm = jnp.where(own_head, jnp.concatenate([q4] * GLA_HEADS, axis=0), 0.0).astype(bf16)
        o_inter = _dot(qm, s0.astype(bf16))
        for hd in range(GLA_HEADS):
            lanes = slice(hd * GLA_DV, (hd + 1) * GLA_DV)
            oa_ref[:, bi, lanes] = o_inter[hd * NT:(hd + 1) * NT, :] + oin_ref[:, bi, lanes]
        k4 = kl_ref[:, bi, :]
        km = jnp.where(own_head, jnp.concatenate([k4] * GLA_HEADS, axis=0), 0.0).astype(bf16)
        e = e3_ref[bi:bi + 1, :]
        e_hi = e.astype(bf16)
        r1 = e - e_hi.astype(f32)
        e_mid = r1.astype(bf16)
        e_lo = (r1 - e_mid.astype(f32)).astype(bf16)
        e_rows = jnp.concatenate([e_hi, e_mid, e_lo, jnp.zeros((13, GLA_K), bf16)], axis=0)
        lhs = jnp.concatenate([km, e_rows], axis=0)
        v4 = va_ref[:, bi, :].astype(bf16)
        vrep = jnp.concatenate([v4[:, hd * GLA_DV:(hd + 1) * GLA_DV] for hd in range(GLA_HEADS)], axis=0)
        rhs = jnp.concatenate([jnp.concatenate([vrep, zero_ht], axis=1),
                               jnp.concatenate([zero_rows, ones_rows], axis=1)], axis=0)
        res = _dot_tn(lhs, rhs)
        s1_ref[bi] = res[:, GLA_DV:] * s0 + res[:, :GLA_DV]

        kt = kt_ref[bi]
        vt = vt_ref[bi]
        knew_t = jnp.concatenate([kb_ref[:, bi, :], pad_rows], axis=0).T
        vnew_t = jnp.concatenate([vb_ref[:, bi, :], pad_rows], axis=0).T
        kt1_ref[bi] = jnp.concatenate([kt[:, NT:], knew_t[:, 0:NT]], axis=1)
        vt1_ref[bi] = jnp.concatenate([vt[:, NT:], vnew_t[:, 0:NT]], axis=1)
        kcat = jnp.concatenate([kt, knew_t, pad_lanes], axis=1).astype(bf16)
        vcat = jnp.concatenate([vt, vnew_t, pad_lanes], axis=1).astype(bf16)
        q4b = qb_ref[:, bi, :].astype(bf16)
        for kv in range(SWA_KV_HEADS):
            p0 = 2 * kv
            q8 = jnp.concatenate([q4b[:, p0 * LANES:(p0 + 1) * LANES],
                                  q4b[:, (p0 + 1) * LANES:(p0 + 2) * LANES]], axis=0)
            scores = [_dot(q8, kvar) for kvar in head_variants(kcat, kv)]
            pending.append((bi, kv, scores, head_variants(vcat, kv)))

    for bi, kv, scores, v_vars in pending:
        p0 = 2 * kv
        o8_t = None
        for e_ in range(2):
            h_first = 2 * p0 + e_
            h_second = 2 * (p0 + 1) + e_
            slope = jnp.where(first_pair, _alibi_slope(h_first), _alibi_slope(h_second))
            sink = jnp.where(first_pair, sink_ref[h_first] * LOG2E, sink_ref[h_second] * LOG2E)
            s = jnp.where(smask, scores[e_] - slope * relf, -jnp.inf)
            pr, inv = _softmax_sink(s, sink)
            o_t = _dot_nt(v_vars[e_], (pr * inv).astype(bf16))
            o8_t = o_t if o8_t is None else o8_t + o_t
        o8 = o8_t.T
        ob_ref[:, bi, p0 * LANES:(p0 + 1) * LANES] = o8[0:NT, :]
        ob_ref[:, bi, (p0 + 1) * LANES:(p0 + 2) * LANES] = o8[NT:2 * NT, :]


def _state_sample(sinks, qe, kl, e3, oin, va, qb, kb, vb, s0, kt, vt, *, NT, BB):
    NBS = s0.shape[0]
    assert NBS % BB == 0
    body = functools.partial(_state_sample_body, BB=BB, NT=NT)

    def tm(a):
        return a.reshape(NT, NBS, a.shape[-1])

    def rows(n):
        return pl.BlockSpec((NT, BB, n), lambda i: (0, i, 0))

    def per_seq(shape):
        return pl.BlockSpec((BB,) + shape, lambda i: (i, 0, 0))

    oa, ob, s1, kt1, vt1 = pl.pallas_call(
        body,
        grid=(NBS // BB,),
        in_specs=[
            pl.BlockSpec(memory_space=pltpu.SMEM),
            rows(GLA_K), rows(GLA_K), pl.BlockSpec((BB, GLA_K), lambda i: (i, 0)),
            rows(GLA_V), rows(GLA_V), rows(SWA_Q), rows(SWA_KV), rows(SWA_KV),
            per_seq((GLA_K, GLA_DV)), per_seq((SWA_KV, WINDOW)), per_seq((SWA_KV, WINDOW)),
        ],
        out_specs=[
            rows(GLA_V), rows(SWA_Q),
            per_seq((GLA_K, GLA_DV)), per_seq((SWA_KV, WINDOW)), per_seq((SWA_KV, WINDOW)),
        ],
        out_shape=[
            jax.ShapeDtypeStruct((NT, NBS, GLA_V), f32),
            jax.ShapeDtypeStruct((NT, NBS, SWA_Q), f32),
            jax.ShapeDtypeStruct((NBS, GLA_K, GLA_DV), f32),
            jax.ShapeDtypeStruct((NBS, SWA_KV, WINDOW), f32),
            jax.ShapeDtypeStruct((NBS, SWA_KV, WINDOW), f32),
        ],
        compiler_params=pltpu.CompilerParams(
            dimension_semantics=("arbitrary",), vmem_limit_bytes=VMEM_LIMIT),
        name="state_sample",
    )(sinks, tm(qe), tm(kl), e3, tm(oin), tm(va), tm(qb), tm(kb), tm(vb), s0, kt, vt)
    return oa.reshape(NT * NBS, GLA_V), ob.reshape(NT * NBS, SWA_Q), s1, kt1, vt1


def _post_sample_body(x_ref, oa_ref, ga_ref, ob_ref, gta_ref, gtb_ref, gn_ref,
                      wba_ref, wbb_ref, wout_ref, npost_ref, y_ref, x_scr, *, NB, NT):
    for t in range(NT):
        x_scr[t * NB:(t + 1) * NB, :] = x_ref[:, t, :]
    oa = _gla_out_norm(oa_ref[...], gn_ref, ga_ref[...])
    y_ref[...] = _mix_tail(x_scr[...], oa, ob_ref[...], gta_ref[...], gtb_ref[...],
                           wba_ref, wbb_ref, wout_ref, npost_ref)


def _post_sample(xs, oa, ga, ob, gta, gtb, gn, wba, wbb, wout, npost):
    NB, NT, _ = xs.shape
    return pl.pallas_call(
        functools.partial(_post_sample_body, NB=NB, NT=NT),
        out_shape=jax.ShapeDtypeStruct((NT * NB, D_MODEL), f32),
        scratch_shapes=[pltpu.VMEM((NT * NB, D_MODEL), f32)],
        compiler_params=pltpu.CompilerParams(vmem_limit_bytes=VMEM_LIMIT),
        name="post_sample",
    )(xs, oa, ga, ob, gta, gtb, gn, wba, wbb, wout, npost)


def _ffn_columns(h, wffn_ref, cw_ref, cb_ref, up_scr, y_scr, *, T, base, shift):
    for c0 in range(0, D_FF, FFN_COLS):
        cols = slice(c0, c0 + FFN_COLS)
        u = _dot(h, wffn_ref[:, c0:c0 + FFN_COLS])
        g = _dot(h, wffn_ref[:, D_FF + c0:D_FF + c0 + FFN_COLS])
        up_scr[base:base + T, cols] = u
        u1 = up_scr[base - shift:base - shift + T, cols]
        u2 = up_scr[base - 2 * shift:base - 2 * shift + T, cols]
        cv = (cb_ref[:, cols] + cw_ref[2:3, cols] * u + cw_ref[1:2, cols] * u1 + cw_ref[0:1, cols] * u2)
        y_scr[:, cols] = (_gelu_tanh(cv) * g).astype(bf16)


def _ffn_prompt_body(x_ref, npre_ref, wffn_ref, cw_ref, cb_ref, wo_ref, npost_ref,
                     y_ref, conv_out_ref, up_scr, y_scr, *, T):
    i = pl.program_id(0)
    base = 8

    @pl.when(i == 0)
    def _():
        up_scr[0:base, :] = jnp.zeros((base, D_FF), f32)

    @pl.when(i > 0)
    def _():
        up_scr[0:base, :] = up_scr[T:T + base, :]

    x = x_ref[...]
    h = _rms(x, npre_ref[...]).astype(bf16)
    _ffn_columns(h, wffn_ref, cw_ref, cb_ref, up_scr, y_scr, T=T, base=base, shift=1)
    f = _dot(y_scr[...], wo_ref[...])
    y_ref[...] = x + _rms(f, npost_ref[...])

    @pl.when(i == pl.num_programs(0) - 1)
    def _():
        conv_out_ref[...] = up_scr[T:T + base, :]


def _ffn_prompt(x, npre, wffn, cw, cb, wo, npost, *, T):
    L = x.shape[0]
    body = functools.partial(_ffn_prompt_body, T=T)
    return pl.pallas_call(
        body,
        grid=(L // T,),
        in_specs=[
            pl.BlockSpec((T, D_MODEL), lambda i: (i, 0)),
            _const_spec(npre.shape), _const_spec(wffn.shape), _const_spec(cw.shape),
            _const_spec(cb.shape), _const_spec(wo.shape), _const_spec(npost.shape),
        ],
        out_specs=[
            pl.BlockSpec((T, D_MODEL), lambda i: (i, 0)),
            pl.BlockSpec((8, D_FF), lambda i: (0, 0)),
        ],
        out_shape=[
            jax.ShapeDtypeStruct((L, D_MODEL), f32),
            jax.ShapeDtypeStruct((8, D_FF), f32),
        ],
        scratch_shapes=[
            pltpu.VMEM((T + 8, D_FF), f32),
            pltpu.VMEM((T, D_FF), bf16),
        ],
        compiler_params=pltpu.CompilerParams(
            dimension_semantics=("arbitrary",), vmem_limit_bytes=VMEM_LIMIT),
        name="ffn_prompt",
    )(x, npre, wffn, cw, cb, wo, npost)


def _ffn_sample_body(x_ref, cst_ref, npre_ref, wffn_ref, cw_ref, cb_ref, wo_ref, npost_ref,
                     y_ref, conv_out_ref, up_scr, y_scr, *, NB, NT):
    T = NB * NT
    for t in range(CONV_W - 1):
        up_scr[t * NB:(t + 1) * NB, :] = cst_ref[:, t, :]
    base = (CONV_W - 1) * NB
    x = x_ref[...]
    h = _rms(x, npre_ref[...]).astype(bf16)
    _ffn_columns(h, wffn_ref, cw_ref, cb_ref, up_scr, y_scr, T=T, base=base, shift=NB)
    f = _dot(y_scr[...], wo_ref[...])
    y = x + _rms(f, npost_ref[...])
    for t in range(NT):
        y_ref[:, t, :] = y[t * NB:(t + 1) * NB, :]
    for t in range(CONV_W - 1):
        conv_out_ref[:, t, :] = up_scr[T + t * NB:T + (t + 1) * NB, :]


def _ffn_sample(x, cst, npre, wffn, cw, cb, wo, npost):
    NB = cst.shape[0]
    T = x.shape[0]
    NT = T // NB
    body = functools.partial(_ffn_sample_body, NB=NB, NT=NT)
    return pl.pallas_call(
        body,
        out_shape=[
            jax.ShapeDtypeStruct((NB, NT, D_MODEL), f32),
            jax.ShapeDtypeStruct((NB, CONV_W - 1, D_FF), f32),
        ],
        scratch_shapes=[
            pltpu.VMEM((T + (CONV_W - 1) * NB, D_FF), f32),
            pltpu.VMEM((T, D_FF), bf16),
        ],
        compiler_params=pltpu.CompilerParams(vmem_limit_bytes=VMEM_LIMIT),
        name="ffn_sample",
    )(x, cst, npre, wffn, cw, cb, wo, npost)


def _prep_w_in_body(*refs, n_plain, per_step):
    o_ref = refs[-1]
    j = pl.program_id(0)
    for k, wt_ref in enumerate(refs[:-1]):
        x = wt_ref[...]
        r = lax.broadcasted_iota(jnp.int32, x.shape, 0)
        x = jnp.where((j * per_step + k < n_plain) | (r < GLA_RANK), x, 0.0)
        o_ref[:, k * PREP_ROWS:(k + 1) * PREP_ROWS] = x.T.astype(bf16)


def _prep_w_in(w_in):
    d_in, n_cols = w_in.shape
    head = C_GA
    tail_src = head + GLA_RANK
    n_head = head // PREP_ROWS
    n_tail = (n_cols - tail_src) // PREP_ROWS
    assert head % PREP_ROWS == 0 and (n_cols - tail_src) % PREP_ROWS == 0
    n_plain = n_head + n_tail
    assert C_RA == n_plain * PREP_ROWS
    per_step = 2
    assert (n_plain + 1) % per_step == 0

    def row_off(blk):
        off = jnp.where(blk < n_head, blk * PREP_ROWS,
                        jnp.where(blk < n_plain, tail_src + (blk - n_head) * PREP_ROWS, head))
        return pl.multiple_of(off, 8)

    def in_spec(k):
        return pl.BlockSpec((pl.Element(PREP_ROWS), pl.Element(d_in)), lambda j: (row_off(j * per_step + k), 0))

    wt = jnp.swapaxes(w_in, 0, 1)
    return pl.pallas_call(
        functools.partial(_prep_w_in_body, n_plain=n_plain, per_step=per_step),
        grid=((n_plain + 1) // per_step,),
        in_specs=[in_spec(k) for k in range(per_step)],
        out_specs=pl.BlockSpec((d_in, per_step * PREP_ROWS), lambda j: (0, j)),
        out_shape=jax.ShapeDtypeStruct((d_in, (n_plain + 1) * PREP_ROWS), bf16),
        compiler_params=pltpu.CompilerParams(dimension_semantics=("arbitrary",)),
        name="prep_w_in",
    )(*([wt] * per_step))


def kernel(x_prompt, x_sample, state_gla, cache_swa_k, cache_swa_v, state_ffn_conv, norm_mix_pre, norm_mix_post, w_in, w_gate_up, b_gate, gla_norm, sinks, w_branch_a, w_branch_b, w_out, norm_ffn_pre, norm_ffn_post, w_ffn_in, conv_w, conv_b, w_ffn_out):
    depth = w_in.shape[0]
    assert depth == 1
    l = 0
    B, L, _ = x_prompt.shape
    assert B == 1
    NBS, NT, _ = x_sample.shape
    assert L % MIX_BLOCK == 0 and L % FFN_BLOCK == 0 and NBS % STATE_SEQS == 0
    assert cache_swa_k.shape[2] == WINDOW and NT < 8

    win = _prep_w_in(w_in[l])
    wup = jnp.zeros((RA_PAD, GLA_K), f32).at[:GLA_RANK].set(w_gate_up[l]).astype(bf16)
    bg = b_gate[l].reshape(1, GLA_K)
    gn = gla_norm[l].reshape(1, GLA_DV)
    npre = norm_mix_pre[l].reshape(1, D_MODEL)
    npost = norm_mix_post[l].reshape(1, D_MODEL)
    wba = w_branch_a[l].astype(bf16)
    wbb = w_branch_b[l].astype(bf16)
    wout = w_out[l].astype(bf16)
    fpre = norm_ffn_pre[l].reshape(1, D_MODEL)
    fpost = norm_ffn_post[l].reshape(1, D_MODEL)
    wffn = w_ffn_in[l].astype(bf16)
    cw = conv_w[l]
    cb = conv_b[l].reshape(1, D_FF)
    wo = w_ffn_out[l].astype(bf16)
    sk = sinks[l]

    x1, st_p, k_p, v_p = _mix_prompt(x_prompt[0], sk, npre, win, wup, bg, gn, wba, wbb, wout, npost, T=MIX_BLOCK)
    y_p, conv_p = _ffn_prompt(x1, fpre, wffn, cw, cb, wo, fpost, T=FFN_BLOCK)

    y_prompt = y_p[None]
    gla_state_prompt = st_p.reshape(1, 1, GLA_HEADS, GLA_DK, GLA_DV)
    swa_k_prompt = jnp.transpose(k_p.reshape(SWA_KV_HEADS, SWA_HD, WINDOW), (2, 0, 1))[None, None]
    swa_v_prompt = jnp.transpose(v_p.reshape(SWA_KV_HEADS, SWA_HD, WINDOW), (2, 0, 1))[None, None]
    conv_prompt = conv_p[8 - (CONV_W - 1):].reshape(1, 1, CONV_W - 1, D_FF)

    qe, kl, e3, oin, va, ga, qb, kb, vb, gta, gtb = _pre_sample(x_sample, npre, win, wup, bg)
    kt = jnp.transpose(cache_swa_k[l], (0, 2, 3, 1)).reshape(NBS, SWA_KV, WINDOW)
    vt = jnp.transpose(cache_swa_v[l], (0, 2, 3, 1)).reshape(NBS, SWA_KV, WINDOW)
    oa_raw, ob, s1, kt1, vt1 = _state_sample(
        sk, qe, kl, e3, oin, va, qb, kb, vb, state_gla[l].reshape(NBS, GLA_K, GLA_DV), kt, vt, NT=NT, BB=STATE_SEQS)
    x1s = _post_sample(x_sample, oa_raw, ga, ob, gta, gtb, gn, wba, wbb, wout, npost)
    y_sample, conv_s = _ffn_sample(x1s, state_ffn_conv[l], fpre, wffn, cw, cb, wo, fpost)

    def cache_out(t):
        return jnp.transpose(t.reshape(NBS, SWA_KV_HEADS, SWA_HD, WINDOW), (0, 3, 1, 2))[None]

    gla_state_sample = s1.reshape(1, NBS, GLA_HEADS, GLA_DK, GLA_DV)
    swa_k_sample = cache_out(kt1)
    swa_v_sample = cache_out(vt1)
    conv_sample = conv_s[None]
    return (y_prompt, y_sample, gla_state_prompt, gla_state_sample, swa_k_prompt, swa_v_prompt,
            swa_k_sample, swa_v_sample, conv_prompt, conv_sample)
```

```python
import functools

import jax
import jax.numpy as jnp
from jax import lax
from jax.experimental import pallas as pl
from jax.experimental.pallas import tpu as pltpu

f32 = jnp.float32
bf16 = jnp.bfloat16

D_MODEL = 1024
GLA_HEADS = 4
GLA_DK = 64
GLA_DV = 128
GLA_RANK = 16
GLA_TAU = 16.0
GLA_CHUNK = 64
GLA_SAFE_DECAY = 60.0
SWA_HEADS = 8
SWA_KV_HEADS = 2
SWA_HD = 64
WINDOW = 128
D_FF = 2816
CONV_W = 3
EPS = 1e-6
GLA_K = GLA_HEADS * GLA_DK
GLA_V = GLA_HEADS * GLA_DV
SWA_Q = SWA_HEADS * SWA_HD
SWA_KV = SWA_KV_HEADS * SWA_HD
LANES = 128
LOG2E = 1.4426950408889634

C_QA = 0
C_KA = C_QA + GLA_K
C_VA = C_KA + GLA_K
C_GA = C_VA + GLA_V
C_QB = C_GA + GLA_V
C_KB = C_QB + SWA_Q
C_VB = C_KB + SWA_KV
C_GTA = C_VB + SWA_KV
C_GTB = C_GTA + D_MODEL
C_RA = C_GTB + D_MODEL
RA_PAD = LANES
PREP_ROWS = 256
IN_COLS_PAD = C_RA + PREP_ROWS

MIX_BLOCK = 256
FFN_BLOCK = 1024
FFN_COLS = 256
STATE_SEQS = 16
VMEM_LIMIT = 56 * 1024 * 1024


def _dot(a, b):
    return jnp.dot(a, b, preferred_element_type=f32)


def _dot_nt(a, b):
    return lax.dot_general(a, b, (((1,), (1,)), ((), ())), preferred_element_type=f32)


def _dot_tn(a, b):
    return lax.dot_general(a, b, (((0,), (0,)), ((), ())), preferred_element_type=f32)


def _rms(x, w):
    return x * lax.rsqrt(jnp.mean(x * x, axis=-1, keepdims=True) + EPS) * w


def _gelu_tanh(x):
    k = -2.0 * 0.7978845608028654 * LOG2E
    return x / (1.0 + jnp.exp2(x * (k + (k * 0.044715) * (x * x))))


def _split_hi_lo(x):
    hi = x.astype(bf16)
    lo = (x - hi.astype(f32)).astype(bf16)
    return hi, lo


def _chunk_cumsum(la, chunk):
    n = la.shape[0]
    r = lax.broadcasted_iota(jnp.int32, (n, n), 0)
    c = lax.broadcasted_iota(jnp.int32, (n, n), 1)
    tri = jnp.where((c <= r) & ((r // chunk) == (c // chunk)), 1.0, 0.0).astype(bf16)
    hi, lo = _split_hi_lo(la)
    return _dot(tri, hi) + _dot(tri, lo)


def _even_head_lanes(shape):
    lane = lax.broadcasted_iota(jnp.int32, shape, len(shape) - 1)
    return (lane % LANES) < GLA_DK


def _gla_out_norm(o, gn_ref, ga):
    outs = []
    for h in range(GLA_HEADS):
        oh = o[:, h * GLA_DV:(h + 1) * GLA_DV]
        outs.append(_rms(oh, gn_ref[...]))
    on = jnp.concatenate(outs, axis=1)
    return on * (ga * jax.nn.sigmoid(ga))


def _mix_tail(x, oa, ob, gate_a, gate_b, wba_ref, wbb_ref, wout_ref, npost_ref):
    merged = (jax.nn.sigmoid(gate_a) * _dot(oa.astype(bf16), wba_ref[...])
              + jax.nn.sigmoid(gate_b) * _dot(ob.astype(bf16), wbb_ref[...]))
    m = _dot(merged.astype(bf16), wout_ref[...])
    return x + _rms(m, npost_ref[...])


def _alibi_slope(head):
    return LOG2E * 2.0 ** (-(8.0 / SWA_HEADS) * (head + 1))


SWA_Q_SCALE = LOG2E * SWA_HD ** -0.5


def _kv_variants(x):
    lo = _even_head_lanes(x.shape)
    xr = pltpu.roll(x, SWA_HD, 1)
    zero = jnp.zeros_like(x)
    h0_lo = jnp.where(lo, x, zero).astype(bf16)
    h1_hi = jnp.where(lo, zero, x).astype(bf16)
    h1_lo = jnp.where(lo, xr, zero).astype(bf16)
    h0_hi = jnp.where(lo, zero, xr).astype(bf16)
    return (h0_lo, h0_hi), (h1_lo, h1_hi)


def _softmax_sink(s, sink):
    m = jnp.maximum(jnp.max(s, axis=-1, keepdims=True), sink)
    p = jnp.exp2(s - m)
    denom = jnp.sum(p, axis=-1, keepdims=True) + jnp.exp2(sink - m)
    return p, 1.0 / denom


def _mix_prompt_body(sink_ref, x_ref, npre_ref, win_ref, wup_ref, bg_ref, gn_ref,
                     wba_ref, wbb_ref, wout_ref, npost_ref,
                     y_ref, st_out_ref, k_out_ref, v_out_ref,
                     st_scr, kcat_scr, vcat_scr, oa_scr, ob_scr, gate_scr, inter_scr, *, T):
    i = pl.program_id(0)
    W = WINDOW
    C = GLA_CHUNK

    @pl.when(i == 0)
    def _():
        st_scr[...] = jnp.zeros_like(st_scr)
        kcat_scr[0:W, :] = jnp.zeros((W, SWA_KV), f32)
        vcat_scr[0:W, :] = jnp.zeros((W, SWA_KV), f32)

    @pl.when(i > 0)
    def _():
        kcat_scr[0:W, :] = kcat_scr[T:T + W, :]
        vcat_scr[0:W, :] = vcat_scr[T:T + W, :]

    x = x_ref[...]
    rms_f = lax.rsqrt(jnp.mean(x * x, axis=-1, keepdims=True) + EPS)
    h = (x * npre_ref[...]).astype(bf16)
    rms_b = {n: jnp.broadcast_to(rms_f, (T, n)) for n in (LANES, 2 * LANES)}

    def proj(c0, n):
        w = 2 * LANES if n % (2 * LANES) == 0 else LANES
        return jnp.concatenate([_dot(h, win_ref[:, c:c + w]) * rms_b[w] for c in range(c0, c0 + n, w)], axis=1)


    xg = _dot(proj(C_RA, RA_PAD).astype(bf16), wup_ref[...]) + bg_ref[...]
    kcat_scr[W:W + T, :] = proj(C_KB, SWA_KV)
    vcat_scr[W:W + T, :] = proj(C_VB, SWA_KV)
    qb = (proj(C_QB, SWA_Q) * SWA_Q_SCALE).astype(bf16)
    la = jax.nn.log_sigmoid(xg) * (1.0 / GLA_TAU)
    b = _chunk_cumsum(la, C)
    decay_floor = jnp.min(b)
    qa = proj(C_QA, GLA_K)
    ka = proj(C_KA, GLA_K)
    va_b = proj(C_VA, GLA_V).astype(bf16)

    qe = qa * jnp.exp(b) * (GLA_DK ** -0.5)
    ke = (ka * jnp.exp(-b)).astype(bf16)
    even = _even_head_lanes((T, GLA_K))
    qe_even = jnp.where(even, qe, 0.0).astype(bf16)
    qe_odd = jnp.where(even, 0.0, qe).astype(bf16)
    k_var = _kv_variants(kcat_scr[...])
    v_var = _kv_variants(vcat_scr[...])

    r2 = lax.broadcasted_iota(jnp.int32, (2 * C, 2 * C), 0)
    c2 = lax.broadcasted_iota(jnp.int32, (2 * C, 2 * C), 1)
    pair_causal = ((r2 // C) == (c2 // C)) & ((c2 % C) <= (r2 % C))
    even_c = _even_head_lanes((C, LANES))
    st = [st_scr[:, p * LANES:(p + 1) * LANES] for p in range(GLA_HEADS // 2)]

    def gla_scores(c):
        rows = slice(c * C, (c + 1) * C)
        out = []
        for p in range(GLA_HEADS // 2):
            lanes = slice(p * LANES, (p + 1) * LANES)
            q2 = jnp.concatenate([qe_even[rows, lanes], qe_odd[rows, lanes]], axis=0)
            ke_p = ke[rows, lanes]
            rhs = jnp.concatenate([ke_p, ke_p, st[p].astype(bf16)], axis=0)
            r = _dot_nt(q2, rhs)
            att = jnp.where(pair_causal, r[:, 0:2 * C], 0.0).astype(bf16)
            out.append((att, r[:, 2 * C:]))
        return out

    def gla_update(c, sc):
        rows = slice(c * C, (c + 1) * C)
        b_c = b[rows]
        bl = b_c[C - 1:C, :]
        kl = ka[rows] * jnp.exp(bl - b_c)
        ebl = jnp.exp(bl)
        for p in range(GLA_HEADS // 2):
            lanes = slice(p * LANES, (p + 1) * LANES)
            att, inter = sc[p]
            v2 = jnp.concatenate(
                [va_b[rows, (2 * p) * GLA_DV:(2 * p + 1) * GLA_DV],
                 va_b[rows, (2 * p + 1) * GLA_DV:(2 * p + 2) * GLA_DV]], axis=0)
            o2 = inter + _dot(att, v2)
            for e in range(2):
                hl = slice((2 * p + e) * GLA_DV, (2 * p + e + 1) * GLA_DV)
                oa_scr[rows, hl] = o2[e * C:(e + 1) * C]
                inter_scr[rows, hl] = inter[e * C:(e + 1) * C]
            kl_p = kl[:, lanes]
            kl_stack = jnp.concatenate(
                [jnp.where(even_c, kl_p, 0.0), jnp.where(even_c, 0.0, kl_p)], axis=0).astype(bf16)
            st[p] = st[p] * ebl[:, lanes] + _dot_tn(v2, kl_stack)

    qi = lax.broadcasted_iota(jnp.int32, (W, 2 * W), 0)
    kc = lax.broadcasted_iota(jnp.int32, (W, 2 * W), 1)
    rel = qi + W - kc
    relf = rel.astype(f32)
    in_window = (rel >= 0) & (rel < W)

    def swa_probs(j, kv):
        qrows = slice(j * W, (j + 1) * W)
        band = slice(j * W, j * W + 2 * W)
        if j == 0:
            mask = in_window & ((kc >= W) | (i > 0))
        else:
            mask = in_window
        pairs = (2 * kv, 2 * kv + 1)
        q2 = jnp.concatenate([qb[qrows, p * LANES:(p + 1) * LANES] for p in pairs], axis=0)
        out = []
        for e in range(2):
            s2 = _dot_nt(q2, k_var[kv][e][band])
            probs = []
            for half, p in enumerate(pairs):
                hd = 2 * p + e
                s = s2[half * W:(half + 1) * W]
                s = jnp.where(mask, s - _alibi_slope(hd) * relf, -jnp.inf)
                pr, inv = _softmax_sink(s, sink_ref[hd] * LOG2E)
                probs.append((pr * inv).astype(bf16))
            out.append(jnp.concatenate(probs, axis=0))
        return out

    def swa_out(j, kv, probs):
        qrows = slice(j * W, (j + 1) * W)
        band = slice(j * W, j * W + 2 * W)
        o2 = _dot(probs[0], v_var[kv][0][band]) + _dot(probs[1], v_var[kv][1][band])
        for half, p in enumerate((2 * kv, 2 * kv + 1)):
            ob_scr[qrows, p * LANES:(p + 1) * LANES] = o2[half * W:(half + 1) * W]

    n_chunks = T // C
    assert n_chunks == (T // W) * SWA_KV_HEADS
    gw = 2 * D_MODEL // n_chunks
    for idx in range(n_chunks):
        j, kv = idx // SWA_KV_HEADS, idx % SWA_KV_HEADS
        probs = swa_probs(j, kv)
        sc = gla_scores(idx)
        gate_scr[:, idx * gw:(idx + 1) * gw] = proj(C_GTA + idx * gw, gw)
        gla_update(idx, sc)
        swa_out(j, kv, probs)
    for p in range(GLA_HEADS // 2):
        st_scr[:, p * LANES:(p + 1) * LANES] = st[p]
    ga = proj(C_GA, GLA_V)
    gated_b = jax.nn.sigmoid(gate_scr[:, D_MODEL:2 * D_MODEL]) * _dot(ob_scr[...].astype(bf16), wbb_ref[...])
    sig_a = jax.nn.sigmoid(gate_scr[:, 0:D_MODEL])

    def finish(oa_raw):
        oa = _gla_out_norm(oa_raw, gn_ref, ga)
        merged = sig_a * _dot(oa.astype(bf16), wba_ref[...]) + gated_b
        m = _dot(merged.astype(bf16), wout_ref[...])
        y_ref[...] = x + _rms(m, npost_ref[...])

    finish(oa_scr[...])

    @pl.when(decay_floor < -GLA_SAFE_DECAY)
    def _():
        qs = qa * (GLA_DK ** -0.5)
        va_f = va_b.astype(f32)
        pos = lax.broadcasted_iota(jnp.int32, (T, 1), 0) % C
        er = lax.broadcasted_iota(jnp.int32, (GLA_K, GLA_V), 0)
        ec = lax.broadcasted_iota(jnp.int32, (GLA_K, GLA_V), 1)
        expand = jnp.where((er // GLA_DK) == (ec // GLA_DV), 1.0, 0.0).astype(bf16)

        def offset_term(d, acc):
            valid = pos >= d
            expo = jnp.where(valid, b - pltpu.roll(b, d, 0), 0.0)
            prod = jnp.where(valid, qs * pltpu.roll(ka, d, 0) * jnp.exp(expo), 0.0)
            return acc + _dot(prod.astype(bf16), expand) * pltpu.roll(va_f, d, 0)

        intra = lax.fori_loop(0, C, offset_term, jnp.zeros((T, GLA_V), f32))
        finish(inter_scr[...] + intra)

    @pl.when(i == pl.num_programs(0) - 1)
    def _():
        st_out_ref[...] = st_scr[...].T
        k_out_ref[...] = kcat_scr[T:T + W, :].T
        v_out_ref[...] = vcat_scr[T:T + W, :].T


def _const_spec(shape):
    nd = len(shape)
    return pl.BlockSpec(shape, lambda i: (0,) * nd, pipeline_mode=pl.Buffered(1))


def _mix_prompt(x, sinks, npre, win, wup, bg, gn, wba, wbb, wout, npost, *, T):
    L = x.shape[0]
    nb = L // T
    body = functools.partial(_mix_prompt_body, T=T)
    return pl.pallas_call(
        body,
        grid=(nb,),
        in_specs=[
            pl.BlockSpec(memory_space=pltpu.SMEM),
            pl.BlockSpec((T, D_MODEL), lambda i: (i, 0)),
            _const_spec(npre.shape), _const_spec(win.shape), _const_spec(wup.shape),
            _const_spec(bg.shape), _const_spec(gn.shape), _const_spec(wba.shape),
            _const_spec(wbb.shape), _const_spec(wout.shape), _const_spec(npost.shape),
        ],
        out_specs=[
            pl.BlockSpec((T, D_MODEL), lambda i: (i, 0)),
            pl.BlockSpec((GLA_K, GLA_DV), lambda i: (0, 0)),
            pl.BlockSpec((WINDOW, SWA_KV), lambda i: (0, 0)),
            pl.BlockSpec((WINDOW, SWA_KV), lambda i: (0, 0)),
        ],
        out_shape=[
            jax.ShapeDtypeStruct((L, D_MODEL), f32),
            jax.ShapeDtypeStruct((GLA_K, GLA_DV), f32),
            jax.ShapeDtypeStruct((WINDOW, SWA_KV), f32),
            jax.ShapeDtypeStruct((WINDOW, SWA_KV), f32),
        ],
        scratch_shapes=[
            pltpu.VMEM((GLA_DV, GLA_K), f32),
            pltpu.VMEM((T + WINDOW, SWA_KV), f32),
            pltpu.VMEM((T + WINDOW, SWA_KV), f32),
            pltpu.VMEM((T, GLA_V), f32),
            pltpu.VMEM((T, SWA_Q), f32),
            pltpu.VMEM((T, 2 * D_MODEL), f32),
            pltpu.VMEM((T, GLA_V), f32),
        ],
        compiler_params=pltpu.CompilerParams(
            dimension_semantics=("arbitrary",), vmem_limit_bytes=VMEM_LIMIT),
        name="mix_prompt",
    )(sinks, x, npre, win, wup, bg, gn, wba, wbb, wout, npost)


def _pre_sample_body(x_ref, npre_ref, win_ref, wup_ref, bg_ref,
                     qe_ref, kl_ref, e3_ref, oin_ref, va_ref, ga_ref, qb_ref, kb_ref, vb_ref,
                     gta_ref, gtb_ref, x_scr, *, NB, NT):
    for t in range(NT):
        x_scr[t * NB:(t + 1) * NB, :] = x_ref[:, t, :]
    h = _rms(x_scr[...], npre_ref[...]).astype(bf16)

    def proj(c0, n):
        return _dot(h, win_ref[:, c0:c0 + n])

    def blk(val, t):
        return val[t * NB:(t + 1) * NB, :]

    xg = _dot(proj(C_RA, RA_PAD).astype(bf16), wup_ref[...]) + bg_ref[...]
    qa = proj(C_QA, GLA_K) * (GLA_DK ** -0.5)
    ka = proj(C_KA, GLA_K)
    va = proj(C_VA, GLA_V)
    va_ref[...] = va
    la = jax.nn.log_sigmoid(xg) * (1.0 / GLA_TAU)
    b = [blk(la, 0)]
    for t in range(1, NT):
        b.append(b[-1] + blk(la, t))
    e3_ref[...] = jnp.exp(b[NT - 1])
    for t in range(NT):
        qe_ref[t * NB:(t + 1) * NB, :] = blk(qa, t) * jnp.exp(b[t])
        kl_ref[t * NB:(t + 1) * NB, :] = blk(ka, t) * jnp.exp(b[NT - 1] - b[t])
    pairs = [(t, j) for t in range(NT) for j in range(t + 1)]
    prods = [(blk(qa, t) * blk(ka, j) * jnp.exp(b[t] - b[j])).astype(bf16) for t, j in pairs]
    r = lax.broadcasted_iota(jnp.int32, (GLA_K, GLA_V), 0)
    c = lax.broadcasted_iota(jnp.int32, (GLA_K, GLA_V), 1)
    expand = jnp.where((r // GLA_DK) == (c // GLA_DV), 1.0, 0.0).astype(bf16)
    ga_ref[...] = proj(C_GA, GLA_V)
    qb_ref[...] = proj(C_QB, SWA_Q) * SWA_Q_SCALE
    kb_ref[...] = proj(C_KB, SWA_KV)
    vb_ref[...] = proj(C_VB, SWA_KV)
    att = _dot(jnp.concatenate(prods, axis=0), expand)
    gta_ref[...] = proj(C_GTA, D_MODEL)
    gtb_ref[...] = proj(C_GTB, D_MODEL)
    for t in range(NT):
        acc = None
        for idx, (tt, j) in enumerate(pairs):
            if tt != t:
                continue
            term = att[idx * NB:(idx + 1) * NB, :] * blk(va, j)
            acc = term if acc is None else acc + term
        oin_ref[t * NB:(t + 1) * NB, :] = acc


def _pre_sample(xs, npre, win, wup, bg):
    NB, NT, _ = xs.shape
    body = functools.partial(_pre_sample_body, NB=NB, NT=NT)
    widths = (GLA_K, GLA_K, None, GLA_V, GLA_V, GLA_V, SWA_Q, SWA_KV, SWA_KV, D_MODEL, D_MODEL)
    out_shape = [jax.ShapeDtypeStruct((NB, GLA_K) if w is None else (NT * NB, w), f32) for w in widths]
    return pl.pallas_call(
        body,
        out_shape=out_shape,
        scratch_shapes=[pltpu.VMEM((NB * NT, D_MODEL), f32)],
        compiler_params=pltpu.CompilerParams(vmem_limit_bytes=VMEM_LIMIT),
        name="pre_sample",
    )(xs, npre, win, wup, bg)


def _state_sample_body(sink_ref, qe_ref, kl_ref, e3_ref, oin_ref, va_ref, qb_ref, kb_ref, vb_ref,
                       s0_ref, kt_ref, vt_ref,
                       oa_ref, ob_ref, s1_ref, kt1_ref, vt1_ref, *, BB, NT):
    W = WINDOW
    SK = 2 * W
    HT = GLA_HEADS * NT
    HALF = SWA_HD
    hr = lax.broadcasted_iota(jnp.int32, (HT, GLA_K), 0) // NT
    hc = lax.broadcasted_iota(jnp.int32, (HT, GLA_K), 1) // GLA_DK
    own_head = hr == hc
    ones_rows = jnp.ones((16, GLA_DV), bf16)
    zero_rows = jnp.zeros((16, GLA_DV), bf16)
    zero_ht = jnp.zeros((HT, GLA_DV), bf16)
    G2 = 2 * NT
    row = lax.broadcasted_iota(jnp.int32, (G2, SK), 0)
    col = lax.broadcasted_iota(jnp.int32, (G2, SK), 1)
    rel = (row % NT) + W - col
    relf = rel.astype(f32)
    smask = (rel >= 0) & (rel < W)
    first_pair = lax.broadcasted_iota(jnp.int32, (G2, 1), 0) < NT
    pad_rows = jnp.zeros((8 - NT, SWA_KV), f32)
    pad_lanes = jnp.zeros((SWA_KV, SK - W - 8), f32)
    zero_half = jnp.zeros((HALF, SK), bf16)

    def head_variants(cat_t, kv):
        blk = cat_t[kv * HALF:(kv + 1) * HALF]
        return (jnp.concatenate([blk, zero_half], axis=0), jnp.concatenate([zero_half, blk], axis=0))

    pending = []
    for bi in range(BB):
        s0 = s0_ref[bi]
        q4 = qe_ref[:, bi, :]
        qm = jnp.where(own_head, jnp.concatenate([q4] * GLA_HEADS, axis=0), 0.0).astype(bf16)
        o_inter = _dot(qm, s0.astype(bf16))
        for hd in range(GLA_HEADS):
            lanes = slice(hd * GLA_DV, (hd + 1) * GLA_DV)
            oa_ref[:, bi, lanes] = o_inter[hd * NT:(hd + 1) * NT, :] + oin_ref[:, bi, lanes]
        k4 = kl_ref[:, bi, :]
        km = jnp.where(own_head, jnp.concatenate([k4] * GLA_HEADS, axis=0), 0.0).astype(bf16)
        e = e3_ref[bi:bi + 1, :]
        e_hi = e.astype(bf16)
        r1 = e - e_hi.astype(f32)
        e_mid = r1.astype(bf16)
        e_lo = (r1 - e_mid.astype(f32)).astype(bf16)
        e_rows = jnp.concatenate([e_hi, e_mid, e_lo, jnp.zeros((13, GLA_K), bf16)], axis=0)
        lhs = jnp.concatenate([km, e_rows], axis=0)
        v4 = va_ref[:, bi, :].astype(bf16)
        vrep = jnp.concatenate([v4[:, hd * GLA_DV:(hd + 1) * GLA_DV] for hd in range(GLA_HEADS)], axis=0)
        rhs = jnp.concatenate([jnp.concatenate([vrep, zero_ht], axis=1),
                               jnp.concatenate([zero_rows, ones_rows], axis=1)], axis=0)
        res = _dot_tn(lhs, rhs)
        s1_ref[bi] = res[:, GLA_DV:] * s0 + res[:, :GLA_DV]

        kt = kt_ref[bi]
        vt = vt_ref[bi]
        knew_t = jnp.concatenate([kb_ref[:, bi, :], pad_rows], axis=0).T
        vnew_t = jnp.concatenate([vb_ref[:, bi, :], pad_rows], axis=0).T
        kt1_ref[bi] = jnp.concatenate([kt[:, NT:], knew_t[:, 0:NT]], axis=1)
        vt1_ref[bi] = jnp.concatenate([vt[:, NT:], vnew_t[:, 0:NT]], axis=1)
        kcat = jnp.concatenate([kt, knew_t, pad_lanes], axis=1).astype(bf16)
        vcat = jnp.concatenate([vt, vnew_t, pad_lanes], axis=1).astype(bf16)
        q4b = qb_ref[:, bi, :].astype(bf16)
        for kv in range(SWA_KV_HEADS):
            p0 = 2 * kv
            q8 = jnp.concatenate([q4b[:, p0 * LANES:(p0 + 1) * LANES],
                                  q4b[:, (p0 + 1) * LANES:(p0 + 2) * LANES]], axis=0)
            scores = [_dot(q8, kvar) for kvar in head_variants(kcat, kv)]
            pending.append((bi, kv, scores, head_variants(vcat, kv)))

    for bi, kv, scores, v_vars in pending:
        p0 = 2 * kv
        o8_t = None
        for e_ in range(2):
            h_first = 2 * p0 + e_
            h_second = 2 * (p0 + 1) + e_
            slope = jnp.where(first_pair, _alibi_slope(h_first), _alibi_slope(h_second))
            sink = jnp.where(first_pair, sink_ref[h_first] * LOG2E, sink_ref[h_second] * LOG2E)
            s = jnp.where(smask, scores[e_] - slope * relf, -jnp.inf)
            pr, inv = _softmax_sink(s, sink)
            o_t = _dot_nt(v_vars[e_], (pr * inv).astype(bf16))
            o8_t = o_t if o8_t is None else o8_t + o_t
        o8 = o8_t.T
        ob_ref[:, bi, p0 * LANES:(p0 + 1) * LANES] = o8[0:NT, :]
        ob_ref[:, bi, (p0 + 1) * LANES:(p0 + 2) * LANES] = o8[NT:2 * NT, :]


def _state_sample(sinks, qe, kl, e3, oin, va, qb, kb, vb, s0, kt, vt, *, NT, BB):
    NBS = s0.shape[0]
    assert NBS % BB == 0
    body = functools.partial(_state_sample_body, BB=BB, NT=NT)

    def tm(a):
        return a.reshape(NT, NBS, a.shape[-1])

    def rows(n):
        return pl.BlockSpec((NT, BB, n), lambda i: (0, i, 0))

    def per_seq(shape):
        return pl.BlockSpec((BB,) + shape, lambda i: (i, 0, 0))

    oa, ob, s1, kt1, vt1 = pl.pallas_call(
        body,
        grid=(NBS // BB,),
        in_specs=[
            pl.BlockSpec(memory_space=pltpu.SMEM),
            rows(GLA_K), rows(GLA_K), pl.BlockSpec((BB, GLA_K), lambda i: (i, 0)),
            rows(GLA_V), rows(GLA_V), rows(SWA_Q), rows(SWA_KV), rows(SWA_KV),
            per_seq((GLA_K, GLA_DV)), per_seq((SWA_KV, WINDOW)), per_seq((SWA_KV, WINDOW)),
        ],
        out_specs=[
            rows(GLA_V), rows(SWA_Q),
            per_seq((GLA_K, GLA_DV)), per_seq((SWA_KV, WINDOW)), per_seq((SWA_KV, WINDOW)),
        ],
        out_shape=[
            jax.ShapeDtypeStruct((NT, NBS, GLA_V), f32),
            jax.ShapeDtypeStruct((NT, NBS, SWA_Q), f32),
            jax.ShapeDtypeStruct((NBS, GLA_K, GLA_DV), f32),
            jax.ShapeDtypeStruct((NBS, SWA_KV, WINDOW), f32),
            jax.ShapeDtypeStruct((NBS, SWA_KV, WINDOW), f32),
        ],
        compiler_params=pltpu.CompilerParams(
            dimension_semantics=("arbitrary",), vmem_limit_bytes=VMEM_LIMIT),
        name="state_sample",
    )(sinks, tm(qe), tm(kl), e3, tm(oin), tm(va), tm(qb), tm(kb), tm(vb), s0, kt, vt)
    return oa.reshape(NT * NBS, GLA_V), ob.reshape(NT * NBS, SWA_Q), s1, kt1, vt1


def _post_sample_body(x_ref, oa_ref, ga_ref, ob_ref, gta_ref, gtb_ref, gn_ref,
                      wba_ref, wbb_ref, wout_ref, npost_ref, y_ref, x_scr, *, NB, NT):
    for t in range(NT):
        x_scr[t * NB:(t + 1) * NB, :] = x_ref[:, t, :]
    oa = _gla_out_norm(oa_ref[...], gn_ref, ga_ref[...])
    y_ref[...] = _mix_tail(x_scr[...], oa, ob_ref[...], gta_ref[...], gtb_ref[...],
                           wba_ref, wbb_ref, wout_ref, npost_ref)


def _post_sample(xs, oa, ga, ob, gta, gtb, gn, wba, wbb, wout, npost):
    NB, NT, _ = xs.shape
    return pl.pallas_call(
        functools.partial(_post_sample_body, NB=NB, NT=NT),
        out_shape=jax.ShapeDtypeStruct((NT * NB, D_MODEL), f32),
        scratch_shapes=[pltpu.VMEM((NT * NB, D_MODEL), f32)],
        compiler_params=pltpu.CompilerParams(vmem_limit_bytes=VMEM_LIMIT),
        name="post_sample",
    )(xs, oa, ga, ob, gta, gtb, gn, wba, wbb, wout, npost)


def _ffn_columns(h, wffn_ref, cw_ref, cb_ref, up_scr, y_scr, *, T, base, shift):
    for c0 in range(0, D_FF, FFN_COLS):
        cols = slice(c0, c0 + FFN_COLS)
        u = _dot(h, wffn_ref[:, c0:c0 + FFN_COLS])
        g = _dot(h, wffn_ref[:, D_FF + c0:D_FF + c0 + FFN_COLS])
        up_scr[base:base + T, cols] = u
        u1 = up_scr[base - shift:base - shift + T, cols]
        u2 = up_scr[base - 2 * shift:base - 2 * shift + T, cols]
        cv = (cb_ref[:, cols] + cw_ref[2:3, cols] * u + cw_ref[1:2, cols] * u1 + cw_ref[0:1, cols] * u2)
        y_scr[:, cols] = (_gelu_tanh(cv) * g).astype(bf16)


def _ffn_prompt_body(x_ref, npre_ref, wffn_ref, cw_ref, cb_ref, wo_ref, npost_ref,
                     y_ref, conv_out_ref, up_scr, y_scr, *, T):
    i = pl.program_id(0)
    base = 8

    @pl.when(i == 0)
    def _():
        up_scr[0:base, :] = jnp.zeros((base, D_FF), f32)

    @pl.when(i > 0)
    def _():
        up_scr[0:base, :] = up_scr[T:T + base, :]

    x = x_ref[...]
    h = _rms(x, npre_ref[...]).astype(bf16)
    _ffn_columns(h, wffn_ref, cw_ref, cb_ref, up_scr, y_scr, T=T, base=base, shift=1)
    f = _dot(y_scr[...], wo_ref[...])
    y_ref[...] = x + _rms(f, npost_ref[...])

    @pl.when(i == pl.num_programs(0) - 1)
    def _():
        conv_out_ref[...] = up_scr[T:T + base, :]


def _ffn_prompt(x, npre, wffn, cw, cb, wo, npost, *, T):
    L = x.shape[0]
    body = functools.partial(_ffn_prompt_body, T=T)
    return pl.pallas_call(
        body,
        grid=(L // T,),
        in_specs=[
            pl.BlockSpec((T, D_MODEL), lambda i: (i, 0)),
            _const_spec(npre.shape), _const_spec(wffn.shape), _const_spec(cw.shape),
            _const_spec(cb.shape), _const_spec(wo.shape), _const_spec(npost.shape),
        ],
        out_specs=[
            pl.BlockSpec((T, D_MODEL), lambda i: (i, 0)),
            pl.BlockSpec((8, D_FF), lambda i: (0, 0)),
        ],
        out_shape=[
            jax.ShapeDtypeStruct((L, D_MODEL), f32),
            jax.ShapeDtypeStruct((8, D_FF), f32),
        ],
        scratch_shapes=[
            pltpu.VMEM((T + 8, D_FF), f32),
            pltpu.VMEM((T, D_FF), bf16),
        ],
        compiler_params=pltpu.CompilerParams(
            dimension_semantics=("arbitrary",), vmem_limit_bytes=VMEM_LIMIT),
        name="ffn_prompt",
    )(x, npre, wffn, cw, cb, wo, npost)


def _ffn_sample_body(x_ref, cst_ref, npre_ref, wffn_ref, cw_ref, cb_ref, wo_ref, npost_ref,
                     y_ref, conv_out_ref, up_scr, y_scr, *, NB, NT):
    T = NB * NT
    for t in range(CONV_W - 1):
        up_scr[t * NB:(t + 1) * NB, :] = cst_ref[:, t, :]
    base = (CONV_W - 1) * NB
    x = x_ref[...]
    h = _rms(x, npre_ref[...]).astype(bf16)
    _ffn_columns(h, wffn_ref, cw_ref, cb_ref, up_scr, y_scr, T=T, base=base, shift=NB)
    f = _dot(y_scr[...], wo_ref[...])
    y = x + _rms(f, npost_ref[...])
    for t in range(NT):
        y_ref[:, t, :] = y[t * NB:(t + 1) * NB, :]
    for t in range(CONV_W - 1):
        conv_out_ref[:, t, :] = up_scr[T + t * NB:T + (t + 1) * NB, :]


def _ffn_sample(x, cst, npre, wffn, cw, cb, wo, npost):
    NB = cst.shape[0]
    T = x.shape[0]
    NT = T // NB
    body = functools.partial(_ffn_sample_body, NB=NB, NT=NT)
    return pl.pallas_call(
        body,
        out_shape=[
            jax.ShapeDtypeStruct((NB, NT, D_MODEL), f32),
            jax.ShapeDtypeStruct((NB, CONV_W - 1, D_FF), f32),
        ],
        scratch_shapes=[
            pltpu.VMEM((T + (CONV_W - 1) * NB, D_FF), f32),
            pltpu.VMEM((T, D_FF), bf16),
        ],
        compiler_params=pltpu.CompilerParams(vmem_limit_bytes=VMEM_LIMIT),
        name="ffn_sample",
    )(x, cst, npre, wffn, cw, cb, wo, npost)


def _prep_w_in_body(*refs, n_plain, per_step):
    o_ref = refs[-1]
    j = pl.program_id(0)
    for k, wt_ref in enumerate(refs[:-1]):
        x = wt_ref[...]
        r = lax.broadcasted_iota(jnp.int32, x.shape, 0)
        x = jnp.where((j * per_step + k < n_plain) | (r < GLA_RANK), x, 0.0)
        o_ref[:, k * PREP_ROWS:(k + 1) * PREP_ROWS] = x.T.astype(bf16)


def _prep_w_in(w_in):
    d_in, n_cols = w_in.shape
    head = C_GA
    tail_src = head + GLA_RANK
    n_head = head // PREP_ROWS
    n_tail = (n_cols - tail_src) // PREP_ROWS
    assert head % PREP_ROWS == 0 and (n_cols - tail_src) % PREP_ROWS == 0
    n_plain = n_head + n_tail
    assert C_RA == n_plain * PREP_ROWS
    per_step = 2
    assert (n_plain + 1) % per_step == 0

    def row_off(blk):
        off = jnp.where(blk < n_head, blk * PREP_ROWS,
                        jnp.where(blk < n_plain, tail_src + (blk - n_head) * PREP_ROWS, head))
        return pl.multiple_of(off, 8)

    def in_spec(k):
        return pl.BlockSpec((pl.Element(PREP_ROWS), pl.Element(d_in)), lambda j: (row_off(j * per_step + k), 0))

    wt = jnp.swapaxes(w_in, 0, 1)
    return pl.pallas_call(
        functools.partial(_prep_w_in_body, n_plain=n_plain, per_step=per_step),
        grid=((n_plain + 1) // per_step,),
        in_specs=[in_spec(k) for k in range(per_step)],
        out_specs=pl.BlockSpec((d_in, per_step * PREP_ROWS), lambda j: (0, j)),
        out_shape=jax.ShapeDtypeStruct((d_in, (n_plain + 1) * PREP_ROWS), bf16),
        compiler_params=pltpu.CompilerParams(dimension_semantics=("arbitrary",)),
        name="prep_w_in",
    )(*([wt] * per_step))


def kernel(x_prompt, x_sample, state_gla, cache_swa_k, cache_swa_v, state_ffn_conv, norm_mix_pre, norm_mix_post, w_in, w_gate_up, b_gate, gla_norm, sinks, w_branch_a, w_branch_b, w_out, norm_ffn_pre, norm_ffn_post, w_ffn_in, conv_w, conv_b, w_ffn_out):
    depth = w_in.shape[0]
    assert depth == 1
    l = 0
    B, L, _ = x_prompt.shape
    assert B == 1
    NBS, NT, _ = x_sample.shape
    assert L % MIX_BLOCK == 0 and L % FFN_BLOCK == 0 and NBS % STATE_SEQS == 0
    assert cache_swa_k.shape[2] == WINDOW and NT < 8

    win = _prep_w_in(w_in[l])
    wup = jnp.zeros((RA_PAD, GLA_K), f32).at[:GLA_RANK].set(w_gate_up[l]).astype(bf16)
    bg = b_gate[l].reshape(1, GLA_K)
    gn = gla_norm[l].reshape(1, GLA_DV)
    npre = norm_mix_pre[l].reshape(1, D_MODEL)
    npost = norm_mix_post[l].reshape(1, D_MODEL)
    wba = w_branch_a[l].astype(bf16)
    wbb = w_branch_b[l].astype(bf16)
    wout = w_out[l].astype(bf16)
    fpre = norm_ffn_pre[l].reshape(1, D_MODEL)
    fpost = norm_ffn_post[l].reshape(1, D_MODEL)
    wffn = w_ffn_in[l].astype(bf16)
    cw = conv_w[l]
    cb = conv_b[l].reshape(1, D_FF)
    wo = w_ffn_out[l].astype(bf16)
    sk = sinks[l]

    x1, st_p, k_p, v_p = _mix_prompt(x_prompt[0], sk, npre, win, wup, bg, gn, wba, wbb, wout, npost, T=MIX_BLOCK)
    y_p, conv_p = _ffn_prompt(x1, fpre, wffn, cw, cb, wo, fpost, T=FFN_BLOCK)

    y_prompt = y_p[None]
    gla_state_prompt = st_p.reshape(1, 1, GLA_HEADS, GLA_DK, GLA_DV)
    swa_k_prompt = jnp.transpose(k_p.reshape(SWA_KV_HEADS, SWA_HD, WINDOW), (2, 0, 1))[None, None]
    swa_v_prompt = jnp.transpose(v_p.reshape(SWA_KV_HEADS, SWA_HD, WINDOW), (2, 0, 1))[None, None]
    conv_prompt = conv_p[8 - (CONV_W - 1):].reshape(1, 1, CONV_W - 1, D_FF)

    qe, kl, e3, oin, va, ga, qb, kb, vb, gta, gtb = _pre_sample(x_sample, npre, win, wup, bg)
    kt = jnp.transpose(cache_swa_k[l], (0, 2, 3, 1)).reshape(NBS, SWA_KV, WINDOW)
    vt = jnp.transpose(cache_swa_v[l], (0, 2, 3, 1)).reshape(NBS, SWA_KV, WINDOW)
    oa_raw, ob, s1, kt1, vt1 = _state_sample(
        sk, qe, kl, e3, oin, va, qb, kb, vb, state_gla[l].reshape(NBS, GLA_K, GLA_DV), kt, vt, NT=NT, BB=STATE_SEQS)
    x1s = _post_sample(x_sample, oa_raw, ga, ob, gta, gtb, gn, wba, wbb, wout, npost)
    y_sample, conv_s = _ffn_sample(x1s, state_ffn_conv[l], fpre, wffn, cw, cb, wo, fpost)

    def cache_out(t):
        return jnp.transpose(t.reshape(NBS, SWA_KV_HEADS, SWA_HD, WINDOW), (0, 3, 1, 2))[None]

    gla_state_sample = s1.reshape(1, NBS, GLA_HEADS, GLA_DK, GLA_DV)
    swa_k_sample = cache_out(kt1)
    swa_v_sample = cache_out(vt1)
    conv_sample = conv_s[None]
    return (y_prompt, y_sample, gla_state_prompt, gla_state_sample, swa_k_prompt, swa_v_prompt,
            swa_k_sample, swa_v_sample, conv_prompt, conv_sample)
```

```python
import functools

import jax
import jax.numpy as jnp
from jax import lax
from jax.experimental import pallas as pl
from jax.experimental.pallas import tpu as pltpu

f32 = jnp.float32
bf16 = jnp.bfloat16

D_MODEL = 1024
GLA_HEADS = 4
GLA_DK = 64
GLA_DV = 128
GLA_RANK = 16
GLA_TAU = 16.0
GLA_CHUNK = 64
GLA_SAFE_DECAY = 60.0
SWA_HEADS = 8
SWA_KV_HEADS = 2
SWA_HD = 64
WINDOW = 128
D_FF = 2816
CONV_W = 3
EPS = 1e-6
GLA_K = GLA_HEADS * GLA_DK
GLA_V = GLA_HEADS * GLA_DV
SWA_Q = SWA_HEADS * SWA_HD
SWA_KV = SWA_KV_HEADS * SWA_HD
LANES = 128
LOG2E = 1.4426950408889634

C_QA = 0
C_KA = C_QA + GLA_K
C_VA = C_KA + GLA_K
C_GA = C_VA + GLA_V
C_QB = C_GA + GLA_V
C_KB = C_QB + SWA_Q
C_VB = C_KB + SWA_KV
C_GTA = C_VB + SWA_KV
C_GTB = C_GTA + D_MODEL
C_RA = C_GTB + D_MODEL
RA_PAD = LANES
PREP_ROWS = 256
IN_COLS_PAD = C_RA + PREP_ROWS

MIX_BLOCK = 256
FFN_BLOCK = 1024
FFN_COLS = 256
STATE_SEQS = 16
VMEM_LIMIT = 56 * 1024 * 1024


def _dot(a, b):
    return jnp.dot(a, b, preferred_element_type=f32)


def _dot_nt(a, b):
    return lax.dot_general(a, b, (((1,), (1,)), ((), ())), preferred_element_type=f32)


def _dot_tn(a, b):
    return lax.dot_general(a, b, (((0,), (0,)), ((), ())), preferred_element_type=f32)


def _rms(x, w):
    return x * lax.rsqrt(jnp.mean(x * x, axis=-1, keepdims=True) + EPS) * w


def _gelu_tanh(x):
    k = -2.0 * 0.7978845608028654 * LOG2E
    return x / (1.0 + jnp.exp2(x * (k + (k * 0.044715) * (x * x))))


def _split_hi_lo(x):
    hi = x.astype(bf16)
    lo = (x - hi.astype(f32)).astype(bf16)
    return hi, lo


def _chunk_cumsum(la, chunk):
    n = la.shape[0]
    r = lax.broadcasted_iota(jnp.int32, (n, n), 0)
    c = lax.broadcasted_iota(jnp.int32, (n, n), 1)
    tri = jnp.where((c <= r) & ((r // chunk) == (c // chunk)), 1.0, 0.0).astype(bf16)
    hi, lo = _split_hi_lo(la)
    return _dot(tri, hi) + _dot(tri, lo)


def _even_head_lanes(shape):
    lane = lax.broadcasted_iota(jnp.int32, shape, len(shape) - 1)
    return (lane % LANES) < GLA_DK


def _gla_out_norm(o, gn_ref, ga):
    outs = []
    for h in range(GLA_HEADS):
        oh = o[:, h * GLA_DV:(h + 1) * GLA_DV]
        outs.append(_rms(oh, gn_ref[...]))
    on = jnp.concatenate(outs, axis=1)
    return on * (ga * jax.nn.sigmoid(ga))


def _mix_tail(x, oa, ob, gate_a, gate_b, wba_ref, wbb_ref, wout_ref, npost_ref):
    merged = (jax.nn.sigmoid(gate_a) * _dot(oa.astype(bf16), wba_ref[...])
              + jax.nn.sigmoid(gate_b) * _dot(ob.astype(bf16), wbb_ref[...]))
    m = _dot(merged.astype(bf16), wout_ref[...])
    return x + _rms(m, npost_ref[...])


def _alibi_slope(head):
    return LOG2E * 2.0 ** (-(8.0 / SWA_HEADS) * (head + 1))


SWA_Q_SCALE = LOG2E * SWA_HD ** -0.5


def _kv_variants(x):
    lo = _even_head_lanes(x.shape)
    xr = pltpu.roll(x, SWA_HD, 1)
    zero = jnp.zeros_like(x)
    h0_lo = jnp.where(lo, x, zero).astype(bf16)
    h1_hi = jnp.where(lo, zero, x).astype(bf16)
    h1_lo = jnp.where(lo, xr, zero).astype(bf16)
    h0_hi = jnp.where(lo, zero, xr).astype(bf16)
    return (h0_lo, h0_hi), (h1_lo, h1_hi)


def _softmax_sink(s, sink):
    m = jnp.maximum(jnp.max(s, axis=-1, keepdims=True), sink)
    p = jnp.exp2(s - m)
    denom = jnp.sum(p, axis=-1, keepdims=True) + jnp.exp2(sink - m)
    return p, 1.0 / denom


def _mix_prompt_body(sink_ref, x_ref, npre_ref, win_ref, wup_ref, bg_ref, gn_ref,
                     wba_ref, wbb_ref, wout_ref, npost_ref,
                     y_ref, st_out_ref, k_out_ref, v_out_ref,
                     st_scr, kcat_scr, vcat_scr, oa_scr, ob_scr, gate_scr, inter_scr, *, T):
    i = pl.program_id(0)
    W = WINDOW
    C = GLA_CHUNK

    @pl.when(i == 0)
    def _():
        st_scr[...] = jnp.zeros_like(st_scr)
        kcat_scr[0:W, :] = jnp.zeros((W, SWA_KV), f32)
        vcat_scr[0:W, :] = jnp.zeros((W, SWA_KV), f32)

    @pl.when(i > 0)
    def _():
        kcat_scr[0:W, :] = kcat_scr[T:T + W, :]
        vcat_scr[0:W, :] = vcat_scr[T:T + W, :]

    x = x_ref[...]
    rms_f = lax.rsqrt(jnp.mean(x * x, axis=-1, keepdims=True) + EPS)
    h = (x * npre_ref[...]).astype(bf16)
    rms_b = {n: jnp.broadcast_to(rms_f, (T, n)) for n in (LANES, 2 * LANES)}

    def proj(c0, n):
        w = 2 * LANES if n % (2 * LANES) == 0 else LANES
        return jnp.concatenate([_dot(h, win_ref[:, c:c + w]) * rms_b[w] for c in range(c0, c0 + n, w)], axis=1)


    xg = _dot(proj(C_RA, RA_PAD).astype(bf16), wup_ref[...]) + bg_ref[...]
    kcat_scr[W:W + T, :] = proj(C_KB, SWA_KV)
    vcat_scr[W:W + T, :] = proj(C_VB, SWA_KV)
    qb = (proj(C_QB, SWA_Q) * SWA_Q_SCALE).astype(bf16)
    la = jax.nn.log_sigmoid(xg) * (1.0 / GLA_TAU)
    b = _chunk_cumsum(la, C)
    decay_floor = jnp.min(b)
    qa = proj(C_QA, GLA_K)
    ka = proj(C_KA, GLA_K)
    va_b = proj(C_VA, GLA_V).astype(bf16)

    qe = qa * jnp.exp(b) * (GLA_DK ** -0.5)
    ke = (ka * jnp.exp(-b)).astype(bf16)
    even = _even_head_lanes((T, GLA_K))
    qe_even = jnp.where(even, qe, 0.0).astype(bf16)
    qe_odd = jnp.where(even, 0.0, qe).astype(bf16)
    k_var = _kv_variants(kcat_scr[...])
    v_var = _kv_variants(vcat_scr[...])

    r2 = lax.broadcasted_iota(jnp.int32, (2 * C, 2 * C), 0)
    c2 = lax.broadcasted_iota(jnp.int32, (2 * C, 2 * C), 1)
    pair_causal = ((r2 // C) == (c2 // C)) & ((c2 % C) <= (r2 % C))
    even_c = _even_head_lanes((C, LANES))
    st = [st_scr[:, p * LANES:(p + 1) * LANES] for p in range(GLA_HEADS // 2)]

    def gla_scores(c):
        rows = slice(c * C, (c + 1) * C)
        out = []
        for p in range(GLA_HEADS // 2):
            lanes = slice(p * LANES, (p + 1) * LANES)
            q2 = jnp.concatenate([qe_even[rows, lanes], qe_odd[rows, lanes]], axis=0)
            ke_p = ke[rows, lanes]
            rhs = jnp.concatenate([ke_p, ke_p, st[p].astype(bf16)], axis=0)
            r = _dot_nt(q2, rhs)
            att = jnp.where(pair_causal, r[:, 0:2 * C], 0.0).astype(bf16)
            out.append((att, r[:, 2 * C:]))
        return out

    def gla_update(c, sc):
        rows = slice(c * C, (c + 1) * C)
        b_c = b[rows]
        bl = b_c[C - 1:C, :]
        kl = ka[rows] * jnp.exp(bl - b_c)
        ebl = jnp.exp(bl)
        for p in range(GLA_HEADS // 2):
            lanes = slice(p * LANES, (p + 1) * LANES)
            att, inter = sc[p]
            v2 = jnp.concatenate(
                [va_b[rows, (2 * p) * GLA_DV:(2 * p + 1) * GLA_DV],
                 va_b[rows, (2 * p + 1) * GLA_DV:(2 * p + 2) * GLA_DV]], axis=0)
            o2 = inter + _dot(att, v2)
            for e in range(2):
                hl = slice((2 * p + e) * GLA_DV, (2 * p + e + 1) * GLA_DV)
                oa_scr[rows, hl] = o2[e * C:(e + 1) * C]
                inter_scr[rows, hl] = inter[e * C:(e + 1) * C]
            kl_p = kl[:, lanes]
            kl_stack = jnp.concatenate(
                [jnp.where(even_c, kl_p, 0.0), jnp.where(even_c, 0.0, kl_p)], axis=0).astype(bf16)
            st[p] = st[p] * ebl[:, lanes] + _dot_tn(v2, kl_stack)

    qi = lax.broadcasted_iota(jnp.int32, (W, 2 * W), 0)
    kc = lax.broadcasted_iota(jnp.int32, (W, 2 * W), 1)
    rel = qi + W - kc
    relf = rel.astype(f32)
    in_window = (rel >= 0) & (rel < W)

    def swa_probs(j, kv):
        qrows = slice(j * W, (j + 1) * W)
        band = slice(j * W, j * W + 2 * W)
        if j == 0:
            mask = in_window & ((kc >= W) | (i > 0))
        else:
            mask = in_window
        pairs = (2 * kv, 2 * kv + 1)
        q2 = jnp.concatenate([qb[qrows, p * LANES:(p + 1) * LANES] for p in pairs], axis=0)
        out = []
        for e in range(2):
            s2 = _dot_nt(q2, k_var[kv][e][band])
            probs = []
            for half, p in enumerate(pairs):
                hd = 2 * p + e
                s = s2[half * W:(half + 1) * W]
                s = jnp.where(mask, s - _alibi_slope(hd) * relf, -jnp.inf)
                pr, inv = _softmax_sink(s, sink_ref[hd] * LOG2E)
                probs.append((pr * inv).astype(bf16))
            out.append(jnp.concatenate(probs, axis=0))
        return out

    def swa_out(j, kv, probs):
        qrows = slice(j * W, (j + 1) * W)
        band = slice(j * W, j * W + 2 * W)
        o2 = _dot(probs[0], v_var[kv][0][band]) + _dot(probs[1], v_var[kv][1][band])
        for half, p in enumerate((2 * kv, 2 * kv + 1)):
            ob_scr[qrows, p * LANES:(p + 1) * LANES] = o2[half * W:(half + 1) * W]

    n_chunks = T // C
    assert n_chunks == (T // W) * SWA_KV_HEADS
    gw = 2 * D_MODEL // n_chunks
    for idx in range(n_chunks):
        j, kv = idx // SWA_KV_HEADS, idx % SWA_KV_HEADS
        probs = swa_probs(j, kv)
        sc = gla_scores(idx)
        gate_scr[:, idx * gw:(idx + 1) * gw] = proj(C_GTA + idx * gw, gw)
        gla_update(idx, sc)
        swa_out(j, kv, probs)
    for p in range(GLA_HEADS // 2):
        st_scr[:, p * LANES:(p + 1) * LANES] = st[p]
    ga = proj(C_GA, GLA_V)
    gated_b = jax.nn.sigmoid(gate_scr[:, D_MODEL:2 * D_MODEL]) * _dot(ob_scr[...].astype(bf16), wbb_ref[...])
    sig_a = jax.nn.sigmoid(gate_scr[:, 0:D_MODEL])

    def finish(oa_raw):
        oa = _gla_out_norm(oa_raw, gn_ref, ga)
        merged = sig_a * _dot(oa.astype(bf16), wba_ref[...]) + gated_b
        m = _dot(merged.astype(bf16), wout_ref[...])
        y_ref[...] = x + _rms(m, npost_ref[...])

    finish(oa_scr[...])

    @pl.when(decay_floor < -GLA_SAFE_DECAY)
    def _():
        qs = qa * (GLA_DK ** -0.5)
        va_f = va_b.astype(f32)
        pos = lax.broadcasted_iota(jnp.int32, (T, 1), 0) % C
        er = lax.broadcasted_iota(jnp.int32, (GLA_K, GLA_V), 0)
        ec = lax.broadcasted_iota(jnp.int32, (GLA_K, GLA_V), 1)
        expand = jnp.where((er // GLA_DK) == (ec // GLA_DV), 1.0, 0.0).astype(bf16)

        def offset_term(d, acc):
            valid = pos >= d
            expo = jnp.where(valid, b - pltpu.roll(b, d, 0), 0.0)
            prod = jnp.where(valid, qs * pltpu.roll(ka, d, 0) * jnp.exp(expo), 0.0)
            return acc + _dot(prod.astype(bf16), expand) * pltpu.roll(va_f, d, 0)

        intra = lax.fori_loop(0, C, offset_term, jnp.zeros((T, GLA_V), f32))
        finish(inter_scr[...] + intra)

    @pl.when(i == pl.num_programs(0) - 1)
    def _():
        st_out_ref[...] = st_scr[...].T
        k_out_ref[...] = kcat_scr[T:T + W, :].T
        v_out_ref[...] = vcat_scr[T:T + W, :].T


def _const_spec(shape):
    nd = len(shape)
    return pl.BlockSpec(shape, lambda i: (0,) * nd, pipeline_mode=pl.Buffered(1))


def _mix_prompt(x, sinks, npre, win, wup, bg, gn, wba, wbb, wout, npost, *, T):
    L = x.shape[0]
    nb = L // T
    body = functools.partial(_mix_prompt_body, T=T)
    return pl.pallas_call(
        body,
        grid=(nb,),
        in_specs=[
            pl.BlockSpec(memory_space=pltpu.SMEM),
            pl.BlockSpec((T, D_MODEL), lambda i: (i, 0)),
            _const_spec(npre.shape), _const_spec(win.shape), _const_spec(wup.shape),
            _const_spec(bg.shape), _const_spec(gn.shape), _const_spec(wba.shape),
            _const_spec(wbb.shape), _const_spec(wout.shape), _const_spec(npost.shape),
        ],
        out_specs=[
            pl.BlockSpec((T, D_MODEL), lambda i: (i, 0)),
            pl.BlockSpec((GLA_K, GLA_DV), lambda i: (0, 0)),
            pl.BlockSpec((WINDOW, SWA_KV), lambda i: (0, 0)),
            pl.BlockSpec((WINDOW, SWA_KV), lambda i: (0, 0)),
        ],
        out_shape=[
            jax.ShapeDtypeStruct((L, D_MODEL), f32),
            jax.ShapeDtypeStruct((GLA_K, GLA_DV), f32),
            jax.ShapeDtypeStruct((WINDOW, SWA_KV), f32),
            jax.ShapeDtypeStruct((WINDOW, SWA_KV), f32),
        ],
        scratch_shapes=[
            pltpu.VMEM((GLA_DV, GLA_K), f32),
            pltpu.VMEM((T + WINDOW, SWA_KV), f32),
            pltpu.VMEM((T + WINDOW, SWA_KV), f32),
            pltpu.VMEM((T, GLA_V), f32),
            pltpu.VMEM((T, SWA_Q), f32),
            pltpu.VMEM((T, 2 * D_MODEL), f32),
            pltpu.VMEM((T, GLA_V), f32),
        ],
        compiler_params=pltpu.CompilerParams(
            dimension_semantics=("arbitrary",), vmem_limit_bytes=VMEM_LIMIT),
        name="mix_prompt",
    )(sinks, x, npre, win, wup, bg, gn, wba, wbb, wout, npost)


def _pre_sample_body(x_ref, npre_ref, win_ref, wup_ref, bg_ref,
                     qe_ref, kl_ref, e3_ref, oin_ref, va_ref, ga_ref, qb_ref, kb_ref, vb_ref,
                     gta_ref, gtb_ref, x_scr, *, NB, NT):
    for t in range(NT):
        x_scr[t * NB:(t + 1) * NB, :] = x_ref[:, t, :]
    h = _rms(x_scr[...], npre_ref[...]).astype(bf16)

    def proj(c0, n):
        return _dot(h, win_ref[:, c0:c0 + n])

    def blk(val, t):
        return val[t * NB:(t + 1) * NB, :]

    xg = _dot(proj(C_RA, RA_PAD).astype(bf16), wup_ref[...]) + bg_ref[...]
    qa = proj(C_QA, GLA_K) * (GLA_DK ** -0.5)
    ka = proj(C_KA, GLA_K)
    va = proj(C_VA, GLA_V)
    va_ref[...] = va
    la = jax.nn.log_sigmoid(xg) * (1.0 / GLA_TAU)
    b = [blk(la, 0)]
    for t in range(1, NT):
        b.append(b[-1] + blk(la, t))
    e3_ref[...] = jnp.exp(b[NT - 1])
    for t in range(NT):
        qe_ref[t * NB:(t + 1) * NB, :] = blk(qa, t) * jnp.exp(b[t])
        kl_ref[t * NB:(t + 1) * NB, :] = blk(ka, t) * jnp.exp(b[NT - 1] - b[t])
    pairs = [(t, j) for t in range(NT) for j in range(t + 1)]
    prods = [(blk(qa, t) * blk(ka, j) * jnp.exp(b[t] - b[j])).astype(bf16) for t, j in pairs]
    r = lax.broadcasted_iota(jnp.int32, (GLA_K, GLA_V), 0)
    c = lax.broadcasted_iota(jnp.int32, (GLA_K, GLA_V), 1)
    expand = jnp.where((r // GLA_DK) == (c // GLA_DV), 1.0, 0.0).astype(bf16)
    ga_ref[...] = proj(C_GA, GLA_V)
    qb_ref[...] = proj(C_QB, SWA_Q) * SWA_Q_SCALE
    kb_ref[...] = proj(C_KB, SWA_KV)
    vb_ref[...] = proj(C_VB, SWA_KV)
    att = _dot(jnp.concatenate(prods, axis=0), expand)
    gta_ref[...] = proj(C_GTA, D_MODEL)
    gtb_ref[...] = proj(C_GTB, D_MODEL)
    for t in range(NT):
        acc = None
        for idx, (tt, j) in enumerate(pairs):
            if tt != t:
                continue
            term = att[idx * NB:(idx + 1) * NB, :] * blk(va, j)
            acc = term if acc is None else acc + term
        oin_ref[t * NB:(t + 1) * NB, :] = acc


def _pre_sample(xs, npre, win, wup, bg):
    NB, NT, _ = xs.shape
    body = functools.partial(_pre_sample_body, NB=NB, NT=NT)
    widths = (GLA_K, GLA_K, None, GLA_V, GLA_V, GLA_V, SWA_Q, SWA_KV, SWA_KV, D_MODEL, D_MODEL)
    out_shape = [jax.ShapeDtypeStruct((NB, GLA_K) if w is None else (NT * NB, w), f32) for w in widths]
    return pl.pallas_call(
        body,
        out_shape=out_shape,
        scratch_shapes=[pltpu.VMEM((NB * NT, D_MODEL), f32)],
        compiler_params=pltpu.CompilerParams(vmem_limit_bytes=VMEM_LIMIT),
        name="pre_sample",
    )(xs, npre, win, wup, bg)


def _state_sample_body(sink_ref, qe_ref, kl_ref, e3_ref, oin_ref, va_ref, qb_ref, kb_ref, vb_ref,
                       s0_ref, kt_ref, vt_ref,
                       oa_ref, ob_ref, s1_ref, kt1_ref, vt1_ref, *, BB, NT):
    W = WINDOW
    SK = 2 * W
    HT = GLA_HEADS * NT
    HALF = SWA_HD
    hr = lax.broadcasted_iota(jnp.int32, (HT, GLA_K), 0) // NT
    hc = lax.broadcasted_iota(jnp.int32, (HT, GLA_K), 1) // GLA_DK
    own_head = hr == hc
    ones_rows = jnp.ones((16, GLA_DV), bf16)
    zero_rows = jnp.zeros((16, GLA_DV), bf16)
    zero_ht = jnp.zeros((HT, GLA_DV), bf16)
    G2 = 2 * NT
    row = lax.broadcasted_iota(jnp.int32, (G2, SK), 0)
    col = lax.broadcasted_iota(jnp.int32, (G2, SK), 1)
    rel = (row % NT) + W - col
    relf = rel.astype(f32)
    smask = (rel >= 0) & (rel < W)
    first_pair = lax.broadcasted_iota(jnp.int32, (G2, 1), 0) < NT
    pad_rows = jnp.zeros((8 - NT, SWA_KV), f32)
    pad_lanes = jnp.zeros((SWA_KV, SK - W - 8), f32)
    zero_half = jnp.zeros((HALF, SK), bf16)

    def head_variants(cat_t, kv):
        blk = cat_t[kv * HALF:(kv + 1) * HALF]
        return (jnp.concatenate([blk, zero_half], axis=0), jnp.concatenate([zero_half, blk], axis=0))

    pending = []
    for bi in range(BB):
        s0 = s0_ref[bi]
        q4 = qe_ref[:, bi, :]
        qm = jnp.where(own_head, jnp.concatenate([q4] * GLA_HEADS, axis=0), 0.0).astype(bf16)
        o_inter = _dot(qm, s0.astype(bf16))
        for hd in range(GLA_HEADS):
            lanes = slice(hd * GLA_DV, (hd + 1) * GLA_DV)
            oa_ref[:, bi, lanes] = o_inter[hd * NT:(hd + 1) * NT, :] + oin_ref[:, bi, lanes]
        k4 = kl_ref[:, bi, :]
        km = jnp.where(own_head, jnp.concatenate([k4] * GLA_HEADS, axis=0), 0.0).astype(bf16)
        e = e3_ref[bi:bi + 1, :]
        e_hi = e.astype(bf16)
        r1 = e - e_hi.astype(f32)
        e_mid = r1.astype(bf16)
        e_lo = (r1 - e_mid.astype(f32)).astype(bf16)
        e_rows = jnp.concatenate([e_hi, e_mid, e_lo, jnp.zeros((13, GLA_K), bf16)], axis=0)
        lhs = jnp.concatenate([km, e_rows], axis=0)
        v4 = va_ref[:, bi, :].astype(bf16)
        vrep = jnp.concatenate([v4[:, hd * GLA_DV:(hd + 1) * GLA_DV] for hd in range(GLA_HEADS)], axis=0)
        rhs = jnp.concatenate([jnp.concatenate([vrep, zero_ht], axis=1),
                               jnp.concatenate([zero_rows, ones_rows], axis=1)], axis=0)
        res = _dot_tn(lhs, rhs)
        s1_ref[bi] = res[:, GLA_DV:] * s0 + res[:, :GLA_DV]

        kt = kt_ref[bi]
        vt = vt_ref[bi]
        knew_t = jnp.concatenate([kb_ref[:, bi, :], pad_rows], axis=0).T
        vnew_t = jnp.concatenate([vb_ref[:, bi, :], pad_rows], axis=0).T
        kt1_ref[bi] = jnp.concatenate([kt[:, NT:], knew_t[:, 0:NT]], axis=1)
        vt1_ref[bi] = jnp.concatenate([vt[:, NT:], vnew_t[:, 0:NT]], axis=1)
        kcat = jnp.concatenate([kt, knew_t, pad_lanes], axis=1).astype(bf16)
        vcat = jnp.concatenate([vt, vnew_t, pad_lanes], axis=1).astype(bf16)
        q4b = qb_ref[:, bi, :].astype(bf16)
        for kv in range(SWA_KV_HEADS):
            p0 = 2 * kv
            q8 = jnp.concatenate([q4b[:, p0 * LANES:(p0 + 1) * LANES],
                                  q4b[:, (p0 + 1) * LANES:(p0 + 2) * LANES]], axis=0)
            scores = [_dot(q8, kvar) for kvar in head_variants(kcat, kv)]
            pending.append((bi, kv, scores, head_variants(vcat, kv)))

    for bi, kv, scores, v_vars in pending:
        p0 = 2 * kv
        o8_t = None
        for e_ in range(2):
            h_first = 2 * p0 + e_
            h_second = 2 * (p0 + 1) + e_
            slope = jnp.where(first_pair, _alibi_slope(h_first), _alibi_slope(h_second))
            sink = jnp.where(first_pair, sink_ref[h_first] * LOG2E, sink_ref[h_second] * LOG2E)
            s = jnp.where(smask, scores[e_] - slope * relf, -jnp.inf)
            pr, inv = _softmax_sink(s, sink)
            o_t = _dot_nt(v_vars[e_], (pr * inv).astype(bf16))
            o8_t = o_t if o8_t is None else o8_t + o_t
        o8 = o8_t.T
        ob_ref[:, bi, p0 * LANES:(p0 + 1) * LANES] = o8[0:NT, :]
        ob_ref[:, bi, (p0 + 1) * LANES:(p0 + 2) * LANES] = o8[NT:2 * NT, :]


def _state_sample(sinks, qe, kl, e3, oin, va, qb, kb, vb, s0, kt, vt, *, NT, BB):
    NBS = s0.shape[0]
    assert NBS % BB == 0
    body = functools.partial(_state_sample_body, BB=BB, NT=NT)

    def tm(a):
        return a.reshape(NT, NBS, a.shape[-1])

    def rows(n):
        return pl.BlockSpec((NT, BB, n), lambda i: (0, i, 0))

    def per_seq(shape):
        return pl.BlockSpec((BB,) + shape, lambda i: (i, 0, 0))

    oa, ob, s1, kt1, vt1 = pl.pallas_call(
        body,
        grid=(NBS // BB,),
        in_specs=[
            pl.BlockSpec(memory_space=pltpu.SMEM),
            rows(GLA_K), rows(GLA_K), pl.BlockSpec((BB, GLA_K), lambda i: (i, 0)),
            rows(GLA_V), rows(GLA_V), rows(SWA_Q), rows(SWA_KV), rows(SWA_KV),
            per_seq((GLA_K, GLA_DV)), per_seq((SWA_KV, WINDOW)), per_seq((SWA_KV, WINDOW)),
        ],
        out_specs=[
            rows(GLA_V), rows(SWA_Q),
            per_seq((GLA_K, GLA_DV)), per_seq((SWA_KV, WINDOW)), per_seq((SWA_KV, WINDOW)),
        ],
        out_shape=[
            jax.ShapeDtypeStruct((NT, NBS, GLA_V), f32),
            jax.ShapeDtypeStruct((NT, NBS, SWA_Q), f32),
            jax.ShapeDtypeStruct((NBS, GLA_K, GLA_DV), f32),
            jax.ShapeDtypeStruct((NBS, SWA_KV, WINDOW), f32),
            jax.ShapeDtypeStruct((NBS, SWA_KV, WINDOW), f32),
        ],
        compiler_params=pltpu.CompilerParams(
            dimension_semantics=("arbitrary",), vmem_limit_bytes=VMEM_LIMIT),
        name="state_sample",
    )(sinks, tm(qe), tm(kl), e3, tm(oin), tm(va), tm(qb), tm(kb), tm(vb), s0, kt, vt)
    return oa.reshape(NT * NBS, GLA_V), ob.reshape(NT * NBS, SWA_Q), s1, kt1, vt1


def _post_sample_body(x_ref, oa_ref, ga_ref, ob_ref, gta_ref, gtb_ref, gn_ref,
                      wba_ref, wbb_ref, wout_ref, npost_ref, y_ref, x_scr, *, NB, NT):
    for t in range(NT):
        x_scr[t * NB:(t + 1) * NB, :] = x_ref[:, t, :]
    oa = _gla_out_norm(oa_ref[...], gn_ref, ga_ref[...])
    y_ref[...] = _mix_tail(x_scr[...], oa, ob_ref[...], gta_ref[...], gtb_ref[...],
                           wba_ref, wbb_ref, wout_ref, npost_ref)


def _post_sample(xs, oa, ga, ob, gta, gtb, gn, wba, wbb, wout, npost):
    NB, NT, _ = xs.shape
    return pl.pallas_call(
        functools.partial(_post_sample_body, NB=NB, NT=NT),
        out_shape=jax.ShapeDtypeStruct((NT * NB, D_MODEL), f32),
        scratch_shapes=[pltpu.VMEM((NT * NB, D_MODEL), f32)],
        compiler_params=pltpu.CompilerParams(vmem_limit_bytes=VMEM_LIMIT),
        name="post_sample",
    )(xs, oa, ga, ob, gta, gtb, gn, wba, wbb, wout, npost)


def _ffn_columns(h, wffn_ref, cw_ref, cb_ref, up_scr, y_scr, *, T, base, shift):
    for c0 in range(0, D_FF, FFN_COLS):
        cols = slice(c0, c0 + FFN_COLS)
        u = _dot(h, wffn_ref[:, c0:c0 + FFN_COLS])
        g = _dot(h, wffn_ref[:, D_FF + c0:D_FF + c0 + FFN_COLS])
        up_scr[base:base + T, cols] = u
        u1 = up_scr[base - shift:base - shift + T, cols]
        u2 = up_scr[base - 2 * shift:base - 2 * shift + T, cols]
        cv = (cb_ref[:, cols] + cw_ref[2:3, cols] * u + cw_ref[1:2, cols] * u1 + cw_ref[0:1, cols] * u2)
        y_scr[:, cols] = (_gelu_tanh(cv) * g).astype(bf16)


def _ffn_prompt_body(x_ref, npre_ref, wffn_ref, cw_ref, cb_ref, wo_ref, npost_ref,
                     y_ref, conv_out_ref, up_scr, y_scr, *, T):
    i = pl.program_id(0)
    base = 8

    @pl.when(i == 0)
    def _():
        up_scr[0:base, :] = jnp.zeros((base, D_FF), f32)

    @pl.when(i > 0)
    def _():
        up_scr[0:base, :] = up_scr[T:T + base, :]

    x = x_ref[...]
    h = _rms(x, npre_ref[...]).astype(bf16)
    _ffn_columns(h, wffn_ref, cw_ref, cb_ref, up_scr, y_scr, T=T, base=base, shift=1)
    f = _dot(y_scr[...], wo_ref[...])
    y_ref[...] = x + _rms(f, npost_ref[...])

    @pl.when(i == pl.num_programs(0) - 1)
    def _():
        conv_out_ref[...] = up_scr[T + base - (CONV_W - 1):T + base, :]


def _ffn_prompt(x, npre, wffn, cw, cb, wo, npost, *, T):
    L = x.shape[0]
    body = functools.partial(_ffn_prompt_body, T=T)
    return pl.pallas_call(
        body,
        grid=(L // T,),
        in_specs=[
            pl.BlockSpec((T, D_MODEL), lambda i: (i, 0)),
            _const_spec(npre.shape), _const_spec(wffn.shape), _const_spec(cw.shape),
            _const_spec(cb.shape), _const_spec(wo.shape), _const_spec(npost.shape),
        ],
        out_specs=[
            pl.BlockSpec((T, D_MODEL), lambda i: (i, 0)),
            pl.BlockSpec((CONV_W - 1, D_FF), lambda i: (0, 0)),
        ],
        out_shape=[
            jax.ShapeDtypeStruct((L, D_MODEL), f32),
            jax.ShapeDtypeStruct((CONV_W - 1, D_FF), f32),
        ],
        scratch_shapes=[
            pltpu.VMEM((T + 8, D_FF), f32),
            pltpu.VMEM((T, D_FF), bf16),
        ],
        compiler_params=pltpu.CompilerParams(
            dimension_semantics=("arbitrary",), vmem_limit_bytes=VMEM_LIMIT),
        name="ffn_prompt",
    )(x, npre, wffn, cw, cb, wo, npost)


def _ffn_sample_body(x_ref, cst_ref, npre_ref, wffn_ref, cw_ref, cb_ref, wo_ref, npost_ref,
                     y_ref, conv_out_ref, up_scr, y_scr, *, NB, NT):
    T = NB * NT
    for t in range(CONV_W - 1):
        up_scr[t * NB:(t + 1) * NB, :] = cst_ref[:, t, :]
    base = (CONV_W - 1) * NB
    x = x_ref[...]
    h = _rms(x, npre_ref[...]).astype(bf16)
    _ffn_columns(h, wffn_ref, cw_ref, cb_ref, up_scr, y_scr, T=T, base=base, shift=NB)
    f = _dot(y_scr[...], wo_ref[...])
    y = x + _rms(f, npost_ref[...])
    for t in range(NT):
        y_ref[:, t, :] = y[t * NB:(t + 1) * NB, :]
    for t in range(CONV_W - 1):
        conv_out_ref[:, t, :] = up_scr[T + t * NB:T + (t + 1) * NB, :]


def _ffn_sample(x, cst, npre, wffn, cw, cb, wo, npost):
    NB = cst.shape[0]
    T = x.shape[0]
    NT = T // NB
    body = functools.partial(_ffn_sample_body, NB=NB, NT=NT)
    return pl.pallas_call(
        body,
        out_shape=[
            jax.ShapeDtypeStruct((NB, NT, D_MODEL), f32),
            jax.ShapeDtypeStruct((NB, CONV_W - 1, D_FF), f32),
        ],
        scratch_shapes=[
            pltpu.VMEM((T + (CONV_W - 1) * NB, D_FF), f32),
            pltpu.VMEM((T, D_FF), bf16),
        ],
        compiler_params=pltpu.CompilerParams(vmem_limit_bytes=VMEM_LIMIT),
        name="ffn_sample",
    )(x, cst, npre, wffn, cw, cb, wo, npost)


def _prep_w_in_body(*refs, n_plain, per_step):
    o_ref = refs[-1]
    j = pl.program_id(0)
    for k, wt_ref in enumerate(refs[:-1]):
        x = wt_ref[...]
        r = lax.broadcasted_iota(jnp.int32, x.shape, 0)
        x = jnp.where((j * per_step + k < n_plain) | (r < GLA_RANK), x, 0.0)
        o_ref[:, k * PREP_ROWS:(k + 1) * PREP_ROWS] = x.T.astype(bf16)


def _prep_w_in(w_in):
    d_in, n_cols = w_in.shape
    head = C_GA
    tail_src = head + GLA_RANK
    n_head = head // PREP_ROWS
    n_tail = (n_cols - tail_src) // PREP_ROWS
    assert head % PREP_ROWS == 0 and (n_cols - tail_src) % PREP_ROWS == 0
    n_plain = n_head + n_tail
    assert C_RA == n_plain * PREP_ROWS
    per_step = 3
    assert (n_plain + 1) % per_step == 0

    def row_off(blk):
        off = jnp.where(blk < n_head, blk * PREP_ROWS,
                        jnp.where(blk < n_plain, tail_src + (blk - n_head) * PREP_ROWS, head))
        return pl.multiple_of(off, 8)

    def in_spec(k):
        return pl.BlockSpec((pl.Element(PREP_ROWS), pl.Element(d_in)), lambda j: (row_off(j * per_step + k), 0))

    wt = jnp.swapaxes(w_in, 0, 1)
    return pl.pallas_call(
        functools.partial(_prep_w_in_body, n_plain=n_plain, per_step=per_step),
        grid=((n_plain + 1) // per_step,),
        in_specs=[in_spec(k) for k in range(per_step)],
        out_specs=pl.BlockSpec((d_in, per_step * PREP_ROWS), lambda j: (0, j)),
        out_shape=jax.ShapeDtypeStruct((d_in, (n_plain + 1) * PREP_ROWS), bf16),
        compiler_params=pltpu.CompilerParams(dimension_semantics=("arbitrary",)),
        name="prep_w_in",
    )(*([wt] * per_step))


def kernel(x_prompt, x_sample, state_gla, cache_swa_k, cache_swa_v, state_ffn_conv, norm_mix_pre, norm_mix_post, w_in, w_gate_up, b_gate, gla_norm, sinks, w_branch_a, w_branch_b, w_out, norm_ffn_pre, norm_ffn_post, w_ffn_in, conv_w, conv_b, w_ffn_out):
    depth = w_in.shape[0]
    assert depth == 1
    l = 0
    B, L, _ = x_prompt.shape
    assert B == 1
    NBS, NT, _ = x_sample.shape
    assert L % MIX_BLOCK == 0 and L % FFN_BLOCK == 0 and NBS % STATE_SEQS == 0
    assert cache_swa_k.shape[2] == WINDOW and NT < 8

    win = _prep_w_in(w_in[l])
    wup = jnp.zeros((RA_PAD, GLA_K), f32).at[:GLA_RANK].set(w_gate_up[l]).astype(bf16)
    bg = b_gate[l].reshape(1, GLA_K)
    gn = gla_norm[l].reshape(1, GLA_DV)
    npre = norm_mix_pre[l].reshape(1, D_MODEL)
    npost = norm_mix_post[l].reshape(1, D_MODEL)
    wba = w_branch_a[l].astype(bf16)
    wbb = w_branch_b[l].astype(bf16)
    wout = w_out[l].astype(bf16)
    fpre = norm_ffn_pre[l].reshape(1, D_MODEL)
    fpost = norm_ffn_post[l].reshape(1, D_MODEL)
    wffn = w_ffn_in[l].astype(bf16)
    cw = conv_w[l]
    cb = conv_b[l].reshape(1, D_FF)
    wo = w_ffn_out[l].astype(bf16)
    sk = sinks[l]

    x1, st_p, k_p, v_p = _mix_prompt(x_prompt[0], sk, npre, win, wup, bg, gn, wba, wbb, wout, npost, T=MIX_BLOCK)
    y_p, conv_p = _ffn_prompt(x1, fpre, wffn, cw, cb, wo, fpost, T=FFN_BLOCK)

    y_prompt = y_p[None]
    gla_state_prompt = st_p.reshape(1, 1, GLA_HEADS, GLA_DK, GLA_DV)
    swa_k_prompt = jnp.transpose(k_p.reshape(SWA_KV_HEADS, SWA_HD, WINDOW), (2, 0, 1))[None, None]
    swa_v_prompt = jnp.transpose(v_p.reshape(SWA_KV_HEADS, SWA_HD, WINDOW), (2, 0, 1))[None, None]
    conv_prompt = conv_p[None, None]

    qe, kl, e3, oin, va, ga, qb, kb, vb, gta, gtb = _pre_sample(x_sample, npre, win, wup, bg)
    kt = jnp.transpose(cache_swa_k[l], (0, 2, 3, 1)).reshape(NBS, SWA_KV, WINDOW)
    vt = jnp.transpose(cache_swa_v[l], (0, 2, 3, 1)).reshape(NBS, SWA_KV, WINDOW)
    oa_raw, ob, s1, kt1, vt1 = _state_sample(
        sk, qe, kl, e3, oin, va, qb, kb, vb, state_gla[l].reshape(NBS, GLA_K, GLA_DV), kt, vt, NT=NT, BB=STATE_SEQS)
    x1s = _post_sample(x_sample, oa_raw, ga, ob, gta, gtb, gn, wba, wbb, wout, npost)
    y_sample, conv_s = _ffn_sample(x1s, state_ffn_conv[l], fpre, wffn, cw, cb, wo, fpost)

    def cache_out(t):
        return jnp.transpose(t.reshape(NBS, SWA_KV_HEADS, SWA_HD, WINDOW), (0, 3, 1, 2))[None]

    gla_state_sample = s1.reshape(1, NBS, GLA_HEADS, GLA_DK, GLA_DV)
    swa_k_sample = cache_out(kt1)
    swa_v_sample = cache_out(vt1)
    conv_sample = conv_s[None]
    return (y_prompt, y_sample, gla_state_prompt, gla_state_sample, swa_k_prompt, swa_v_prompt,
            swa_k_sample, swa_v_sample, conv_prompt, conv_sample)
```

```python
import functools

import jax
import jax.numpy as jnp
from jax import lax
from jax.experimental import pallas as pl
from jax.experimental.pallas import tpu as pltpu

f32 = jnp.float32
bf16 = jnp.bfloat16

D_MODEL = 1024
GLA_HEADS = 4
GLA_DK = 64
GLA_DV = 128
GLA_RANK = 16
GLA_TAU = 16.0
GLA_CHUNK = 64
GLA_SAFE_DECAY = 60.0
SWA_HEADS = 8
SWA_KV_HEADS = 2
SWA_HD = 64
WINDOW = 128
D_FF = 2816
CONV_W = 3
EPS = 1e-6
GLA_K = GLA_HEADS * GLA_DK
GLA_V = GLA_HEADS * GLA_DV
SWA_Q = SWA_HEADS * SWA_HD
SWA_KV = SWA_KV_HEADS * SWA_HD
LANES = 128
LOG2E = 1.4426950408889634

C_QA = 0
C_KA = C_QA + GLA_K
C_VA = C_KA + GLA_K
C_GA = C_VA + GLA_V
C_QB = C_GA + GLA_V
C_KB = C_QB + SWA_Q
C_VB = C_KB + SWA_KV
C_GTA = C_VB + SWA_KV
C_GTB = C_GTA + D_MODEL
C_RA = C_GTB + D_MODEL
RA_PAD = LANES
PREP_ROWS = 256
IN_COLS_PAD = C_RA + PREP_ROWS

MIX_BLOCK = 256
FFN_BLOCK = 1024
FFN_COLS = 256
STATE_SEQS = 16
VMEM_LIMIT = 56 * 1024 * 1024


def _dot(a, b):
    return jnp.dot(a, b, preferred_element_type=f32)


def _dot_nt(a, b):
    return lax.dot_general(a, b, (((1,), (1,)), ((), ())), preferred_element_type=f32)


def _dot_tn(a, b):
    return lax.dot_general(a, b, (((0,), (0,)), ((), ())), preferred_element_type=f32)


def _rms(x, w):
    return x * lax.rsqrt(jnp.mean(x * x, axis=-1, keepdims=True) + EPS) * w


def _gelu_tanh(x):
    k = -2.0 * 0.7978845608028654 * LOG2E
    return x / (1.0 + jnp.exp2(x * (k + (k * 0.044715) * (x * x))))


def _split_hi_lo(x):
    hi = x.astype(bf16)
    lo = (x - hi.astype(f32)).astype(bf16)
    return hi, lo


def _chunk_cumsum(la, chunk):
    n = la.shape[0]
    r = lax.broadcasted_iota(jnp.int32, (n, n), 0)
    c = lax.broadcasted_iota(jnp.int32, (n, n), 1)
    tri = jnp.where((c <= r) & ((r // chunk) == (c // chunk)), 1.0, 0.0).astype(bf16)
    hi, lo = _split_hi_lo(la)
    return _dot(tri, hi) + _dot(tri, lo)


def _even_head_lanes(shape):
    lane = lax.broadcasted_iota(jnp.int32, shape, len(shape) - 1)
    return (lane % LANES) < GLA_DK


def _gla_out_norm(o, gn_ref, ga):
    outs = []
    for h in range(GLA_HEADS):
        oh = o[:, h * GLA_DV:(h + 1) * GLA_DV]
        outs.append(_rms(oh, gn_ref[...]))
    on = jnp.concatenate(outs, axis=1)
    return on * (ga * jax.nn.sigmoid(ga))


def _mix_tail(x, oa, ob, gate_a, gate_b, wba_ref, wbb_ref, wout_ref, npost_ref):
    merged = (jax.nn.sigmoid(gate_a) * _dot(oa.astype(bf16), wba_ref[...])
              + jax.nn.sigmoid(gate_b) * _dot(ob.astype(bf16), wbb_ref[...]))
    m = _dot(merged.astype(bf16), wout_ref[...])
    return x + _rms(m, npost_ref[...])


def _alibi_slope(head):
    return LOG2E * 2.0 ** (-(8.0 / SWA_HEADS) * (head + 1))


SWA_Q_SCALE = LOG2E * SWA_HD ** -0.5


def _kv_variants(x):
    lo = _even_head_lanes(x.shape)
    xr = pltpu.roll(x, SWA_HD, 1)
    zero = jnp.zeros_like(x)
    h0_lo = jnp.where(lo, x, zero).astype(bf16)
    h1_hi = jnp.where(lo, zero, x).astype(bf16)
    h1_lo = jnp.where(lo, xr, zero).astype(bf16)
    h0_hi = jnp.where(lo, zero, xr).astype(bf16)
    return (h0_lo, h0_hi), (h1_lo, h1_hi)


def _softmax_sink(s, sink):
    m = jnp.maximum(jnp.max(s, axis=-1, keepdims=True), sink)
    p = jnp.exp2(s - m)
    denom = jnp.sum(p, axis=-1, keepdims=True) + jnp.exp2(sink - m)
    return p, 1.0 / denom


def _mix_prompt_body(sink_ref, x_ref, npre_ref, win_ref, wup_ref, bg_ref, gn_ref,
                     wba_ref, wbb_ref, wout_ref, npost_ref,
                     y_ref, st_out_ref, k_out_ref, v_out_ref,
                     st_scr, kcat_scr, vcat_scr, oa_scr, ob_scr, gate_scr, inter_scr, *, T):
    i = pl.program_id(0)
    W = WINDOW
    C = GLA_CHUNK

    @pl.when(i == 0)
    def _():
        st_scr[...] = jnp.zeros_like(st_scr)
        kcat_scr[0:W, :] = jnp.zeros((W, SWA_KV), f32)
        vcat_scr[0:W, :] = jnp.zeros((W, SWA_KV), f32)

    @pl.when(i > 0)
    def _():
        kcat_scr[0:W, :] = kcat_scr[T:T + W, :]
        vcat_scr[0:W, :] = vcat_scr[T:T + W, :]

    x = x_ref[...]
    rms_f = lax.rsqrt(jnp.mean(x * x, axis=-1, keepdims=True) + EPS)
    h = (x * npre_ref[...]).astype(bf16)
    rms_b = {n: jnp.broadcast_to(rms_f, (T, n)) for n in (LANES, 2 * LANES)}

    def proj(c0, n):
        w = 2 * LANES if n % (2 * LANES) == 0 else LANES
        return jnp.concatenate([_dot(h, win_ref[:, c:c + w]) * rms_b[w] for c in range(c0, c0 + n, w)], axis=1)


    xg = _dot(proj(C_RA, RA_PAD).astype(bf16), wup_ref[...]) + bg_ref[...]
    kcat_scr[W:W + T, :] = proj(C_KB, SWA_KV)
    vcat_scr[W:W + T, :] = proj(C_VB, SWA_KV)
    qb = (proj(C_QB, SWA_Q) * SWA_Q_SCALE).astype(bf16)
    la = jax.nn.log_sigmoid(xg) * (1.0 / GLA_TAU)
    b = _chunk_cumsum(la, C)
    decay_floor = jnp.min(b)
    qa = proj(C_QA, GLA_K)
    ka = proj(C_KA, GLA_K)
    va_b = proj(C_VA, GLA_V).astype(bf16)

    qe = qa * jnp.exp(b) * (GLA_DK ** -0.5)
    ke = (ka * jnp.exp(-b)).astype(bf16)
    even = _even_head_lanes((T, GLA_K))
    qe_even = jnp.where(even, qe, 0.0).astype(bf16)
    qe_odd = jnp.where(even, 0.0, qe).astype(bf16)
    k_var = _kv_variants(kcat_scr[...])
    v_var = _kv_variants(vcat_scr[...])

    r2 = lax.broadcasted_iota(jnp.int32, (2 * C, 2 * C), 0)
    c2 = lax.broadcasted_iota(jnp.int32, (2 * C, 2 * C), 1)
    pair_causal = ((r2 // C) == (c2 // C)) & ((c2 % C) <= (r2 % C))
    even_c = _even_head_lanes((C, LANES))
    st = [st_scr[:, p * LANES:(p + 1) * LANES] for p in range(GLA_HEADS // 2)]

    def gla_scores(c):
        rows = slice(c * C, (c + 1) * C)
        out = []
        for p in range(GLA_HEADS // 2):
            lanes = slice(p * LANES, (p + 1) * LANES)
            q2 = jnp.concatenate([qe_even[rows, lanes], qe_odd[rows, lanes]], axis=0)
            ke_p = ke[rows, lanes]
            rhs = jnp.concatenate([ke_p, ke_p, st[p].astype(bf16)], axis=0)
            r = _dot_nt(q2, rhs)
            att = jnp.where(pair_causal, r[:, 0:2 * C], 0.0).astype(bf16)
            out.append((att, r[:, 2 * C:]))
        return out

    def gla_update(c, sc):
        rows = slice(c * C, (c + 1) * C)
        b_c = b[rows]
        bl = b_c[C - 1:C, :]
        kl = ka[rows] * jnp.exp(bl - b_c)
        ebl = jnp.exp(bl)
        for p in range(GLA_HEADS // 2):
            lanes = slice(p * LANES, (p + 1) * LANES)
            att, inter = sc[p]
            v2 = jnp.concatenate(
                [va_b[rows, (2 * p) * GLA_DV:(2 * p + 1) * GLA_DV],
                 va_b[rows, (2 * p + 1) * GLA_DV:(2 * p + 2) * GLA_DV]], axis=0)
            o2 = inter + _dot(att, v2)
            for e in range(2):
                hl = slice((2 * p + e) * GLA_DV, (2 * p + e + 1) * GLA_DV)
                oa_scr[rows, hl] = o2[e * C:(e + 1) * C]
                inter_scr[rows, hl] = inter[e * C:(e + 1) * C]
            kl_p = kl[:, lanes]
            kl_stack = jnp.concatenate(
                [jnp.where(even_c, kl_p, 0.0), jnp.where(even_c, 0.0, kl_p)], axis=0).astype(bf16)
            st[p] = st[p] * ebl[:, lanes] + _dot_tn(v2, kl_stack)

    qi = lax.broadcasted_iota(jnp.int32, (W, 2 * W), 0)
    kc = lax.broadcasted_iota(jnp.int32, (W, 2 * W), 1)
    rel = qi + W - kc
    relf = rel.astype(f32)
    in_window = (rel >= 0) & (rel < W)

    def swa_probs(j, kv):
        qrows = slice(j * W, (j + 1) * W)
        band = slice(j * W, j * W + 2 * W)
        if j == 0:
            mask = in_window & ((kc >= W) | (i > 0))
        else:
            mask = in_window
        pairs = (2 * kv, 2 * kv + 1)
        q2 = jnp.concatenate([qb[qrows, p * LANES:(p + 1) * LANES] for p in pairs], axis=0)
        out = []
        for e in range(2):
            s2 = _dot_nt(q2, k_var[kv][e][band])
            probs = []
            for half, p in enumerate(pairs):
                hd = 2 * p + e
                s = s2[half * W:(half + 1) * W]
                s = jnp.where(mask, s - _alibi_slope(hd) * relf, -jnp.inf)
                pr, inv = _softmax_sink(s, sink_ref[hd] * LOG2E)
                probs.append((pr * inv).astype(bf16))
            out.append(jnp.concatenate(probs, axis=0))
        return out

    def swa_out(j, kv, probs):
        qrows = slice(j * W, (j + 1) * W)
        band = slice(j * W, j * W + 2 * W)
        o2 = _dot(probs[0], v_var[kv][0][band]) + _dot(probs[1], v_var[kv][1][band])
        for half, p in enumerate((2 * kv, 2 * kv + 1)):
            ob_scr[qrows, p * LANES:(p + 1) * LANES] = o2[half * W:(half + 1) * W]

    n_chunks = T // C
    assert n_chunks == (T // W) * SWA_KV_HEADS
    gw = 2 * D_MODEL // n_chunks
    for idx in range(n_chunks):
        j, kv = idx // SWA_KV_HEADS, idx % SWA_KV_HEADS
        probs = swa_probs(j, kv)
        sc = gla_scores(idx)
        gate_scr[:, idx * gw:(idx + 1) * gw] = proj(C_GTA + idx * gw, gw)
        gla_update(idx, sc)
        swa_out(j, kv, probs)
    for p in range(GLA_HEADS // 2):
        st_scr[:, p * LANES:(p + 1) * LANES] = st[p]
    ga = proj(C_GA, GLA_V)
    gated_b = jax.nn.sigmoid(gate_scr[:, D_MODEL:2 * D_MODEL]) * _dot(ob_scr[...].astype(bf16), wbb_ref[...])
    sig_a = jax.nn.sigmoid(gate_scr[:, 0:D_MODEL])

    def finish(oa_raw):
        oa = _gla_out_norm(oa_raw, gn_ref, ga)
        merged = sig_a * _dot(oa.astype(bf16), wba_ref[...]) + gated_b
        m = _dot(merged.astype(bf16), wout_ref[...])
        y_ref[...] = x + _rms(m, npost_ref[...])

    finish(oa_scr[...])

    @pl.when(decay_floor < -GLA_SAFE_DECAY)
    def _():
        qs = qa * (GLA_DK ** -0.5)
        va_f = va_b.astype(f32)
        pos = lax.broadcasted_iota(jnp.int32, (T, 1), 0) % C
        er = lax.broadcasted_iota(jnp.int32, (GLA_K, GLA_V), 0)
        ec = lax.broadcasted_iota(jnp.int32, (GLA_K, GLA_V), 1)
        expand = jnp.where((er // GLA_DK) == (ec // GLA_DV), 1.0, 0.0).astype(bf16)

        def offset_term(d, acc):
            valid = pos >= d
            expo = jnp.where(valid, b - pltpu.roll(b, d, 0), 0.0)
            prod = jnp.where(valid, qs * pltpu.roll(ka, d, 0) * jnp.exp(expo), 0.0)
            return acc + _dot(prod.astype(bf16), expand) * pltpu.roll(va_f, d, 0)

        intra = lax.fori_loop(0, C, offset_term, jnp.zeros((T, GLA_V), f32))
        finish(inter_scr[...] + intra)

    @pl.when(i == pl.num_programs(0) - 1)
    def _():
        st_out_ref[...] = st_scr[...].T
        k_out_ref[...] = kcat_scr[T:T + W, :].T
        v_out_ref[...] = vcat_scr[T:T + W, :].T


def _const_spec(shape):
    nd = len(shape)
    return pl.BlockSpec(shape, lambda i: (0,) * nd, pipeline_mode=pl.Buffered(1))


def _mix_prompt(x, sinks, npre, win, wup, bg, gn, wba, wbb, wout, npost, *, T):
    L = x.shape[0]
    nb = L // T
    body = functools.partial(_mix_prompt_body, T=T)
    return pl.pallas_call(
        body,
        grid=(nb,),
        in_specs=[
            pl.BlockSpec(memory_space=pltpu.SMEM),
            pl.BlockSpec((T, D_MODEL), lambda i: (i, 0)),
            _const_spec(npre.shape), _const_spec(win.shape), _const_spec(wup.shape),
            _const_spec(bg.shape), _const_spec(gn.shape), _const_spec(wba.shape),
            _const_spec(wbb.shape), _const_spec(wout.shape), _const_spec(npost.shape),
        ],
        out_specs=[
            pl.BlockSpec((T, D_MODEL), lambda i: (i, 0)),
            pl.BlockSpec((GLA_K, GLA_DV), lambda i: (0, 0)),
            pl.BlockSpec((WINDOW, SWA_KV), lambda i: (0, 0)),
            pl.BlockSpec((WINDOW, SWA_KV), lambda i: (0, 0)),
        ],
        out_shape=[
            jax.ShapeDtypeStruct((L, D_MODEL), f32),
            jax.ShapeDtypeStruct((GLA_K, GLA_DV), f32),
            jax.ShapeDtypeStruct((WINDOW, SWA_KV), f32),
            jax.ShapeDtypeStruct((WINDOW, SWA_KV), f32),
        ],
        scratch_shapes=[
            pltpu.VMEM((GLA_DV, GLA_K), f32),
            pltpu.VMEM((T + WINDOW, SWA_KV), f32),
            pltpu.VMEM((T + WINDOW, SWA_KV), f32),
            pltpu.VMEM((T, GLA_V), f32),
            pltpu.VMEM((T, SWA_Q), f32),
            pltpu.VMEM((T, 2 * D_MODEL), f32),
            pltpu.VMEM((T, GLA_V), f32),
        ],
        compiler_params=pltpu.CompilerParams(
            dimension_semantics=("arbitrary",), vmem_limit_bytes=VMEM_LIMIT),
        name="mix_prompt",
    )(sinks, x, npre, win, wup, bg, gn, wba, wbb, wout, npost)


def _pre_sample_body(x_ref, npre_ref, win_ref, wup_ref, bg_ref,
                     qe_ref, kl_ref, e3_ref, oin_ref, va_ref, ga_ref, qb_ref, kb_ref, vb_ref,
                     gta_ref, gtb_ref, x_scr, *, NB, NT):
    for t in range(NT):
        x_scr[t * NB:(t + 1) * NB, :] = x_ref[:, t, :]
    h = _rms(x_scr[...], npre_ref[...]).astype(bf16)

    def proj(c0, n):
        return _dot(h, win_ref[:, c0:c0 + n])

    def blk(val, t):
        return val[t * NB:(t + 1) * NB, :]

    xg = _dot(proj(C_RA, RA_PAD).astype(bf16), wup_ref[...]) + bg_ref[...]
    qa = proj(C_QA, GLA_K) * (GLA_DK ** -0.5)
    ka = proj(C_KA, GLA_K)
    va = proj(C_VA, GLA_V)
    va_ref[...] = va
    la = jax.nn.log_sigmoid(xg) * (1.0 / GLA_TAU)
    b = [blk(la, 0)]
    for t in range(1, NT):
        b.append(b[-1] + blk(la, t))
    e3_ref[...] = jnp.exp(b[NT - 1])
    for t in range(NT):
        qe_ref[t * NB:(t + 1) * NB, :] = blk(qa, t) * jnp.exp(b[t])
        kl_ref[t * NB:(t + 1) * NB, :] = blk(ka, t) * jnp.exp(b[NT - 1] - b[t])
    pairs = [(t, j) for t in range(NT) for j in range(t + 1)]
    prods = [(blk(qa, t) * blk(ka, j) * jnp.exp(b[t] - b[j])).astype(bf16) for t, j in pairs]
    r = lax.broadcasted_iota(jnp.int32, (GLA_K, GLA_V), 0)
    c = lax.broadcasted_iota(jnp.int32, (GLA_K, GLA_V), 1)
    expand = jnp.where((r // GLA_DK) == (c // GLA_DV), 1.0, 0.0).astype(bf16)
    ga_ref[...] = proj(C_GA, GLA_V)
    qb_ref[...] = proj(C_QB, SWA_Q) * SWA_Q_SCALE
    kb_ref[...] = proj(C_KB, SWA_KV)
    vb_ref[...] = proj(C_VB, SWA_KV)
    att = _dot(jnp.concatenate(prods, axis=0), expand)
    gta_ref[...] = proj(C_GTA, D_MODEL)
    gtb_ref[...] = proj(C_GTB, D_MODEL)
    for t in range(NT):
        acc = None
        for idx, (tt, j) in enumerate(pairs):
            if tt != t:
                continue
            term = att[idx * NB:(idx + 1) * NB, :] * blk(va, j)
            acc = term if acc is None else acc + term
        oin_ref[t * NB:(t + 1) * NB, :] = acc


def _pre_sample(xs, npre, win, wup, bg):
    NB, NT, _ = xs.shape
    body = functools.partial(_pre_sample_body, NB=NB, NT=NT)
    widths = (GLA_K, GLA_K, None, GLA_V, GLA_V, GLA_V, SWA_Q, SWA_KV, SWA_KV, D_MODEL, D_MODEL)
    out_shape = [jax.ShapeDtypeStruct((NB, GLA_K) if w is None else (NT * NB, w), f32) for w in widths]
    return pl.pallas_call(
        body,
        out_shape=out_shape,
        scratch_shapes=[pltpu.VMEM((NB * NT, D_MODEL), f32)],
        compiler_params=pltpu.CompilerParams(vmem_limit_bytes=VMEM_LIMIT),
        name="pre_sample",
    )(xs, npre, win, wup, bg)


def _state_sample_body(sink_ref, qe_ref, kl_ref, e3_ref, oin_ref, va_ref, qb_ref, kb_ref, vb_ref,
                       s0_ref, kt_ref, vt_ref,
                       oa_ref, ob_ref, s1_ref, kt1_ref, vt1_ref, *, BB, NT):
    W = WINDOW
    SK = 2 * W
    HT = GLA_HEADS * NT
    HALF = SWA_HD
    hr = lax.broadcasted_iota(jnp.int32, (HT, GLA_K), 0) // NT
    hc = lax.broadcasted_iota(jnp.int32, (HT, GLA_K), 1) // GLA_DK
    own_head = hr == hc
    ones_rows = jnp.ones((16, GLA_DV), bf16)
    zero_rows = jnp.zeros((16, GLA_DV), bf16)
    zero_ht = jnp.zeros((HT, GLA_DV), bf16)
    G2 = 2 * NT
    row = lax.broadcasted_iota(jnp.int32, (G2, SK), 0)
    col = lax.broadcasted_iota(jnp.int32, (G2, SK), 1)
    rel = (row % NT) + W - col
    relf = rel.astype(f32)
    smask = (rel >= 0) & (rel < W)
    first_pair = lax.broadcasted_iota(jnp.int32, (G2, 1), 0) < NT
    pad_rows = jnp.zeros((8 - NT, SWA_KV), f32)
    pad_lanes = jnp.zeros((SWA_KV, SK - W - 8), f32)
    zero_half = jnp.zeros((HALF, SK), bf16)

    def head_variants(cat_t, kv):
        blk = cat_t[kv * HALF:(kv + 1) * HALF]
        return (jnp.concatenate([blk, zero_half], axis=0), jnp.concatenate([zero_half, blk], axis=0))

    pending = []
    for bi in range(BB):
        s0 = s0_ref[bi]
        q4 = qe_ref[:, bi, :]
        qm = jnp.where(own_head, jnp.concatenate([q4] * GLA_HEADS, axis=0), 0.0).astype(bf16)
        o_inter = _dot(qm, s0.astype(bf16))
        for hd in range(GLA_HEADS):
            lanes = slice(hd * GLA_DV, (hd + 1) * GLA_DV)
            oa_ref[:, bi, lanes] = o_inter[hd * NT:(hd + 1) * NT, :] + oin_ref[:, bi, lanes]
        k4 = kl_ref[:, bi, :]
        km = jnp.where(own_head, jnp.concatenate([k4] * GLA_HEADS, axis=0), 0.0).astype(bf16)
        e = e3_ref[bi:bi + 1, :]
        e_hi = e.astype(bf16)
        r1 = e - e_hi.astype(f32)
        e_mid = r1.astype(bf16)
        e_lo = (r1 - e_mid.astype(f32)).astype(bf16)
        e_rows = jnp.concatenate([e_hi, e_mid, e_lo, jnp.zeros((13, GLA_K), bf16)], axis=0)
        lhs = jnp.concatenate([km, e_rows], axis=0)
        v4 = va_ref[:, bi, :].astype(bf16)
        vrep = jnp.concatenate([v4[:, hd * GLA_DV:(hd + 1) * GLA_DV] for hd in range(GLA_HEADS)], axis=0)
        rhs = jnp.concatenate([jnp.concatenate([vrep, zero_ht], axis=1),
                               jnp.concatenate([zero_rows, ones_rows], axis=1)], axis=0)
        res = _dot_tn(lhs, rhs)
        s1_ref[bi] = res[:, GLA_DV:] * s0 + res[:, :GLA_DV]

        kt = kt_ref[bi]
        vt = vt_ref[bi]
        knew_t = jnp.concatenate([kb_ref[:, bi, :], pad_rows], axis=0).T
        vnew_t = jnp.concatenate([vb_ref[:, bi, :], pad_rows], axis=0).T
        kt1_ref[bi] = jnp.concatenate([kt[:, NT:], knew_t[:, 0:NT]], axis=1)
        vt1_ref[bi] = jnp.concatenate([vt[:, NT:], vnew_t[:, 0:NT]], axis=1)
        kcat = jnp.concatenate([kt, knew_t, pad_lanes], axis=1).astype(bf16)
        vcat = jnp.concatenate([vt, vnew_t, pad_lanes], axis=1).astype(bf16)
        q4b = qb_ref[:, bi, :].astype(bf16)
        for kv in range(SWA_KV_HEADS):
            p0 = 2 * kv
            q8 = jnp.concatenate([q4b[:, p0 * LANES:(p0 + 1) * LANES],
                                  q4b[:, (p0 + 1) * LANES:(p0 + 2) * LANES]], axis=0)
            scores = [_dot(q8, kvar) for kvar in head_variants(kcat, kv)]
            pending.append((bi, kv, scores, head_variants(vcat, kv)))

    for bi, kv, scores, v_vars in pending:
        p0 = 2 * kv
        o8_t = None
        for e_ in range(2):
            h_first = 2 * p0 + e_
            h_second = 2 * (p0 + 1) + e_
            slope = jnp.where(first_pair, _alibi_slope(h_first), _alibi_slope(h_second))
            sink = jnp.where(first_pair, sink_ref[h_first] * LOG2E, sink_ref[h_second] * LOG2E)
            s = jnp.where(smask, scores[e_] - slope * relf, -jnp.inf)
            pr, inv = _softmax_sink(s, sink)
            o_t = _dot_nt(v_vars[e_], (pr * inv).astype(bf16))
            o8_t = o_t if o8_t is None else o8_t + o_t
        o8 = o8_t.T
        ob_ref[:, bi, p0 * LANES:(p0 + 1) * LANES] = o8[0:NT, :]
        ob_ref[:, bi, (p0 + 1) * LANES:(p0 + 2) * LANES] = o8[NT:2 * NT, :]


def _state_sample(sinks, qe, kl, e3, oin, va, qb, kb, vb, s0, kt, vt, *, NT, BB):
    NBS = s0.shape[0]
    assert NBS % BB == 0
    body = functools.partial(_state_sample_body, BB=BB, NT=NT)

    def tm(a):
        return a.reshape(NT, NBS, a.shape[-1])

    def rows(n):
        return pl.BlockSpec((NT, BB, n), lambda i: (0, i, 0))

    def per_seq(shape):
        return pl.BlockSpec((BB,) + shape, lambda i: (i, 0, 0))

    oa, ob, s1, kt1, vt1 = pl.pallas_call(
        body,
        grid=(NBS // BB,),
        in_specs=[
            pl.BlockSpec(memory_space=pltpu.SMEM),
            rows(GLA_K), rows(GLA_K), pl.BlockSpec((BB, GLA_K), lambda i: (i, 0)),
            rows(GLA_V), rows(GLA_V), rows(SWA_Q), rows(SWA_KV), rows(SWA_KV),
            per_seq((GLA_K, GLA_DV)), per_seq((SWA_KV, WINDOW)), per_seq((SWA_KV, WINDOW)),
        ],
        out_specs=[
            rows(GLA_V), rows(SWA_Q),
            per_seq((GLA_K, GLA_DV)), per_seq((SWA_KV, WINDOW)), per_seq((SWA_KV, WINDOW)),
        ],
        out_shape=[
            jax.ShapeDtypeStruct((NT, NBS, GLA_V), f32),
            jax.ShapeDtypeStruct((NT, NBS, SWA_Q), f32),
            jax.ShapeDtypeStruct((NBS, GLA_K, GLA_DV), f32),
            jax.ShapeDtypeStruct((NBS, SWA_KV, WINDOW), f32),
            jax.ShapeDtypeStruct((NBS, SWA_KV, WINDOW), f32),
        ],
        compiler_params=pltpu.CompilerParams(
            dimension_semantics=("arbitrary",), vmem_limit_bytes=VMEM_LIMIT),
        name="state_sample",
    )(sinks, tm(qe), tm(kl), e3, tm(oin), tm(va), tm(qb), tm(kb), tm(vb), s0, kt, vt)
    return oa.reshape(NT * NBS, GLA_V), ob.reshape(NT * NBS, SWA_Q), s1, kt1, vt1


def _post_sample_body(x_ref, oa_ref, ga_ref, ob_ref, gta_ref, gtb_ref, gn_ref,
                      wba_ref, wbb_ref, wout_ref, npost_ref, y_ref, x_scr, *, NB, NT):
    for t in range(NT):
        x_scr[t * NB:(t + 1) * NB, :] = x_ref[:, t, :]
    oa = _gla_out_norm(oa_ref[...], gn_ref, ga_ref[...])
    y_ref[...] = _mix_tail(x_scr[...], oa, ob_ref[...], gta_ref[...], gtb_ref[...],
                           wba_ref, wbb_ref, wout_ref, npost_ref)


def _post_sample(xs, oa, ga, ob, gta, gtb, gn, wba, wbb, wout, npost):
    NB, NT, _ = xs.shape
    return pl.pallas_call(
        functools.partial(_post_sample_body, NB=NB, NT=NT),
        out_shape=jax.ShapeDtypeStruct((NT * NB, D_MODEL), f32),
        scratch_shapes=[pltpu.VMEM((NT * NB, D_MODEL), f32)],
        compiler_params=pltpu.CompilerParams(vmem_limit_bytes=VMEM_LIMIT),
        name="post_sample",
    )(xs, oa, ga, ob, gta, gtb, gn, wba, wbb, wout, npost)


def _ffn_columns(h, wffn_ref, cw_ref, cb_ref, up_scr, y_scr, *, T, base, shift):
    for c0 in range(0, D_FF, FFN_COLS):
        cols = slice(c0, c0 + FFN_COLS)
        u = _dot(h, wffn_ref[:, c0:c0 + FFN_COLS])
        g = _dot(h, wffn_ref[:, D_FF + c0:D_FF + c0 + FFN_COLS])
        up_scr[base:base + T, cols] = u
        u1 = up_scr[base - shift:base - shift + T, cols]
        u2 = up_scr[base - 2 * shift:base - 2 * shift + T, cols]
        cv = (cb_ref[:, cols] + cw_ref[2:3, cols] * u + cw_ref[1:2, cols] * u1 + cw_ref[0:1, cols] * u2)
        y_scr[:, cols] = (_gelu_tanh(cv) * g).astype(bf16)


def _ffn_prompt_body(x_ref, npre_ref, wffn_ref, cw_ref, cb_ref, wo_ref, npost_ref,
                     y_ref, conv_out_ref, up_scr, y_scr, *, T):
    i = pl.program_id(0)
    base = 8

    @pl.when(i == 0)
    def _():
        up_scr[0:base, :] = jnp.zeros((base, D_FF), f32)

    @pl.when(i > 0)
    def _():
        up_scr[0:base, :] = up_scr[T:T + base, :]

    x = x_ref[...]
    h = _rms(x, npre_ref[...]).astype(bf16)
    _ffn_columns(h, wffn_ref, cw_ref, cb_ref, up_scr, y_scr, T=T, base=base, shift=1)
    f = _dot(y_scr[...], wo_ref[...])
    y_ref[...] = x + _rms(f, npost_ref[...])

    @pl.when(i == pl.num_programs(0) - 1)
    def _():
        conv_out_ref[...] = up_scr[T:T + base, :]


def _ffn_prompt(x, npre, wffn, cw, cb, wo, npost, *, T):
    L = x.shape[0]
    body = functools.partial(_ffn_prompt_body, T=T)
    return pl.pallas_call(
        body,
        grid=(L // T,),
        in_specs=[
            pl.BlockSpec((T, D_MODEL), lambda i: (i, 0)),
            _const_spec(npre.shape), _const_spec(wffn.shape), _const_spec(cw.shape),
            _const_spec(cb.shape), _const_spec(wo.shape), _const_spec(npost.shape),
        ],
        out_specs=[
            pl.BlockSpec((T, D_MODEL), lambda i: (i, 0)),
            pl.BlockSpec((8, D_FF), lambda i: (0, 0)),
        ],
        out_shape=[
            jax.ShapeDtypeStruct((L, D_MODEL), f32),
            jax.ShapeDtypeStruct((8, D_FF), f32),
        ],
        scratch_shapes=[
            pltpu.VMEM((T + 8, D_FF), f32),
            pltpu.VMEM((T, D_FF), bf16),
        ],
        compiler_params=pltpu.CompilerParams(
            dimension_semantics=("arbitrary",), vmem_limit_bytes=VMEM_LIMIT),
        name="ffn_prompt",
    )(x, npre, wffn, cw, cb, wo, npost)


def _ffn_sample_body(x_ref, cst_ref, npre_ref, wffn_ref, cw_ref, cb_ref, wo_ref, npost_ref,
                     y_ref, conv_out_ref, up_scr, y_scr, *, NB, NT):
    T = NB * NT
    for t in range(CONV_W - 1):
        up_scr[t * NB:(t + 1) * NB, :] = cst_ref[:, t, :]
    base = (CONV_W - 1) * NB
    x = x_ref[...]
    h = _rms(x, npre_ref[...]).astype(bf16)
    _ffn_columns(h, wffn_ref, cw_ref, cb_ref, up_scr, y_scr, T=T, base=base, shift=NB)
    f = _dot(y_scr[...], wo_ref[...])
    y = x + _rms(f, npost_ref[...])
    for t in range(NT):
        y_ref[:, t, :] = y[t * NB:(t + 1) * NB, :]
    for t in range(CONV_W - 1):
        conv_out_ref[:, t, :] = up_scr[T + t * NB:T + (t + 1) * NB, :]


def _ffn_sample(x, cst, npre, wffn, cw, cb, wo, npost):
    NB = cst.shape[0]
    T = x.shape[0]
    NT = T // NB
    body = functools.partial(_ffn_sample_body, NB=NB, NT=NT)
    return pl.pallas_call(
        body,
        out_shape=[
            jax.ShapeDtypeStruct((NB, NT, D_MODEL), f32),
            jax.ShapeDtypeStruct((NB, CONV_W - 1, D_FF), f32),
        ],
        scratch_shapes=[
            pltpu.VMEM((T + (CONV_W - 1) * NB, D_FF), f32),
            pltpu.VMEM((T, D_FF), bf16),
        ],
        compiler_params=pltpu.CompilerParams(vmem_limit_bytes=VMEM_LIMIT),
        name="ffn_sample",
    )(x, cst, npre, wffn, cw, cb, wo, npost)


def _prep_w_in_body(*refs, n_plain, per_step):
    o_ref = refs[-1]
    j = pl.program_id(0)
    for k, wt_ref in enumerate(refs[:-1]):
        x = wt_ref[...]
        r = lax.broadcasted_iota(jnp.int32, x.shape, 0)
        x = jnp.where((j * per_step + k < n_plain) | (r < GLA_RANK), x, 0.0)
        o_ref[:, k * PREP_ROWS:(k + 1) * PREP_ROWS] = x.T.astype(bf16)


def _prep_w_in(w_in):
    d_in, n_cols = w_in.shape
    head = C_GA
    tail_src = head + GLA_RANK
    n_head = head // PREP_ROWS
    n_tail = (n_cols - tail_src) // PREP_ROWS
    assert head % PREP_ROWS == 0 and (n_cols - tail_src) % PREP_ROWS == 0
    n_plain = n_head + n_tail
    assert C_RA == n_plain * PREP_ROWS
    per_step = 3
    assert (n_plain + 1) % per_step == 0

    def row_off(blk):
        off = jnp.where(blk < n_head, blk * PREP_ROWS,
                        jnp.where(blk < n_plain, tail_src + (blk - n_head) * PREP_ROWS, head))
        return pl.multiple_of(off, 8)

    def in_spec(k):
        return pl.BlockSpec((pl.Element(PREP_ROWS), pl.Element(d_in)), lambda j: (row_off(j * per_step + k), 0))

    wt = jnp.swapaxes(w_in, 0, 1)
    return pl.pallas_call(
        functools.partial(_prep_w_in_body, n_plain=n_plain, per_step=per_step),
        grid=((n_plain + 1) // per_step,),
        in_specs=[in_spec(k) for k in range(per_step)],
        out_specs=pl.BlockSpec((d_in, per_step * PREP_ROWS), lambda j: (0, j)),
        out_shape=jax.ShapeDtypeStruct((d_in, (n_plain + 1) * PREP_ROWS), bf16),
        compiler_params=pltpu.CompilerParams(dimension_semantics=("arbitrary",)),
        name="prep_w_in",
    )(*([wt] * per_step))


def kernel(x_prompt, x_sample, state_gla, cache_swa_k, cache_swa_v, state_ffn_conv, norm_mix_pre, norm_mix_post, w_in, w_gate_up, b_gate, gla_norm, sinks, w_branch_a, w_branch_b, w_out, norm_ffn_pre, norm_ffn_post, w_ffn_in, conv_w, conv_b, w_ffn_out):
    depth = w_in.shape[0]
    assert depth == 1
    l = 0
    B, L, _ = x_prompt.shape
    assert B == 1
    NBS, NT, _ = x_sample.shape
    assert L % MIX_BLOCK == 0 and L % FFN_BLOCK == 0 and NBS % STATE_SEQS == 0
    assert cache_swa_k.shape[2] == WINDOW and NT < 8

    win = _prep_w_in(w_in[l])
    wup = jnp.zeros((RA_PAD, GLA_K), f32).at[:GLA_RANK].set(w_gate_up[l]).astype(bf16)
    bg = b_gate[l].reshape(1, GLA_K)
    gn = gla_norm[l].reshape(1, GLA_DV)
    npre = norm_mix_pre[l].reshape(1, D_MODEL)
    npost = norm_mix_post[l].reshape(1, D_MODEL)
    wba = w_branch_a[l].astype(bf16)
    wbb = w_branch_b[l].astype(bf16)
    wout = w_out[l].astype(bf16)
    fpre = norm_ffn_pre[l].reshape(1, D_MODEL)
    fpost = norm_ffn_post[l].reshape(1, D_MODEL)
    wffn = w_ffn_in[l].astype(bf16)
    cw = conv_w[l]
    cb = conv_b[l].reshape(1, D_FF)
    wo = w_ffn_out[l].astype(bf16)
    sk = sinks[l]

    x1, st_p, k_p, v_p = _mix_prompt(x_prompt[0], sk, npre, win, wup, bg, gn, wba, wbb, wout, npost, T=MIX_BLOCK)
    y_p, conv_p = _ffn_prompt(x1, fpre, wffn, cw, cb, wo, fpost, T=FFN_BLOCK)

    y_prompt = y_p[None]
    gla_state_prompt = st_p.reshape(1, 1, GLA_HEADS, GLA_DK, GLA_DV)
    swa_k_prompt = jnp.transpose(k_p.reshape(SWA_KV_HEADS, SWA_HD, WINDOW), (2, 0, 1))[None, None]
    swa_v_prompt = jnp.transpose(v_p.reshape(SWA_KV_HEADS, SWA_HD, WINDOW), (2, 0, 1))[None, None]
    conv_prompt = conv_p[8 - (CONV_W - 1):].reshape(1, 1, CONV_W - 1, D_FF)

    qe, kl, e3, oin, va, ga, qb, kb, vb, gta, gtb = _pre_sample(x_sample, npre, win, wup, bg)
    kt = jnp.transpose(cache_swa_k[l], (0, 2, 3, 1)).reshape(NBS, SWA_KV, WINDOW)
    vt = jnp.transpose(cache_swa_v[l], (0, 2, 3, 1)).reshape(NBS, SWA_KV, WINDOW)
    oa_raw, ob, s1, kt1, vt1 = _state_sample(
        sk, qe, kl, e3, oin, va, qb, kb, vb, state_gla[l].reshape(NBS, GLA_K, GLA_DV), kt, vt, NT=NT, BB=STATE_SEQS)
    x1s = _post_sample(x_sample, oa_raw, ga, ob, gta, gtb, gn, wba, wbb, wout, npost)
    y_sample, conv_s = _ffn_sample(x1s, state_ffn_conv[l], fpre, wffn, cw, cb, wo, fpost)

    def cache_out(t):
        return jnp.transpose(t.reshape(NBS, SWA_KV_HEADS, SWA_HD, WINDOW), (0, 3, 1, 2))[None]

    gla_state_sample = s1.reshape(1, NBS, GLA_HEADS, GLA_DK, GLA_DV)
    swa_k_sample = cache_out(kt1)
    swa_v_sample = cache_out(vt1)
    conv_sample = conv_s[None]
    return (y_prompt, y_sample, gla_state_prompt, gla_state_sample, swa_k_prompt, swa_v_prompt,
            swa_k_sample, swa_v_sample, conv_prompt, conv_sample)
```

```python
import functools

import jax
import jax.numpy as jnp
from jax import lax
from jax.experimental import pallas as pl
from jax.experimental.pallas import tpu as pltpu

f32 = jnp.float32
bf16 = jnp.bfloat16

D_MODEL = 1024
GLA_HEADS = 4
GLA_DK = 64
GLA_DV = 128
GLA_RANK = 16
GLA_TAU = 16.0
GLA_CHUNK = 64
GLA_SAFE_DECAY = 60.0
SWA_HEADS = 8
SWA_KV_HEADS = 2
SWA_HD = 64
WINDOW = 128
D_FF = 2816
CONV_W = 3
EPS = 1e-6
GLA_K = GLA_HEADS * GLA_DK
GLA_V = GLA_HEADS * GLA_DV
SWA_Q = SWA_HEADS * SWA_HD
SWA_KV = SWA_KV_HEADS * SWA_HD
LANES = 128
LOG2E = 1.4426950408889634

C_QA = 0
C_KA = C_QA + GLA_K
C_VA = C_KA + GLA_K
C_GA = C_VA + GLA_V
C_QB = C_GA + GLA_V
C_KB = C_QB + SWA_Q
C_VB = C_KB + SWA_KV
C_GTA = C_VB + SWA_KV
C_GTB = C_GTA + D_MODEL
C_RA = C_GTB + D_MODEL
RA_PAD = LANES
PREP_ROWS = 256
IN_COLS_PAD = C_RA + PREP_ROWS

MIX_BLOCK = 256
FFN_BLOCK = 512
FFN_COLS = 256
STATE_SEQS = 16
VMEM_LIMIT = 56 * 1024 * 1024


def _dot(a, b):
    return jnp.dot(a, b, preferred_element_type=f32)


def _dot_nt(a, b):
    return lax.dot_general(a, b, (((1,), (1,)), ((), ())), preferred_element_type=f32)


def _dot_tn(a, b):
    return lax.dot_general(a, b, (((0,), (0,)), ((), ())), preferred_element_type=f32)


def _rms(x, w):
    return x * lax.rsqrt(jnp.mean(x * x, axis=-1, keepdims=True) + EPS) * w


def _gelu_tanh(x):
    k = -2.0 * 0.7978845608028654 * LOG2E
    return x / (1.0 + jnp.exp2(x * (k + (k * 0.044715) * (x * x))))


def _split_hi_lo(x):
    hi = x.astype(bf16)
    lo = (x - hi.astype(f32)).astype(bf16)
    return hi, lo


def _chunk_cumsum(la, chunk):
    n = la.shape[0]
    r = lax.broadcasted_iota(jnp.int32, (n, n), 0)
    c = lax.broadcasted_iota(jnp.int32, (n, n), 1)
    tri = jnp.where((c <= r) & ((r // chunk) == (c // chunk)), 1.0, 0.0).astype(bf16)
    hi, lo = _split_hi_lo(la)
    return _dot(tri, hi) + _dot(tri, lo)


def _even_head_lanes(shape):
    lane = lax.broadcasted_iota(jnp.int32, shape, len(shape) - 1)
    return (lane % LANES) < GLA_DK


def _gla_out_norm(o, gn_ref, ga):
    outs = []
    for h in range(GLA_HEADS):
        oh = o[:, h * GLA_DV:(h + 1) * GLA_DV]
        outs.append(_rms(oh, gn_ref[...]))
    on = jnp.concatenate(outs, axis=1)
    return on * (ga * jax.nn.sigmoid(ga))


def _mix_tail(x, oa, ob, gate_a, gate_b, wba_ref, wbb_ref, wout_ref, npost_ref):
    merged = (jax.nn.sigmoid(gate_a) * _dot(oa.astype(bf16), wba_ref[...])
              + jax.nn.sigmoid(gate_b) * _dot(ob.astype(bf16), wbb_ref[...]))
    m = _dot(merged.astype(bf16), wout_ref[...])
    return x + _rms(m, npost_ref[...])


def _alibi_slope(head):
    return LOG2E * 2.0 ** (-(8.0 / SWA_HEADS) * (head + 1))


SWA_Q_SCALE = LOG2E * SWA_HD ** -0.5


def _kv_variants(x):
    lo = _even_head_lanes(x.shape)
    xr = pltpu.roll(x, SWA_HD, 1)
    zero = jnp.zeros_like(x)
    h0_lo = jnp.where(lo, x, zero).astype(bf16)
    h1_hi = jnp.where(lo, zero, x).astype(bf16)
    h1_lo = jnp.where(lo, xr, zero).astype(bf16)
    h0_hi = jnp.where(lo, zero, xr).astype(bf16)
    return (h0_lo, h0_hi), (h1_lo, h1_hi)


def _softmax_sink(s, sink):
    m = jnp.maximum(jnp.max(s, axis=-1, keepdims=True), sink)
    p = jnp.exp2(s - m)
    denom = jnp.sum(p, axis=-1, keepdims=True) + jnp.exp2(sink - m)
    return p, 1.0 / denom


def _mix_prompt_body(sink_ref, x_ref, npre_ref, win_ref, wup_ref, bg_ref, gn_ref,
                     wba_ref, wbb_ref, wout_ref, npost_ref,
                     y_ref, st_out_ref, k_out_ref, v_out_ref,
                     st_scr, kcat_scr, vcat_scr, oa_scr, ob_scr, gate_scr, inter_scr, *, T):
    i = pl.program_id(0)
    W = WINDOW
    C = GLA_CHUNK

    @pl.when(i == 0)
    def _():
        st_scr[...] = jnp.zeros_like(st_scr)
        kcat_scr[0:W, :] = jnp.zeros((W, SWA_KV), f32)
        vcat_scr[0:W, :] = jnp.zeros((W, SWA_KV), f32)

    @pl.when(i > 0)
    def _():
        kcat_scr[0:W, :] = kcat_scr[T:T + W, :]
        vcat_scr[0:W, :] = vcat_scr[T:T + W, :]

    x = x_ref[...]
    rms_f = lax.rsqrt(jnp.mean(x * x, axis=-1, keepdims=True) + EPS)
    h = (x * npre_ref[...]).astype(bf16)
    rms_b = {n: jnp.broadcast_to(rms_f, (T, n)) for n in (LANES, 2 * LANES)}

    def proj(c0, n):
        w = 2 * LANES if n % (2 * LANES) == 0 else LANES
        return jnp.concatenate([_dot(h, win_ref[:, c:c + w]) * rms_b[w] for c in range(c0, c0 + n, w)], axis=1)


    xg = _dot(proj(C_RA, RA_PAD).astype(bf16), wup_ref[...]) + bg_ref[...]
    kcat_scr[W:W + T, :] = proj(C_KB, SWA_KV)
    vcat_scr[W:W + T, :] = proj(C_VB, SWA_KV)
    qb = (proj(C_QB, SWA_Q) * SWA_Q_SCALE).astype(bf16)
    la = jax.nn.log_sigmoid(xg) * (1.0 / GLA_TAU)
    b = _chunk_cumsum(la, C)
    decay_floor = jnp.min(b)
    qa = proj(C_QA, GLA_K)
    ka = proj(C_KA, GLA_K)
    va_b = proj(C_VA, GLA_V).astype(bf16)

    qe = qa * jnp.exp(b) * (GLA_DK ** -0.5)
    ke = (ka * jnp.exp(-b)).astype(bf16)
    even = _even_head_lanes((T, GLA_K))
    qe_even = jnp.where(even, qe, 0.0).astype(bf16)
    qe_odd = jnp.where(even, 0.0, qe).astype(bf16)
    k_var = _kv_variants(kcat_scr[...])
    v_var = _kv_variants(vcat_scr[...])

    r2 = lax.broadcasted_iota(jnp.int32, (2 * C, 2 * C), 0)
    c2 = lax.broadcasted_iota(jnp.int32, (2 * C, 2 * C), 1)
    pair_causal = ((r2 // C) == (c2 // C)) & ((c2 % C) <= (r2 % C))
    even_c = _even_head_lanes((C, LANES))
    st = [st_scr[:, p * LANES:(p + 1) * LANES] for p in range(GLA_HEADS // 2)]

    def gla_scores(c):
        rows = slice(c * C, (c + 1) * C)
        out = []
        for p in range(GLA_HEADS // 2):
            lanes = slice(p * LANES, (p + 1) * LANES)
            q2 = jnp.concatenate([qe_even[rows, lanes], qe_odd[rows, lanes]], axis=0)
            ke_p = ke[rows, lanes]
            rhs = jnp.concatenate([ke_p, ke_p, st[p].astype(bf16)], axis=0)
            r = _dot_nt(q2, rhs)
            att = jnp.where(pair_causal, r[:, 0:2 * C], 0.0).astype(bf16)
            out.append((att, r[:, 2 * C:]))
        return out

    def gla_update(c, sc):
        rows = slice(c * C, (c + 1) * C)
        b_c = b[rows]
        bl = b_c[C - 1:C, :]
        kl = ka[rows] * jnp.exp(bl - b_c)
        ebl = jnp.exp(bl)
        for p in range(GLA_HEADS // 2):
            lanes = slice(p * LANES, (p + 1) * LANES)
            att, inter = sc[p]
            v2 = jnp.concatenate(
                [va_b[rows, (2 * p) * GLA_DV:(2 * p + 1) * GLA_DV],
                 va_b[rows, (2 * p + 1) * GLA_DV:(2 * p + 2) * GLA_DV]], axis=0)
            o2 = inter + _dot(att, v2)
            for e in range(2):
                hl = slice((2 * p + e) * GLA_DV, (2 * p + e + 1) * GLA_DV)
                oa_scr[rows, hl] = o2[e * C:(e + 1) * C]
                inter_scr[rows, hl] = inter[e * C:(e + 1) * C]
            kl_p = kl[:, lanes]
            kl_stack = jnp.concatenate(
                [jnp.where(even_c, kl_p, 0.0), jnp.where(even_c, 0.0, kl_p)], axis=0).astype(bf16)
            st[p] = st[p] * ebl[:, lanes] + _dot_tn(v2, kl_stack)

    qi = lax.broadcasted_iota(jnp.int32, (W, 2 * W), 0)
    kc = lax.broadcasted_iota(jnp.int32, (W, 2 * W), 1)
    rel = qi + W - kc
    relf = rel.astype(f32)
    in_window = (rel >= 0) & (rel < W)

    def swa_probs(j, kv):
        qrows = slice(j * W, (j + 1) * W)
        band = slice(j * W, j * W + 2 * W)
        if j == 0:
            mask = in_window & ((kc >= W) | (i > 0))
        else:
            mask = in_window
        pairs = (2 * kv, 2 * kv + 1)
        q2 = jnp.concatenate([qb[qrows, p * LANES:(p + 1) * LANES] for p in pairs], axis=0)
        out = []
        for e in range(2):
            s2 = _dot_nt(q2, k_var[kv][e][band])
            probs = []
            for half, p in enumerate(pairs):
                hd = 2 * p + e
                s = s2[half * W:(half + 1) * W]
                s = jnp.where(mask, s - _alibi_slope(hd) * relf, -jnp.inf)
                pr, inv = _softmax_sink(s, sink_ref[hd] * LOG2E)
                probs.append((pr * inv).astype(bf16))
            out.append(jnp.concatenate(probs, axis=0))
        return out

    def swa_out(j, kv, probs):
        qrows = slice(j * W, (j + 1) * W)
        band = slice(j * W, j * W + 2 * W)
        o2 = _dot(probs[0], v_var[kv][0][band]) + _dot(probs[1], v_var[kv][1][band])
        for half, p in enumerate((2 * kv, 2 * kv + 1)):
            ob_scr[qrows, p * LANES:(p + 1) * LANES] = o2[half * W:(half + 1) * W]

    n_chunks = T // C
    assert n_chunks == (T // W) * SWA_KV_HEADS
    gw = 2 * D_MODEL // n_chunks
    for idx in range(n_chunks):
        j, kv = idx // SWA_KV_HEADS, idx % SWA_KV_HEADS
        probs = swa_probs(j, kv)
        sc = gla_scores(idx)
        gate_scr[:, idx * gw:(idx + 1) * gw] = proj(C_GTA + idx * gw, gw)
        gla_update(idx, sc)
        swa_out(j, kv, probs)
    for p in range(GLA_HEADS // 2):
        st_scr[:, p * LANES:(p + 1) * LANES] = st[p]
    ga = proj(C_GA, GLA_V)
    gated_b = jax.nn.sigmoid(gate_scr[:, D_MODEL:2 * D_MODEL]) * _dot(ob_scr[...].astype(bf16), wbb_ref[...])
    sig_a = jax.nn.sigmoid(gate_scr[:, 0:D_MODEL])

    def finish(oa_raw):
        oa = _gla_out_norm(oa_raw, gn_ref, ga)
        merged = sig_a * _dot(oa.astype(bf16), wba_ref[...]) + gated_b
        m = _dot(merged.astype(bf16), wout_ref[...])
        y_ref[...] = x + _rms(m, npost_ref[...])

    finish(oa_scr[...])

    @pl.when(decay_floor < -GLA_SAFE_DECAY)
    def _():
        qs = qa * (GLA_DK ** -0.5)
        va_f = va_b.astype(f32)
        pos = lax.broadcasted_iota(jnp.int32, (T, 1), 0) % C
        er = lax.broadcasted_iota(jnp.int32, (GLA_K, GLA_V), 0)
        ec = lax.broadcasted_iota(jnp.int32, (GLA_K, GLA_V), 1)
        expand = jnp.where((er // GLA_DK) == (ec // GLA_DV), 1.0, 0.0).astype(bf16)

        def offset_term(d, acc):
            valid = pos >= d
            expo = jnp.where(valid, b - pltpu.roll(b, d, 0), 0.0)
            prod = jnp.where(valid, qs * pltpu.roll(ka, d, 0) * jnp.exp(expo), 0.0)
            return acc + _dot(prod.astype(bf16), expand) * pltpu.roll(va_f, d, 0)

        intra = lax.fori_loop(0, C, offset_term, jnp.zeros((T, GLA_V), f32))
        finish(inter_scr[...] + intra)

    @pl.when(i == pl.num_programs(0) - 1)
    def _():
        st_out_ref[...] = st_scr[...].T
        k_out_ref[...] = kcat_scr[T:T + W, :].T
        v_out_ref[...] = vcat_scr[T:T + W, :].T


def _const_spec(shape):
    nd = len(shape)
    return pl.BlockSpec(shape, lambda i: (0,) * nd, pipeline_mode=pl.Buffered(1))


def _mix_prompt(x, sinks, npre, win, wup, bg, gn, wba, wbb, wout, npost, *, T):
    L = x.shape[0]
    nb = L // T
    body = functools.partial(_mix_prompt_body, T=T)
    return pl.pallas_call(
        body,
        grid=(nb,),
        in_specs=[
            pl.BlockSpec(memory_space=pltpu.SMEM),
            pl.BlockSpec((T, D_MODEL), lambda i: (i, 0)),
            _const_spec(npre.shape), _const_spec(win.shape), _const_spec(wup.shape),
            _const_spec(bg.shape), _const_spec(gn.shape), _const_spec(wba.shape),
            _const_spec(wbb.shape), _const_spec(wout.shape), _const_spec(npost.shape),
        ],
        out_specs=[
            pl.BlockSpec((T, D_MODEL), lambda i: (i, 0)),
            pl.BlockSpec((GLA_K, GLA_DV), lambda i: (0, 0)),
            pl.BlockSpec((WINDOW, SWA_KV), lambda i: (0, 0)),
            pl.BlockSpec((WINDOW, SWA_KV), lambda i: (0, 0)),
        ],
        out_shape=[
            jax.ShapeDtypeStruct((L, D_MODEL), f32),
            jax.ShapeDtypeStruct((GLA_K, GLA_DV), f32),
            jax.ShapeDtypeStruct((WINDOW, SWA_KV), f32),
            jax.ShapeDtypeStruct((WINDOW, SWA_KV), f32),
        ],
        scratch_shapes=[
            pltpu.VMEM((GLA_DV, GLA_K), f32),
            pltpu.VMEM((T + WINDOW, SWA_KV), f32),
            pltpu.VMEM((T + WINDOW, SWA_KV), f32),
            pltpu.VMEM((T, GLA_V), f32),
            pltpu.VMEM((T, SWA_Q), f32),
            pltpu.VMEM((T, 2 * D_MODEL), f32),
            pltpu.VMEM((T, GLA_V), f32),
        ],
        compiler_params=pltpu.CompilerParams(
            dimension_semantics=("arbitrary",), vmem_limit_bytes=VMEM_LIMIT),
        name="mix_prompt",
    )(sinks, x, npre, win, wup, bg, gn, wba, wbb, wout, npost)


def _pre_sample_body(x_ref, npre_ref, win_ref, wup_ref, bg_ref,
                     qe_ref, kl_ref, e3_ref, oin_ref, va_ref, ga_ref, qb_ref, kb_ref, vb_ref,
                     gta_ref, gtb_ref, x_scr, *, NB, NT):
    for t in range(NT):
        x_scr[t * NB:(t + 1) * NB, :] = x_ref[:, t, :]
    h = _rms(x_scr[...], npre_ref[...]).astype(bf16)

    def proj(c0, n):
        return _dot(h, win_ref[:, c0:c0 + n])

    def blk(val, t):
        return val[t * NB:(t + 1) * NB, :]

    xg = _dot(proj(C_RA, RA_PAD).astype(bf16), wup_ref[...]) + bg_ref[...]
    qa = proj(C_QA, GLA_K) * (GLA_DK ** -0.5)
    ka = proj(C_KA, GLA_K)
    va = proj(C_VA, GLA_V)
    va_ref[...] = va
    la = jax.nn.log_sigmoid(xg) * (1.0 / GLA_TAU)
    b = [blk(la, 0)]
    for t in range(1, NT):
        b.append(b[-1] + blk(la, t))
    e3_ref[...] = jnp.exp(b[NT - 1])
    for t in range(NT):
        qe_ref[t * NB:(t + 1) * NB, :] = blk(qa, t) * jnp.exp(b[t])
        kl_ref[t * NB:(t + 1) * NB, :] = blk(ka, t) * jnp.exp(b[NT - 1] - b[t])
    pairs = [(t, j) for t in range(NT) for j in range(t + 1)]
    prods = [(blk(qa, t) * blk(ka, j) * jnp.exp(b[t] - b[j])).astype(bf16) for t, j in pairs]
    r = lax.broadcasted_iota(jnp.int32, (GLA_K, GLA_V), 0)
    c = lax.broadcasted_iota(jnp.int32, (GLA_K, GLA_V), 1)
    expand = jnp.where((r // GLA_DK) == (c // GLA_DV), 1.0, 0.0).astype(bf16)
    ga_ref[...] = proj(C_GA, GLA_V)
    qb_ref[...] = proj(C_QB, SWA_Q) * SWA_Q_SCALE
    kb_ref[...] = proj(C_KB, SWA_KV)
    vb_ref[...] = proj(C_VB, SWA_KV)
    att = _dot(jnp.concatenate(prods, axis=0), expand)
    gta_ref[...] = proj(C_GTA, D_MODEL)
    gtb_ref[...] = proj(C_GTB, D_MODEL)
    for t in range(NT):
        acc = None
        for idx, (tt, j) in enumerate(pairs):
            if tt != t:
                continue
            term = att[idx * NB:(idx + 1) * NB, :] * blk(va, j)
            acc = term if acc is None else acc + term
        oin_ref[t * NB:(t + 1) * NB, :] = acc


def _pre_sample(xs, npre, win, wup, bg):
    NB, NT, _ = xs.shape
    body = functools.partial(_pre_sample_body, NB=NB, NT=NT)
    widths = (GLA_K, GLA_K, None, GLA_V, GLA_V, GLA_V, SWA_Q, SWA_KV, SWA_KV, D_MODEL, D_MODEL)
    out_shape = [jax.ShapeDtypeStruct((NB, GLA_K) if w is None else (NT * NB, w), f32) for w in widths]
    return pl.pallas_call(
        body,
        out_shape=out_shape,
        scratch_shapes=[pltpu.VMEM((NB * NT, D_MODEL), f32)],
        compiler_params=pltpu.CompilerParams(vmem_limit_bytes=VMEM_LIMIT),
        name="pre_sample",
    )(xs, npre, win, wup, bg)


def _state_sample_body(sink_ref, qe_ref, kl_ref, e3_ref, oin_ref, va_ref, qb_ref, kb_ref, vb_ref,
                       s0_ref, kt_ref, vt_ref,
                       oa_ref, ob_ref, s1_ref, kt1_ref, vt1_ref, *, BB, NT):
    W = WINDOW
    SK = 2 * W
    HT = GLA_HEADS * NT
    HALF = SWA_HD
    hr = lax.broadcasted_iota(jnp.int32, (HT, GLA_K), 0) // NT
    hc = lax.broadcasted_iota(jnp.int32, (HT, GLA_K), 1) // GLA_DK
    own_head = hr == hc
    ones_rows = jnp.ones((16, GLA_DV), bf16)
    zero_rows = jnp.zeros((16, GLA_DV), bf16)
    zero_ht = jnp.zeros((HT, GLA_DV), bf16)
    G2 = 2 * NT
    row = lax.broadcasted_iota(jnp.int32, (G2, SK), 0)
    col = lax.broadcasted_iota(jnp.int32, (G2, SK), 1)
    rel = (row % NT) + W - col
    relf = rel.astype(f32)
    smask = (rel >= 0) & (rel < W)
    first_pair = lax.broadcasted_iota(jnp.int32, (G2, 1), 0) < NT
    pad_rows = jnp.zeros((8 - NT, SWA_KV), f32)
    pad_lanes = jnp.zeros((SWA_KV, SK - W - 8), f32)
    zero_half = jnp.zeros((HALF, SK), bf16)

    def head_variants(cat_t, kv):
        blk = cat_t[kv * HALF:(kv + 1) * HALF]
        return (jnp.concatenate([blk, zero_half], axis=0), jnp.concatenate([zero_half, blk], axis=0))

    pending = []
    for bi in range(BB):
        s0 = s0_ref[bi]
        q4 = qe_ref[:, bi, :]
        qm = jnp.where(own_head, jnp.concatenate([q4] * GLA_HEADS, axis=0), 0.0).astype(bf16)
        o_inter = _dot(qm, s0.astype(bf16))
        for hd in range(GLA_HEADS):
            lanes = slice(hd * GLA_DV, (hd + 1) * GLA_DV)
            oa_ref[:, bi, lanes] = o_inter[hd * NT:(hd + 1) * NT, :] + oin_ref[:, bi, lanes]
        k4 = kl_ref[:, bi, :]
        km = jnp.where(own_head, jnp.concatenate([k4] * GLA_HEADS, axis=0), 0.0).astype(bf16)
        e = e3_ref[bi:bi + 1, :]
        e_hi = e.astype(bf16)
        r1 = e - e_hi.astype(f32)
        e_mid = r1.astype(bf16)
        e_lo = (r1 - e_mid.astype(f32)).astype(bf16)
        e_rows = jnp.concatenate([e_hi, e_mid, e_lo, jnp.zeros((13, GLA_K), bf16)], axis=0)
        lhs = jnp.concatenate([km, e_rows], axis=0)
        v4 = va_ref[:, bi, :].astype(bf16)
        vrep = jnp.concatenate([v4[:, hd * GLA_DV:(hd + 1) * GLA_DV] for hd in range(GLA_HEADS)], axis=0)
        rhs = jnp.concatenate([jnp.concatenate([vrep, zero_ht], axis=1),
                               jnp.concatenate([zero_rows, ones_rows], axis=1)], axis=0)
        res = _dot_tn(lhs, rhs)
        s1_ref[bi] = res[:, GLA_DV:] * s0 + res[:, :GLA_DV]

        kt = kt_ref[bi]
        vt = vt_ref[bi]
        knew_t = jnp.concatenate([kb_ref[:, bi, :], pad_rows], axis=0).T
        vnew_t = jnp.concatenate([vb_ref[:, bi, :], pad_rows], axis=0).T
        kt1_ref[bi] = jnp.concatenate([kt[:, NT:], knew_t[:, 0:NT]], axis=1)
        vt1_ref[bi] = jnp.concatenate([vt[:, NT:], vnew_t[:, 0:NT]], axis=1)
        kcat = jnp.concatenate([kt, knew_t, pad_lanes], axis=1).astype(bf16)
        vcat = jnp.concatenate([vt, vnew_t, pad_lanes], axis=1).astype(bf16)
        q4b = qb_ref[:, bi, :].astype(bf16)
        for kv in range(SWA_KV_HEADS):
            p0 = 2 * kv
            q8 = jnp.concatenate([q4b[:, p0 * LANES:(p0 + 1) * LANES],
                                  q4b[:, (p0 + 1) * LANES:(p0 + 2) * LANES]], axis=0)
            scores = [_dot(q8, kvar) for kvar in head_variants(kcat, kv)]
            pending.append((bi, kv, scores, head_variants(vcat, kv)))

    for bi, kv, scores, v_vars in pending:
        p0 = 2 * kv
        o8_t = None
        for e_ in range(2):
            h_first = 2 * p0 + e_
            h_second = 2 * (p0 + 1) + e_
            slope = jnp.where(first_pair, _alibi_slope(h_first), _alibi_slope(h_second))
            sink = jnp.where(first_pair, sink_ref[h_first] * LOG2E, sink_ref[h_second] * LOG2E)
            s = jnp.where(smask, scores[e_] - slope * relf, -jnp.inf)
            pr, inv = _softmax_sink(s, sink)
            o_t = _dot_nt(v_vars[e_], (pr * inv).astype(bf16))
            o8_t = o_t if o8_t is None else o8_t + o_t
        o8 = o8_t.T
        ob_ref[:, bi, p0 * LANES:(p0 + 1) * LANES] = o8[0:NT, :]
        ob_ref[:, bi, (p0 + 1) * LANES:(p0 + 2) * LANES] = o8[NT:2 * NT, :]


def _state_sample(sinks, qe, kl, e3, oin, va, qb, kb, vb, s0, kt, vt, *, NT, BB):
    NBS = s0.shape[0]
    assert NBS % BB == 0
    body = functools.partial(_state_sample_body, BB=BB, NT=NT)

    def tm(a):
        return a.reshape(NT, NBS, a.shape[-1])

    def rows(n):
        return pl.BlockSpec((NT, BB, n), lambda i: (0, i, 0))

    def per_seq(shape):
        return pl.BlockSpec((BB,) + shape, lambda i: (i, 0, 0))

    oa, ob, s1, kt1, vt1 = pl.pallas_call(
        body,
        grid=(NBS // BB,),
        in_specs=[
            pl.BlockSpec(memory_space=pltpu.SMEM),
            rows(GLA_K), rows(GLA_K), pl.BlockSpec((BB, GLA_K), lambda i: (i, 0)),
            rows(GLA_V), rows(GLA_V), rows(SWA_Q), rows(SWA_KV), rows(SWA_KV),
            per_seq((GLA_K, GLA_DV)), per_seq((SWA_KV, WINDOW)), per_seq((SWA_KV, WINDOW)),
        ],
        out_specs=[
            rows(GLA_V), rows(SWA_Q),
            per_seq((GLA_K, GLA_DV)), per_seq((SWA_KV, WINDOW)), per_seq((SWA_KV, WINDOW)),
        ],
        out_shape=[
            jax.ShapeDtypeStruct((NT, NBS, GLA_V), f32),
            jax.ShapeDtypeStruct((NT, NBS, SWA_Q), f32),
            jax.ShapeDtypeStruct((NBS, GLA_K, GLA_DV), f32),
            jax.ShapeDtypeStruct((NBS, SWA_KV, WINDOW), f32),
            jax.ShapeDtypeStruct((NBS, SWA_KV, WINDOW), f32),
        ],
        compiler_params=pltpu.CompilerParams(
            dimension_semantics=("arbitrary",), vmem_limit_bytes=VMEM_LIMIT),
        name="state_sample",
    )(sinks, tm(qe), tm(kl), e3, tm(oin), tm(va), tm(qb), tm(kb), tm(vb), s0, kt, vt)
    return oa.reshape(NT * NBS, GLA_V), ob.reshape(NT * NBS, SWA_Q), s1, kt1, vt1


def _post_sample_body(x_ref, oa_ref, ga_ref, ob_ref, gta_ref, gtb_ref, gn_ref,
                      wba_ref, wbb_ref, wout_ref, npost_ref, y_ref, x_scr, *, NB, NT):
    for t in range(NT):
        x_scr[t * NB:(t + 1) * NB, :] = x_ref[:, t, :]
    oa = _gla_out_norm(oa_ref[...], gn_ref, ga_ref[...])
    y_ref[...] = _mix_tail(x_scr[...], oa, ob_ref[...], gta_ref[...], gtb_ref[...],
                           wba_ref, wbb_ref, wout_ref, npost_ref)


def _post_sample(xs, oa, ga, ob, gta, gtb, gn, wba, wbb, wout, npost):
    NB, NT, _ = xs.shape
    return pl.pallas_call(
        functools.partial(_post_sample_body, NB=NB, NT=NT),
        out_shape=jax.ShapeDtypeStruct((NT * NB, D_MODEL), f32),
        scratch_shapes=[pltpu.VMEM((NT * NB, D_MODEL), f32)],
        compiler_params=pltpu.CompilerParams(vmem_limit_bytes=VMEM_LIMIT),
        name="post_sample",
    )(xs, oa, ga, ob, gta, gtb, gn, wba, wbb, wout, npost)


def _ffn_columns(h, wffn_ref, cw_ref, cb_ref, up_scr, y_scr, *, T, base, shift):
    for c0 in range(0, D_FF, FFN_COLS):
        cols = slice(c0, c0 + FFN_COLS)
        u = _dot(h, wffn_ref[:, c0:c0 + FFN_COLS])
        g = _dot(h, wffn_ref[:, D_FF + c0:D_FF + c0 + FFN_COLS])
        up_scr[base:base + T, cols] = u
        u1 = up_scr[base - shift:base - shift + T, cols]
        u2 = up_scr[base - 2 * shift:base - 2 * shift + T, cols]
        cv = (cb_ref[:, cols] + cw_ref[2:3, cols] * u + cw_ref[1:2, cols] * u1 + cw_ref[0:1, cols] * u2)
        y_scr[:, cols] = (_gelu_tanh(cv) * g).astype(bf16)


def _ffn_prompt_body(x_ref, npre_ref, wffn_ref, cw_ref, cb_ref, wo_ref, npost_ref,
                     y_ref, conv_out_ref, up_scr, y_scr, *, T):
    i = pl.program_id(0)
    base = 8

    @pl.when(i == 0)
    def _():
        up_scr[0:base, :] = jnp.zeros((base, D_FF), f32)

    @pl.when(i > 0)
    def _():
        up_scr[0:base, :] = up_scr[T:T + base, :]

    x = x_ref[...]
    h = _rms(x, npre_ref[...]).astype(bf16)
    _ffn_columns(h, wffn_ref, cw_ref, cb_ref, up_scr, y_scr, T=T, base=base, shift=1)
    f = _dot(y_scr[...], wo_ref[...])
    y_ref[...] = x + _rms(f, npost_ref[...])

    @pl.when(i == pl.num_programs(0) - 1)
    def _():
        conv_out_ref[...] = up_scr[T:T + base, :]


def _ffn_prompt(x, npre, wffn, cw, cb, wo, npost, *, T):
    L = x.shape[0]
    body = functools.partial(_ffn_prompt_body, T=T)
    return pl.pallas_call(
        body,
        grid=(L // T,),
        in_specs=[
            pl.BlockSpec((T, D_MODEL), lambda i: (i, 0)),
            _const_spec(npre.shape), _const_spec(wffn.shape), _const_spec(cw.shape),
            _const_spec(cb.shape), _const_spec(wo.shape), _const_spec(npost.shape),
        ],
        out_specs=[
            pl.BlockSpec((T, D_MODEL), lambda i: (i, 0)),
            pl.BlockSpec((8, D_FF), lambda i: (0, 0)),
        ],
        out_shape=[
            jax.ShapeDtypeStruct((L, D_MODEL), f32),
            jax.ShapeDtypeStruct((8, D_FF), f32),
        ],
        scratch_shapes=[
            pltpu.VMEM((T + 8, D_FF), f32),
            pltpu.VMEM((T, D_FF), bf16),
        ],
        compiler_params=pltpu.CompilerParams(
            dimension_semantics=("arbitrary",), vmem_limit_bytes=VMEM_LIMIT),
        name="ffn_prompt",
    )(x, npre, wffn, cw, cb, wo, npost)


def _ffn_sample_body(x_ref, cst_ref, npre_ref, wffn_ref, cw_ref, cb_ref, wo_ref, npost_ref,
                     y_ref, conv_out_ref, up_scr, y_scr, *, NB, NT):
    T = NB * NT
    for t in range(CONV_W - 1):
        up_scr[t * NB:(t + 1) * NB, :] = cst_ref[:, t, :]
    base = (CONV_W - 1) * NB
    x = x_ref[...]
    h = _rms(x, npre_ref[...]).astype(bf16)
    _ffn_columns(h, wffn_ref, cw_ref, cb_ref, up_scr, y_scr, T=T, base=base, shift=NB)
    f = _dot(y_scr[...], wo_ref[...])
    y = x + _rms(f, npost_ref[...])
    for t in range(NT):
        y_ref[:, t, :] = y[t * NB:(t + 1) * NB, :]
    for t in range(CONV_W - 1):
        conv_out_ref[:, t, :] = up_scr[T + t * NB:T + (t + 1) * NB, :]


def _ffn_sample(x, cst, npre, wffn, cw, cb, wo, npost):
    NB = cst.shape[0]
    T = x.shape[0]
    NT = T // NB
    body = functools.partial(_ffn_sample_body, NB=NB, NT=NT)
    return pl.pallas_call(
        body,
        out_shape=[
            jax.ShapeDtypeStruct((NB, NT, D_MODEL), f32),
            jax.ShapeDtypeStruct((NB, CONV_W - 1, D_FF), f32),
        ],
        scratch_shapes=[
            pltpu.VMEM((T + (CONV_W - 1) * NB, D_FF), f32),
            pltpu.VMEM((T, D_FF), bf16),
        ],
        compiler_params=pltpu.CompilerParams(vmem_limit_bytes=VMEM_LIMIT),
        name="ffn_sample",
    )(x, cst, npre, wffn, cw, cb, wo, npost)


def _prep_w_in_body(*refs, n_plain, per_step):
    o_ref = refs[-1]
    j = pl.program_id(0)
    for k, wt_ref in enumerate(refs[:-1]):
        x = wt_ref[...]
        r = lax.broadcasted_iota(jnp.int32, x.shape, 0)
        x = jnp.where((j * per_step + k < n_plain) | (r < GLA_RANK), x, 0.0)
        o_ref[:, k * PREP_ROWS:(k + 1) * PREP_ROWS] = x.T.astype(bf16)


def _prep_w_in(w_in):
    d_in, n_cols = w_in.shape
    head = C_GA
    tail_src = head + GLA_RANK
    n_head = head // PREP_ROWS
    n_tail = (n_cols - tail_src) // PREP_ROWS
    assert head % PREP_ROWS == 0 and (n_cols - tail_src) % PREP_ROWS == 0
    n_plain = n_head + n_tail
    assert C_RA == n_plain * PREP_ROWS
    per_step = 3
    assert (n_plain + 1) % per_step == 0

    def row_off(blk):
        off = jnp.where(blk < n_head, blk * PREP_ROWS,
                        jnp.where(blk < n_plain, tail_src + (blk - n_head) * PREP_ROWS, head))
        return pl.multiple_of(off, 8)

    def in_spec(k):
        return pl.BlockSpec((pl.Element(PREP_ROWS), pl.Element(d_in)), lambda j: (row_off(j * per_step + k), 0))

    wt = jnp.swapaxes(w_in, 0, 1)
    return pl.pallas_call(
        functools.partial(_prep_w_in_body, n_plain=n_plain, per_step=per_step),
        grid=((n_plain + 1) // per_step,),
        in_specs=[in_spec(k) for k in range(per_step)],
        out_specs=pl.BlockSpec((d_in, per_step * PREP_ROWS), lambda j: (0, j)),
        out_shape=jax.ShapeDtypeStruct((d_in, (n_plain + 1) * PREP_ROWS), bf16),
        compiler_params=pltpu.CompilerParams(dimension_semantics=("arbitrary",)),
        name="prep_w_in",
    )(*([wt] * per_step))


def kernel(x_prompt, x_sample, state_gla, cache_swa_k, cache_swa_v, state_ffn_conv, norm_mix_pre, norm_mix_post, w_in, w_gate_up, b_gate, gla_norm, sinks, w_branch_a, w_branch_b, w_out, norm_ffn_pre, norm_ffn_post, w_ffn_in, conv_w, conv_b, w_ffn_out):
    depth = w_in.shape[0]
    assert depth == 1
    l = 0
    B, L, _ = x_prompt.shape
    assert B == 1
    NBS, NT, _ = x_sample.shape
    assert L % MIX_BLOCK == 0 and L % FFN_BLOCK == 0 and NBS % STATE_SEQS == 0
    assert cache_swa_k.shape[2] == WINDOW and NT < 8

    win = _prep_w_in(w_in[l])
    wup = jnp.zeros((RA_PAD, GLA_K), f32).at[:GLA_RANK].set(w_gate_up[l]).astype(bf16)
    bg = b_gate[l].reshape(1, GLA_K)
    gn = gla_norm[l].reshape(1, GLA_DV)
    npre = norm_mix_pre[l].reshape(1, D_MODEL)
    npost = norm_mix_post[l].reshape(1, D_MODEL)
    wba = w_branch_a[l].astype(bf16)
    wbb = w_branch_b[l].astype(bf16)
    wout = w_out[l].astype(bf16)
    fpre = norm_ffn_pre[l].reshape(1, D_MODEL)
    fpost = norm_ffn_post[l].reshape(1, D_MODEL)
    wffn = w_ffn_in[l].astype(bf16)
    cw = conv_w[l]
    cb = conv_b[l].reshape(1, D_FF)
    wo = w_ffn_out[l].astype(bf16)
    sk = sinks[l]

    x1, st_p, k_p, v_p = _mix_prompt(x_prompt[0], sk, npre, win, wup, bg, gn, wba, wbb, wout, npost, T=MIX_BLOCK)
    y_p, conv_p = _ffn_prompt(x1, fpre, wffn, cw, cb, wo, fpost, T=FFN_BLOCK)

    y_prompt = y_p[None]
    gla_state_prompt = st_p.reshape(1, 1, GLA_HEADS, GLA_DK, GLA_DV)
    swa_k_prompt = jnp.transpose(k_p.reshape(SWA_KV_HEADS, SWA_HD, WINDOW), (2, 0, 1))[None, None]
    swa_v_prompt = jnp.transpose(v_p.reshape(SWA_KV_HEADS, SWA_HD, WINDOW), (2, 0, 1))[None, None]
    conv_prompt = conv_p[8 - (CONV_W - 1):].reshape(1, 1, CONV_W - 1, D_FF)

    qe, kl, e3, oin, va, ga, qb, kb, vb, gta, gtb = _pre_sample(x_sample, npre, win, wup, bg)
    kt = jnp.transpose(cache_swa_k[l], (0, 2, 3, 1)).reshape(NBS, SWA_KV, WINDOW)
    vt = jnp.transpose(cache_swa_v[l], (0, 2, 3, 1)).reshape(NBS, SWA_KV, WINDOW)
    oa_raw, ob, s1, kt1, vt1 = _state_sample(
        sk, qe, kl, e3, oin, va, qb, kb, vb, state_gla[l].reshape(NBS, GLA_K, GLA_DV), kt, vt, NT=NT, BB=STATE_SEQS)
    x1s = _post_sample(x_sample, oa_raw, ga, ob, gta, gtb, gn, wba, wbb, wout, npost)
    y_sample, conv_s = _ffn_sample(x1s, state_ffn_conv[l], fpre, wffn, cw, cb, wo, fpost)

    def cache_out(t):
        return jnp.transpose(t.reshape(NBS, SWA_KV_HEADS, SWA_HD, WINDOW), (0, 3, 1, 2))[None]

    gla_state_sample = s1.reshape(1, NBS, GLA_HEADS, GLA_DK, GLA_DV)
    swa_k_sample = cache_out(kt1)
    swa_v_sample = cache_out(vt1)
    conv_sample = conv_s[None]
    return (y_prompt, y_sample, gla_state_prompt, gla_state_sample, swa_k_prompt, swa_v_prompt,
            swa_k_sample, swa_v_sample, conv_prompt, conv_sample)
```

```python
import functools

import jax
import jax.numpy as jnp
from jax import lax
from jax.experimental import pallas as pl
from jax.experimental.pallas import tpu as pltpu

f32 = jnp.float32
bf16 = jnp.bfloat16

D_MODEL = 1024
GLA_HEADS = 4
GLA_DK = 64
GLA_DV = 128
GLA_RANK = 16
GLA_TAU = 16.0
GLA_CHUNK = 64
GLA_SAFE_DECAY = 60.0
SWA_HEADS = 8
SWA_KV_HEADS = 2
SWA_HD = 64
WINDOW = 128
D_FF = 2816
CONV_W = 3
EPS = 1e-6
GLA_K = GLA_HEADS * GLA_DK
GLA_V = GLA_HEADS * GLA_DV
SWA_Q = SWA_HEADS * SWA_HD
SWA_KV = SWA_KV_HEADS * SWA_HD
LANES = 128
LOG2E = 1.4426950408889634

C_QA = 0
C_KA = C_QA + GLA_K
C_VA = C_KA + GLA_K
C_GA = C_VA + GLA_V
C_QB = C_GA + GLA_V
C_KB = C_QB + SWA_Q
C_VB = C_KB + SWA_KV
C_GTA = C_VB + SWA_KV
C_GTB = C_GTA + D_MODEL
C_RA = C_GTB + D_MODEL
RA_PAD = LANES
PREP_ROWS = 256
IN_COLS_PAD = C_RA + PREP_ROWS

MIX_BLOCK = 256
FFN_BLOCK = 512
FFN_COLS = 256
STATE_SEQS = 16
VMEM_LIMIT = 56 * 1024 * 1024


def _dot(a, b):
    return jnp.dot(a, b, preferred_element_type=f32)


def _dot_nt(a, b):
    return lax.dot_general(a, b, (((1,), (1,)), ((), ())), preferred_element_type=f32)


def _dot_tn(a, b):
    return lax.dot_general(a, b, (((0,), (0,)), ((), ())), preferred_element_type=f32)


def _rms(x, w):
    return x * lax.rsqrt(jnp.mean(x * x, axis=-1, keepdims=True) + EPS) * w


def _gelu_tanh(x):
    k = -2.0 * 0.7978845608028654 * LOG2E
    return x / (1.0 + jnp.exp2(x * (k + (k * 0.044715) * (x * x))))


def _split_hi_lo(x):
    hi = x.astype(bf16)
    lo = (x - hi.astype(f32)).astype(bf16)
    return hi, lo


def _chunk_cumsum(la, chunk):
    n = la.shape[0]
    r = lax.broadcasted_iota(jnp.int32, (n, n), 0)
    c = lax.broadcasted_iota(jnp.int32, (n, n), 1)
    tri = jnp.where((c <= r) & ((r // chunk) == (c // chunk)), 1.0, 0.0).astype(bf16)
    hi, lo = _split_hi_lo(la)
    return _dot(tri, hi) + _dot(tri, lo)


def _even_head_lanes(shape):
    lane = lax.broadcasted_iota(jnp.int32, shape, len(shape) - 1)
    return (lane % LANES) < GLA_DK


def _gla_out_norm(o, gn_ref, ga):
    outs = []
    for h in range(GLA_HEADS):
        oh = o[:, h * GLA_DV:(h + 1) * GLA_DV]
        outs.append(_rms(oh, gn_ref[...]))
    on = jnp.concatenate(outs, axis=1)
    return on * (ga * jax.nn.sigmoid(ga))


def _mix_tail(x, oa, ob, gate_a, gate_b, wba_ref, wbb_ref, wout_ref, npost_ref):
    merged = (jax.nn.sigmoid(gate_a) * _dot(oa.astype(bf16), wba_ref[...])
              + jax.nn.sigmoid(gate_b) * _dot(ob.astype(bf16), wbb_ref[...]))
    m = _dot(merged.astype(bf16), wout_ref[...])
    return x + _rms(m, npost_ref[...])


def _alibi_slope(head):
    return LOG2E * 2.0 ** (-(8.0 / SWA_HEADS) * (head + 1))


SWA_Q_SCALE = LOG2E * SWA_HD ** -0.5


def _kv_variants(x):
    lo = _even_head_lanes(x.shape)
    xr = pltpu.roll(x, SWA_HD, 1)
    zero = jnp.zeros_like(x)
    h0_lo = jnp.where(lo, x, zero).astype(bf16)
    h1_hi = jnp.where(lo, zero, x).astype(bf16)
    h1_lo = jnp.where(lo, xr, zero).astype(bf16)
    h0_hi = jnp.where(lo, zero, xr).astype(bf16)
    return (h0_lo, h0_hi), (h1_lo, h1_hi)


def _softmax_sink(s, sink):
    m = jnp.maximum(jnp.max(s, axis=-1, keepdims=True), sink)
    p = jnp.exp2(s - m)
    denom = jnp.sum(p, axis=-1, keepdims=True) + jnp.exp2(sink - m)
    return p, 1.0 / denom


def _mix_prompt_body(sink_ref, x_ref, npre_ref, win_ref, wup_ref, bg_ref, gn_ref,
                     wba_ref, wbb_ref, wout_ref, npost_ref,
                     y_ref, st_out_ref, k_out_ref, v_out_ref,
                     st_scr, kcat_scr, vcat_scr, oa_scr, ob_scr, gate_scr, inter_scr, *, T):
    i = pl.program_id(0)
    W = WINDOW
    C = GLA_CHUNK

    @pl.when(i == 0)
    def _():
        st_scr[...] = jnp.zeros_like(st_scr)
        kcat_scr[0:W, :] = jnp.zeros((W, SWA_KV), f32)
        vcat_scr[0:W, :] = jnp.zeros((W, SWA_KV), f32)

    @pl.when(i > 0)
    def _():
        kcat_scr[0:W, :] = kcat_scr[T:T + W, :]
        vcat_scr[0:W, :] = vcat_scr[T:T + W, :]

    x = x_ref[...]
    rms_f = lax.rsqrt(jnp.mean(x * x, axis=-1, keepdims=True) + EPS)
    h = (x * npre_ref[...]).astype(bf16)
    rms_b = {n: jnp.broadcast_to(rms_f, (T, n)) for n in (LANES, 2 * LANES)}

    def proj(c0, n):
        w = 2 * LANES if n % (2 * LANES) == 0 else LANES
        return jnp.concatenate([_dot(h, win_ref[:, c:c + w]) * rms_b[w] for c in range(c0, c0 + n, w)], axis=1)


    xg = _dot(proj(C_RA, RA_PAD).astype(bf16), wup_ref[...]) + bg_ref[...]
    kcat_scr[W:W + T, :] = proj(C_KB, SWA_KV)
    vcat_scr[W:W + T, :] = proj(C_VB, SWA_KV)
    qb = (proj(C_QB, SWA_Q) * SWA_Q_SCALE).astype(bf16)
    la = jax.nn.log_sigmoid(xg) * (1.0 / GLA_TAU)
    b = _chunk_cumsum(la, C)
    decay_floor = jnp.min(b)
    qa = proj(C_QA, GLA_K)
    ka = proj(C_KA, GLA_K)
    va_b = proj(C_VA, GLA_V).astype(bf16)

    qe = qa * jnp.exp(b) * (GLA_DK ** -0.5)
    ke = (ka * jnp.exp(-b)).astype(bf16)
    even = _even_head_lanes((T, GLA_K))
    qe_even = jnp.where(even, qe, 0.0).astype(bf16)
    qe_odd = jnp.where(even, 0.0, qe).astype(bf16)
    k_var = _kv_variants(kcat_scr[...])
    v_var = _kv_variants(vcat_scr[...])

    r2 = lax.broadcasted_iota(jnp.int32, (2 * C, 2 * C), 0)
    c2 = lax.broadcasted_iota(jnp.int32, (2 * C, 2 * C), 1)
    pair_causal = ((r2 // C) == (c2 // C)) & ((c2 % C) <= (r2 % C))
    even_c = _even_head_lanes((C, LANES))
    st = [st_scr[:, p * LANES:(p + 1) * LANES] for p in range(GLA_HEADS // 2)]

    def gla_scores(c):
        rows = slice(c * C, (c + 1) * C)
        out = []
        for p in range(GLA_HEADS // 2):
            lanes = slice(p * LANES, (p + 1) * LANES)
            q2 = jnp.concatenate([qe_even[rows, lanes], qe_odd[rows, lanes]], axis=0)
            ke_p = ke[rows, lanes]
            rhs = jnp.concatenate([ke_p, ke_p, st[p].astype(bf16)], axis=0)
            r = _dot_nt(q2, rhs)
            att = jnp.where(pair_causal, r[:, 0:2 * C], 0.0).astype(bf16)
            out.append((att, r[:, 2 * C:]))
        return out

    def gla_update(c, sc):
        rows = slice(c * C, (c + 1) * C)
        b_c = b[rows]
        bl = b_c[C - 1:C, :]
        kl = ka[rows] * jnp.exp(bl - b_c)
        ebl = jnp.exp(bl)
        for p in range(GLA_HEADS // 2):
            lanes = slice(p * LANES, (p + 1) * LANES)
            att, inter = sc[p]
            v2 = jnp.concatenate(
                [va_b[rows, (2 * p) * GLA_DV:(2 * p + 1) * GLA_DV],
                 va_b[rows, (2 * p + 1) * GLA_DV:(2 * p + 2) * GLA_DV]], axis=0)
            o2 = inter + _dot(att, v2)
            for e in range(2):
                hl = slice((2 * p + e) * GLA_DV, (2 * p + e + 1) * GLA_DV)
                oa_scr[rows, hl] = o2[e * C:(e + 1) * C]
                inter_scr[rows, hl] = inter[e * C:(e + 1) * C]
            kl_p = kl[:, lanes]
            kl_stack = jnp.concatenate(
                [jnp.where(even_c, kl_p, 0.0), jnp.where(even_c, 0.0, kl_p)], axis=0).astype(bf16)
            st[p] = st[p] * ebl[:, lanes] + _dot_tn(v2, kl_stack)

    qi = lax.broadcasted_iota(jnp.int32, (W, 2 * W), 0)
    kc = lax.broadcasted_iota(jnp.int32, (W, 2 * W), 1)
    rel = qi + W - kc
    relf = rel.astype(f32)
    in_window = (rel >= 0) & (rel < W)

    def swa_probs(j, kv):
        qrows = slice(j * W, (j + 1) * W)
        band = slice(j * W, j * W + 2 * W)
        if j == 0:
            mask = in_window & ((kc >= W) | (i > 0))
        else:
            mask = in_window
        pairs = (2 * kv, 2 * kv + 1)
        q2 = jnp.concatenate([qb[qrows, p * LANES:(p + 1) * LANES] for p in pairs], axis=0)
        out = []
        for e in range(2):
            s2 = _dot_nt(q2, k_var[kv][e][band])
            probs = []
            for half, p in enumerate(pairs):
                hd = 2 * p + e
                s = s2[half * W:(half + 1) * W]
                s = jnp.where(mask, s - _alibi_slope(hd) * relf, -jnp.inf)
                pr, inv = _softmax_sink(s, sink_ref[hd] * LOG2E)
                probs.append((pr * inv).astype(bf16))
            out.append(jnp.concatenate(probs, axis=0))
        return out

    def swa_out(j, kv, probs):
        qrows = slice(j * W, (j + 1) * W)
        band = slice(j * W, j * W + 2 * W)
        o2 = _dot(probs[0], v_var[kv][0][band]) + _dot(probs[1], v_var[kv][1][band])
        for half, p in enumerate((2 * kv, 2 * kv + 1)):
            ob_scr[qrows, p * LANES:(p + 1) * LANES] = o2[half * W:(half + 1) * W]

    n_chunks = T // C
    assert n_chunks == (T // W) * SWA_KV_HEADS
    gw = 2 * D_MODEL // n_chunks
    for idx in range(n_chunks):
        j, kv = idx // SWA_KV_HEADS, idx % SWA_KV_HEADS
        probs = swa_probs(j, kv)
        sc = gla_scores(idx)
        gate_scr[:, idx * gw:(idx + 1) * gw] = proj(C_GTA + idx * gw, gw)
        gla_update(idx, sc)
        swa_out(j, kv, probs)
    for p in range(GLA_HEADS // 2):
        st_scr[:, p * LANES:(p + 1) * LANES] = st[p]
    ga = proj(C_GA, GLA_V)
    gated_b = jax.nn.sigmoid(gate_scr[:, D_MODEL:2 * D_MODEL]) * _dot(ob_scr[...].astype(bf16), wbb_ref[...])
    sig_a = jax.nn.sigmoid(gate_scr[:, 0:D_MODEL])

    def finish(oa_raw):
        oa = _gla_out_norm(oa_raw, gn_ref, ga)
        merged = sig_a * _dot(oa.astype(bf16), wba_ref[...]) + gated_b
        m = _dot(merged.astype(bf16), wout_ref[...])
        y_ref[...] = x + _rms(m, npost_ref[...])

    finish(oa_scr[...])

    @pl.when(decay_floor < -GLA_SAFE_DECAY)
    def _():
        qs = qa * (GLA_DK ** -0.5)
        va_f = va_b.astype(f32)
        pos = lax.broadcasted_iota(jnp.int32, (T, 1), 0) % C
        er = lax.broadcasted_iota(jnp.int32, (GLA_K, GLA_V), 0)
        ec = lax.broadcasted_iota(jnp.int32, (GLA_K, GLA_V), 1)
        expand = jnp.where((er // GLA_DK) == (ec // GLA_DV), 1.0, 0.0).astype(bf16)

        def offset_term(d, acc):
            valid = pos >= d
            expo = jnp.where(valid, b - pltpu.roll(b, d, 0), 0.0)
            prod = jnp.where(valid, qs * pltpu.roll(ka, d, 0) * jnp.exp(expo), 0.0)
            return acc + _dot(prod.astype(bf16), expand) * pltpu.roll(va_f, d, 0)

        intra = lax.fori_loop(0, C, offset_term, jnp.zeros((T, GLA_V), f32))
        finish(inter_scr[...] + intra)

    @pl.when(i == pl.num_programs(0) - 1)
    def _():
        st_out_ref[...] = st_scr[...].T
        k_out_ref[...] = kcat_scr[T:T + W, :].T
        v_out_ref[...] = vcat_scr[T:T + W, :].T


def _const_spec(shape):
    nd = len(shape)
    return pl.BlockSpec(shape, lambda i: (0,) * nd, pipeline_mode=pl.Buffered(1))


def _mix_prompt(x, sinks, npre, win, wup, bg, gn, wba, wbb, wout, npost, *, T):
    L = x.shape[0]
    nb = L // T
    body = functools.partial(_mix_prompt_body, T=T)
    return pl.pallas_call(
        body,
        grid=(nb,),
        in_specs=[
            pl.BlockSpec(memory_space=pltpu.SMEM),
            pl.BlockSpec((T, D_MODEL), lambda i: (i, 0)),
            _const_spec(npre.shape), _const_spec(win.shape), _const_spec(wup.shape),
            _const_spec(bg.shape), _const_spec(gn.shape), _const_spec(wba.shape),
            _const_spec(wbb.shape), _const_spec(wout.shape), _const_spec(npost.shape),
        ],
        out_specs=[
            pl.BlockSpec((T, D_MODEL), lambda i: (i, 0)),
            pl.BlockSpec((GLA_K, GLA_DV), lambda i: (0, 0)),
            pl.BlockSpec((WINDOW, SWA_KV), lambda i: (0, 0)),
            pl.BlockSpec((WINDOW, SWA_KV), lambda i: (0, 0)),
        ],
        out_shape=[
            jax.ShapeDtypeStruct((L, D_MODEL), f32),
            jax.ShapeDtypeStruct((GLA_K, GLA_DV), f32),
            jax.ShapeDtypeStruct((WINDOW, SWA_KV), f32),
            jax.ShapeDtypeStruct((WINDOW, SWA_KV), f32),
        ],
        scratch_shapes=[
            pltpu.VMEM((GLA_DV, GLA_K), f32),
            pltpu.VMEM((T + WINDOW, SWA_KV), f32),
            pltpu.VMEM((T + WINDOW, SWA_KV), f32),
            pltpu.VMEM((T, GLA_V), f32),
            pltpu.VMEM((T, SWA_Q), f32),
            pltpu.VMEM((T, 2 * D_MODEL), f32),
            pltpu.VMEM((T, GLA_V), f32),
        ],
        compiler_params=pltpu.CompilerParams(
            dimension_semantics=("arbitrary",), vmem_limit_bytes=VMEM_LIMIT),
        name="mix_prompt",
    )(sinks, x, npre, win, wup, bg, gn, wba, wbb, wout, npost)


def _pre_sample_body(x_ref, npre_ref, win_ref, wup_ref, bg_ref,
                     qe_ref, kl_ref, e3_ref, oin_ref, va_ref, ga_ref, qb_ref, kb_ref, vb_ref,
                     gta_ref, gtb_ref, x_scr, *, NB, NT):
    for t in range(NT):
        x_scr[t * NB:(t + 1) * NB, :] = x_ref[:, t, :]
    h = _rms(x_scr[...], npre_ref[...]).astype(bf16)

    def proj(c0, n):
        return _dot(h, win_ref[:, c0:c0 + n])

    def blk(val, t):
        return val[t * NB:(t + 1) * NB, :]

    xg = _dot(proj(C_RA, RA_PAD).astype(bf16), wup_ref[...]) + bg_ref[...]
    qa = proj(C_QA, GLA_K) * (GLA_DK ** -0.5)
    ka = proj(C_KA, GLA_K)
    va = proj(C_VA, GLA_V)
    va_ref[...] = va
    la = jax.nn.log_sigmoid(xg) * (1.0 / GLA_TAU)
    b = [blk(la, 0)]
    for t in range(1, NT):
        b.append(b[-1] + blk(la, t))
    e3_ref[...] = jnp.exp(b[NT - 1])
    for t in range(NT):
        qe_ref[t * NB:(t + 1) * NB, :] = blk(qa, t) * jnp.exp(b[t])
        kl_ref[t * NB:(t + 1) * NB, :] = blk(ka, t) * jnp.exp(b[NT - 1] - b[t])
    pairs = [(t, j) for t in range(NT) for j in range(t + 1)]
    prods = [(blk(qa, t) * blk(ka, j) * jnp.exp(b[t] - b[j])).astype(bf16) for t, j in pairs]
    r = lax.broadcasted_iota(jnp.int32, (GLA_K, GLA_V), 0)
    c = lax.broadcasted_iota(jnp.int32, (GLA_K, GLA_V), 1)
    expand = jnp.where((r // GLA_DK) == (c // GLA_DV), 1.0, 0.0).astype(bf16)
    ga_ref[...] = proj(C_GA, GLA_V)
    qb_ref[...] = proj(C_QB, SWA_Q) * SWA_Q_SCALE
    kb_ref[...] = proj(C_KB, SWA_KV)
    vb_ref[...] = proj(C_VB, SWA_KV)
    att = _dot(jnp.concatenate(prods, axis=0), expand)
    gta_ref[...] = proj(C_GTA, D_MODEL)
    gtb_ref[...] = proj(C_GTB, D_MODEL)
    for t in range(NT):
        acc = None
        for idx, (tt, j) in enumerate(pairs):
            if tt != t:
                continue
            term = att[idx * NB:(idx + 1) * NB, :] * blk(va, j)
            acc = term if acc is None else acc + term
        oin_ref[t * NB:(t + 1) * NB, :] = acc


def _pre_sample(xs, npre, win, wup, bg):
    NB, NT, _ = xs.shape
    body = functools.partial(_pre_sample_body, NB=NB, NT=NT)
    widths = (GLA_K, GLA_K, None, GLA_V, GLA_V, GLA_V, SWA_Q, SWA_KV, SWA_KV, D_MODEL, D_MODEL)
    out_shape = [jax.ShapeDtypeStruct((NB, GLA_K) if w is None else (NT * NB, w), f32) for w in widths]
    return pl.pallas_call(
        body,
        out_shape=out_shape,
        scratch_shapes=[pltpu.VMEM((NB * NT, D_MODEL), f32)],
        compiler_params=pltpu.CompilerParams(vmem_limit_bytes=VMEM_LIMIT),
        name="pre_sample",
    )(xs, npre, win, wup, bg)


def _state_sample_body(sink_ref, qe_ref, kl_ref, e3_ref, oin_ref, va_ref, qb_ref, kb_ref, vb_ref,
                       s0_ref, kt_ref, vt_ref,
                       oa_ref, ob_ref, s1_ref, kt1_ref, vt1_ref, *, BB, NT):
    W = WINDOW
    SK = 2 * W
    HT = GLA_HEADS * NT
    HALF = SWA_HD
    hr = lax.broadcasted_iota(jnp.int32, (HT, GLA_K), 0) // NT
    hc = lax.broadcasted_iota(jnp.int32, (HT, GLA_K), 1) // GLA_DK
    own_head = hr == hc
    ones_rows = jnp.ones((16, GLA_DV), bf16)
    zero_rows = jnp.zeros((16, GLA_DV), bf16)
    zero_ht = jnp.zeros((HT, GLA_DV), bf16)
    G2 = 2 * NT
    row = lax.broadcasted_iota(jnp.int32, (G2, SK), 0)
    col = lax.broadcasted_iota(jnp.int32, (G2, SK), 1)
    rel = (row % NT) + W - col
    relf = rel.astype(f32)
    smask = (rel >= 0) & (rel < W)
    first_pair = lax.broadcasted_iota(jnp.int32, (G2, 1), 0) < NT
    pad_rows = jnp.zeros((8 - NT, SWA_KV), f32)
    pad_lanes = jnp.zeros((SWA_KV, SK - W - 8), f32)
    zero_half = jnp.zeros((HALF, SK), bf16)

    def head_variants(cat_t, kv):
        blk = cat_t[kv * HALF:(kv + 1) * HALF]
        return (jnp.concatenate([blk, zero_half], axis=0), jnp.concatenate([zero_half, blk], axis=0))

    pending = []
    for bi in range(BB):
        s0 = s0_ref[bi]
        q4 = qe_ref[:, bi, :]
        qm = jnp.where(own_head, jnp.concatenate([q4] * GLA_HEADS, axis=0), 0.0).astype(bf16)
        o_inter = _dot(qm, s0.astype(bf16))
        for hd in range(GLA_HEADS):
            lanes = slice(hd * GLA_DV, (hd + 1) * GLA_DV)
            oa_ref[:, bi, lanes] = o_inter[hd * NT:(hd + 1) * NT, :] + oin_ref[:, bi, lanes]
        k4 = kl_ref[:, bi, :]
        km = jnp.where(own_head, jnp.concatenate([k4] * GLA_HEADS, axis=0), 0.0).astype(bf16)
        e = e3_ref[bi:bi + 1, :]
        e_hi = e.astype(bf16)
        r1 = e - e_hi.astype(f32)
        e_mid = r1.astype(bf16)
        e_lo = (r1 - e_mid.astype(f32)).astype(bf16)
        e_rows = jnp.concatenate([e_hi, e_mid, e_lo, jnp.zeros((13, GLA_K), bf16)], axis=0)
        lhs = jnp.concatenate([km, e_rows], axis=0)
        v4 = va_ref[:, bi, :].astype(bf16)
        vrep = jnp.concatenate([v4[:, hd * GLA_DV:(hd + 1) * GLA_DV] for hd in range(GLA_HEADS)], axis=0)
        rhs = jnp.concatenate([jnp.concatenate([vrep, zero_ht], axis=1),
                               jnp.concatenate([zero_rows, ones_rows], axis=1)], axis=0)
        res = _dot_tn(lhs, rhs)
        s1_ref[bi] = res[:, GLA_DV:] * s0 + res[:, :GLA_DV]

        kt = kt_ref[bi]
        vt = vt_ref[bi]
        knew_t = jnp.concatenate([kb_ref[:, bi, :], pad_rows], axis=0).T
        vnew_t = jnp.concatenate([vb_ref[:, bi, :], pad_rows], axis=0).T
        kt1_ref[bi] = jnp.concatenate([kt[:, NT:], knew_t[:, 0:NT]], axis=1)
        vt1_ref[bi] = jnp.concatenate([vt[:, NT:], vnew_t[:, 0:NT]], axis=1)
        kcat = jnp.concatenate([kt, knew_t, pad_lanes], axis=1).astype(bf16)
        vcat = jnp.concatenate([vt, vnew_t, pad_lanes], axis=1).astype(bf16)
        q4b = qb_ref[:, bi, :].astype(bf16)
        for kv in range(SWA_KV_HEADS):
            p0 = 2 * kv
            q8 = jnp.concatenate([q4b[:, p0 * LANES:(p0 + 1) * LANES],
                                  q4b[:, (p0 + 1) * LANES:(p0 + 2) * LANES]], axis=0)
            scores = [_dot(q8, kvar) for kvar in head_variants(kcat, kv)]
            pending.append((bi, kv, scores, head_variants(vcat, kv)))

    for bi, kv, scores, v_vars in pending:
        p0 = 2 * kv
        o8_t = None
        for e_ in range(2):
            h_first = 2 * p0 + e_
            h_second = 2 * (p0 + 1) + e_
            slope = jnp.where(first_pair, _alibi_slope(h_first), _alibi_slope(h_second))
            sink = jnp.where(first_pair, sink_ref[h_first] * LOG2E, sink_ref[h_second] * LOG2E)
            s = jnp.where(smask, scores[e_] - slope * relf, -jnp.inf)
            pr, inv = _softmax_sink(s, sink)
            o_t = _dot_nt(v_vars[e_], (pr * inv).astype(bf16))
            o8_t = o_t if o8_t is None else o8_t + o_t
        o8 = o8_t.T
        ob_ref[:, bi, p0 * LANES:(p0 + 1) * LANES] = o8[0:NT, :]
        ob_ref[:, bi, (p0 + 1) * LANES:(p0 + 2) * LANES] = o8[NT:2 * NT, :]


def _state_sample(sinks, qe, kl, e3, oin, va, qb, kb, vb, s0, kt, vt, *, NT, BB):
    NBS = s0.shape[0]
    assert NBS % BB == 0
    body = functools.partial(_state_sample_body, BB=BB, NT=NT)

    def tm(a):
        return a.reshape(NT, NBS, a.shape[-1])

    def rows(n):
        return pl.BlockSpec((NT, BB, n), lambda i: (0, i, 0))

    def per_seq(shape):
        return pl.BlockSpec((BB,) + shape, lambda i: (i, 0, 0))

    oa, ob, s1, kt1, vt1 = pl.pallas_call(
        body,
        grid=(NBS // BB,),
        in_specs=[
            pl.BlockSpec(memory_space=pltpu.SMEM),
            rows(GLA_K), rows(GLA_K), pl.BlockSpec((BB, GLA_K), lambda i: (i, 0)),
            rows(GLA_V), rows(GLA_V), rows(SWA_Q), rows(SWA_KV), rows(SWA_KV),
            per_seq((GLA_K, GLA_DV)), per_seq((SWA_KV, WINDOW)), per_seq((SWA_KV, WINDOW)),
        ],
        out_specs=[
            rows(GLA_V), rows(SWA_Q),
            per_seq((GLA_K, GLA_DV)), per_seq((SWA_KV, WINDOW)), per_seq((SWA_KV, WINDOW)),
        ],
        out_shape=[
            jax.ShapeDtypeStruct((NT, NBS, GLA_V), f32),
            jax.ShapeDtypeStruct((NT, NBS, SWA_Q), f32),
            jax.ShapeDtypeStruct((NBS, GLA_K, GLA_DV), f32),
            jax.ShapeDtypeStruct((NBS, SWA_KV, WINDOW), f32),
            jax.ShapeDtypeStruct((NBS, SWA_KV, WINDOW), f32),
        ],
        compiler_params=pltpu.CompilerParams(
            dimension_semantics=("arbitrary",), vmem_limit_bytes=VMEM_LIMIT),
        name="state_sample",
    )(sinks, tm(qe), tm(kl), e3, tm(oin), tm(va), tm(qb), tm(kb), tm(vb), s0, kt, vt)
    return oa.reshape(NT * NBS, GLA_V), ob.reshape(NT * NBS, SWA_Q), s1, kt1, vt1


def _post_sample_body(x_ref, oa_ref, ga_ref, ob_ref, gta_ref, gtb_ref, gn_ref,
                      wba_ref, wbb_ref, wout_ref, npost_ref, y_ref, x_scr, *, NB, NT):
    for t in range(NT):
        x_scr[t * NB:(t + 1) * NB, :] = x_ref[:, t, :]
    oa = _gla_out_norm(oa_ref[...], gn_ref, ga_ref[...])
    y_ref[...] = _mix_tail(x_scr[...], oa, ob_ref[...], gta_ref[...], gtb_ref[...],
                           wba_ref, wbb_ref, wout_ref, npost_ref)


def _post_sample(xs, oa, ga, ob, gta, gtb, gn, wba, wbb, wout, npost):
    NB, NT, _ = xs.shape
    return pl.pallas_call(
        functools.partial(_post_sample_body, NB=NB, NT=NT),
        out_shape=jax.ShapeDtypeStruct((NT * NB, D_MODEL), f32),
        scratch_shapes=[pltpu.VMEM((NT * NB, D_MODEL), f32)],
        compiler_params=pltpu.CompilerParams(vmem_limit_bytes=VMEM_LIMIT),
        name="post_sample",
    )(xs, oa, ga, ob, gta, gtb, gn, wba, wbb, wout, npost)


def _ffn_columns(h, wffn_ref, cw_ref, cb_ref, up_scr, y_scr, *, T, base, shift):
    for c0 in range(0, D_FF, FFN_COLS):
        cols = slice(c0, c0 + FFN_COLS)
        u = _dot(h, wffn_ref[:, c0:c0 + FFN_COLS])
        g = _dot(h, wffn_ref[:, D_FF + c0:D_FF + c0 + FFN_COLS])
        up_scr[base:base + T, cols] = u
        u1 = up_scr[base - shift:base - shift + T, cols]
        u2 = up_scr[base - 2 * shift:base - 2 * shift + T, cols]
        cv = (cb_ref[:, cols] + cw_ref[2:3, cols] * u + cw_ref[1:2, cols] * u1 + cw_ref[0:1, cols] * u2)
        y_scr[:, cols] = (_gelu_tanh(cv) * g).astype(bf16)


def _ffn_prompt_body(x_ref, npre_ref, wffn_ref, cw_ref, cb_ref, wo_ref, npost_ref,
                     y_ref, conv_out_ref, up_scr, y_scr, *, T):
    i = pl.program_id(0)
    base = 8

    @pl.when(i == 0)
    def _():
        up_scr[0:base, :] = jnp.zeros((base, D_FF), f32)

    @pl.when(i > 0)
    def _():
        up_scr[0:base, :] = up_scr[T:T + base, :]

    x = x_ref[...]
    h = _rms(x, npre_ref[...]).astype(bf16)
    _ffn_columns(h, wffn_ref, cw_ref, cb_ref, up_scr, y_scr, T=T, base=base, shift=1)
    f = _dot(y_scr[...], wo_ref[...])
    y_ref[...] = x + _rms(f, npost_ref[...])

    @pl.when(i == pl.num_programs(0) - 1)
    def _():
        conv_out_ref[...] = up_scr[T + base - (CONV_W - 1):T + base, :]


def _ffn_prompt(x, npre, wffn, cw, cb, wo, npost, *, T):
    L = x.shape[0]
    body = functools.partial(_ffn_prompt_body, T=T)
    return pl.pallas_call(
        body,
        grid=(L // T,),
        in_specs=[
            pl.BlockSpec((T, D_MODEL), lambda i: (i, 0)),
            _const_spec(npre.shape), _const_spec(wffn.shape), _const_spec(cw.shape),
            _const_spec(cb.shape), _const_spec(wo.shape), _const_spec(npost.shape),
        ],
        out_specs=[
            pl.BlockSpec((T, D_MODEL), lambda i: (i, 0)),
            pl.BlockSpec((CONV_W - 1, D_FF), lambda i: (0, 0)),
        ],
        out_shape=[
            jax.ShapeDtypeStruct((L, D_MODEL), f32),
            jax.ShapeDtypeStruct((CONV_W - 1, D_FF), f32),
        ],
        scratch_shapes=[
            pltpu.VMEM((T + 8, D_FF), f32),
            pltpu.VMEM((T, D_FF), bf16),
        ],
        compiler_params=pltpu.CompilerParams(
            dimension_semantics=("arbitrary",), vmem_limit_bytes=VMEM_LIMIT),
        name="ffn_prompt",
    )(x, npre, wffn, cw, cb, wo, npost)


def _ffn_sample_body(x_ref, cst_ref, npre_ref, wffn_ref, cw_ref, cb_ref, wo_ref, npost_ref,
                     y_ref, conv_out_ref, up_scr, y_scr, *, NB, NT):
    T = NB * NT
    for t in range(CONV_W - 1):
        up_scr[t * NB:(t + 1) * NB, :] = cst_ref[:, t, :]
    base = (CONV_W - 1) * NB
    x = x_ref[...]
    h = _rms(x, npre_ref[...]).astype(bf16)
    _ffn_columns(h, wffn_ref, cw_ref, cb_ref, up_scr, y_scr, T=T, base=base, shift=NB)
    f = _dot(y_scr[...], wo_ref[...])
    y = x + _rms(f, npost_ref[...])
    for t in range(NT):
        y_ref[:, t, :] = y[t * NB:(t + 1) * NB, :]
    for t in range(CONV_W - 1):
        conv_out_ref[:, t, :] = up_scr[T + t * NB:T + (t + 1) * NB, :]


def _ffn_sample(x, cst, npre, wffn, cw, cb, wo, npost):
    NB = cst.shape[0]
    T = x.shape[0]
    NT = T // NB
    body = functools.partial(_ffn_sample_body, NB=NB, NT=NT)
    return pl.pallas_call(
        body,
        out_shape=[
            jax.ShapeDtypeStruct((NB, NT, D_MODEL), f32),
            jax.ShapeDtypeStruct((NB, CONV_W - 1, D_FF), f32),
        ],
        scratch_shapes=[
            pltpu.VMEM((T + (CONV_W - 1) * NB, D_FF), f32),
            pltpu.VMEM((T, D_FF), bf16),
        ],
        compiler_params=pltpu.CompilerParams(vmem_limit_bytes=VMEM_LIMIT),
        name="ffn_sample",
    )(x, cst, npre, wffn, cw, cb, wo, npost)


def _prep_w_in_body(*refs, n_plain, per_step):
    o_ref = refs[-1]
    j = pl.program_id(0)
    for k, wt_ref in enumerate(refs[:-1]):
        x = wt_ref[...]
        r = lax.broadcasted_iota(jnp.int32, x.shape, 0)
        x = jnp.where((j * per_step + k < n_plain) | (r < GLA_RANK), x, 0.0)
        o_ref[:, k * PREP_ROWS:(k + 1) * PREP_ROWS] = x.T.astype(bf16)


def _prep_w_in(w_in):
    d_in, n_cols = w_in.shape
    head = C_GA
    tail_src = head + GLA_RANK
    n_head = head // PREP_ROWS
    n_tail = (n_cols - tail_src) // PREP_ROWS
    assert head % PREP_ROWS == 0 and (n_cols - tail_src) % PREP_ROWS == 0
    n_plain = n_head + n_tail
    assert C_RA == n_plain * PREP_ROWS
    per_step = 3
    assert (n_plain + 1) % per_step == 0

    def row_off(blk):
        off = jnp.where(blk < n_head, blk * PREP_ROWS,
                        jnp.where(blk < n_plain, tail_src + (blk - n_head) * PREP_ROWS, head))
        return pl.multiple_of(off, 8)

    def in_spec(k):
        return pl.BlockSpec((pl.Element(PREP_ROWS), pl.Element(d_in)), lambda j: (row_off(j * per_step + k), 0))

    wt = jnp.swapaxes(w_in, 0, 1)
    return pl.pallas_call(
        functools.partial(_prep_w_in_body, n_plain=n_plain, per_step=per_step),
        grid=((n_plain + 1) // per_step,),
        in_specs=[in_spec(k) for k in range(per_step)],
        out_specs=pl.BlockSpec((d_in, per_step * PREP_ROWS), lambda j: (0, j)),
        out_shape=jax.ShapeDtypeStruct((d_in, (n_plain + 1) * PREP_ROWS), bf16),
        compiler_params=pltpu.CompilerParams(dimension_semantics=("arbitrary",)),
        name="prep_w_in",
    )(*([wt] * per_step))


def kernel(x_prompt, x_sample, state_gla, cache_swa_k, cache_swa_v, state_ffn_conv, norm_mix_pre, norm_mix_post, w_in, w_gate_up, b_gate, gla_norm, sinks, w_branch_a, w_branch_b, w_out, norm_ffn_pre, norm_ffn_post, w_ffn_in, conv_w, conv_b, w_ffn_out):
    depth = w_in.shape[0]
    assert depth == 1
    l = 0
    B, L, _ = x_prompt.shape
    assert B == 1
    NBS, NT, _ = x_sample.shape
    assert L % MIX_BLOCK == 0 and L % FFN_BLOCK == 0 and NBS % STATE_SEQS == 0
    assert cache_swa_k.shape[2] == WINDOW and NT < 8

    win = _prep_w_in(w_in[l])
    wup = jnp.zeros((RA_PAD, GLA_K), f32).at[:GLA_RANK].set(w_gate_up[l]).astype(bf16)
    bg = b_gate[l].reshape(1, GLA_K)
    gn = gla_norm[l].reshape(1, GLA_DV)
    npre = norm_mix_pre[l].reshape(1, D_MODEL)
    npost = norm_mix_post[l].reshape(1, D_MODEL)
    wba = w_branch_a[l].astype(bf16)
    wbb = w_branch_b[l].astype(bf16)
    wout = w_out[l].astype(bf16)
    fpre = norm_ffn_pre[l].reshape(1, D_MODEL)
    fpost = norm_ffn_post[l].reshape(1, D_MODEL)
    wffn = w_ffn_in[l].astype(bf16)
    cw = conv_w[l]
    cb = conv_b[l].reshape(1, D_FF)
    wo = w_ffn_out[l].astype(bf16)
    sk = sinks[l]

    x1, st_p, k_p, v_p = _mix_prompt(x_prompt[0], sk, npre, win, wup, bg, gn, wba, wbb, wout, npost, T=MIX_BLOCK)
    y_p, conv_p = _ffn_prompt(x1, fpre, wffn, cw, cb, wo, fpost, T=FFN_BLOCK)

    y_prompt = y_p[None]
    gla_state_prompt = st_p.reshape(1, 1, GLA_HEADS, GLA_DK, GLA_DV)
    swa_k_prompt = jnp.transpose(k_p.reshape(SWA_KV_HEADS, SWA_HD, WINDOW), (2, 0, 1))[None, None]
    swa_v_prompt = jnp.transpose(v_p.reshape(SWA_KV_HEADS, SWA_HD, WINDOW), (2, 0, 1))[None, None]
    conv_prompt = conv_p[None, None]

    qe, kl, e3, oin, va, ga, qb, kb, vb, gta, gtb = _pre_sample(x_sample, npre, win, wup, bg)
    kt = jnp.transpose(cache_swa_k[l], (0, 2, 3, 1)).reshape(NBS, SWA_KV, WINDOW)
    vt = jnp.transpose(cache_swa_v[l], (0, 2, 3, 1)).reshape(NBS, SWA_KV, WINDOW)
    oa_raw, ob, s1, kt1, vt1 = _state_sample(
        sk, qe, kl, e3, oin, va, qb, kb, vb, state_gla[l].reshape(NBS, GLA_K, GLA_DV), kt, vt, NT=NT, BB=STATE_SEQS)
    x1s = _post_sample(x_sample, oa_raw, ga, ob, gta, gtb, gn, wba, wbb, wout, npost)
    y_sample, conv_s = _ffn_sample(x1s, state_ffn_conv[l], fpre, wffn, cw, cb, wo, fpost)

    def cache_out(t):
        return jnp.transpose(t.reshape(NBS, SWA_KV_HEADS, SWA_HD, WINDOW), (0, 3, 1, 2))[None]

    gla_state_sample = s1.reshape(1, NBS, GLA_HEADS, GLA_DK, GLA_DV)
    swa_k_sample = cache_out(kt1)
    swa_v_sample = cache_out(vt1)
    conv_sample = conv_s[None]
    return (y_prompt, y_sample, gla_state_prompt, gla_state_sample, swa_k_prompt, swa_v_prompt,
            swa_k_sample, swa_v_sample, conv_prompt, conv_sample)
```

```python
import functools

import jax
import jax.numpy as jnp
from jax import lax
from jax.experimental import pallas as pl
from jax.experimental.pallas import tpu as pltpu

f32 = jnp.float32
bf16 = jnp.bfloat16

D_MODEL = 1024
GLA_HEADS = 4
GLA_DK = 64
GLA_DV = 128
GLA_RANK = 16
GLA_TAU = 16.0
GLA_CHUNK = 64
GLA_SAFE_DECAY = 60.0
SWA_HEADS = 8
SWA_KV_HEADS = 2
SWA_HD = 64
WINDOW = 128
D_FF = 2816
CONV_W = 3
EPS = 1e-6
GLA_K = GLA_HEADS * GLA_DK
GLA_V = GLA_HEADS * GLA_DV
SWA_Q = SWA_HEADS * SWA_HD
SWA_KV = SWA_KV_HEADS * SWA_HD
LANES = 128
LOG2E = 1.4426950408889634

C_QA = 0
C_KA = C_QA + GLA_K
C_VA = C_KA + GLA_K
C_GA = C_VA + GLA_V
C_QB = C_GA + GLA_V
C_KB = C_QB + SWA_Q
C_VB = C_KB + SWA_KV
C_GTA = C_VB + SWA_KV
C_GTB = C_GTA + D_MODEL
C_RA = C_GTB + D_MODEL
RA_PAD = LANES
PREP_ROWS = 256
IN_COLS_PAD = C_RA + PREP_ROWS

MIX_BLOCK = 256
FFN_BLOCK = 512
FFN_COLS = 256
STATE_SEQS = 16
VMEM_LIMIT = 56 * 1024 * 1024


def _dot(a, b):
    return jnp.dot(a, b, preferred_element_type=f32)


def _dot_nt(a, b):
    return lax.dot_general(a, b, (((1,), (1,)), ((), ())), preferred_element_type=f32)


def _dot_tn(a, b):
    return lax.dot_general(a, b, (((0,), (0,)), ((), ())), preferred_element_type=f32)


def _rms(x, w):
    return x * lax.rsqrt(jnp.mean(x * x, axis=-1, keepdims=True) + EPS) * w


def _gelu_tanh(x):
    k = -2.0 * 0.7978845608028654 * LOG2E
    return x / (1.0 + jnp.exp2(x * (k + (k * 0.044715) * (x * x))))


def _split_hi_lo(x):
    hi = x.astype(bf16)
    lo = (x - hi.astype(f32)).astype(bf16)
    return hi, lo


def _chunk_cumsum(la, chunk):
    n = la.shape[0]
    r = lax.broadcasted_iota(jnp.int32, (n, n), 0)
    c = lax.broadcasted_iota(jnp.int32, (n, n), 1)
    tri = jnp.where((c <= r) & ((r // chunk) == (c // chunk)), 1.0, 0.0).astype(bf16)
    hi, lo = _split_hi_lo(la)
    return _dot(tri, hi) + _dot(tri, lo)


def _even_head_lanes(shape):
    lane = lax.broadcasted_iota(jnp.int32, shape, len(shape) - 1)
    return (lane % LANES) < GLA_DK


def _gla_out_norm(o, gn_ref, ga):
    outs = []
    for h in range(GLA_HEADS):
        oh = o[:, h * GLA_DV:(h + 1) * GLA_DV]
        outs.append(_rms(oh, gn_ref[...]))
    on = jnp.concatenate(outs, axis=1)
    return on * (ga * jax.nn.sigmoid(ga))


def _mix_tail(x, oa, ob, gate_a, gate_b, wba_ref, wbb_ref, wout_ref, npost_ref):
    merged = (jax.nn.sigmoid(gate_a) * _dot(oa.astype(bf16), wba_ref[...])
              + jax.nn.sigmoid(gate_b) * _dot(ob.astype(bf16), wbb_ref[...]))
    m = _dot(merged.astype(bf16), wout_ref[...])
    return x + _rms(m, npost_ref[...])


def _alibi_slope(head):
    return LOG2E * 2.0 ** (-(8.0 / SWA_HEADS) * (head + 1))


SWA_Q_SCALE = LOG2E * SWA_HD ** -0.5


def _kv_variants(x):
    lo = _even_head_lanes(x.shape)
    xr = pltpu.roll(x, SWA_HD, 1)
    zero = jnp.zeros_like(x)
    h0_lo = jnp.where(lo, x, zero).astype(bf16)
    h1_hi = jnp.where(lo, zero, x).astype(bf16)
    h1_lo = jnp.where(lo, xr, zero).astype(bf16)
    h0_hi = jnp.where(lo, zero, xr).astype(bf16)
    return (h0_lo, h0_hi), (h1_lo, h1_hi)


def _softmax_sink(s, sink):
    m = jnp.maximum(jnp.max(s, axis=-1, keepdims=True), sink)
    p = jnp.exp2(s - m)
    denom = jnp.sum(p, axis=-1, keepdims=True) + jnp.exp2(sink - m)
    return p, 1.0 / denom


def _mix_prompt_body(sink_ref, x_ref, npre_ref, win_ref, wup_ref, bg_ref, gn_ref,
                     wba_ref, wbb_ref, wout_ref, npost_ref,
                     y_ref, st_out_ref, k_out_ref, v_out_ref,
                     st_scr, kcat_scr, vcat_scr, oa_scr, ob_scr, gate_scr, inter_scr, *, T):
    i = pl.program_id(0)
    W = WINDOW
    C = GLA_CHUNK

    @pl.when(i == 0)
    def _():
        st_scr[...] = jnp.zeros_like(st_scr)
        kcat_scr[0:W, :] = jnp.zeros((W, SWA_KV), f32)
        vcat_scr[0:W, :] = jnp.zeros((W, SWA_KV), f32)

    @pl.when(i > 0)
    def _():
        kcat_scr[0:W, :] = kcat_scr[T:T + W, :]
        vcat_scr[0:W, :] = vcat_scr[T:T + W, :]

    x = x_ref[...]
    rms_f = lax.rsqrt(jnp.mean(x * x, axis=-1, keepdims=True) + EPS)
    h = (x * npre_ref[...]).astype(bf16)
    rms_b = {n: jnp.broadcast_to(rms_f, (T, n)) for n in (LANES, 2 * LANES)}

    def proj(c0, n):
        w = 2 * LANES if n % (2 * LANES) == 0 else LANES
        return jnp.concatenate([_dot(h, win_ref[:, c:c + w]) * rms_b[w] for c in range(c0, c0 + n, w)], axis=1)


    xg = _dot(proj(C_RA, RA_PAD).astype(bf16), wup_ref[...]) + bg_ref[...]
    qb = (proj(C_QB, SWA_Q) * SWA_Q_SCALE).astype(bf16)
    kcat_scr[W:W + T, :] = proj(C_KB, SWA_KV)
    vcat_scr[W:W + T, :] = proj(C_VB, SWA_KV)
    qa = proj(C_QA, GLA_K)
    la = jax.nn.log_sigmoid(xg) * (1.0 / GLA_TAU)
    b = _chunk_cumsum(la, C)
    decay_floor = jnp.min(b)
    ka = proj(C_KA, GLA_K)
    va_b = proj(C_VA, GLA_V).astype(bf16)

    qe = qa * jnp.exp(b) * (GLA_DK ** -0.5)
    ke = (ka * jnp.exp(-b)).astype(bf16)
    even = _even_head_lanes((T, GLA_K))
    qe_even = jnp.where(even, qe, 0.0).astype(bf16)
    qe_odd = jnp.where(even, 0.0, qe).astype(bf16)
    k_var = _kv_variants(kcat_scr[...])
    v_var = _kv_variants(vcat_scr[...])

    r2 = lax.broadcasted_iota(jnp.int32, (2 * C, 2 * C), 0)
    c2 = lax.broadcasted_iota(jnp.int32, (2 * C, 2 * C), 1)
    pair_causal = ((r2 // C) == (c2 // C)) & ((c2 % C) <= (r2 % C))
    even_c = _even_head_lanes((C, LANES))
    st = [st_scr[:, p * LANES:(p + 1) * LANES] for p in range(GLA_HEADS // 2)]

    def gla_scores(c):
        rows = slice(c * C, (c + 1) * C)
        out = []
        for p in range(GLA_HEADS // 2):
            lanes = slice(p * LANES, (p + 1) * LANES)
            q2 = jnp.concatenate([qe_even[rows, lanes], qe_odd[rows, lanes]], axis=0)
            ke_p = ke[rows, lanes]
            rhs = jnp.concatenate([ke_p, ke_p, st[p].astype(bf16)], axis=0)
            r = _dot_nt(q2, rhs)
            att = jnp.where(pair_causal, r[:, 0:2 * C], 0.0).astype(bf16)
            out.append((att, r[:, 2 * C:]))
        return out

    def gla_update(c, sc):
        rows = slice(c * C, (c + 1) * C)
        b_c = b[rows]
        bl = b_c[C - 1:C, :]
        kl = ka[rows] * jnp.exp(bl - b_c)
        ebl = jnp.exp(bl)
        for p in range(GLA_HEADS // 2):
            lanes = slice(p * LANES, (p + 1) * LANES)
            att, inter = sc[p]
            v2 = jnp.concatenate(
                [va_b[rows, (2 * p) * GLA_DV:(2 * p + 1) * GLA_DV],
                 va_b[rows, (2 * p + 1) * GLA_DV:(2 * p + 2) * GLA_DV]], axis=0)
            o2 = inter + _dot(att, v2)
            for e in range(2):
                hl = slice((2 * p + e) * GLA_DV, (2 * p + e + 1) * GLA_DV)
                oa_scr[rows, hl] = o2[e * C:(e + 1) * C]
                inter_scr[rows, hl] = inter[e * C:(e + 1) * C]
            kl_p = kl[:, lanes]
            kl_stack = jnp.concatenate(
                [jnp.where(even_c, kl_p, 0.0), jnp.where(even_c, 0.0, kl_p)], axis=0).astype(bf16)
            st[p] = st[p] * ebl[:, lanes] + _dot_tn(v2, kl_stack)

    qi = lax.broadcasted_iota(jnp.int32, (W, 2 * W), 0)
    kc = lax.broadcasted_iota(jnp.int32, (W, 2 * W), 1)
    rel = qi + W - kc
    relf = rel.astype(f32)
    in_window = (rel >= 0) & (rel < W)

    def swa_probs(j, kv):
        qrows = slice(j * W, (j + 1) * W)
        band = slice(j * W, j * W + 2 * W)
        if j == 0:
            mask = in_window & ((kc >= W) | (i > 0))
        else:
            mask = in_window
        pairs = (2 * kv, 2 * kv + 1)
        q2 = jnp.concatenate([qb[qrows, p * LANES:(p + 1) * LANES] for p in pairs], axis=0)
        out = []
        for e in range(2):
            s2 = _dot_nt(q2, k_var[kv][e][band])
            probs = []
            for half, p in enumerate(pairs):
                hd = 2 * p + e
                s = s2[half * W:(half + 1) * W]
                s = jnp.where(mask, s - _alibi_slope(hd) * relf, -jnp.inf)
                pr, inv = _softmax_sink(s, sink_ref[hd] * LOG2E)
                probs.append((pr * inv).astype(bf16))
            out.append(jnp.concatenate(probs, axis=0))
        return out

    def swa_out(j, kv, probs):
        qrows = slice(j * W, (j + 1) * W)
        band = slice(j * W, j * W + 2 * W)
        o2 = _dot(probs[0], v_var[kv][0][band]) + _dot(probs[1], v_var[kv][1][band])
        for half, p in enumerate((2 * kv, 2 * kv + 1)):
            ob_scr[qrows, p * LANES:(p + 1) * LANES] = o2[half * W:(half + 1) * W]

    n_chunks = T // C
    assert n_chunks == (T // W) * SWA_KV_HEADS
    gw = 2 * D_MODEL // n_chunks
    for idx in range(n_chunks):
        j, kv = idx // SWA_KV_HEADS, idx % SWA_KV_HEADS
        probs = swa_probs(j, kv)
        sc = gla_scores(idx)
        gate_scr[:, idx * gw:(idx + 1) * gw] = proj(C_GTA + idx * gw, gw)
        gla_update(idx, sc)
        swa_out(j, kv, probs)
    for p in range(GLA_HEADS // 2):
        st_scr[:, p * LANES:(p + 1) * LANES] = st[p]
    ga = proj(C_GA, GLA_V)
    gated_b = jax.nn.sigmoid(gate_scr[:, D_MODEL:2 * D_MODEL]) * _dot(ob_scr[...].astype(bf16), wbb_ref[...])
    sig_a = jax.nn.sigmoid(gate_scr[:, 0:D_MODEL])

    def finish(oa_raw):
        oa = _gla_out_norm(oa_raw, gn_ref, ga)
        merged = sig_a * _dot(oa.astype(bf16), wba_ref[...]) + gated_b
        m = _dot(merged.astype(bf16), wout_ref[...])
        y_ref[...] = x + _rms(m, npost_ref[...])

    finish(oa_scr[...])

    @pl.when(decay_floor < -GLA_SAFE_DECAY)
    def _():
        qs = qa * (GLA_DK ** -0.5)
        va_f = va_b.astype(f32)
        pos = lax.broadcasted_iota(jnp.int32, (T, 1), 0) % C
        er = lax.broadcasted_iota(jnp.int32, (GLA_K, GLA_V), 0)
        ec = lax.broadcasted_iota(jnp.int32, (GLA_K, GLA_V), 1)
        expand = jnp.where((er // GLA_DK) == (ec // GLA_DV), 1.0, 0.0).astype(bf16)

        def offset_term(d, acc):
            valid = pos >= d
            expo = jnp.where(valid, b - pltpu.roll(b, d, 0), 0.0)
            prod = jnp.where(valid, qs * pltpu.roll(ka, d, 0) * jnp.exp(expo), 0.0)
            return acc + _dot(prod.astype(bf16), expand) * pltpu.roll(va_f, d, 0)

        intra = lax.fori_loop(0, C, offset_term, jnp.zeros((T, GLA_V), f32))
        finish(inter_scr[...] + intra)

    @pl.when(i == pl.num_programs(0) - 1)
    def _():
        st_out_ref[...] = st_scr[...].T
        k_out_ref[...] = kcat_scr[T:T + W, :].T
        v_out_ref[...] = vcat_scr[T:T + W, :].T


def _const_spec(shape):
    nd = len(shape)
    return pl.BlockSpec(shape, lambda i: (0,) * nd, pipeline_mode=pl.Buffered(1))


def _mix_prompt(x, sinks, npre, win, wup, bg, gn, wba, wbb, wout, npost, *, T):
    L = x.shape[0]
    nb = L // T
    body = functools.partial(_mix_prompt_body, T=T)
    return pl.pallas_call(
        body,
        grid=(nb,),
        in_specs=[
            pl.BlockSpec(memory_space=pltpu.SMEM),
            pl.BlockSpec((T, D_MODEL), lambda i: (i, 0)),
            _const_spec(npre.shape), _const_spec(win.shape), _const_spec(wup.shape),
            _const_spec(bg.shape), _const_spec(gn.shape), _const_spec(wba.shape),
            _const_spec(wbb.shape), _const_spec(wout.shape), _const_spec(npost.shape),
        ],
        out_specs=[
            pl.BlockSpec((T, D_MODEL), lambda i: (i, 0)),
            pl.BlockSpec((GLA_K, GLA_DV), lambda i: (0, 0)),
            pl.BlockSpec((WINDOW, SWA_KV), lambda i: (0, 0)),
            pl.BlockSpec((WINDOW, SWA_KV), lambda i: (0, 0)),
        ],
        out_shape=[
            jax.ShapeDtypeStruct((L, D_MODEL), f32),
            jax.ShapeDtypeStruct((GLA_K, GLA_DV), f32),
            jax.ShapeDtypeStruct((WINDOW, SWA_KV), f32),
            jax.ShapeDtypeStruct((WINDOW, SWA_KV), f32),
        ],
        scratch_shapes=[
            pltpu.VMEM((GLA_DV, GLA_K), f32),
            pltpu.VMEM((T + WINDOW, SWA_KV), f32),
            pltpu.VMEM((T + WINDOW, SWA_KV), f32),
            pltpu.VMEM((T, GLA_V), f32),
            pltpu.VMEM((T, SWA_Q), f32),
            pltpu.VMEM((T, 2 * D_MODEL), f32),
            pltpu.VMEM((T, GLA_V), f32),
        ],
        compiler_params=pltpu.CompilerParams(
            dimension_semantics=("arbitrary",), vmem_limit_bytes=VMEM_LIMIT),
        name="mix_prompt",
    )(sinks, x, npre, win, wup, bg, gn, wba, wbb, wout, npost)


def _pre_sample_body(x_ref, npre_ref, win_ref, wup_ref, bg_ref,
                     qe_ref, kl_ref, e3_ref, oin_ref, va_ref, ga_ref, qb_ref, kb_ref, vb_ref,
                     gta_ref, gtb_ref, x_scr, *, NB, NT):
    for t in range(NT):
        x_scr[t * NB:(t + 1) * NB, :] = x_ref[:, t, :]
    h = _rms(x_scr[...], npre_ref[...]).astype(bf16)

    def proj(c0, n):
        return _dot(h, win_ref[:, c0:c0 + n])

    def blk(val, t):
        return val[t * NB:(t + 1) * NB, :]

    xg = _dot(proj(C_RA, RA_PAD).astype(bf16), wup_ref[...]) + bg_ref[...]
    qa = proj(C_QA, GLA_K) * (GLA_DK ** -0.5)
    ka = proj(C_KA, GLA_K)
    va = proj(C_VA, GLA_V)
    va_ref[...] = va
    la = jax.nn.log_sigmoid(xg) * (1.0 / GLA_TAU)
    b = [blk(la, 0)]
    for t in range(1, NT):
        b.append(b[-1] + blk(la, t))
    e3_ref[...] = jnp.exp(b[NT - 1])
    for t in range(NT):
        qe_ref[t * NB:(t + 1) * NB, :] = blk(qa, t) * jnp.exp(b[t])
        kl_ref[t * NB:(t + 1) * NB, :] = blk(ka, t) * jnp.exp(b[NT - 1] - b[t])
    pairs = [(t, j) for t in range(NT) for j in range(t + 1)]
    prods = [(blk(qa, t) * blk(ka, j) * jnp.exp(b[t] - b[j])).astype(bf16) for t, j in pairs]
    r = lax.broadcasted_iota(jnp.int32, (GLA_K, GLA_V), 0)
    c = lax.broadcasted_iota(jnp.int32, (GLA_K, GLA_V), 1)
    expand = jnp.where((r // GLA_DK) == (c // GLA_DV), 1.0, 0.0).astype(bf16)
    ga_ref[...] = proj(C_GA, GLA_V)
    qb_ref[...] = proj(C_QB, SWA_Q) * SWA_Q_SCALE
    kb_ref[...] = proj(C_KB, SWA_KV)
    vb_ref[...] = proj(C_VB, SWA_KV)
    att = _dot(jnp.concatenate(prods, axis=0), expand)
    gta_ref[...] = proj(C_GTA, D_MODEL)
    gtb_ref[...] = proj(C_GTB, D_MODEL)
    for t in range(NT):
        acc = None
        for idx, (tt, j) in enumerate(pairs):
            if tt != t:
                continue
            term = att[idx * NB:(idx + 1) * NB, :] * blk(va, j)
            acc = term if acc is None else acc + term
        oin_ref[t * NB:(t + 1) * NB, :] = acc


def _pre_sample(xs, npre, win, wup, bg):
    NB, NT, _ = xs.shape
    body = functools.partial(_pre_sample_body, NB=NB, NT=NT)
    widths = (GLA_K, GLA_K, None, GLA_V, GLA_V, GLA_V, SWA_Q, SWA_KV, SWA_KV, D_MODEL, D_MODEL)
    out_shape = [jax.ShapeDtypeStruct((NB, GLA_K) if w is None else (NT * NB, w), f32) for w in widths]
    return pl.pallas_call(
        body,
        out_shape=out_shape,
        scratch_shapes=[pltpu.VMEM((NB * NT, D_MODEL), f32)],
        compiler_params=pltpu.CompilerParams(vmem_limit_bytes=VMEM_LIMIT),
        name="pre_sample",
    )(xs, npre, win, wup, bg)


def _state_sample_body(sink_ref, qe_ref, kl_ref, e3_ref, oin_ref, va_ref, qb_ref, kb_ref, vb_ref,
                       s0_ref, kt_ref, vt_ref,
                       oa_ref, ob_ref, s1_ref, kt1_ref, vt1_ref, *, BB, NT):
    W = WINDOW
    SK = 2 * W
    HT = GLA_HEADS * NT
    HALF = SWA_HD
    hr = lax.broadcasted_iota(jnp.int32, (HT, GLA_K), 0) // NT
    hc = lax.broadcasted_iota(jnp.int32, (HT, GLA_K), 1) // GLA_DK
    own_head = hr == hc
    ones_rows = jnp.ones((16, GLA_DV), bf16)
    zero_rows = jnp.zeros((16, GLA_DV), bf16)
    zero_ht = jnp.zeros((HT, GLA_DV), bf16)
    G2 = 2 * NT
    row = lax.broadcasted_iota(jnp.int32, (G2, SK), 0)
    col = lax.broadcasted_iota(jnp.int32, (G2, SK), 1)
    rel = (row % NT) + W - col
    relf = rel.astype(f32)
    smask = (rel >= 0) & (rel < W)
    first_pair = lax.broadcasted_iota(jnp.int32, (G2, 1), 0) < NT
    pad_rows = jnp.zeros((8 - NT, SWA_KV), f32)
    pad_lanes = jnp.zeros((SWA_KV, SK - W - 8), f32)
    zero_half = jnp.zeros((HALF, SK), bf16)

    def head_variants(cat_t, kv):
        blk = cat_t[kv * HALF:(kv + 1) * HALF]
        return (jnp.concatenate([blk, zero_half], axis=0), jnp.concatenate([zero_half, blk], axis=0))

    pending = []
    for bi in range(BB):
        s0 = s0_ref[bi]
        q4 = qe_ref[:, bi, :]
        qm = jnp.where(own_head, jnp.concatenate([q4] * GLA_HEADS, axis=0), 0.0).astype(bf16)
        o_inter = _dot(qm, s0.astype(bf16))
        for hd in range(GLA_HEADS):
            lanes = slice(hd * GLA_DV, (hd + 1) * GLA_DV)
            oa_ref[:, bi, lanes] = o_inter[hd * NT:(hd + 1) * NT, :] + oin_ref[:, bi, lanes]
        k4 = kl_ref[:, bi, :]
        km = jnp.where(own_head, jnp.concatenate([k4] * GLA_HEADS, axis=0), 0.0).astype(bf16)
        e = e3_ref[bi:bi + 1, :]
        e_hi = e.astype(bf16)
        r1 = e - e_hi.astype(f32)
        e_mid = r1.astype(bf16)
        e_lo = (r1 - e_mid.astype(f32)).astype(bf16)
        e_rows = jnp.concatenate([e_hi, e_mid, e_lo, jnp.zeros((13, GLA_K), bf16)], axis=0)
        lhs = jnp.concatenate([km, e_rows], axis=0)
        v4 = va_ref[:, bi, :].astype(bf16)
        vrep = jnp.concatenate([v4[:, hd * GLA_DV:(hd + 1) * GLA_DV] for hd in range(GLA_HEADS)], axis=0)
        rhs = jnp.concatenate([jnp.concatenate([vrep, zero_ht], axis=1),
                               jnp.concatenate([zero_rows, ones_rows], axis=1)], axis=0)
        res = _dot_tn(lhs, rhs)
        s1_ref[bi] = res[:, GLA_DV:] * s0 + res[:, :GLA_DV]

        kt = kt_ref[bi]
        vt = vt_ref[bi]
        knew_t = jnp.concatenate([kb_ref[:, bi, :], pad_rows], axis=0).T
        vnew_t = jnp.concatenate([vb_ref[:, bi, :], pad_rows], axis=0).T
        kt1_ref[bi] = jnp.concatenate([kt[:, NT:], knew_t[:, 0:NT]], axis=1)
        vt1_ref[bi] = jnp.concatenate([vt[:, NT:], vnew_t[:, 0:NT]], axis=1)
        kcat = jnp.concatenate([kt, knew_t, pad_lanes], axis=1).astype(bf16)
        vcat = jnp.concatenate([vt, vnew_t, pad_lanes], axis=1).astype(bf16)
        q4b = qb_ref[:, bi, :].astype(bf16)
        for kv in range(SWA_KV_HEADS):
            p0 = 2 * kv
            q8 = jnp.concatenate([q4b[:, p0 * LANES:(p0 + 1) * LANES],
                                  q4b[:, (p0 + 1) * LANES:(p0 + 2) * LANES]], axis=0)
            scores = [_dot(q8, kvar) for kvar in head_variants(kcat, kv)]
            pending.append((bi, kv, scores, head_variants(vcat, kv)))

    for bi, kv, scores, v_vars in pending:
        p0 = 2 * kv
        o8_t = None
        for e_ in range(2):
            h_first = 2 * p0 + e_
            h_second = 2 * (p0 + 1) + e_
            slope = jnp.where(first_pair, _alibi_slope(h_first), _alibi_slope(h_second))
            sink = jnp.where(first_pair, sink_ref[h_first] * LOG2E, sink_ref[h_second] * LOG2E)
            s = jnp.where(smask, scores[e_] - slope * relf, -jnp.inf)
            pr, inv = _softmax_sink(s, sink)
            o_t = _dot_nt(v_vars[e_], (pr * inv).astype(bf16))
            o8_t = o_t if o8_t is None else o8_t + o_t
        o8 = o8_t.T
        ob_ref[:, bi, p0 * LANES:(p0 + 1) * LANES] = o8[0:NT, :]
        ob_ref[:, bi, (p0 + 1) * LANES:(p0 + 2) * LANES] = o8[NT:2 * NT, :]


def _state_sample(sinks, qe, kl, e3, oin, va, qb, kb, vb, s0, kt, vt, *, NT, BB):
    NBS = s0.shape[0]
    assert NBS % BB == 0
    body = functools.partial(_state_sample_body, BB=BB, NT=NT)

    def tm(a):
        return a.reshape(NT, NBS, a.shape[-1])

    def rows(n):
        return pl.BlockSpec((NT, BB, n), lambda i: (0, i, 0))

    def per_seq(shape):
        return pl.BlockSpec((BB,) + shape, lambda i: (i, 0, 0))

    oa, ob, s1, kt1, vt1 = pl.pallas_call(
        body,
        grid=(NBS // BB,),
        in_specs=[
            pl.BlockSpec(memory_space=pltpu.SMEM),
            rows(GLA_K), rows(GLA_K), pl.BlockSpec((BB, GLA_K), lambda i: (i, 0)),
            rows(GLA_V), rows(GLA_V), rows(SWA_Q), rows(SWA_KV), rows(SWA_KV),
            per_seq((GLA_K, GLA_DV)), per_seq((SWA_KV, WINDOW)), per_seq((SWA_KV, WINDOW)),
        ],
        out_specs=[
            rows(GLA_V), rows(SWA_Q),
            per_seq((GLA_K, GLA_DV)), per_seq((SWA_KV, WINDOW)), per_seq((SWA_KV, WINDOW)),
        ],
        out_shape=[
            jax.ShapeDtypeStruct((NT, NBS, GLA_V), f32),
            jax.ShapeDtypeStruct((NT, NBS, SWA_Q), f32),
            jax.ShapeDtypeStruct((NBS, GLA_K, GLA_DV), f32),
            jax.ShapeDtypeStruct((NBS, SWA_KV, WINDOW), f32),
            jax.ShapeDtypeStruct((NBS, SWA_KV, WINDOW), f32),
        ],
        compiler_params=pltpu.CompilerParams(
            dimension_semantics=("arbitrary",), vmem_limit_bytes=VMEM_LIMIT),
        name="state_sample",
    )(sinks, tm(qe), tm(kl), e3, tm(oin), tm(va), tm(qb), tm(kb), tm(vb), s0, kt, vt)
    return oa.reshape(NT * NBS, GLA_V), ob.reshape(NT * NBS, SWA_Q), s1, kt1, vt1


def _post_sample_body(x_ref, oa_ref, ga_ref, ob_ref, gta_ref, gtb_ref, gn_ref,
                      wba_ref, wbb_ref, wout_ref, npost_ref, y_ref, x_scr, *, NB, NT):
    for t in range(NT):
        x_scr[t * NB:(t + 1) * NB, :] = x_ref[:, t, :]
    oa = _gla_out_norm(oa_ref[...], gn_ref, ga_ref[...])
    y_ref[...] = _mix_tail(x_scr[...], oa, ob_ref[...], gta_ref[...], gtb_ref[...],
                           wba_ref, wbb_ref, wout_ref, npost_ref)


def _post_sample(xs, oa, ga, ob, gta, gtb, gn, wba, wbb, wout, npost):
    NB, NT, _ = xs.shape
    return pl.pallas_call(
        functools.partial(_post_sample_body, NB=NB, NT=NT),
        out_shape=jax.ShapeDtypeStruct((NT * NB, D_MODEL), f32),
        scratch_shapes=[pltpu.VMEM((NT * NB, D_MODEL), f32)],
        compiler_params=pltpu.CompilerParams(vmem_limit_bytes=VMEM_LIMIT),
        name="post_sample",
    )(xs, oa, ga, ob, gta, gtb, gn, wba, wbb, wout, npost)


def _ffn_columns(h, wffn_ref, cw_ref, cb_ref, up_scr, y_scr, *, T, base, shift):
    for c0 in range(0, D_FF, FFN_COLS):
        cols = slice(c0, c0 + FFN_COLS)
        u = _dot(h, wffn_ref[:, c0:c0 + FFN_COLS])
        g = _dot(h, wffn_ref[:, D_FF + c0:D_FF + c0 + FFN_COLS])
        up_scr[base:base + T, cols] = u
        u1 = up_scr[base - shift:base - shift + T, cols]
        u2 = up_scr[base - 2 * shift:base - 2 * shift + T, cols]
        cv = (cb_ref[:, cols] + cw_ref[2:3, cols] * u + cw_ref[1:2, cols] * u1 + cw_ref[0:1, cols] * u2)
        y_scr[:, cols] = (_gelu_tanh(cv) * g).astype(bf16)


def _ffn_prompt_body(x_ref, npre_ref, wffn_ref, cw_ref, cb_ref, wo_ref, npost_ref,
                     y_ref, conv_out_ref, up_scr, y_scr, *, T):
    i = pl.program_id(0)
    base = 8

    @pl.when(i == 0)
    def _():
        up_scr[0:base, :] = jnp.zeros((base, D_FF), f32)

    @pl.when(i > 0)
    def _():
        up_scr[0:base, :] = up_scr[T:T + base, :]

    x = x_ref[...]
    h = _rms(x, npre_ref[...]).astype(bf16)
    _ffn_columns(h, wffn_ref, cw_ref, cb_ref, up_scr, y_scr, T=T, base=base, shift=1)
    f = _dot(y_scr[...], wo_ref[...])
    y_ref[...] = x + _rms(f, npost_ref[...])

    @pl.when(i == pl.num_programs(0) - 1)
    def _():
        conv_out_ref[...] = up_scr[T + base - (CONV_W - 1):T + base, :]


def _ffn_prompt(x, npre, wffn, cw, cb, wo, npost, *, T):
    L = x.shape[0]
    body = functools.partial(_ffn_prompt_body, T=T)
    return pl.pallas_call(
        body,
        grid=(L // T,),
        in_specs=[
            pl.BlockSpec((T, D_MODEL), lambda i: (i, 0)),
            _const_spec(npre.shape), _const_spec(wffn.shape), _const_spec(cw.shape),
            _const_spec(cb.shape), _const_spec(wo.shape), _const_spec(npost.shape),
        ],
        out_specs=[
            pl.BlockSpec((T, D_MODEL), lambda i: (i, 0)),
            pl.BlockSpec((CONV_W - 1, D_FF), lambda i: (0, 0)),
        ],
        out_shape=[
            jax.ShapeDtypeStruct((L, D_MODEL), f32),
            jax.ShapeDtypeStruct((CONV_W - 1, D_FF), f32),
        ],
        scratch_shapes=[
            pltpu.VMEM((T + 8, D_FF), f32),
            pltpu.VMEM((T, D_FF), bf16),
        ],
        compiler_params=pltpu.CompilerParams(
            dimension_semantics=("arbitrary",), vmem_limit_bytes=VMEM_LIMIT),
        name="ffn_prompt",
    )(x, npre, wffn, cw, cb, wo, npost)


def _ffn_sample_body(x_ref, cst_ref, npre_ref, wffn_ref, cw_ref, cb_ref, wo_ref, npost_ref,
                     y_ref, conv_out_ref, up_scr, y_scr, *, NB, NT):
    T = NB * NT
    for t in range(CONV_W - 1):
        up_scr[t * NB:(t + 1) * NB, :] = cst_ref[:, t, :]
    base = (CONV_W - 1) * NB
    x = x_ref[...]
    h = _rms(x, npre_ref[...]).astype(bf16)
    _ffn_columns(h, wffn_ref, cw_ref, cb_ref, up_scr, y_scr, T=T, base=base, shift=NB)
    f = _dot(y_scr[...], wo_ref[...])
    y = x + _rms(f, npost_ref[...])
    for t in range(NT):
        y_ref[:, t, :] = y[t * NB:(t + 1) * NB, :]
    for t in range(CONV_W - 1):
        conv_out_ref[:, t, :] = up_scr[T + t * NB:T + (t + 1) * NB, :]


def _ffn_sample(x, cst, npre, wffn, cw, cb, wo, npost):
    NB = cst.shape[0]
    T = x.shape[0]
    NT = T // NB
    body = functools.partial(_ffn_sample_body, NB=NB, NT=NT)
    return pl.pallas_call(
        body,
        out_shape=[
            jax.ShapeDtypeStruct((NB, NT, D_MODEL), f32),
            jax.ShapeDtypeStruct((NB, CONV_W - 1, D_FF), f32),
        ],
        scratch_shapes=[
            pltpu.VMEM((T + (CONV_W - 1) * NB, D_FF), f32),
            pltpu.VMEM((T, D_FF), bf16),
        ],
        compiler_params=pltpu.CompilerParams(vmem_limit_bytes=VMEM_LIMIT),
        name="ffn_sample",
    )(x, cst, npre, wffn, cw, cb, wo, npost)


def _prep_w_in_body(*refs, n_plain, per_step):
    o_ref = refs[-1]
    j = pl.program_id(0)
    for k, wt_ref in enumerate(refs[:-1]):
        x = wt_ref[...]
        r = lax.broadcasted_iota(jnp.int32, x.shape, 0)
        x = jnp.where((j * per_step + k < n_plain) | (r < GLA_RANK), x, 0.0)
        o_ref[:, k * PREP_ROWS:(k + 1) * PREP_ROWS] = x.T.astype(bf16)


def _prep_w_in(w_in):
    d_in, n_cols = w_in.shape
    head = C_GA
    tail_src = head + GLA_RANK
    n_head = head // PREP_ROWS
    n_tail = (n_cols - tail_src) // PREP_ROWS
    assert head % PREP_ROWS == 0 and (n_cols - tail_src) % PREP_ROWS == 0
    n_plain = n_head + n_tail
    assert C_RA == n_plain * PREP_ROWS
    per_step = 3
    assert (n_plain + 1) % per_step == 0

    def row_off(blk):
        off = jnp.where(blk < n_head, blk * PREP_ROWS,
                        jnp.where(blk < n_plain, tail_src + (blk - n_head) * PREP_ROWS, head))
        return pl.multiple_of(off, 8)

    def in_spec(k):
        return pl.BlockSpec((pl.Element(PREP_ROWS), pl.Element(d_in)), lambda j: (row_off(j * per_step + k), 0))

    wt = jnp.swapaxes(w_in, 0, 1)
    return pl.pallas_call(
        functools.partial(_prep_w_in_body, n_plain=n_plain, per_step=per_step),
        grid=((n_plain + 1) // per_step,),
        in_specs=[in_spec(k) for k in range(per_step)],
        out_specs=pl.BlockSpec((d_in, per_step * PREP_ROWS), lambda j: (0, j)),
        out_shape=jax.ShapeDtypeStruct((d_in, (n_plain + 1) * PREP_ROWS), bf16),
        compiler_params=pltpu.CompilerParams(dimension_semantics=("arbitrary",)),
        name="prep_w_in",
    )(*([wt] * per_step))


def kernel(x_prompt, x_sample, state_gla, cache_swa_k, cache_swa_v, state_ffn_conv, norm_mix_pre, norm_mix_post, w_in, w_gate_up, b_gate, gla_norm, sinks, w_branch_a, w_branch_b, w_out, norm_ffn_pre, norm_ffn_post, w_ffn_in, conv_w, conv_b, w_ffn_out):
    depth = w_in.shape[0]
    assert depth == 1
    l = 0
    B, L, _ = x_prompt.shape
    assert B == 1
    NBS, NT, _ = x_sample.shape
    assert L % MIX_BLOCK == 0 and L % FFN_BLOCK == 0 and NBS % STATE_SEQS == 0
    assert cache_swa_k.shape[2] == WINDOW and NT < 8

    win = _prep_w_in(w_in[l])
    wup = jnp.zeros((RA_PAD, GLA_K), f32).at[:GLA_RANK].set(w_gate_up[l]).astype(bf16)
    bg = b_gate[l].reshape(1, GLA_K)
    gn = gla_norm[l].reshape(1, GLA_DV)
    npre = norm_mix_pre[l].reshape(1, D_MODEL)
    npost = norm_mix_post[l].reshape(1, D_MODEL)
    wba = w_branch_a[l].astype(bf16)
    wbb = w_branch_b[l].astype(bf16)
    wout = w_out[l].astype(bf16)
    fpre = norm_ffn_pre[l].reshape(1, D_MODEL)
    fpost = norm_ffn_post[l].reshape(1, D_MODEL)
    wffn = w_ffn_in[l].astype(bf16)
    cw = conv_w[l]
    cb = conv_b[l].reshape(1, D_FF)
    wo = w_ffn_out[l].astype(bf16)
    sk = sinks[l]

    x1, st_p, k_p, v_p = _mix_prompt(x_prompt[0], sk, npre, win, wup, bg, gn, wba, wbb, wout, npost, T=MIX_BLOCK)
    y_p, conv_p = _ffn_prompt(x1, fpre, wffn, cw, cb, wo, fpost, T=FFN_BLOCK)

    y_prompt = y_p[None]
    gla_state_prompt = st_p.reshape(1, 1, GLA_HEADS, GLA_DK, GLA_DV)
    swa_k_prompt = jnp.transpose(k_p.reshape(SWA_KV_HEADS, SWA_HD, WINDOW), (2, 0, 1))[None, None]
    swa_v_prompt = jnp.transpose(v_p.reshape(SWA_KV_HEADS, SWA_HD, WINDOW), (2, 0, 1))[None, None]
    conv_prompt = conv_p[None, None]

    qe, kl, e3, oin, va, ga, qb, kb, vb, gta, gtb = _pre_sample(x_sample, npre, win, wup, bg)
    kt = jnp.transpose(cache_swa_k[l], (0, 2, 3, 1)).reshape(NBS, SWA_KV, WINDOW)
    vt = jnp.transpose(cache_swa_v[l], (0, 2, 3, 1)).reshape(NBS, SWA_KV, WINDOW)
    oa_raw, ob, s1, kt1, vt1 = _state_sample(
        sk, qe, kl, e3, oin, va, qb, kb, vb, state_gla[l].reshape(NBS, GLA_K, GLA_DV), kt, vt, NT=NT, BB=STATE_SEQS)
    x1s = _post_sample(x_sample, oa_raw, ga, ob, gta, gtb, gn, wba, wbb, wout, npost)
    y_sample, conv_s = _ffn_sample(x1s, state_ffn_conv[l], fpre, wffn, cw, cb, wo, fpost)

    def cache_out(t):
        return jnp.transpose(t.reshape(NBS, SWA_KV_HEADS, SWA_HD, WINDOW), (0, 3, 1, 2))[None]

    gla_state_sample = s1.reshape(1, NBS, GLA_HEADS, GLA_DK, GLA_DV)
    swa_k_sample = cache_out(kt1)
    swa_v_sample = cache_out(vt1)
    conv_sample = conv_s[None]
    return (y_prompt, y_sample, gla_state_prompt, gla_state_sample, swa_k_prompt, swa_v_prompt,
            swa_k_sample, swa_v_sample, conv_prompt, conv_sample)
```

```python
import functools

import jax
import jax.numpy as jnp
from jax import lax
from jax.experimental import pallas as pl
from jax.experimental.pallas import tpu as pltpu

f32 = jnp.float32
bf16 = jnp.bfloat16

D_MODEL = 1024
GLA_HEADS = 4
GLA_DK = 64
GLA_DV = 128
GLA_RANK = 16
GLA_TAU = 16.0
GLA_CHUNK = 64
GLA_SAFE_DECAY = 60.0
SWA_HEADS = 8
SWA_KV_HEADS = 2
SWA_HD = 64
WINDOW = 128
D_FF = 2816
CONV_W = 3
EPS = 1e-6
GLA_K = GLA_HEADS * GLA_DK
GLA_V = GLA_HEADS * GLA_DV
SWA_Q = SWA_HEADS * SWA_HD
SWA_KV = SWA_KV_HEADS * SWA_HD
LANES = 128
LOG2E = 1.4426950408889634

C_QA = 0
C_KA = C_QA + GLA_K
C_VA = C_KA + GLA_K
C_GA = C_VA + GLA_V
C_QB = C_GA + GLA_V
C_KB = C_QB + SWA_Q
C_VB = C_KB + SWA_KV
C_GTA = C_VB + SWA_KV
C_GTB = C_GTA + D_MODEL
C_RA = C_GTB + D_MODEL
RA_PAD = LANES
PREP_ROWS = 256
IN_COLS_PAD = C_RA + PREP_ROWS

MIX_BLOCK = 256
FFN_BLOCK = 512
FFN_COLS = 256
STATE_SEQS = 16
VMEM_LIMIT = 56 * 1024 * 1024


def _dot(a, b):
    return jnp.dot(a, b, preferred_element_type=f32)


def _dot_nt(a, b):
    return lax.dot_general(a, b, (((1,), (1,)), ((), ())), preferred_element_type=f32)


def _dot_tn(a, b):
    return lax.dot_general(a, b, (((0,), (0,)), ((), ())), preferred_element_type=f32)


def _rms(x, w):
    return x * lax.rsqrt(jnp.mean(x * x, axis=-1, keepdims=True) + EPS) * w


def _gelu_tanh(x):
    k = -2.0 * 0.7978845608028654 * LOG2E
    return x / (1.0 + jnp.exp2(x * (k + (k * 0.044715) * (x * x))))


def _split_hi_lo(x):
    hi = x.astype(bf16)
    lo = (x - hi.astype(f32)).astype(bf16)
    return hi, lo


def _chunk_cumsum(la, chunk):
    n = la.shape[0]
    r = lax.broadcasted_iota(jnp.int32, (n, n), 0)
    c = lax.broadcasted_iota(jnp.int32, (n, n), 1)
    tri = jnp.where((c <= r) & ((r // chunk) == (c // chunk)), 1.0, 0.0).astype(bf16)
    hi, lo = _split_hi_lo(la)
    return _dot(tri, hi) + _dot(tri, lo)


def _even_head_lanes(shape):
    lane = lax.broadcasted_iota(jnp.int32, shape, len(shape) - 1)
    return (lane % LANES) < GLA_DK


def _gla_out_norm(o, gn_ref, ga):
    outs = []
    for h in range(GLA_HEADS):
        oh = o[:, h * GLA_DV:(h + 1) * GLA_DV]
        outs.append(_rms(oh, gn_ref[...]))
    on = jnp.concatenate(outs, axis=1)
    return on * (ga * jax.nn.sigmoid(ga))


def _mix_tail(x, oa, ob, gate_a, gate_b, wba_ref, wbb_ref, wout_ref, npost_ref):
    merged = (jax.nn.sigmoid(gate_a) * _dot(oa.astype(bf16), wba_ref[...])
              + jax.nn.sigmoid(gate_b) * _dot(ob.astype(bf16), wbb_ref[...]))
    m = _dot(merged.astype(bf16), wout_ref[...])
    return x + _rms(m, npost_ref[...])


def _alibi_slope(head):
    return LOG2E * 2.0 ** (-(8.0 / SWA_HEADS) * (head + 1))


SWA_Q_SCALE = LOG2E * SWA_HD ** -0.5


def _kv_variants(x):
    lo = _even_head_lanes(x.shape)
    xr = pltpu.roll(x, SWA_HD, 1)
    zero = jnp.zeros_like(x)
    h0_lo = jnp.where(lo, x, zero).astype(bf16)
    h1_hi = jnp.where(lo, zero, x).astype(bf16)
    h1_lo = jnp.where(lo, xr, zero).astype(bf16)
    h0_hi = jnp.where(lo, zero, xr).astype(bf16)
    return (h0_lo, h0_hi), (h1_lo, h1_hi)


def _softmax_sink(s, sink):
    m = jnp.maximum(jnp.max(s, axis=-1, keepdims=True), sink)
    p = jnp.exp2(s - m)
    denom = jnp.sum(p, axis=-1, keepdims=True) + jnp.exp2(sink - m)
    return p, 1.0 / denom


def _mix_prompt_body(sink_ref, x_ref, npre_ref, win_ref, wup_ref, bg_ref, gn_ref,
                     wba_ref, wbb_ref, wout_ref, npost_ref,
                     y_ref, st_out_ref, k_out_ref, v_out_ref,
                     st_scr, kcat_scr, vcat_scr, oa_scr, ob_scr, gate_scr, inter_scr, *, T):
    i = pl.program_id(0)
    W = WINDOW
    C = GLA_CHUNK

    @pl.when(i == 0)
    def _():
        st_scr[...] = jnp.zeros_like(st_scr)
        kcat_scr[0:W, :] = jnp.zeros((W, SWA_KV), f32)
        vcat_scr[0:W, :] = jnp.zeros((W, SWA_KV), f32)

    @pl.when(i > 0)
    def _():
        kcat_scr[0:W, :] = kcat_scr[T:T + W, :]
        vcat_scr[0:W, :] = vcat_scr[T:T + W, :]

    x = x_ref[...]
    rms_f = lax.rsqrt(jnp.mean(x * x, axis=-1, keepdims=True) + EPS)
    h = (x * npre_ref[...]).astype(bf16)
    rms_b = {n: jnp.broadcast_to(rms_f, (T, n)) for n in (LANES, 2 * LANES)}

    def proj(c0, n):
        w = 2 * LANES if n % (2 * LANES) == 0 else LANES
        return jnp.concatenate([_dot(h, win_ref[:, c:c + w]) * rms_b[w] for c in range(c0, c0 + n, w)], axis=1)


    xg = _dot(proj(C_RA, RA_PAD).astype(bf16), wup_ref[...]) + bg_ref[...]
    qb = (proj(C_QB, SWA_Q) * SWA_Q_SCALE).astype(bf16)
    kcat_scr[W:W + T, :] = proj(C_KB, SWA_KV)
    vcat_scr[W:W + T, :] = proj(C_VB, SWA_KV)
    qa = proj(C_QA, GLA_K)
    la = jax.nn.log_sigmoid(xg) * (1.0 / GLA_TAU)
    b = _chunk_cumsum(la, C)
    decay_floor = jnp.min(b)
    ka = proj(C_KA, GLA_K)
    va_b = proj(C_VA, GLA_V).astype(bf16)

    qe = qa * jnp.exp(b) * (GLA_DK ** -0.5)
    ke = (ka * jnp.exp(-b)).astype(bf16)
    even = _even_head_lanes((T, GLA_K))
    qe_even = jnp.where(even, qe, 0.0).astype(bf16)
    qe_odd = jnp.where(even, 0.0, qe).astype(bf16)
    k_var = _kv_variants(kcat_scr[...])
    v_var = _kv_variants(vcat_scr[...])

    r2 = lax.broadcasted_iota(jnp.int32, (2 * C, 2 * C), 0)
    c2 = lax.broadcasted_iota(jnp.int32, (2 * C, 2 * C), 1)
    pair_causal = ((r2 // C) == (c2 // C)) & ((c2 % C) <= (r2 % C))
    even_c = _even_head_lanes((C, LANES))
    st = [st_scr[:, p * LANES:(p + 1) * LANES] for p in range(GLA_HEADS // 2)]

    def gla_scores(c):
        rows = slice(c * C, (c + 1) * C)
        out = []
        for p in range(GLA_HEADS // 2):
            lanes = slice(p * LANES, (p + 1) * LANES)
            q2 = jnp.concatenate([qe_even[rows, lanes], qe_odd[rows, lanes]], axis=0)
            ke_p = ke[rows, lanes]
            rhs = jnp.concatenate([ke_p, ke_p, st[p].astype(bf16)], axis=0)
            r = _dot_nt(q2, rhs)
            att = jnp.where(pair_causal, r[:, 0:2 * C], 0.0).astype(bf16)
            out.append((att, r[:, 2 * C:]))
        return out

    def gla_update(c, sc):
        rows = slice(c * C, (c + 1) * C)
        b_c = b[rows]
        bl = b_c[C - 1:C, :]
        kl = ka[rows] * jnp.exp(bl - b_c)
        ebl = jnp.exp(bl)
        for p in range(GLA_HEADS // 2):
            lanes = slice(p * LANES, (p + 1) * LANES)
            att, inter = sc[p]
            v2 = jnp.concatenate(
                [va_b[rows, (2 * p) * GLA_DV:(2 * p + 1) * GLA_DV],
                 va_b[rows, (2 * p + 1) * GLA_DV:(2 * p + 2) * GLA_DV]], axis=0)
            o2 = inter + _dot(att, v2)
            for e in range(2):
                hl = slice((2 * p + e) * GLA_DV, (2 * p + e + 1) * GLA_DV)
                oa_scr[rows, hl] = o2[e * C:(e + 1) * C]
                inter_scr[rows, hl] = inter[e * C:(e + 1) * C]
            kl_p = kl[:, lanes]
            kl_stack = jnp.concatenate(
                [jnp.where(even_c, kl_p, 0.0), jnp.where(even_c, 0.0, kl_p)], axis=0).astype(bf16)
            st[p] = st[p] * ebl[:, lanes] + _dot_tn(v2, kl_stack)

    qi = lax.broadcasted_iota(jnp.int32, (W, 2 * W), 0)
    kc = lax.broadcasted_iota(jnp.int32, (W, 2 * W), 1)
    rel = qi + W - kc
    relf = rel.astype(f32)
    in_window = (rel >= 0) & (rel < W)

    def swa_probs(j, kv):
        qrows = slice(j * W, (j + 1) * W)
        band = slice(j * W, j * W + 2 * W)
        if j == 0:
            mask = in_window & ((kc >= W) | (i > 0))
        else:
            mask = in_window
        pairs = (2 * kv, 2 * kv + 1)
        q2 = jnp.concatenate([qb[qrows, p * LANES:(p + 1) * LANES] for p in pairs], axis=0)
        out = []
        for e in range(2):
            s2 = _dot_nt(q2, k_var[kv][e][band])
            probs = []
            for half, p in enumerate(pairs):
                hd = 2 * p + e
                s = s2[half * W:(half + 1) * W]
                s = jnp.where(mask, s - _alibi_slope(hd) * relf, -jnp.inf)
                pr, inv = _softmax_sink(s, sink_ref[hd] * LOG2E)
                probs.append((pr * inv).astype(bf16))
            out.append(jnp.concatenate(probs, axis=0))
        return out

    def swa_out(j, kv, probs):
        qrows = slice(j * W, (j + 1) * W)
        band = slice(j * W, j * W + 2 * W)
        o2 = _dot(probs[0], v_var[kv][0][band]) + _dot(probs[1], v_var[kv][1][band])
        for half, p in enumerate((2 * kv, 2 * kv + 1)):
            ob_scr[qrows, p * LANES:(p + 1) * LANES] = o2[half * W:(half + 1) * W]

    n_chunks = T // C
    assert n_chunks == (T // W) * SWA_KV_HEADS
    gw = 2 * D_MODEL // n_chunks
    for idx in range(n_chunks):
        j, kv = idx // SWA_KV_HEADS, idx % SWA_KV_HEADS
        probs = swa_probs(j, kv)
        sc = gla_scores(idx)
        gate_scr[:, idx * gw:(idx + 1) * gw] = proj(C_GTA + idx * gw, gw)
        gla_update(idx, sc)
        swa_out(j, kv, probs)
    for p in range(GLA_HEADS // 2):
        st_scr[:, p * LANES:(p + 1) * LANES] = st[p]
    ga = proj(C_GA, GLA_V)
    gated_b = jax.nn.sigmoid(gate_scr[:, D_MODEL:2 * D_MODEL]) * _dot(ob_scr[...].astype(bf16), wbb_ref[...])
    sig_a = jax.nn.sigmoid(gate_scr[:, 0:D_MODEL])

    def finish(oa_raw):
        oa = _gla_out_norm(oa_raw, gn_ref, ga)
        merged = sig_a * _dot(oa.astype(bf16), wba_ref[...]) + gated_b
        m = _dot(merged.astype(bf16), wout_ref[...])
        y_ref[...] = x + _rms(m, npost_ref[...])

    finish(oa_scr[...])

    @pl.when(decay_floor < -GLA_SAFE_DECAY)
    def _():
        qs = qa * (GLA_DK ** -0.5)
        va_f = va_b.astype(f32)
        pos = lax.broadcasted_iota(jnp.int32, (T, 1), 0) % C
        er = lax.broadcasted_iota(jnp.int32, (GLA_K, GLA_V), 0)
        ec = lax.broadcasted_iota(jnp.int32, (GLA_K, GLA_V), 1)
        expand = jnp.where((er // GLA_DK) == (ec // GLA_DV), 1.0, 0.0).astype(bf16)

        def offset_term(d, acc):
            valid = pos >= d
            expo = jnp.where(valid, b - pltpu.roll(b, d, 0), 0.0)
            prod = jnp.where(valid, qs * pltpu.roll(ka, d, 0) * jnp.exp(expo), 0.0)
            return acc + _dot(prod.astype(bf16), expand) * pltpu.roll(va_f, d, 0)

        intra = lax.fori_loop(0, C, offset_term, jnp.zeros((T, GLA_V), f32))
        finish(inter_scr[...] + intra)

    @pl.when(i == pl.num_programs(0) - 1)
    def _():
        st_out_ref[...] = st_scr[...].T
        k_out_ref[...] = kcat_scr[T:T + W, :].T
        v_out_ref[...] = vcat_scr[T:T + W, :].T


def _const_spec(shape):
    nd = len(shape)
    return pl.BlockSpec(shape, lambda i: (0,) * nd, pipeline_mode=pl.Buffered(1))


def _mix_prompt(x, sinks, npre, win, wup, bg, gn, wba, wbb, wout, npost, *, T):
    L = x.shape[0]
    nb = L // T
    body = functools.partial(_mix_prompt_body, T=T)
    return pl.pallas_call(
        body,
        grid=(nb,),
        in_specs=[
            pl.BlockSpec(memory_space=pltpu.SMEM),
            pl.BlockSpec((T, D_MODEL), lambda i: (i, 0)),
            _const_spec(npre.shape), _const_spec(win.shape), _const_spec(wup.shape),
            _const_spec(bg.shape), _const_spec(gn.shape), _const_spec(wba.shape),
            _const_spec(wbb.shape), _const_spec(wout.shape), _const_spec(npost.shape),
        ],
        out_specs=[
            pl.BlockSpec((T, D_MODEL), lambda i: (i, 0)),
            pl.BlockSpec((GLA_K, GLA_DV), lambda i: (0, 0)),
            pl.BlockSpec((WINDOW, SWA_KV), lambda i: (0, 0)),
            pl.BlockSpec((WINDOW, SWA_KV), lambda i: (0, 0)),
        ],
        out_shape=[
            jax.ShapeDtypeStruct((L, D_MODEL), f32),
            jax.ShapeDtypeStruct((GLA_K, GLA_DV), f32),
            jax.ShapeDtypeStruct((WINDOW, SWA_KV), f32),
            jax.ShapeDtypeStruct((WINDOW, SWA_KV), f32),
        ],
        scratch_shapes=[
            pltpu.VMEM((GLA_DV, GLA_K), f32),
            pltpu.VMEM((T + WINDOW, SWA_KV), f32),
            pltpu.VMEM((T + WINDOW, SWA_KV), f32),
            pltpu.VMEM((T, GLA_V), f32),
            pltpu.VMEM((T, SWA_Q), f32),
            pltpu.VMEM((T, 2 * D_MODEL), f32),
            pltpu.VMEM((T, GLA_V), f32),
        ],
        compiler_params=pltpu.CompilerParams(
            dimension_semantics=("arbitrary",), vmem_limit_bytes=VMEM_LIMIT),
        name="mix_prompt",
    )(sinks, x, npre, win, wup, bg, gn, wba, wbb, wout, npost)


def _pre_sample_body(x_ref, npre_ref, win_ref, wup_ref, bg_ref,
                     qe_ref, kl_ref, e3_ref, oin_ref, va_ref, ga_ref, qb_ref, kb_ref, vb_ref,
                     gta_ref, gtb_ref, x_scr, *, NB, NT):
    for t in range(NT):
        x_scr[t * NB:(t + 1) * NB, :] = x_ref[:, t, :]
    h = _rms(x_scr[...], npre_ref[...]).astype(bf16)

    def proj(c0, n):
        return _dot(h, win_ref[:, c0:c0 + n])

    def blk(val, t):
        return val[t * NB:(t + 1) * NB, :]

    xg = _dot(proj(C_RA, RA_PAD).astype(bf16), wup_ref[...]) + bg_ref[...]
    qa = proj(C_QA, GLA_K) * (GLA_DK ** -0.5)
    ka = proj(C_KA, GLA_K)
    va = proj(C_VA, GLA_V)
    va_ref[...] = va
    la = jax.nn.log_sigmoid(xg) * (1.0 / GLA_TAU)
    b = [blk(la, 0)]
    for t in range(1, NT):
        b.append(b[-1] + blk(la, t))
    e3_ref[...] = jnp.exp(b[NT - 1])
    for t in range(NT):
        qe_ref[t * NB:(t + 1) * NB, :] = blk(qa, t) * jnp.exp(b[t])
        kl_ref[t * NB:(t + 1) * NB, :] = blk(ka, t) * jnp.exp(b[NT - 1] - b[t])
    pairs = [(t, j) for t in range(NT) for j in range(t + 1)]
    prods = [(blk(qa, t) * blk(ka, j) * jnp.exp(b[t] - b[j])).astype(bf16) for t, j in pairs]
    r = lax.broadcasted_iota(jnp.int32, (GLA_K, GLA_V), 0)
    c = lax.broadcasted_iota(jnp.int32, (GLA_K, GLA_V), 1)
    expand = jnp.where((r // GLA_DK) == (c // GLA_DV), 1.0, 0.0).astype(bf16)
    ga_ref[...] = proj(C_GA, GLA_V)
    qb_ref[...] = proj(C_QB, SWA_Q) * SWA_Q_SCALE
    kb_ref[...] = proj(C_KB, SWA_KV)
    vb_ref[...] = proj(C_VB, SWA_KV)
    att = _dot(jnp.concatenate(prods, axis=0), expand)
    gta_ref[...] = proj(C_GTA, D_MODEL)
    gtb_ref[...] = proj(C_GTB, D_MODEL)
    for t in range(NT):
        acc = None
        for idx, (tt, j) in enumerate(pairs):
            if tt != t:
                continue
            term = att[idx * NB:(idx + 1) * NB, :] * blk(va, j)
            acc = term if acc is None else acc + term
        oin_ref[t * NB:(t + 1) * NB, :] = acc


def _pre_sample(xs, npre, win, wup, bg):
    NB, NT, _ = xs.shape
    body = functools.partial(_pre_sample_body, NB=NB, NT=NT)
    widths = (GLA_K, GLA_K, None, GLA_V, GLA_V, GLA_V, SWA_Q, SWA_KV, SWA_KV, D_MODEL, D_MODEL)
    out_shape = [jax.ShapeDtypeStruct((NB, GLA_K) if w is None else (NT * NB, w), f32) for w in widths]
    return pl.pallas_call(
        body,
        out_shape=out_shape,
        scratch_shapes=[pltpu.VMEM((NB * NT, D_MODEL), f32)],
        compiler_params=pltpu.CompilerParams(vmem_limit_bytes=VMEM_LIMIT),
        name="pre_sample",
    )(xs, npre, win, wup, bg)


def _state_sample_body(sink_ref, qe_ref, kl_ref, e3_ref, oin_ref, va_ref, qb_ref, kb_ref, vb_ref,
                       s0_ref, kt_ref, vt_ref,
                       oa_ref, ob_ref, s1_ref, kt1_ref, vt1_ref, *, BB, NT):
    W = WINDOW
    SK = 2 * W
    HT = GLA_HEADS * NT
    HALF = SWA_HD
    hr = lax.broadcasted_iota(jnp.int32, (HT, GLA_K), 0) // NT
    hc = lax.broadcasted_iota(jnp.int32, (HT, GLA_K), 1) // GLA_DK
    own_head = hr == hc
    ones_rows = jnp.ones((16, GLA_DV), bf16)
    zero_rows = jnp.zeros((16, GLA_DV), bf16)
    zero_ht = jnp.zeros((HT, GLA_DV), bf16)
    G2 = 2 * NT
    row = lax.broadcasted_iota(jnp.int32, (G2, SK), 0)
    col = lax.broadcasted_iota(jnp.int32, (G2, SK), 1)
    rel = (row % NT) + W - col
    relf = rel.astype(f32)
    smask = (rel >= 0) & (rel < W)
    first_pair = lax.broadcasted_iota(jnp.int32, (G2, 1), 0) < NT
    pad_rows = jnp.zeros((8 - NT, SWA_KV), f32)
    pad_lanes = jnp.zeros((SWA_KV, SK - W - 8), f32)
    zero_half = jnp.zeros((HALF, SK), bf16)

    def head_variants(cat_t, kv):
        blk = cat_t[kv * HALF:(kv + 1) * HALF]
        return (jnp.concatenate([blk, zero_half], axis=0), jnp.concatenate([zero_half, blk], axis=0))

    pending = []
    for bi in range(BB):
        kt = kt_ref[bi]
        vt = vt_ref[bi]
        knew_t = jnp.concatenate([kb_ref[:, bi, :], pad_rows], axis=0).T
        vnew_t = jnp.concatenate([vb_ref[:, bi, :], pad_rows], axis=0).T
        kt1_ref[bi] = jnp.concatenate([kt[:, NT:], knew_t[:, 0:NT]], axis=1)
        vt1_ref[bi] = jnp.concatenate([vt[:, NT:], vnew_t[:, 0:NT]], axis=1)
        kcat = jnp.concatenate([kt, knew_t, pad_lanes], axis=1).astype(bf16)
        vcat = jnp.concatenate([vt, vnew_t, pad_lanes], axis=1).astype(bf16)
        q4b = qb_ref[:, bi, :].astype(bf16)
        for kv in range(SWA_KV_HEADS):
            p0 = 2 * kv
            q8 = jnp.concatenate([q4b[:, p0 * LANES:(p0 + 1) * LANES],
                                  q4b[:, (p0 + 1) * LANES:(p0 + 2) * LANES]], axis=0)
            scores = [_dot(q8, kvar) for kvar in head_variants(kcat, kv)]
            pending.append((bi, kv, scores, head_variants(vcat, kv)))

    for bi in range(BB):
        s0 = s0_ref[bi]
        q4 = qe_ref[:, bi, :]
        qm = jnp.where(own_head, jnp.concatenate([q4] * GLA_HEADS, axis=0), 0.0).astype(bf16)
        o_inter = _dot(qm, s0.astype(bf16))
        for hd in range(GLA_HEADS):
            lanes = slice(hd * GLA_DV, (hd + 1) * GLA_DV)
            oa_ref[:, bi, lanes] = o_inter[hd * NT:(hd + 1) * NT, :] + oin_ref[:, bi, lanes]
        k4 = kl_ref[:, bi, :]
        km = jnp.where(own_head, jnp.concatenate([k4] * GLA_HEADS, axis=0), 0.0).astype(bf16)
        e = e3_ref[bi:bi + 1, :]
        e_hi = e.astype(bf16)
        r1 = e - e_hi.astype(f32)
        e_mid = r1.astype(bf16)
        e_lo = (r1 - e_mid.astype(f32)).astype(bf16)
        e_rows = jnp.concatenate([e_hi, e_mid, e_lo, jnp.zeros((13, GLA_K), bf16)], axis=0)
        lhs = jnp.concatenate([km, e_rows], axis=0)
        v4 = va_ref[:, bi, :].astype(bf16)
        vrep = jnp.concatenate([v4[:, hd * GLA_DV:(hd + 1) * GLA_DV] for hd in range(GLA_HEADS)], axis=0)
        rhs = jnp.concatenate([jnp.concatenate([vrep, zero_ht], axis=1),
                               jnp.concatenate([zero_rows, ones_rows], axis=1)], axis=0)
        res = _dot_tn(lhs, rhs)
        s1_ref[bi] = res[:, GLA_DV:] * s0 + res[:, :GLA_DV]

    for bi, kv, scores, v_vars in pending:
        p0 = 2 * kv
        o8_t = None
        for e_ in range(2):
            h_first = 2 * p0 + e_
            h_second = 2 * (p0 + 1) + e_
            slope = jnp.where(first_pair, _alibi_slope(h_first), _alibi_slope(h_second))
            sink = jnp.where(first_pair, sink_ref[h_first] * LOG2E, sink_ref[h_second] * LOG2E)
            s = jnp.where(smask, scores[e_] - slope * relf, -jnp.inf)
            pr, inv = _softmax_sink(s, sink)
            o_t = _dot_nt(v_vars[e_], (pr * inv).astype(bf16))
            o8_t = o_t if o8_t is None else o8_t + o_t
        o8 = o8_t.T
        ob_ref[:, bi, p0 * LANES:(p0 + 1) * LANES] = o8[0:NT, :]
        ob_ref[:, bi, (p0 + 1) * LANES:(p0 + 2) * LANES] = o8[NT:2 * NT, :]


def _state_sample(sinks, qe, kl, e3, oin, va, qb, kb, vb, s0, kt, vt, *, NT, BB):
    NBS = s0.shape[0]
    assert NBS % BB == 0
    body = functools.partial(_state_sample_body, BB=BB, NT=NT)

    def tm(a):
        return a.reshape(NT, NBS, a.shape[-1])

    def rows(n):
        return pl.BlockSpec((NT, BB, n), lambda i: (0, i, 0))

    def per_seq(shape):
        return pl.BlockSpec((BB,) + shape, lambda i: (i, 0, 0))

    oa, ob, s1, kt1, vt1 = pl.pallas_call(
        body,
        grid=(NBS // BB,),
        in_specs=[
            pl.BlockSpec(memory_space=pltpu.SMEM),
            rows(GLA_K), rows(GLA_K), pl.BlockSpec((BB, GLA_K), lambda i: (i, 0)),
            rows(GLA_V), rows(GLA_V), rows(SWA_Q), rows(SWA_KV), rows(SWA_KV),
            per_seq((GLA_K, GLA_DV)), per_seq((SWA_KV, WINDOW)), per_seq((SWA_KV, WINDOW)),
        ],
        out_specs=[
            rows(GLA_V), rows(SWA_Q),
            per_seq((GLA_K, GLA_DV)), per_seq((SWA_KV, WINDOW)), per_seq((SWA_KV, WINDOW)),
        ],
        out_shape=[
            jax.ShapeDtypeStruct((NT, NBS, GLA_V), f32),
            jax.ShapeDtypeStruct((NT, NBS, SWA_Q), f32),
            jax.ShapeDtypeStruct((NBS, GLA_K, GLA_DV), f32),
            jax.ShapeDtypeStruct((NBS, SWA_KV, WINDOW), f32),
            jax.ShapeDtypeStruct((NBS, SWA_KV, WINDOW), f32),
        ],
        compiler_params=pltpu.CompilerParams(
            dimension_semantics=("arbitrary",), vmem_limit_bytes=VMEM_LIMIT),
        name="state_sample",
    )(sinks, tm(qe), tm(kl), e3, tm(oin), tm(va), tm(qb), tm(kb), tm(vb), s0, kt, vt)
    return oa.reshape(NT * NBS, GLA_V), ob.reshape(NT * NBS, SWA_Q), s1, kt1, vt1


def _post_sample_body(x_ref, oa_ref, ga_ref, ob_ref, gta_ref, gtb_ref, gn_ref,
                      wba_ref, wbb_ref, wout_ref, npost_ref, y_ref, x_scr, *, NB, NT):
    for t in range(NT):
        x_scr[t * NB:(t + 1) * NB, :] = x_ref[:, t, :]
    oa = _gla_out_norm(oa_ref[...], gn_ref, ga_ref[...])
    y_ref[...] = _mix_tail(x_scr[...], oa, ob_ref[...], gta_ref[...], gtb_ref[...],
                           wba_ref, wbb_ref, wout_ref, npost_ref)


def _post_sample(xs, oa, ga, ob, gta, gtb, gn, wba, wbb, wout, npost):
    NB, NT, _ = xs.shape
    return pl.pallas_call(
        functools.partial(_post_sample_body, NB=NB, NT=NT),
        out_shape=jax.ShapeDtypeStruct((NT * NB, D_MODEL), f32),
        scratch_shapes=[pltpu.VMEM((NT * NB, D_MODEL), f32)],
        compiler_params=pltpu.CompilerParams(vmem_limit_bytes=VMEM_LIMIT),
        name="post_sample",
    )(xs, oa, ga, ob, gta, gtb, gn, wba, wbb, wout, npost)


def _ffn_columns(h, wffn_ref, cw_ref, cb_ref, up_scr, y_scr, *, T, base, shift):
    for c0 in range(0, D_FF, FFN_COLS):
        cols = slice(c0, c0 + FFN_COLS)
        u = _dot(h, wffn_ref[:, c0:c0 + FFN_COLS])
        g = _dot(h, wffn_ref[:, D_FF + c0:D_FF + c0 + FFN_COLS])
        up_scr[base:base + T, cols] = u
        u1 = up_scr[base - shift:base - shift + T, cols]
        u2 = up_scr[base - 2 * shift:base - 2 * shift + T, cols]
        cv = (cb_ref[:, cols] + cw_ref[2:3, cols] * u + cw_ref[1:2, cols] * u1 + cw_ref[0:1, cols] * u2)
        y_scr[:, cols] = (_gelu_tanh(cv) * g).astype(bf16)


def _ffn_prompt_body(x_ref, npre_ref, wffn_ref, cw_ref, cb_ref, wo_ref, npost_ref,
                     y_ref, conv_out_ref, up_scr, y_scr, *, T):
    i = pl.program_id(0)
    base = 8

    @pl.when(i == 0)
    def _():
        up_scr[0:base, :] = jnp.zeros((base, D_FF), f32)

    @pl.when(i > 0)
    def _():
        up_scr[0:base, :] = up_scr[T:T + base, :]

    x = x_ref[...]
    h = _rms(x, npre_ref[...]).astype(bf16)
    _ffn_columns(h, wffn_ref, cw_ref, cb_ref, up_scr, y_scr, T=T, base=base, shift=1)
    f = _dot(y_scr[...], wo_ref[...])
    y_ref[...] = x + _rms(f, npost_ref[...])

    @pl.when(i == pl.num_programs(0) - 1)
    def _():
        conv_out_ref[...] = up_scr[T + base - (CONV_W - 1):T + base, :]


def _ffn_prompt(x, npre, wffn, cw, cb, wo, npost, *, T):
    L = x.shape[0]
    body = functools.partial(_ffn_prompt_body, T=T)
    return pl.pallas_call(
        body,
        grid=(L // T,),
        in_specs=[
            pl.BlockSpec((T, D_MODEL), lambda i: (i, 0)),
            _const_spec(npre.shape), _const_spec(wffn.shape), _const_spec(cw.shape),
            _const_spec(cb.shape), _const_spec(wo.shape), _const_spec(npost.shape),
        ],
        out_specs=[
            pl.BlockSpec((T, D_MODEL), lambda i: (i, 0)),
            pl.BlockSpec((CONV_W - 1, D_FF), lambda i: (0, 0)),
        ],
        out_shape=[
            jax.ShapeDtypeStruct((L, D_MODEL), f32),
            jax.ShapeDtypeStruct((CONV_W - 1, D_FF), f32),
        ],
        scratch_shapes=[
            pltpu.VMEM((T + 8, D_FF), f32),
            pltpu.VMEM((T, D_FF), bf16),
        ],
        compiler_params=pltpu.CompilerParams(
            dimension_semantics=("arbitrary",), vmem_limit_bytes=VMEM_LIMIT),
        name="ffn_prompt",
    )(x, npre, wffn, cw, cb, wo, npost)


def _ffn_sample_body(x_ref, cst_ref, npre_ref, wffn_ref, cw_ref, cb_ref, wo_ref, npost_ref,
                     y_ref, conv_out_ref, up_scr, y_scr, *, NB, NT):
    T = NB * NT
    for t in range(CONV_W - 1):
        up_scr[t * NB:(t + 1) * NB, :] = cst_ref[:, t, :]
    base = (CONV_W - 1) * NB
    x = x_ref[...]
    h = _rms(x, npre_ref[...]).astype(bf16)
    _ffn_columns(h, wffn_ref, cw_ref, cb_ref, up_scr, y_scr, T=T, base=base, shift=NB)
    f = _dot(y_scr[...], wo_ref[...])
    y = x + _rms(f, npost_ref[...])
    for t in range(NT):
        y_ref[:, t, :] = y[t * NB:(t + 1) * NB, :]
    for t in range(CONV_W - 1):
        conv_out_ref[:, t, :] = up_scr[T + t * NB:T + (t + 1) * NB, :]


def _ffn_sample(x, cst, npre, wffn, cw, cb, wo, npost):
    NB = cst.shape[0]
    T = x.shape[0]
    NT = T // NB
    body = functools.partial(_ffn_sample_body, NB=NB, NT=NT)
    return pl.pallas_call(
        body,
        out_shape=[
            jax.ShapeDtypeStruct((NB, NT, D_MODEL), f32),
            jax.ShapeDtypeStruct((NB, CONV_W - 1, D_FF), f32),
        ],
        scratch_shapes=[
            pltpu.VMEM((T + (CONV_W - 1) * NB, D_FF), f32),
            pltpu.VMEM((T, D_FF), bf16),
        ],
        compiler_params=pltpu.CompilerParams(vmem_limit_bytes=VMEM_LIMIT),
        name="ffn_sample",
    )(x, cst, npre, wffn, cw, cb, wo, npost)


def _prep_w_in_body(*refs, n_plain, per_step):
    o_ref = refs[-1]
    j = pl.program_id(0)
    for k, wt_ref in enumerate(refs[:-1]):
        x = wt_ref[...]
        r = lax.broadcasted_iota(jnp.int32, x.shape, 0)
        x = jnp.where((j * per_step + k < n_plain) | (r < GLA_RANK), x, 0.0)
        o_ref[:, k * PREP_ROWS:(k + 1) * PREP_ROWS] = x.T.astype(bf16)


def _prep_w_in(w_in):
    d_in, n_cols = w_in.shape
    head = C_GA
    tail_src = head + GLA_RANK
    n_head = head // PREP_ROWS
    n_tail = (n_cols - tail_src) // PREP_ROWS
    assert head % PREP_ROWS == 0 and (n_cols - tail_src) % PREP_ROWS == 0
    n_plain = n_head + n_tail
    assert C_RA == n_plain * PREP_ROWS
    per_step = 3
    assert (n_plain + 1) % per_step == 0

    def row_off(blk):
        off = jnp.where(blk < n_head, blk * PREP_ROWS,
                        jnp.where(blk < n_plain, tail_src + (blk - n_head) * PREP_ROWS, head))
        return pl.multiple_of(off, 8)

    def in_spec(k):
        return pl.BlockSpec((pl.Element(PREP_ROWS), pl.Element(d_in)), lambda j: (row_off(j * per_step + k), 0))

    wt = jnp.swapaxes(w_in, 0, 1)
    return pl.pallas_call(
        functools.partial(_prep_w_in_body, n_plain=n_plain, per_step=per_step),
        grid=((n_plain + 1) // per_step,),
        in_specs=[in_spec(k) for k in range(per_step)],
        out_specs=pl.BlockSpec((d_in, per_step * PREP_ROWS), lambda j: (0, j)),
        out_shape=jax.ShapeDtypeStruct((d_in, (n_plain + 1) * PREP_ROWS), bf16),
        compiler_params=pltpu.CompilerParams(dimension_semantics=("arbitrary",)),
        name="prep_w_in",
    )(*([wt] * per_step))


def kernel(x_prompt, x_sample, state_gla, cache_swa_k, cache_swa_v, state_ffn_conv, norm_mix_pre, norm_mix_post, w_in, w_gate_up, b_gate, gla_norm, sinks, w_branch_a, w_branch_b, w_out, norm_ffn_pre, norm_ffn_post, w_ffn_in, conv_w, conv_b, w_ffn_out):
    depth = w_in.shape[0]
    assert depth == 1
    l = 0
    B, L, _ = x_prompt.shape
    assert B == 1
    NBS, NT, _ = x_sample.shape
    assert L % MIX_BLOCK == 0 and L % FFN_BLOCK == 0 and NBS % STATE_SEQS == 0
    assert cache_swa_k.shape[2] == WINDOW and NT < 8

    win = _prep_w_in(w_in[l])
    wup = jnp.zeros((RA_PAD, GLA_K), f32).at[:GLA_RANK].set(w_gate_up[l]).astype(bf16)
    bg = b_gate[l].reshape(1, GLA_K)
    gn = gla_norm[l].reshape(1, GLA_DV)
    npre = norm_mix_pre[l].reshape(1, D_MODEL)
    npost = norm_mix_post[l].reshape(1, D_MODEL)
    wba = w_branch_a[l].astype(bf16)
    wbb = w_branch_b[l].astype(bf16)
    wout = w_out[l].astype(bf16)
    fpre = norm_ffn_pre[l].reshape(1, D_MODEL)
    fpost = norm_ffn_post[l].reshape(1, D_MODEL)
    wffn = w_ffn_in[l].astype(bf16)
    cw = conv_w[l]
    cb = conv_b[l].reshape(1, D_FF)
    wo = w_ffn_out[l].astype(bf16)
    sk = sinks[l]

    x1, st_p, k_p, v_p = _mix_prompt(x_prompt[0], sk, npre, win, wup, bg, gn, wba, wbb, wout, npost, T=MIX_BLOCK)
    y_p, conv_p = _ffn_prompt(x1, fpre, wffn, cw, cb, wo, fpost, T=FFN_BLOCK)

    y_prompt = y_p[None]
    gla_state_prompt = st_p.reshape(1, 1, GLA_HEADS, GLA_DK, GLA_DV)
    swa_k_prompt = jnp.transpose(k_p.reshape(SWA_KV_HEADS, SWA_HD, WINDOW), (2, 0, 1))[None, None]
    swa_v_prompt = jnp.transpose(v_p.reshape(SWA_KV_HEADS, SWA_HD, WINDOW), (2, 0, 1))[None, None]
    conv_prompt = conv_p[None, None]

    qe, kl, e3, oin, va, ga, qb, kb, vb, gta, gtb = _pre_sample(x_sample, npre, win, wup, bg)
    kt = jnp.transpose(cache_swa_k[l], (0, 2, 3, 1)).reshape(NBS, SWA_KV, WINDOW)
    vt = jnp.transpose(cache_swa_v[l], (0, 2, 3, 1)).reshape(NBS, SWA_KV, WINDOW)
    oa_raw, ob, s1, kt1, vt1 = _state_sample(
        sk, qe, kl, e3, oin, va, qb, kb, vb, state_gla[l].reshape(NBS, GLA_K, GLA_DV), kt, vt, NT=NT, BB=STATE_SEQS)
    x1s = _post_sample(x_sample, oa_raw, ga, ob, gta, gtb, gn, wba, wbb, wout, npost)
    y_sample, conv_s = _ffn_sample(x1s, state_ffn_conv[l], fpre, wffn, cw, cb, wo, fpost)

    def cache_out(t):
        return jnp.transpose(t.reshape(NBS, SWA_KV_HEADS, SWA_HD, WINDOW), (0, 3, 1, 2))[None]

    gla_state_sample = s1.reshape(1, NBS, GLA_HEADS, GLA_DK, GLA_DV)
    swa_k_sample = cache_out(kt1)
    swa_v_sample = cache_out(vt1)
    conv_sample = conv_s[None]
    return (y_prompt, y_sample, gla_state_prompt, gla_state_sample, swa_k_prompt, swa_v_prompt,
            swa_k_sample, swa_v_sample, conv_prompt, conv_sample)
```

```python
import functools

import jax
import jax.numpy as jnp
from jax import lax
from jax.experimental import pallas as pl
from jax.experimental.pallas import tpu as pltpu

f32 = jnp.float32
bf16 = jnp.bfloat16

D_MODEL = 1024
GLA_HEADS = 4
GLA_DK = 64
GLA_DV = 128
GLA_RANK = 16
GLA_TAU = 16.0
GLA_CHUNK = 64
GLA_SAFE_DECAY = 60.0
SWA_HEADS = 8
SWA_KV_HEADS = 2
SWA_HD = 64
WINDOW = 128
D_FF = 2816
CONV_W = 3
EPS = 1e-6
GLA_K = GLA_HEADS * GLA_DK
GLA_V = GLA_HEADS * GLA_DV
SWA_Q = SWA_HEADS * SWA_HD
SWA_KV = SWA_KV_HEADS * SWA_HD
LANES = 128
LOG2E = 1.4426950408889634

C_QA = 0
C_KA = C_QA + GLA_K
C_VA = C_KA + GLA_K
C_GA = C_VA + GLA_V
C_QB = C_GA + GLA_V
C_KB = C_QB + SWA_Q
C_VB = C_KB + SWA_KV
C_GTA = C_VB + SWA_KV
C_GTB = C_GTA + D_MODEL
C_RA = C_GTB + D_MODEL
RA_PAD = LANES
PREP_ROWS = 256
IN_COLS_PAD = C_RA + PREP_ROWS

MIX_BLOCK = 256
FFN_BLOCK = 512
FFN_COLS = 256
STATE_SEQS = 16
VMEM_LIMIT = 56 * 1024 * 1024


def _dot(a, b):
    return jnp.dot(a, b, preferred_element_type=f32)


def _dot_nt(a, b):
    return lax.dot_general(a, b, (((1,), (1,)), ((), ())), preferred_element_type=f32)


def _dot_tn(a, b):
    return lax.dot_general(a, b, (((0,), (0,)), ((), ())), preferred_element_type=f32)


def _rms(x, w):
    return x * lax.rsqrt(jnp.mean(x * x, axis=-1, keepdims=True) + EPS) * w


def _gelu_tanh(x):
    k = -2.0 * 0.7978845608028654 * LOG2E
    return x / (1.0 + jnp.exp2(x * (k + (k * 0.044715) * (x * x))))


def _split_hi_lo(x):
    hi = x.astype(bf16)
    lo = (x - hi.astype(f32)).astype(bf16)
    return hi, lo


def _chunk_cumsum(la, chunk):
    n = la.shape[0]
    r = lax.broadcasted_iota(jnp.int32, (n, n), 0)
    c = lax.broadcasted_iota(jnp.int32, (n, n), 1)
    tri = jnp.where((c <= r) & ((r // chunk) == (c // chunk)), 1.0, 0.0).astype(bf16)
    hi, lo = _split_hi_lo(la)
    return _dot(tri, hi) + _dot(tri, lo)


def _even_head_lanes(shape):
    lane = lax.broadcasted_iota(jnp.int32, shape, len(shape) - 1)
    return (lane % LANES) < GLA_DK


def _gla_out_norm(o, gn_ref, ga):
    outs = []
    for h in range(GLA_HEADS):
        oh = o[:, h * GLA_DV:(h + 1) * GLA_DV]
        outs.append(_rms(oh, gn_ref[...]))
    on = jnp.concatenate(outs, axis=1)
    return on * (ga * jax.nn.sigmoid(ga))


def _mix_tail(x, oa, ob, gate_a, gate_b, wba_ref, wbb_ref, wout_ref, npost_ref):
    merged = (jax.nn.sigmoid(gate_a) * _dot(oa.astype(bf16), wba_ref[...])
              + jax.nn.sigmoid(gate_b) * _dot(ob.astype(bf16), wbb_ref[...]))
    m = _dot(merged.astype(bf16), wout_ref[...])
    return x + _rms(m, npost_ref[...])


def _alibi_slope(head):
    return LOG2E * 2.0 ** (-(8.0 / SWA_HEADS) * (head + 1))


SWA_Q_SCALE = LOG2E * SWA_HD ** -0.5


def _kv_variants(x):
    lo = _even_head_lanes(x.shape)
    xr = pltpu.roll(x, SWA_HD, 1)
    zero = jnp.zeros_like(x)
    h0_lo = jnp.where(lo, x, zero).astype(bf16)
    h1_hi = jnp.where(lo, zero, x).astype(bf16)
    h1_lo = jnp.where(lo, xr, zero).astype(bf16)
    h0_hi = jnp.where(lo, zero, xr).astype(bf16)
    return (h0_lo, h0_hi), (h1_lo, h1_hi)


def _softmax_sink(s, sink):
    m = jnp.maximum(jnp.max(s, axis=-1, keepdims=True), sink)
    p = jnp.exp2(s - m)
    denom = jnp.sum(p, axis=-1, keepdims=True) + jnp.exp2(sink - m)
    return p, 1.0 / denom


def _mix_prompt_body(sink_ref, x_ref, npre_ref, win_ref, wup_ref, bg_ref, gn_ref,
                     wba_ref, wbb_ref, wout_ref, npost_ref,
                     y_ref, st_out_ref, k_out_ref, v_out_ref,
                     st_scr, kcat_scr, vcat_scr, oa_scr, ob_scr, gate_scr, inter_scr, *, T):
    i = pl.program_id(0)
    W = WINDOW
    C = GLA_CHUNK

    @pl.when(i == 0)
    def _():
        st_scr[...] = jnp.zeros_like(st_scr)
        kcat_scr[0:W, :] = jnp.zeros((W, SWA_KV), f32)
        vcat_scr[0:W, :] = jnp.zeros((W, SWA_KV), f32)

    @pl.when(i > 0)
    def _():
        kcat_scr[0:W, :] = kcat_scr[T:T + W, :]
        vcat_scr[0:W, :] = vcat_scr[T:T + W, :]

    x = x_ref[...]
    rms_f = lax.rsqrt(jnp.mean(x * x, axis=-1, keepdims=True) + EPS)
    h = (x * npre_ref[...]).astype(bf16)
    rms_b = {n: jnp.broadcast_to(rms_f, (T, n)) for n in (LANES, 2 * LANES)}

    def proj(c0, n):
        w = 2 * LANES if n % (2 * LANES) == 0 else LANES
        return jnp.concatenate([_dot(h, win_ref[:, c:c + w]) * rms_b[w] for c in range(c0, c0 + n, w)], axis=1)


    xg = _dot(proj(C_RA, RA_PAD).astype(bf16), wup_ref[...]) + bg_ref[...]
    qb = (proj(C_QB, SWA_Q) * SWA_Q_SCALE).astype(bf16)
    kcat_scr[W:W + T, :] = proj(C_KB, SWA_KV)
    vcat_scr[W:W + T, :] = proj(C_VB, SWA_KV)
    qa = proj(C_QA, GLA_K)
    la = jax.nn.log_sigmoid(xg) * (1.0 / GLA_TAU)
    b = _chunk_cumsum(la, C)
    decay_floor = jnp.min(b)
    ka = proj(C_KA, GLA_K)
    va_b = proj(C_VA, GLA_V).astype(bf16)

    qe = qa * jnp.exp(b) * (GLA_DK ** -0.5)
    ke = (ka * jnp.exp(-b)).astype(bf16)
    even = _even_head_lanes((T, GLA_K))
    qe_even = jnp.where(even, qe, 0.0).astype(bf16)
    qe_odd = jnp.where(even, 0.0, qe).astype(bf16)
    k_var = _kv_variants(kcat_scr[...])
    v_var = _kv_variants(vcat_scr[...])

    r2 = lax.broadcasted_iota(jnp.int32, (2 * C, 2 * C), 0)
    c2 = lax.broadcasted_iota(jnp.int32, (2 * C, 2 * C), 1)
    pair_causal = ((r2 // C) == (c2 // C)) & ((c2 % C) <= (r2 % C))
    even_c = _even_head_lanes((C, LANES))
    st = [st_scr[:, p * LANES:(p + 1) * LANES] for p in range(GLA_HEADS // 2)]

    def gla_scores(c):
        rows = slice(c * C, (c + 1) * C)
        out = []
        for p in range(GLA_HEADS // 2):
            lanes = slice(p * LANES, (p + 1) * LANES)
            q2 = jnp.concatenate([qe_even[rows, lanes], qe_odd[rows, lanes]], axis=0)
            ke_p = ke[rows, lanes]
            rhs = jnp.concatenate([ke_p, ke_p, st[p].astype(bf16)], axis=0)
            r = _dot_nt(q2, rhs)
            att = jnp.where(pair_causal, r[:, 0:2 * C], 0.0).astype(bf16)
            out.append((att, r[:, 2 * C:]))
        return out

    def gla_update(c, sc):
        rows = slice(c * C, (c + 1) * C)
        b_c = b[rows]
        bl = b_c[C - 1:C, :]
        kl = ka[rows] * jnp.exp(bl - b_c)
        ebl = jnp.exp(bl)
        for p in range(GLA_HEADS // 2):
            lanes = slice(p * LANES, (p + 1) * LANES)
            att, inter = sc[p]
            v2 = jnp.concatenate(
                [va_b[rows, (2 * p) * GLA_DV:(2 * p + 1) * GLA_DV],
                 va_b[rows, (2 * p + 1) * GLA_DV:(2 * p + 2) * GLA_DV]], axis=0)
            o2 = inter + _dot(att, v2)
            for e in range(2):
                hl = slice((2 * p + e) * GLA_DV, (2 * p + e + 1) * GLA_DV)
                oa_scr[rows, hl] = o2[e * C:(e + 1) * C]
                inter_scr[rows, hl] = inter[e * C:(e + 1) * C]
            kl_p = kl[:, lanes]
            kl_stack = jnp.concatenate(
                [jnp.where(even_c, kl_p, 0.0), jnp.where(even_c, 0.0, kl_p)], axis=0).astype(bf16)
            st[p] = st[p] * ebl[:, lanes] + _dot_tn(v2, kl_stack)

    qi = lax.broadcasted_iota(jnp.int32, (W, 2 * W), 0)
    kc = lax.broadcasted_iota(jnp.int32, (W, 2 * W), 1)
    rel = qi + W - kc
    relf = rel.astype(f32)
    in_window = (rel >= 0) & (rel < W)

    def swa_probs(j, kv):
        qrows = slice(j * W, (j + 1) * W)
        band = slice(j * W, j * W + 2 * W)
        if j == 0:
            mask = in_window & ((kc >= W) | (i > 0))
        else:
            mask = in_window
        pairs = (2 * kv, 2 * kv + 1)
        q2 = jnp.concatenate([qb[qrows, p * LANES:(p + 1) * LANES] for p in pairs], axis=0)
        out = []
        for e in range(2):
            s2 = _dot_nt(q2, k_var[kv][e][band])
            probs = []
            for half, p in enumerate(pairs):
                hd = 2 * p + e
                s = s2[half * W:(half + 1) * W]
                s = jnp.where(mask, s - _alibi_slope(hd) * relf, -jnp.inf)
                pr, inv = _softmax_sink(s, sink_ref[hd] * LOG2E)
                probs.append((pr * inv).astype(bf16))
            out.append(jnp.concatenate(probs, axis=0))
        return out

    def swa_out(j, kv, probs):
        qrows = slice(j * W, (j + 1) * W)
        band = slice(j * W, j * W + 2 * W)
        o2 = _dot(probs[0], v_var[kv][0][band]) + _dot(probs[1], v_var[kv][1][band])
        for half, p in enumerate((2 * kv, 2 * kv + 1)):
            ob_scr[qrows, p * LANES:(p + 1) * LANES] = o2[half * W:(half + 1) * W]

    n_chunks = T // C
    assert n_chunks == (T // W) * SWA_KV_HEADS
    gw = 2 * D_MODEL // n_chunks
    for idx in range(n_chunks):
        j, kv = idx // SWA_KV_HEADS, idx % SWA_KV_HEADS
        probs = swa_probs(j, kv)
        sc = gla_scores(idx)
        gate_scr[:, idx * gw:(idx + 1) * gw] = proj(C_GTA + idx * gw, gw)
        gla_update(idx, sc)
        swa_out(j, kv, probs)
    for p in range(GLA_HEADS // 2):
        st_scr[:, p * LANES:(p + 1) * LANES] = st[p]
    ga = proj(C_GA, GLA_V)
    gated_b = jax.nn.sigmoid(gate_scr[:, D_MODEL:2 * D_MODEL]) * _dot(ob_scr[...].astype(bf16), wbb_ref[...])
    sig_a = jax.nn.sigmoid(gate_scr[:, 0:D_MODEL])

    def finish(oa_raw):
        oa = _gla_out_norm(oa_raw, gn_ref, ga)
        merged = sig_a * _dot(oa.astype(bf16), wba_ref[...]) + gated_b
        m = _dot(merged.astype(bf16), wout_ref[...])
        y_ref[...] = x + _rms(m, npost_ref[...])

    finish(oa_scr[...])

    @pl.when(decay_floor < -GLA_SAFE_DECAY)
    def _():
        qs = qa * (GLA_DK ** -0.5)
        va_f = va_b.astype(f32)
        pos = lax.broadcasted_iota(jnp.int32, (T, 1), 0) % C
        er = lax.broadcasted_iota(jnp.int32, (GLA_K, GLA_V), 0)
        ec = lax.broadcasted_iota(jnp.int32, (GLA_K, GLA_V), 1)
        expand = jnp.where((er // GLA_DK) == (ec // GLA_DV), 1.0, 0.0).astype(bf16)

        def offset_term(d, acc):
            valid = pos >= d
            expo = jnp.where(valid, b - pltpu.roll(b, d, 0), 0.0)
            prod = jnp.where(valid, qs * pltpu.roll(ka, d, 0) * jnp.exp(expo), 0.0)
            return acc + _dot(prod.astype(bf16), expand) * pltpu.roll(va_f, d, 0)

        intra = lax.fori_loop(0, C, offset_term, jnp.zeros((T, GLA_V), f32))
        finish(inter_scr[...] + intra)

    @pl.when(i == pl.num_programs(0) - 1)
    def _():
        st_out_ref[...] = st_scr[...].T
        k_out_ref[...] = kcat_scr[T:T + W, :].T
        v_out_ref[...] = vcat_scr[T:T + W, :].T


def _const_spec(shape):
    nd = len(shape)
    return pl.BlockSpec(shape, lambda i: (0,) * nd, pipeline_mode=pl.Buffered(1))


def _mix_prompt(x, sinks, npre, win, wup, bg, gn, wba, wbb, wout, npost, *, T):
    L = x.shape[0]
    nb = L // T
    body = functools.partial(_mix_prompt_body, T=T)
    return pl.pallas_call(
        body,
        grid=(nb,),
        in_specs=[
            pl.BlockSpec(memory_space=pltpu.SMEM),
            pl.BlockSpec((T, D_MODEL), lambda i: (i, 0)),
            _const_spec(npre.shape), _const_spec(win.shape), _const_spec(wup.shape),
            _const_spec(bg.shape), _const_spec(gn.shape), _const_spec(wba.shape),
            _const_spec(wbb.shape), _const_spec(wout.shape), _const_spec(npost.shape),
        ],
        out_specs=[
            pl.BlockSpec((T, D_MODEL), lambda i: (i, 0)),
            pl.BlockSpec((GLA_K, GLA_DV), lambda i: (0, 0)),
            pl.BlockSpec((WINDOW, SWA_KV), lambda i: (0, 0)),
            pl.BlockSpec((WINDOW, SWA_KV), lambda i: (0, 0)),
        ],
        out_shape=[
            jax.ShapeDtypeStruct((L, D_MODEL), f32),
            jax.ShapeDtypeStruct((GLA_K, GLA_DV), f32),
            jax.ShapeDtypeStruct((WINDOW, SWA_KV), f32),
            jax.ShapeDtypeStruct((WINDOW, SWA_KV), f32),
        ],
        scratch_shapes=[
            pltpu.VMEM((GLA_DV, GLA_K), f32),
            pltpu.VMEM((T + WINDOW, SWA_KV), f32),
            pltpu.VMEM((T + WINDOW, SWA_KV), f32),
            pltpu.VMEM((T, GLA_V), f32),
            pltpu.VMEM((T, SWA_Q), f32),
            pltpu.VMEM((T, 2 * D_MODEL), f32),
            pltpu.VMEM((T, GLA_V), f32),
        ],
        compiler_params=pltpu.CompilerParams(
            dimension_semantics=("arbitrary",), vmem_limit_bytes=VMEM_LIMIT),
        name="mix_prompt",
    )(sinks, x, npre, win, wup, bg, gn, wba, wbb, wout, npost)


def _pre_sample_body(x_ref, npre_ref, win_ref, wup_ref, bg_ref,
                     qe_ref, kl_ref, e3_ref, oin_ref, va_ref, ga_ref, qb_ref, kb_ref, vb_ref,
                     gta_ref, gtb_ref, x_scr, *, NB, NT):
    for t in range(NT):
        x_scr[t * NB:(t + 1) * NB, :] = x_ref[:, t, :]
    h = _rms(x_scr[...], npre_ref[...]).astype(bf16)

    def proj(c0, n):
        return _dot(h, win_ref[:, c0:c0 + n])

    def blk(val, t):
        return val[t * NB:(t + 1) * NB, :]

    xg = _dot(proj(C_RA, RA_PAD).astype(bf16), wup_ref[...]) + bg_ref[...]
    qa = proj(C_QA, GLA_K) * (GLA_DK ** -0.5)
    ka = proj(C_KA, GLA_K)
    va = proj(C_VA, GLA_V)
    va_ref[...] = va
    la = jax.nn.log_sigmoid(xg) * (1.0 / GLA_TAU)
    b = [blk(la, 0)]
    for t in range(1, NT):
        b.append(b[-1] + blk(la, t))
    e3_ref[...] = jnp.exp(b[NT - 1])
    for t in range(NT):
        qe_ref[t * NB:(t + 1) * NB, :] = blk(qa, t) * jnp.exp(b[t])
        kl_ref[t * NB:(t + 1) * NB, :] = blk(ka, t) * jnp.exp(b[NT - 1] - b[t])
    pairs = [(t, j) for t in range(NT) for j in range(t + 1)]
    prods = [(blk(qa, t) * blk(ka, j) * jnp.exp(b[t] - b[j])).astype(bf16) for t, j in pairs]
    r = lax.broadcasted_iota(jnp.int32, (GLA_K, GLA_V), 0)
    c = lax.broadcasted_iota(jnp.int32, (GLA_K, GLA_V), 1)
    expand = jnp.where((r // GLA_DK) == (c // GLA_DV), 1.0, 0.0).astype(bf16)
    ga_ref[...] = proj(C_GA, GLA_V)
    qb_ref[...] = proj(C_QB, SWA_Q) * SWA_Q_SCALE
    kb_ref[...] = proj(C_KB, SWA_KV)
    vb_ref[...] = proj(C_VB, SWA_KV)
    att = _dot(jnp.concatenate(prods, axis=0), expand)
    gta_ref[...] = proj(C_GTA, D_MODEL)
    gtb_ref[...] = proj(C_GTB, D_MODEL)
    for t in range(NT):
        acc = None
        for idx, (tt, j) in enumerate(pairs):
            if tt != t:
                continue
            term = att[idx * NB:(idx + 1) * NB, :] * blk(va, j)
            acc = term if acc is None else acc + term
        oin_ref[t * NB:(t + 1) * NB, :] = acc


def _pre_sample(xs, npre, win, wup, bg):
    NB, NT, _ = xs.shape
    body = functools.partial(_pre_sample_body, NB=NB, NT=NT)
    widths = (GLA_K, GLA_K, None, GLA_V, GLA_V, GLA_V, SWA_Q, SWA_KV, SWA_KV, D_MODEL, D_MODEL)
    out_shape = [jax.ShapeDtypeStruct((NB, GLA_K) if w is None else (NT * NB, w), f32) for w in widths]
    return pl.pallas_call(
        body,
        out_shape=out_shape,
        scratch_shapes=[pltpu.VMEM((NB * NT, D_MODEL), f32)],
        compiler_params=pltpu.CompilerParams(vmem_limit_bytes=VMEM_LIMIT),
        name="pre_sample",
    )(xs, npre, win, wup, bg)


def _state_sample_body(sink_ref, qe_ref, kl_ref, e3_ref, oin_ref, va_ref, qb_ref, kb_ref, vb_ref,
                       s0_ref, kt_ref, vt_ref,
                       oa_ref, ob_ref, s1_ref, kt1_ref, vt1_ref, *, BB, NT):
    W = WINDOW
    SK = 2 * W
    HT = GLA_HEADS * NT
    HALF = SWA_HD
    hr = lax.broadcasted_iota(jnp.int32, (HT, GLA_K), 0) // NT
    hc = lax.broadcasted_iota(jnp.int32, (HT, GLA_K), 1) // GLA_DK
    own_head = hr == hc
    ones_rows = jnp.ones((16, GLA_DV), bf16)
    zero_rows = jnp.zeros((16, GLA_DV), bf16)
    zero_ht = jnp.zeros((HT, GLA_DV), bf16)
    G2 = 2 * NT
    row = lax.broadcasted_iota(jnp.int32, (G2, SK), 0)
    col = lax.broadcasted_iota(jnp.int32, (G2, SK), 1)
    rel = (row % NT) + W - col
    relf = rel.astype(f32)
    smask = (rel >= 0) & (rel < W)
    first_pair = lax.broadcasted_iota(jnp.int32, (G2, 1), 0) < NT
    pad_rows = jnp.zeros((8 - NT, SWA_KV), f32)
    pad_lanes = jnp.zeros((SWA_KV, SK - W - 8), f32)
    zero_half = jnp.zeros((HALF, SK), bf16)

    def head_variants(cat_t, kv):
        blk = cat_t[kv * HALF:(kv + 1) * HALF]
        return (jnp.concatenate([blk, zero_half], axis=0), jnp.concatenate([zero_half, blk], axis=0))

    pending = []
    for bi in range(BB):
        kt = kt_ref[bi]
        vt = vt_ref[bi]
        knew_t = jnp.concatenate([kb_ref[:, bi, :], pad_rows], axis=0).T
        vnew_t = jnp.concatenate([vb_ref[:, bi, :], pad_rows], axis=0).T
        kt1_ref[bi] = jnp.concatenate([kt[:, NT:], knew_t[:, 0:NT]], axis=1)
        vt1_ref[bi] = jnp.concatenate([vt[:, NT:], vnew_t[:, 0:NT]], axis=1)
        kcat = jnp.concatenate([kt, knew_t, pad_lanes], axis=1).astype(bf16)
        vcat = jnp.concatenate([vt, vnew_t, pad_lanes], axis=1).astype(bf16)
        q4b = qb_ref[:, bi, :].astype(bf16)
        for kv in range(SWA_KV_HEADS):
            p0 = 2 * kv
            q8 = jnp.concatenate([q4b[:, p0 * LANES:(p0 + 1) * LANES],
                                  q4b[:, (p0 + 1) * LANES:(p0 + 2) * LANES]], axis=0)
            scores = [_dot(q8, kvar) for kvar in head_variants(kcat, kv)]
            pending.append((bi, kv, scores, head_variants(vcat, kv)))

    for bi in range(BB):
        s0 = s0_ref[bi]
        q4 = qe_ref[:, bi, :]
        qm = jnp.where(own_head, jnp.concatenate([q4] * GLA_HEADS, axis=0), 0.0).astype(bf16)
        o_inter = _dot(qm, s0.astype(bf16))
        for hd in range(GLA_HEADS):
            lanes = slice(hd * GLA_DV, (hd + 1) * GLA_DV)
            oa_ref[:, bi, lanes] = o_inter[hd * NT:(hd + 1) * NT, :] + oin_ref[:, bi, lanes]
        k4 = kl_ref[:, bi, :]
        km = jnp.where(own_head, jnp.concatenate([k4] * GLA_HEADS, axis=0), 0.0).astype(bf16)
        e = e3_ref[bi:bi + 1, :]
        e_hi = e.astype(bf16)
        r1 = e - e_hi.astype(f32)
        e_mid = r1.astype(bf16)
        e_lo = (r1 - e_mid.astype(f32)).astype(bf16)
        e_rows = jnp.concatenate([e_hi, e_mid, e_lo, jnp.zeros((13, GLA_K), bf16)], axis=0)
        lhs = jnp.concatenate([km, e_rows], axis=0)
        v4 = va_ref[:, bi, :].astype(bf16)
        vrep = jnp.concatenate([v4[:, hd * GLA_DV:(hd + 1) * GLA_DV] for hd in range(GLA_HEADS)], axis=0)
        rhs = jnp.concatenate([jnp.concatenate([vrep, zero_ht], axis=1),
                               jnp.concatenate([zero_rows, ones_rows], axis=1)], axis=0)
        res = _dot_tn(lhs, rhs)
        s1_ref[bi] = res[:, GLA_DV:] * s0 + res[:, :GLA_DV]

    for bi, kv, scores, v_vars in pending:
        p0 = 2 * kv
        o8_t = None
        for e_ in range(2):
            h_first = 2 * p0 + e_
            h_second = 2 * (p0 + 1) + e_
            slope = jnp.where(first_pair, _alibi_slope(h_first), _alibi_slope(h_second))
            sink = jnp.where(first_pair, sink_ref[h_first] * LOG2E, sink_ref[h_second] * LOG2E)
            s = jnp.where(smask, scores[e_] - slope * relf, -jnp.inf)
            pr, inv = _softmax_sink(s, sink)
            o_t = _dot_nt(v_vars[e_], (pr * inv).astype(bf16))
            o8_t = o_t if o8_t is None else o8_t + o_t
        o8 = o8_t.T
        ob_ref[:, bi, p0 * LANES:(p0 + 1) * LANES] = o8[0:NT, :]
        ob_ref[:, bi, (p0 + 1) * LANES:(p0 + 2) * LANES] = o8[NT:2 * NT, :]


def _state_sample(sinks, qe, kl, e3, oin, va, qb, kb, vb, s0, kt, vt, *, NT, BB):
    NBS = s0.shape[0]
    assert NBS % BB == 0
    body = functools.partial(_state_sample_body, BB=BB, NT=NT)

    def tm(a):
        return a.reshape(NT, NBS, a.shape[-1])

    def rows(n):
        return pl.BlockSpec((NT, BB, n), lambda i: (0, i, 0))

    def per_seq(shape):
        return pl.BlockSpec((BB,) + shape, lambda i: (i, 0, 0))

    oa, ob, s1, kt1, vt1 = pl.pallas_call(
        body,
        grid=(NBS // BB,),
        in_specs=[
            pl.BlockSpec(memory_space=pltpu.SMEM),
            rows(GLA_K), rows(GLA_K), pl.BlockSpec((BB, GLA_K), lambda i: (i, 0)),
            rows(GLA_V), rows(GLA_V), rows(SWA_Q), rows(SWA_KV), rows(SWA_KV),
            per_seq((GLA_K, GLA_DV)), per_seq((SWA_KV, WINDOW)), per_seq((SWA_KV, WINDOW)),
        ],
        out_specs=[
            rows(GLA_V), rows(SWA_Q),
            per_seq((GLA_K, GLA_DV)), per_seq((SWA_KV, WINDOW)), per_seq((SWA_KV, WINDOW)),
        ],
        out_shape=[
            jax.ShapeDtypeStruct((NT, NBS, GLA_V), f32),
            jax.ShapeDtypeStruct((NT, NBS, SWA_Q), f32),
            jax.ShapeDtypeStruct((NBS, GLA_K, GLA_DV), f32),
            jax.ShapeDtypeStruct((NBS, SWA_KV, WINDOW), f32),
            jax.ShapeDtypeStruct((NBS, SWA_KV, WINDOW), f32),
        ],
        compiler_params=pltpu.CompilerParams(
            dimension_semantics=("arbitrary",), vmem_limit_bytes=VMEM_LIMIT),
        name="state_sample",
    )(sinks, tm(qe), tm(kl), e3, tm(oin), tm(va), tm(qb), tm(kb), tm(vb), s0, kt, vt)
    return oa.reshape(NT * NBS, GLA_V), ob.reshape(NT * NBS, SWA_Q), s1, kt1, vt1


def _tail_sample_body(x_ref, oa_ref, ga_ref, ob_ref, gta_ref, gtb_ref, gn_ref,
                      wba_ref, wbb_ref, wout_ref, npost_ref,
                      cst_ref, fpre_ref, wffn_ref, cw_ref, cb_ref, wo_ref, fpost_ref,
                      y_ref, conv_out_ref, x_scr, up_scr, y_scr, *, NB, NT):
    T = NB * NT
    for t in range(NT):
        x_scr[t * NB:(t + 1) * NB, :] = x_ref[:, t, :]
    for t in range(CONV_W - 1):
        up_scr[t * NB:(t + 1) * NB, :] = cst_ref[:, t, :]
    base = (CONV_W - 1) * NB
    oa = _gla_out_norm(oa_ref[...], gn_ref, ga_ref[...])
    x1 = _mix_tail(x_scr[...], oa, ob_ref[...], gta_ref[...], gtb_ref[...],
                   wba_ref, wbb_ref, wout_ref, npost_ref)
    h = _rms(x1, fpre_ref[...]).astype(bf16)
    _ffn_columns(h, wffn_ref, cw_ref, cb_ref, up_scr, y_scr, T=T, base=base, shift=NB)
    f = _dot(y_scr[...], wo_ref[...])
    y = x1 + _rms(f, fpost_ref[...])
    for t in range(NT):
        y_ref[:, t, :] = y[t * NB:(t + 1) * NB, :]
    for t in range(CONV_W - 1):
        conv_out_ref[:, t, :] = up_scr[T + t * NB:T + (t + 1) * NB, :]


def _tail_sample(xs, oa, ga, ob, gta, gtb, gn, wba, wbb, wout, npost, cst, fpre, wffn, cw, cb, wo, fpost):
    NB, NT, _ = xs.shape
    T = NB * NT
    return pl.pallas_call(
        functools.partial(_tail_sample_body, NB=NB, NT=NT),
        out_shape=[
            jax.ShapeDtypeStruct((NB, NT, D_MODEL), f32),
            jax.ShapeDtypeStruct((NB, CONV_W - 1, D_FF), f32),
        ],
        scratch_shapes=[
            pltpu.VMEM((T, D_MODEL), f32),
            pltpu.VMEM((T + (CONV_W - 1) * NB, D_FF), f32),
            pltpu.VMEM((T, D_FF), bf16),
        ],
        compiler_params=pltpu.CompilerParams(vmem_limit_bytes=VMEM_LIMIT),
        name="tail_sample",
    )(xs, oa, ga, ob, gta, gtb, gn, wba, wbb, wout, npost, cst, fpre, wffn, cw, cb, wo, fpost)


def _ffn_columns(h, wffn_ref, cw_ref, cb_ref, up_scr, y_scr, *, T, base, shift):
    for c0 in range(0, D_FF, FFN_COLS):
        cols = slice(c0, c0 + FFN_COLS)
        u = _dot(h, wffn_ref[:, c0:c0 + FFN_COLS])
        g = _dot(h, wffn_ref[:, D_FF + c0:D_FF + c0 + FFN_COLS])
        up_scr[base:base + T, cols] = u
        u1 = up_scr[base - shift:base - shift + T, cols]
        u2 = up_scr[base - 2 * shift:base - 2 * shift + T, cols]
        cv = (cb_ref[:, cols] + cw_ref[2:3, cols] * u + cw_ref[1:2, cols] * u1 + cw_ref[0:1, cols] * u2)
        y_scr[:, cols] = (_gelu_tanh(cv) * g).astype(bf16)


def _ffn_prompt_body(x_ref, npre_ref, wffn_ref, cw_ref, cb_ref, wo_ref, npost_ref,
                     y_ref, conv_out_ref, up_scr, y_scr, *, T):
    i = pl.program_id(0)
    base = 8

    @pl.when(i == 0)
    def _():
        up_scr[0:base, :] = jnp.zeros((base, D_FF), f32)

    @pl.when(i > 0)
    def _():
        up_scr[0:base, :] = up_scr[T:T + base, :]

    x = x_ref[...]
    h = _rms(x, npre_ref[...]).astype(bf16)
    _ffn_columns(h, wffn_ref, cw_ref, cb_ref, up_scr, y_scr, T=T, base=base, shift=1)
    f = _dot(y_scr[...], wo_ref[...])
    y_ref[...] = x + _rms(f, npost_ref[...])

    @pl.when(i == pl.num_programs(0) - 1)
    def _():
        conv_out_ref[...] = up_scr[T + base - (CONV_W - 1):T + base, :]


def _ffn_prompt(x, npre, wffn, cw, cb, wo, npost, *, T):
    L = x.shape[0]
    body = functools.partial(_ffn_prompt_body, T=T)
    return pl.pallas_call(
        body,
        grid=(L // T,),
        in_specs=[
            pl.BlockSpec((T, D_MODEL), lambda i: (i, 0)),
            _const_spec(npre.shape), _const_spec(wffn.shape), _const_spec(cw.shape),
            _const_spec(cb.shape), _const_spec(wo.shape), _const_spec(npost.shape),
        ],
        out_specs=[
            pl.BlockSpec((T, D_MODEL), lambda i: (i, 0)),
            pl.BlockSpec((CONV_W - 1, D_FF), lambda i: (0, 0)),
        ],
        out_shape=[
            jax.ShapeDtypeStruct((L, D_MODEL), f32),
            jax.ShapeDtypeStruct((CONV_W - 1, D_FF), f32),
        ],
        scratch_shapes=[
            pltpu.VMEM((T + 8, D_FF), f32),
            pltpu.VMEM((T, D_FF), bf16),
        ],
        compiler_params=pltpu.CompilerParams(
            dimension_semantics=("arbitrary",), vmem_limit_bytes=VMEM_LIMIT),
        name="ffn_prompt",
    )(x, npre, wffn, cw, cb, wo, npost)


def _prep_w_in_body(*refs, n_plain, per_step):
    o_ref = refs[-1]
    j = pl.program_id(0)
    for k, wt_ref in enumerate(refs[:-1]):
        x = wt_ref[...]
        r = lax.broadcasted_iota(jnp.int32, x.shape, 0)
        x = jnp.where((j * per_step + k < n_plain) | (r < GLA_RANK), x, 0.0)
        o_ref[:, k * PREP_ROWS:(k + 1) * PREP_ROWS] = x.T.astype(bf16)


def _prep_w_in(w_in):
    d_in, n_cols = w_in.shape
    head = C_GA
    tail_src = head + GLA_RANK
    n_head = head // PREP_ROWS
    n_tail = (n_cols - tail_src) // PREP_ROWS
    assert head % PREP_ROWS == 0 and (n_cols - tail_src) % PREP_ROWS == 0
    n_plain = n_head + n_tail
    assert C_RA == n_plain * PREP_ROWS
    per_step = 3
    assert (n_plain + 1) % per_step == 0

    def row_off(blk):
        off = jnp.where(blk < n_head, blk * PREP_ROWS,
                        jnp.where(blk < n_plain, tail_src + (blk - n_head) * PREP_ROWS, head))
        return pl.multiple_of(off, 8)

    def in_spec(k):
        return pl.BlockSpec((pl.Element(PREP_ROWS), pl.Element(d_in)), lambda j: (row_off(j * per_step + k), 0))

    wt = jnp.swapaxes(w_in, 0, 1)
    return pl.pallas_call(
        functools.partial(_prep_w_in_body, n_plain=n_plain, per_step=per_step),
        grid=((n_plain + 1) // per_step,),
        in_specs=[in_spec(k) for k in range(per_step)],
        out_specs=pl.BlockSpec((d_in, per_step * PREP_ROWS), lambda j: (0, j)),
        out_shape=jax.ShapeDtypeStruct((d_in, (n_plain + 1) * PREP_ROWS), bf16),
        compiler_params=pltpu.CompilerParams(dimension_semantics=("arbitrary",)),
        name="prep_w_in",
    )(*([wt] * per_step))


def kernel(x_prompt, x_sample, state_gla, cache_swa_k, cache_swa_v, state_ffn_conv, norm_mix_pre, norm_mix_post, w_in, w_gate_up, b_gate, gla_norm, sinks, w_branch_a, w_branch_b, w_out, norm_ffn_pre, norm_ffn_post, w_ffn_in, conv_w, conv_b, w_ffn_out):
    depth = w_in.shape[0]
    assert depth == 1
    l = 0
    B, L, _ = x_prompt.shape
    assert B == 1
    NBS, NT, _ = x_sample.shape
    assert L % MIX_BLOCK == 0 and L % FFN_BLOCK == 0 and NBS % STATE_SEQS == 0
    assert cache_swa_k.shape[2] == WINDOW and NT < 8

    win = _prep_w_in(w_in[l])
    wup = jnp.zeros((RA_PAD, GLA_K), f32).at[:GLA_RANK].set(w_gate_up[l]).astype(bf16)
    bg = b_gate[l].reshape(1, GLA_K)
    gn = gla_norm[l].reshape(1, GLA_DV)
    npre = norm_mix_pre[l].reshape(1, D_MODEL)
    npost = norm_mix_post[l].reshape(1, D_MODEL)
    wba = w_branch_a[l].astype(bf16)
    wbb = w_branch_b[l].astype(bf16)
    wout = w_out[l].astype(bf16)
    fpre = norm_ffn_pre[l].reshape(1, D_MODEL)
    fpost = norm_ffn_post[l].reshape(1, D_MODEL)
    wffn = w_ffn_in[l].astype(bf16)
    cw = conv_w[l]
    cb = conv_b[l].reshape(1, D_FF)
    wo = w_ffn_out[l].astype(bf16)
    sk = sinks[l]

    x1, st_p, k_p, v_p = _mix_prompt(x_prompt[0], sk, npre, win, wup, bg, gn, wba, wbb, wout, npost, T=MIX_BLOCK)
    y_p, conv_p = _ffn_prompt(x1, fpre, wffn, cw, cb, wo, fpost, T=FFN_BLOCK)

    y_prompt = y_p[None]
    gla_state_prompt = st_p.reshape(1, 1, GLA_HEADS, GLA_DK, GLA_DV)
    swa_k_prompt = jnp.transpose(k_p.reshape(SWA_KV_HEADS, SWA_HD, WINDOW), (2, 0, 1))[None, None]
    swa_v_prompt = jnp.transpose(v_p.reshape(SWA_KV_HEADS, SWA_HD, WINDOW), (2, 0, 1))[None, None]
    conv_prompt = conv_p[None, None]

    qe, kl, e3, oin, va, ga, qb, kb, vb, gta, gtb = _pre_sample(x_sample, npre, win, wup, bg)
    kt = jnp.transpose(cache_swa_k[l], (0, 2, 3, 1)).reshape(NBS, SWA_KV, WINDOW)
    vt = jnp.transpose(cache_swa_v[l], (0, 2, 3, 1)).reshape(NBS, SWA_KV, WINDOW)
    oa_raw, ob, s1, kt1, vt1 = _state_sample(
        sk, qe, kl, e3, oin, va, qb, kb, vb, state_gla[l].reshape(NBS, GLA_K, GLA_DV), kt, vt, NT=NT, BB=STATE_SEQS)
    y_sample, conv_s = _tail_sample(x_sample, oa_raw, ga, ob, gta, gtb, gn, wba, wbb, wout, npost,
                                    state_ffn_conv[l], fpre, wffn, cw, cb, wo, fpost)

    def cache_out(t):
        return jnp.transpose(t.reshape(NBS, SWA_KV_HEADS, SWA_HD, WINDOW), (0, 3, 1, 2))[None]

    gla_state_sample = s1.reshape(1, NBS, GLA_HEADS, GLA_DK, GLA_DV)
    swa_k_sample = cache_out(kt1)
    swa_v_sample = cache_out(vt1)
    conv_sample = conv_s[None]
    return (y_prompt, y_sample, gla_state_prompt, gla_state_sample, swa_k_prompt, swa_v_prompt,
            swa_k_sample, swa_v_sample, conv_prompt, conv_sample)
```
